```python
import jax, jax.numpy as jnp
from jax import lax
import numpy as np

D_MODEL = 2048
BATCH = 4
SEQ = 2048
DEPTH = 1
DEC_BATCH = 128
DEC_SEQ = 1
PAST_LEN = 16384
PAGE_SIZE = 128

HEAD_DIM = 64
D_RWKV = D_MODEL // 2
H_A = D_RWKV // HEAD_DIM
D_SGU = D_MODEL - D_RWKV
CHUNK = 128
SGU_GROUP_DIM = 128
G_SGU = D_SGU // SGU_GROUP_DIM
LORA_W = 64
LORA_A = 64
LORA_G = 160
D_SHIFT = 3 * D_RWKV + LORA_W + LORA_A + LORA_G
D_PROJ = D_SHIFT + 2 * D_SGU
D_FF = 4 * D_MODEL
RMS_EPS = 1e-5
LN_EPS = 1e-5
GN_EPS = 64e-5

kernel_name = "rwkv7_sgu_hybrid_step"


def _rmsnorm(x, g):
    xf = x.astype(jnp.float32)
    y = xf * lax.rsqrt(jnp.mean(xf * xf, axis=-1, keepdims=True) + RMS_EPS)
    return (y * g.astype(jnp.float32)).astype(x.dtype)


def _layernorm(x, g, b):
    xf = x.astype(jnp.float32)
    mu = jnp.mean(xf, axis=-1, keepdims=True)
    var = jnp.mean(jnp.square(xf - mu), axis=-1, keepdims=True)
    y = (xf - mu) * lax.rsqrt(var + LN_EPS)
    return (y * g.astype(jnp.float32) + b.astype(jnp.float32)).astype(x.dtype)


def _wkv7_scan(s0, r, w, k, v, aa, bb):
    def step(S, inp):
        r_t, w_t, k_t, v_t, a_t, b_t = inp
        sa = jnp.einsum('bhvk,bhk->bhv', S, a_t)
        S = S * w_t[:, :, None, :] + sa[..., None] * b_t[:, :, None, :] + v_t[..., None] * k_t[:, :, None, :]
        o = jnp.einsum('bhvk,bhk->bhv', S, r_t)
        return S, o
    xs = (jnp.swapaxes(r, 0, 1), jnp.swapaxes(w, 0, 1), jnp.swapaxes(k, 0, 1),
          jnp.swapaxes(v, 0, 1), jnp.swapaxes(aa, 0, 1), jnp.swapaxes(bb, 0, 1))
    s_final, o = lax.scan(step, s0, xs)
    return jnp.swapaxes(o, 0, 1), s_final


def _layer(x, wkv0, shift0, norm1_g, w_in, mu_shift, w0, w_up, a0, a_up, g_up, k_k, k_a, r_k,
           lnx_g, lnx_b, sgu_norm_g, sgu_norm_b, sgu_w, sgu_b, w_out, norm2_g, w_ffn_up, w_ffn_down):
    B, T, _ = x.shape
    f32 = jnp.float32
    h = _rmsnorm(x, norm1_g)
    p = h @ w_in
    p_rw, p_sg = p[..., :D_SHIFT], p[..., D_SHIFT:]

    prev = jnp.concatenate([shift0[:, None, :].astype(p_rw.dtype), p_rw[:, :-1]], axis=1)
    m = p_rw + (prev - p_rw) * mu_shift
    new_shift = p_rw[:, -1].astype(f32)
    r, k, v, w_lo, a_lo, g_lo = jnp.split(
        m, [D_RWKV, 2 * D_RWKV, 3 * D_RWKV, 3 * D_RWKV + LORA_W, 3 * D_RWKV + LORA_W + LORA_A], axis=-1)
    w_log = -jax.nn.softplus(-(w0 + jnp.tanh(w_lo) @ w_up).astype(f32)) - 0.5
    decay = jnp.exp(-jnp.exp(w_log))
    a = jax.nn.sigmoid((a0 + a_lo @ a_up).astype(f32))
    g = jax.nn.sigmoid(g_lo) @ g_up
    hs = (B, T, H_A, HEAD_DIM)
    rh = r.astype(f32).reshape(hs)
    kh = k.astype(f32).reshape(hs)
    vh = v.astype(f32).reshape(hs)
    ah = a.reshape(hs)
    wh = decay.reshape(hs)
    kk = kh * k_k.astype(f32)
    kk = kk / jnp.maximum(jnp.sqrt(jnp.sum(kk * kk, axis=-1, keepdims=True)), 1e-12)
    kh = kh * (1.0 + (ah - 1.0) * k_a.astype(f32))
    o, wkv_new = _wkv7_scan(wkv0.astype(f32), rh, wh, kh, vh, -kk, kk * ah)
    mu = jnp.mean(o, axis=-1, keepdims=True)
    var = jnp.mean(jnp.square(o - mu), axis=-1, keepdims=True)
    o = (o - mu) * lax.rsqrt(var + GN_EPS) * lnx_g.astype(f32) + lnx_b.astype(f32)
    o = o + jnp.sum(rh * kh * r_k.astype(f32), axis=-1, keepdims=True) * vh
    y_a = o.reshape(B, T, D_RWKV).astype(x.dtype) * g

    z = jax.nn.gelu(p_sg)
    u, vs = z[..., :D_SGU], z[..., D_SGU:]
    vs = _layernorm(vs, sgu_norm_g, sgu_norm_b)
    n_ch = -(-T // CHUNK)
    pad = n_ch * CHUNK - T
    vpad = jnp.pad(vs, ((0, 0), (0, pad), (0, 0))).reshape(B, n_ch, CHUNK, G_SGU, SGU_GROUP_DIM)
    tril = jnp.tril(jnp.ones((CHUNK, CHUNK), dtype=bool))
    ws = jnp.where(tril[None], sgu_w, jnp.zeros_like(sgu_w))
    mix = jnp.einsum('gij,bcjgd->bcigd', ws, vpad) + jnp.transpose(sgu_b)[None, None, :, :, None]
    mix = mix.reshape(B, n_ch * CHUNK, D_SGU)[:, :T]
    y_b = u * mix

    x = x + jnp.concatenate([y_a, y_b], axis=-1) @ w_out
    h2 = _rmsnorm(x, norm2_g)
    x = x + jnp.square(jax.nn.relu(h2 @ w_ffn_up)) @ w_ffn_down
    return x, wkv_new, new_shift, vs.astype(f32)


def setup_inputs(seed: int = 0) -> dict:
    key = jax.random.key(seed)
    ks = jax.random.split(key, 32)
    nrm = jax.random.normal
    L = DEPTH
    inp = {}
    inp["x_prompt"] = nrm(ks[0], (BATCH, SEQ, D_MODEL), jnp.float32)
    inp["x_sample"] = nrm(ks[1], (DEC_BATCH, DEC_SEQ, D_MODEL), jnp.float32)
    inp["state_wkv"] = 0.5 * nrm(ks[2], (L, DEC_BATCH, H_A, HEAD_DIM, HEAD_DIM), jnp.float32)
    inp["state_shift"] = nrm(ks[3], (L, DEC_BATCH, D_SHIFT), jnp.float32)
    inp["norm1_g"] = 1.0 + 0.02 * nrm(ks[4], (L, D_MODEL), jnp.float32)
    inp["w_in"] = nrm(ks[5], (L, D_MODEL, D_PROJ), jnp.float32) * D_MODEL ** -0.5
    inp["mu_shift"] = jax.random.uniform(ks[6], (L, D_SHIFT), jnp.float32)
    inp["w0"] = -1.0 + 0.5 * nrm(ks[7], (L, D_RWKV), jnp.float32)
    inp["w_up"] = 0.5 * nrm(ks[8], (L, LORA_W, D_RWKV), jnp.float32) * LORA_W ** -0.5
    inp["a0"] = 0.1 * nrm(ks[9], (L, D_RWKV), jnp.float32)
    inp["a_up"] = nrm(ks[10], (L, LORA_A, D_RWKV), jnp.float32) * LORA_A ** -0.5
    inp["g_up"] = nrm(ks[11], (L, LORA_G, D_RWKV), jnp.float32) * LORA_G ** -0.5
    inp["k_k"] = 0.85 + 0.05 * nrm(ks[12], (L, H_A, HEAD_DIM), jnp.float32)
    inp["k_a"] = 1.0 + 0.05 * nrm(ks[13], (L, H_A, HEAD_DIM), jnp.float32)
    inp["r_k"] = 0.1 * nrm(ks[14], (L, H_A, HEAD_DIM), jnp.float32)
    inp["lnx_g"] = 1.0 + 0.02 * nrm(ks[15], (L, H_A, HEAD_DIM), jnp.float32)
    inp["lnx_b"] = 0.02 * nrm(ks[16], (L, H_A, HEAD_DIM), jnp.float32)
    inp["sgu_norm_g"] = 1.0 + 0.02 * nrm(ks[17], (L, D_SGU), jnp.float32)
    inp["sgu_norm_b"] = 0.02 * nrm(ks[18], (L, D_SGU), jnp.float32)
    inp["sgu_w"] = nrm(ks[19], (L, G_SGU, CHUNK, CHUNK), jnp.float32) * CHUNK ** -0.5
    inp["sgu_b"] = 1.0 + 0.1 * nrm(ks[20], (L, G_SGU, CHUNK), jnp.float32)
    inp["w_out"] = nrm(ks[21], (L, D_MODEL, D_MODEL), jnp.float32) * D_MODEL ** -0.5
    inp["norm2_g"] = 1.0 + 0.02 * nrm(ks[22], (L, D_MODEL), jnp.float32)
    inp["w_ffn_up"] = nrm(ks[23], (L, D_MODEL, D_FF), jnp.float32) * D_MODEL ** -0.5
    inp["w_ffn_down"] = nrm(ks[24], (L, D_FF, D_MODEL), jnp.float32) * D_FF ** -0.5
    inp["norm_f_g"] = 1.0 + 0.02 * nrm(ks[25], (D_MODEL,), jnp.float32)
    return inp


def reference(x_prompt, x_sample, state_wkv, state_shift, norm1_g, w_in, mu_shift, w0, w_up, a0, a_up,
              g_up, k_k, k_a, r_k, lnx_g, lnx_b, sgu_norm_g, sgu_norm_b, sgu_w, sgu_b, w_out, norm2_g,
              w_ffn_up, w_ffn_down, norm_f_g):
    xp, xs = x_prompt, x_sample
    wkv_p, shift_p, wkv_s, shift_s, vrows_s = [], [], [], [], []
    zero_wkv = jnp.zeros((x_prompt.shape[0], H_A, HEAD_DIM, HEAD_DIM), jnp.float32)
    zero_shift = jnp.zeros((x_prompt.shape[0], D_SHIFT), jnp.float32)
    for l in range(DEPTH):
        lw = (norm1_g[l], w_in[l], mu_shift[l], w0[l], w_up[l], a0[l], a_up[l], g_up[l], k_k[l], k_a[l],
              r_k[l], lnx_g[l], lnx_b[l], sgu_norm_g[l], sgu_norm_b[l], sgu_w[l], sgu_b[l], w_out[l],
              norm2_g[l], w_ffn_up[l], w_ffn_down[l])
        xp, s_p, sh_p, _ = _layer(xp, zero_wkv, zero_shift, *lw)
        xs, s_s, sh_s, v_s = _layer(xs, state_wkv[l], state_shift[l], *lw)
        wkv_p.append(s_p); shift_p.append(sh_p)
        wkv_s.append(s_s); shift_s.append(sh_s); vrows_s.append(v_s)
    y_prompt = _rmsnorm(xp, norm_f_g)
    y_sample = _rmsnorm(xs, norm_f_g)
    wkv_prompt = jnp.stack(wkv_p)
    shift_prompt = jnp.stack(shift_p)
    wkv_sample = jnp.stack(wkv_s)
    shift_sample = jnp.stack(shift_s)
    sgu_v_sample = jnp.stack(vrows_s)
    return (y_prompt, y_sample, wkv_prompt, shift_prompt, wkv_sample, shift_sample, sgu_v_sample)
```

```python
import functools
import math

import jax
import jax.numpy as jnp
from jax import lax
from jax.experimental import pallas as pl
from jax.experimental.pallas import tpu as pltpu

F32 = jnp.float32
BF16 = jnp.bfloat16

HEAD = 64
LANES = 128
SGU_CHUNK = 128
WKV_CHUNK = 64
RMS_EPS = 1e-5
LN_EPS = 1e-5
GN_EPS = 64e-5
DECAY_SCALE = math.exp(-0.5)
GELU_C = math.sqrt(2.0 / math.pi)
VMEM_LIMIT = 56 * 1024 * 1024


def _params(sem):
    return pltpu.CompilerParams(dimension_semantics=sem, vmem_limit_bytes=VMEM_LIMIT)


def _dot(a, b):
    return jnp.dot(a.astype(BF16), b.astype(BF16), preferred_element_type=F32)


def _dot_nt(a, b):
    return lax.dot_general(a.astype(BF16), b.astype(BF16), (((1,), (1,)), ((), ())),
                           preferred_element_type=F32)


def _dot_tn(a, b):
    return lax.dot_general(a.astype(BF16), b.astype(BF16), (((0,), (0,)), ((), ())),
                           preferred_element_type=F32)


def _split3(x):
    hi = x.astype(BF16)
    r1 = x - hi.astype(F32)
    mid = r1.astype(BF16)
    lo = (r1 - mid.astype(F32)).astype(BF16)
    return hi, mid, lo


def _dot_exact_rhs(x, m):
    hi, mid, lo = _split3(x)
    mb = m.astype(BF16)
    return (jnp.dot(hi, mb, preferred_element_type=F32) + jnp.dot(mid, mb, preferred_element_type=F32)
            + jnp.dot(lo, mb, preferred_element_type=F32))


def _dot_exact_lhs(m, x):
    hi, mid, lo = _split3(x)
    mb = m.astype(BF16)
    return (jnp.dot(mb, hi, preferred_element_type=F32) + jnp.dot(mb, mid, preferred_element_type=F32)
            + jnp.dot(mb, lo, preferred_element_type=F32))


def _sigmoid(x):
    return 1.0 / (1.0 + jnp.exp(-x))


def _head_ones():
    r = lax.broadcasted_iota(jnp.int32, (LANES, LANES), 0) // HEAD
    c = lax.broadcasted_iota(jnp.int32, (LANES, LANES), 1) // HEAD
    return (r == c).astype(F32)


def _head_sum(x, ones):
    parts = [_dot_exact_rhs(x[:, s:s + LANES], ones) for s in range(0, x.shape[1], LANES)]
    return parts[0] if len(parts) == 1 else jnp.concatenate(parts, axis=1)


def _rms(x, g):
    return x * lax.rsqrt(jnp.mean(x * x, axis=-1, keepdims=True) + RMS_EPS) * g


def _in_proj_kernel(x_ref, g_ref, w_ref, o_ref, h_ref):
    @pl.when(pl.program_id(1) == 0)
    def _():
        h_ref[...] = _rms(x_ref[...], g_ref[...]).astype(BF16)

    o_ref[...] = jnp.dot(h_ref[...], w_ref[...].astype(BF16), preferred_element_type=F32)


def _in_proj(x, g, w, tm, tn):
    m, d = x.shape
    n = w.shape[1]
    return pl.pallas_call(
        _in_proj_kernel,
        grid=(m // tm, n // tn),
        in_specs=[pl.BlockSpec((tm, d), lambda i, j: (i, 0)),
                  pl.BlockSpec((1, d), lambda i, j: (0, 0)),
                  pl.BlockSpec((d, tn), lambda i, j: (0, j))],
        out_specs=pl.BlockSpec((tm, tn), lambda i, j: (i, j)),
        out_shape=jax.ShapeDtypeStruct((m, n), F32),
        scratch_shapes=[pltpu.VMEM((tm, d), BF16)],
        compiler_params=_params(("parallel", "arbitrary")),
        name="in_proj",
    )(x, g, w)


def _rwkv_mix(pr, pk, pv, pg, pw, qr, qk, qv, qg, qw, mu_r, mu_k, mu_v, mu_g, mu_w,
              w0, w_up, a0, a_up, g_up, k_k, k_a, outs):
    r_ref, lw_ref, k_ref, v_ref, aa_ref, bb_ref, g_ref = outs
    r = pr + (qr - pr) * mu_r
    k = pk + (qk - pk) * mu_k
    v = pv + (qv - pv) * mu_v
    gl = pg + (qg - pg) * mu_g
    wa = pw + (qw - pw) * mu_w
    lw = -DECAY_SCALE * _sigmoid(w0 + _dot(jnp.tanh(wa), w_up))
    a = _sigmoid(a0 + _dot(wa, a_up))
    gate = _dot(_sigmoid(gl), g_up)
    kk = k * k_k
    ss = _head_sum(kk * kk, _head_ones())
    kk = kk / jnp.maximum(jnp.sqrt(ss), 1e-12)
    r_ref[...] = r
    lw_ref[...] = lw
    k_ref[...] = k * (1.0 + (a - 1.0) * k_a)
    v_ref[...] = v
    aa_ref[...] = -kk
    bb_ref[...] = kk * a
    g_ref[...] = gate


def _shifted(p, carry_ref, first):
    @pl.when(first)
    def _():
        carry_ref[...] = jnp.zeros_like(carry_ref)

    rows = lax.broadcasted_iota(jnp.int32, p.shape, 0)
    q = jnp.where(rows == 0, carry_ref[...], pltpu.roll(p, 1, axis=0))
    carry_ref[...] = p[p.shape[0] - 1:, :]
    return q


def _prep_prompt_kernel(pr, pk, pv, pg, pw, mu_r, mu_k, mu_v, mu_g, mu_w, w0, w_up, a0, a_up, g_up,
                        k_k, k_a, *rest):
    outs, carries = rest[:7], rest[7:]
    first = pl.program_id(1) == 0
    ps = [ref[...] for ref in (pr, pk, pv, pg, pw)]
    qs = [_shifted(p, c, first) for p, c in zip(ps, carries)]
    _rwkv_mix(*ps, *qs, mu_r[...], mu_k[...], mu_v[...], mu_g[...], mu_w[...], w0[...], w_up[...],
              a0[...], a_up[...], g_up[...], k_k[...], k_a[...], outs)


def _prep_sample_kernel(pr, pk, pv, pg, pw, qr, qk, qv, qg, qw, mu_r, mu_k, mu_v, mu_g, mu_w, w0, w_up,
                        a0, a_up, g_up, k_k, k_a, *outs):
    _rwkv_mix(pr[...], pk[...], pv[...], pg[...], pw[...], qr[...], qk[...], qv[...], qg[...], qw[...],
              mu_r[...], mu_k[...], mu_v[...], mu_g[...], mu_w[...], w0[...], w_up[...], a0[...],
              a_up[...], g_up[...], k_k[...], k_a[...], outs)


class _Layout:
    def __init__(self, d_rwkv, d_sgu, lora_w, lora_a, lora_g):
        self.d_rwkv, self.d_sgu = d_rwkv, d_sgu
        self.lora_w, self.lora_a, self.lora_g = lora_w, lora_a, lora_g
        self.gl_w = -(-lora_g // LANES) * LANES
        self.wa_w = lora_w + lora_a
        assert self.wa_w == LANES and d_rwkv % self.gl_w == 0
        self.u0 = 0
        self.vs0 = d_sgu
        self.r0 = 2 * d_sgu
        self.k0 = self.r0 + d_rwkv
        self.v0 = self.k0 + d_rwkv
        self.gl0 = self.v0 + d_rwkv
        self.wa0 = self.gl0 + self.gl_w
        self.used = self.wa0 + self.wa_w
        self.d_shift = 3 * d_rwkv + lora_w + lora_a + lora_g

    def width(self, tn):
        return -(-self.used // tn) * tn

    def rw_pieces(self, a):
        d = self.d_rwkv
        wa = a[..., 3 * d:3 * d + self.wa_w]
        gl = a[..., 3 * d + self.wa_w:self.d_shift]
        pad = [(0, 0)] * (a.ndim - 1) + [(0, self.gl_w - self.lora_g)]
        return a[..., :d], a[..., d:2 * d], a[..., 2 * d:3 * d], jnp.pad(gl, pad), wa

    def relayout_cols(self, w, tn):
        sg = w[:, self.d_shift:]
        r, k, v, gl, wa = self.rw_pieces(w[:, :self.d_shift])
        out = jnp.concatenate([sg, r, k, v, gl, wa], axis=1)
        return jnp.pad(out, ((0, 0), (0, self.width(tn) - self.used)))

    def shift_row(self, p_rows):
        return jnp.concatenate([p_rows[:, self.r0:self.gl0], p_rows[:, self.wa0:self.wa0 + self.wa_w],
                                p_rows[:, self.gl0:self.gl0 + self.lora_g]], axis=1)


def _prep_weights(lay, mu, w0, w_up, a0, a_up, g_up, k_k, k_a):
    d = lay.d_rwkv
    mus = [m[None, :] for m in lay.rw_pieces(mu)]
    w_up_p = jnp.pad(w_up, ((0, lay.lora_a), (0, 0)))
    a_up_p = jnp.pad(a_up, ((lay.lora_w, 0), (0, 0)))
    g_up_p = jnp.pad(g_up, ((0, lay.gl_w - lay.lora_g), (0, 0)))
    return mus + [w0[None, :], w_up_p, a0[None, :], a_up_p, g_up_p, k_k.reshape(1, d), k_a.reshape(1, d)]


def _full(a):
    return pl.BlockSpec(a.shape, lambda *_: (0,) * a.ndim)


def _prep_prompt(p, lay, weights, batch, seq, tp):
    d = lay.d_rwkv
    nt = seq // tp
    row = lambda b, i: b * nt + i
    p_specs = [pl.BlockSpec((tp, d), lambda b, i: (row(b, i), lay.r0 // d)),
               pl.BlockSpec((tp, d), lambda b, i: (row(b, i), lay.k0 // d)),
               pl.BlockSpec((tp, d), lambda b, i: (row(b, i), lay.v0 // d)),
               pl.BlockSpec((tp, lay.gl_w), lambda b, i: (row(b, i), lay.gl0 // lay.gl_w)),
               pl.BlockSpec((tp, lay.wa_w), lambda b, i: (row(b, i), lay.wa0 // lay.wa_w))]
    out_spec = pl.BlockSpec((tp, d), lambda b, i: (row(b, i), 0))
    return pl.pallas_call(
        _prep_prompt_kernel,
        grid=(batch, nt),
        in_specs=p_specs + [_full(w) for w in weights],
        out_specs=[out_spec] * 7,
        out_shape=[jax.ShapeDtypeStruct((batch * seq, d), F32)] * 7,
        scratch_shapes=[pltpu.VMEM((1, d), F32)] * 3 + [pltpu.VMEM((1, lay.gl_w), F32),
                                                        pltpu.VMEM((1, lay.wa_w), F32)],
        compiler_params=_params(("parallel", "arbitrary")),
        name="rwkv_prep_prompt",
    )(p, p, p, p, p, *weights)


def _prep_sample(p, prev, lay, weights):
    d = lay.d_rwkv
    m = p.shape[0]
    p_specs = [pl.BlockSpec((m, d), lambda i: (0, lay.r0 // d)),
               pl.BlockSpec((m, d), lambda i: (0, lay.k0 // d)),
               pl.BlockSpec((m, d), lambda i: (0, lay.v0 // d)),
               pl.BlockSpec((m, lay.gl_w), lambda i: (0, lay.gl0 // lay.gl_w)),
               pl.BlockSpec((m, lay.wa_w), lambda i: (0, lay.wa0 // lay.wa_w))]
    prevs = list(lay.rw_pieces(prev))
    out_spec = pl.BlockSpec((m, d), lambda i: (0, 0))
    return pl.pallas_call(
        _prep_sample_kernel,
        grid=(1,),
        in_specs=p_specs + [_full(q) for q in prevs] + [_full(w) for w in weights],
        out_specs=[out_spec] * 7,
        out_shape=[jax.ShapeDtypeStruct((m, d), F32)] * 7,
        compiler_params=_params(("arbitrary",)),
        name="rwkv_prep_sample",
    )(p, p, p, p, p, *prevs, *weights)


def _pair_stack(x, first_head):
    zero = jnp.zeros_like(x)
    return jnp.concatenate([jnp.where(first_head, x, zero), jnp.where(first_head, zero, x)], axis=0)


def _wkv_chunk(r, lw, k, v, a, b, s, tri, gram_mask, state_mask):
    c = WKV_CHUNK
    lane = lax.broadcasted_iota(jnp.int32, (c, LANES), 1)
    h0 = lane < HEAD
    h0x2 = jnp.concatenate([h0, h0], axis=1)

    cum = _dot_exact_lhs(tri, lw)
    e_in = jnp.exp(cum)
    e_out = jnp.exp(-cum)
    a_s = a * jnp.exp(cum - lw)
    r_s = r * e_in
    b_s = b * e_out
    k_s = k * e_out
    last = cum[c - 1:, :]
    e_end = jnp.exp(last - cum)
    b_e = b * e_end
    k_e = k * e_end

    gram = _dot_nt(jnp.concatenate([a_s, r_s], axis=0),
                   jnp.concatenate([_pair_stack(b_s, h0), _pair_stack(k_s, h0)], axis=0))
    gram = jnp.where(gram_mask, gram, 0.0)
    a_ab, a_ak = gram[:c, :LANES], gram[:c, LANES:]
    m_rb, m_rk = gram[c:, :LANES], gram[c:, LANES:]
    v_st = _pair_stack(v, h0)

    x = jnp.concatenate([a_s, _dot(a_ak, v_st)], axis=1)
    pw = a_ab
    n = 1
    while True:
        x = x + _dot(pw, _pair_stack(x, h0x2))
        n *= 2
        if n >= c:
            break
        pw = _dot(pw, _pair_stack(pw, h0))
    p1, u2 = x[:, :LANES], x[:, LANES:]

    qo = _dot(m_rb, _pair_stack(x, h0x2))
    q = r_s + qo[:, :LANES]
    o2 = qo[:, LANES:] + _dot(m_rk, v_st)
    ou = _dot_nt(jnp.concatenate([q, p1], axis=0), s)
    o = ou[:c] + o2
    u = ou[c:] + u2
    upd = _dot_tn(jnp.concatenate([u, v], axis=0), jnp.concatenate([b_e, k_e], axis=0))
    s_new = s * jnp.exp(last) + jnp.where(state_mask, upd, 0.0)
    return o, s_new


def _wkv_prompt_kernel(r_ref, lw_ref, k_ref, v_ref, a_ref, b_ref, o_ref, sf_ref, s_ref, *, n_chunks):
    c = WKV_CHUNK
    t = pl.program_id(2)

    @pl.when(t == 0)
    def _():
        s_ref[...] = jnp.zeros_like(s_ref)

    ri = lax.broadcasted_iota(jnp.int32, (c, c), 0)
    ci = lax.broadcasted_iota(jnp.int32, (c, c), 1)
    tri = (ci <= ri).astype(F32)
    gr = lax.broadcasted_iota(jnp.int32, (2 * c, 2 * LANES), 0)
    gc = lax.broadcasted_iota(jnp.int32, (2 * c, 2 * LANES), 1) % c
    gram_mask = gc <= jnp.where(gr < c, gr - 1, gr - c)
    sr = lax.broadcasted_iota(jnp.int32, (LANES, LANES), 0) // HEAD
    sc = lax.broadcasted_iota(jnp.int32, (LANES, LANES), 1) // HEAD
    state_mask = sr == sc

    s = s_ref[...]
    for j in range(n_chunks):
        sl = pl.ds(j * c, c)
        o, s = _wkv_chunk(r_ref[sl, :], lw_ref[sl, :], k_ref[sl, :], v_ref[sl, :], a_ref[sl, :],
                          b_ref[sl, :], s, tri, gram_mask, state_mask)
        o_ref[sl, :] = o
    s_ref[...] = s

    @pl.when(t == pl.num_programs(2) - 1)
    def _():
        sf_ref[0, 0] = s[:HEAD, :HEAD]
        sf_ref[0, 1] = s[HEAD:, HEAD:]


def _wkv_prompt(r, lw, k, v, a, b, batch, seq, tb):
    d = r.shape[1]
    n_heads = d // HEAD
    nt = seq // tb
    spec = pl.BlockSpec((tb, LANES), lambda bi, hp, t: (bi * nt + t, hp))
    return pl.pallas_call(
        functools.partial(_wkv_prompt_kernel, n_chunks=tb // WKV_CHUNK),
        grid=(batch, d // LANES, nt),
        in_specs=[spec] * 6,
        out_specs=[spec, pl.BlockSpec((1, 2, HEAD, HEAD), lambda bi, hp, t: (bi, hp, 0, 0))],
        out_shape=[jax.ShapeDtypeStruct((batch * seq, d), F32),
                   jax.ShapeDtypeStruct((batch, n_heads, HEAD, HEAD), F32)],
        scratch_shapes=[pltpu.VMEM((LANES, LANES), F32)],
        compiler_params=_params(("parallel", "parallel", "arbitrary")),
        name="wkv_prompt",
    )(r, lw, k, v, a, b)


def _wkv_step_kernel(r_ref, lw_ref, k_ref, v_ref, a_ref, b_ref, s_ref, o_ref, sn_ref, *, n_heads):
    eye = (lax.broadcasted_iota(jnp.int32, (HEAD, HEAD), 0)
           == lax.broadcasted_iota(jnp.int32, (HEAD, HEAD), 1)).astype(F32)
    for h in range(n_heads):
        sl = slice(h * HEAD, (h + 1) * HEAD)
        s = s_ref[0, h]
        r, w, k = r_ref[0, :, sl], jnp.exp(lw_ref[0, :, sl]), k_ref[0, :, sl]
        v, a, b = v_ref[0, :, sl], a_ref[0, :, sl], b_ref[0, :, sl]
        sa = jnp.sum(s * a, axis=1, keepdims=True)
        v_col = jnp.sum(eye * v, axis=1, keepdims=True)
        s = s * w + sa * b + v_col * k
        o_col = jnp.sum(s * r, axis=1, keepdims=True)
        o_ref[0, :, sl] = jnp.sum(eye * o_col, axis=0, keepdims=True)
        sn_ref[0, h] = s


def _wkv_step(r, lw, k, v, a, b, state):
    m, d = r.shape
    n_heads = d // HEAD
    vec = pl.BlockSpec((1, 1, d), lambda i: (i, 0, 0))
    st = pl.BlockSpec((1, n_heads, HEAD, HEAD), lambda i: (i, 0, 0, 0))
    o, s = pl.pallas_call(
        functools.partial(_wkv_step_kernel, n_heads=n_heads),
        grid=(m,),
        in_specs=[vec] * 6 + [st],
        out_specs=[vec, st],
        out_shape=[jax.ShapeDtypeStruct((m, 1, d), F32), jax.ShapeDtypeStruct(state.shape, F32)],
        compiler_params=_params(("parallel",)),
        name="wkv_step",
    )(*[t.reshape(m, 1, d) for t in (r, lw, k, v, a, b)], state)
    return o.reshape(m, d), s


def _post_kernel(o_ref, r_ref, k_ref, v_ref, g_ref, lg_ref, lb_ref, rk_ref, y_ref):
    ones = _head_ones()
    o = o_ref[...]
    mu = _head_sum(o, ones) * (1.0 / HEAD)
    oc = o - mu
    var = _head_sum(oc * oc, ones) * (1.0 / HEAD)
    y = oc * lax.rsqrt(var + GN_EPS) * lg_ref[...] + lb_ref[...]
    bonus = _head_sum(r_ref[...] * k_ref[...] * rk_ref[...], ones)
    y_ref[...] = ((y + bonus * v_ref[...]) * g_ref[...]).astype(y_ref.dtype)


def _rwkv_post(o, r, k, v, g, lnx_g, lnx_b, r_k, tm):
    m, d = o.shape
    spec = pl.BlockSpec((tm, d), lambda i: (i, 0))
    vec = pl.BlockSpec((1, d), lambda i: (0, 0))
    return pl.pallas_call(
        _post_kernel,
        grid=(m // tm,),
        in_specs=[spec] * 5 + [vec] * 3,
        out_specs=spec,
        out_shape=jax.ShapeDtypeStruct((m, d), BF16),
        compiler_params=_params(("parallel",)),
        name="rwkv_post",
    )(o, r, k, v, g, lnx_g.reshape(1, d), lnx_b.reshape(1, d), r_k.reshape(1, d))


def _gelu(x):
    return 0.5 * x * (1.0 + jnp.tanh(GELU_C * (x + 0.044715 * (x * x * x))))


def _layernorm(x, g, b):
    mu = jnp.mean(x, axis=-1, keepdims=True)
    xc = x - mu
    var = jnp.mean(xc * xc, axis=-1, keepdims=True)
    return xc * lax.rsqrt(var + LN_EPS) * g + b


def _sgu_prompt_kernel(pu_ref, pv_ref, ng_ref, nb_ref, w_ref, bias_ref, y_ref, *, n_chunks):
    c = SGU_CHUNK
    ri = lax.broadcasted_iota(jnp.int32, (c, c), 0)
    ci = lax.broadcasted_iota(jnp.int32, (c, c), 1)
    causal = ci <= ri
    n_groups = w_ref.shape[0]
    ws = [jnp.where(causal, w_ref[g], 0.0).astype(BF16) for g in range(n_groups)]
    for j in range(n_chunks):
        rows = pl.ds(j * c, c)
        u = _gelu(pu_ref[rows, :])
        vs = _layernorm(_gelu(pv_ref[rows, :]), ng_ref[...], nb_ref[...]).astype(BF16)
        for g in range(n_groups):
            cols = slice(g * c, (g + 1) * c)
            mix = jnp.dot(ws[g], vs[:, cols], preferred_element_type=F32) + bias_ref[:, cols]
            y_ref[rows, cols] = (u[:, cols] * mix).astype(y_ref.dtype)


def _sgu_prompt(p, lay, norm_g, norm_b, sgu_w, sgu_b, batch, seq, ts):
    d = lay.d_sgu
    nt = seq // ts
    bias = jnp.repeat(sgu_b.T, SGU_CHUNK, axis=1)
    return pl.pallas_call(
        functools.partial(_sgu_prompt_kernel, n_chunks=ts // SGU_CHUNK),
        grid=(batch * nt,),
        in_specs=[pl.BlockSpec((ts, d), lambda i: (i, lay.u0 // d)),
                  pl.BlockSpec((ts, d), lambda i: (i, lay.vs0 // d)),
                  pl.BlockSpec((1, d), lambda i: (0, 0)),
                  pl.BlockSpec((1, d), lambda i: (0, 0)),
                  _full(sgu_w), _full(bias)],
        out_specs=pl.BlockSpec((ts, d), lambda i: (i, 0)),
        out_shape=jax.ShapeDtypeStruct((batch * seq, d), BF16),
        compiler_params=_params(("parallel",)),
        name="sgu_prompt",
    )(p, p, norm_g[None, :], norm_b[None, :], sgu_w, bias)


def _sgu_sample_kernel(pu_ref, pv_ref, ng_ref, nb_ref, w_ref, bias_ref, y_ref, vs_ref):
    u = _gelu(pu_ref[...])
    vs = _layernorm(_gelu(pv_ref[...]), ng_ref[...], nb_ref[...])
    vs_ref[...] = vs
    y_ref[...] = (u * (w_ref[...] * vs + bias_ref[...])).astype(y_ref.dtype)


def _sgu_sample(p, lay, norm_g, norm_b, sgu_w, sgu_b):
    d = lay.d_sgu
    m = p.shape[0]
    w0 = jnp.repeat(sgu_w[:, 0, 0], SGU_CHUNK)[None, :]
    b0 = jnp.repeat(sgu_b[:, 0], SGU_CHUNK)[None, :]
    vec = pl.BlockSpec((1, d), lambda i: (0, 0))
    out = pl.BlockSpec((m, d), lambda i: (0, 0))
    return pl.pallas_call(
        _sgu_sample_kernel,
        grid=(1,),
        in_specs=[pl.BlockSpec((m, d), lambda i: (0, lay.u0 // d)),
                  pl.BlockSpec((m, d), lambda i: (0, lay.vs0 // d)), vec, vec, vec, vec],
        out_specs=[out, out],
        out_shape=[jax.ShapeDtypeStruct((m, d), BF16), jax.ShapeDtypeStruct((m, d), F32)],
        compiler_params=_params(("arbitrary",)),
        name="sgu_sample",
    )(p, p, norm_g[None, :], norm_b[None, :], w0, b0)


def _out_proj_kernel(x_ref, ya_ref, yb_ref, wa_ref, wb_ref, o_ref):
    o_ref[...] = (x_ref[...] + jnp.dot(ya_ref[...], wa_ref[...].astype(BF16), preferred_element_type=F32)
                  + jnp.dot(yb_ref[...], wb_ref[...].astype(BF16), preferred_element_type=F32))


def _out_proj(x, ya, yb, w, tm, tn):
    m, d = x.shape
    da = ya.shape[1]
    return pl.pallas_call(
        _out_proj_kernel,
        grid=(m // tm, d // tn),
        in_specs=[pl.BlockSpec((tm, tn), lambda i, j: (i, j)),
                  pl.BlockSpec((tm, da), lambda i, j: (i, 0)),
                  pl.BlockSpec((tm, da), lambda i, j: (i, 0)),
                  pl.BlockSpec((da, tn), lambda i, j: (0, j)),
                  pl.BlockSpec((da, tn), lambda i, j: (1, j))],
        out_specs=pl.BlockSpec((tm, tn), lambda i, j: (i, j)),
        out_shape=jax.ShapeDtypeStruct((m, d), F32),
        compiler_params=_params(("parallel", "arbitrary")),
        name="out_proj",
    )(x, ya, yb, w, w)


def _ffn_kernel(x_ref, g2_ref, wu_ref, wd_ref, gf_ref, o_ref, h_ref):
    f = pl.program_id(1)

    @pl.when(f == 0)
    def _():
        x = x_ref[...]
        h_ref[...] = _rms(x, g2_ref[...]).astype(BF16)
        o_ref[...] = x

    a = jnp.dot(h_ref[...], wu_ref[...].astype(BF16), preferred_element_type=F32)
    a = jnp.square(jnp.maximum(a, 0.0)).astype(BF16)
    o_ref[...] += jnp.dot(a, wd_ref[...].astype(BF16), preferred_element_type=F32)

    @pl.when(f == pl.num_programs(1) - 1)
    def _():
        o_ref[...] = _rms(o_ref[...], gf_ref[...])


def _ffn(x, g2, w_up, w_down, gf, tm, tf):
    m, d = x.shape
    dff = w_up.shape[1]
    return pl.pallas_call(
        _ffn_kernel,
        grid=(m // tm, dff // tf),
        in_specs=[pl.BlockSpec((tm, d), lambda i, f: (i, 0)),
                  pl.BlockSpec((1, d), lambda i, f: (0, 0)),
                  pl.BlockSpec((d, tf), lambda i, f: (0, f)),
                  pl.BlockSpec((tf, d), lambda i, f: (f, 0)),
                  pl.BlockSpec((1, d), lambda i, f: (0, 0))],
        out_specs=pl.BlockSpec((tm, d), lambda i, f: (i, 0)),
        out_shape=jax.ShapeDtypeStruct((m, d), F32),
        scratch_shapes=[pltpu.VMEM((tm, d), BF16)],
        compiler_params=_params(("parallel", "arbitrary")),
        name="ffn",
    )(x, g2[None, :], w_up, w_down, gf[None, :])


def _row_tile(m, cap):
    t = min(m, cap)
    assert m % t == 0
    return t


def kernel(x_prompt, x_sample, state_wkv, state_shift, norm1_g, w_in, mu_shift, w0, w_up, a0, a_up, g_up,
           k_k, k_a, r_k, lnx_g, lnx_b, sgu_norm_g, sgu_norm_b, sgu_w, sgu_b, w_out, norm2_g, w_ffn_up,
           w_ffn_down, norm_f_g):
    batch, seq, d_model = x_prompt.shape
    n_dec, dec_seq, _ = x_sample.shape
    depth = w_in.shape[0]
    assert depth == 1 and dec_seq == 1
    d_rwkv = w0.shape[1]
    d_sgu = sgu_norm_g.shape[1]
    lay = _Layout(d_rwkv, d_sgu, w_up.shape[1], a_up.shape[1], g_up.shape[1])
    tn_in = 512

    w_in_p = lay.relayout_cols(w_in[0], tn_in)
    prep_w = _prep_weights(lay, mu_shift[0], w0[0], w_up[0], a0[0], a_up[0], g_up[0], k_k[0], k_a[0])

    def tail(x, o, r, k, v, g, y_b):
        m = x.shape[0]
        tm = _row_tile(m, 1024)
        y_a = _rwkv_post(o, r, k, v, g, lnx_g[0], lnx_b[0], r_k[0], _row_tile(m, 512))
        x1 = _out_proj(x, y_a, y_b, w_out[0], tm, 512)
        return _ffn(x1, norm2_g[0], w_ffn_up[0], w_ffn_down[0], norm_f_g, tm, 256)

    xp = x_prompt.reshape(batch * seq, d_model)
    pp = _in_proj(xp, norm1_g, w_in_p, _row_tile(batch * seq, 1024), tn_in)
    r, lw, k, v, aa, bb, g = _prep_prompt(pp, lay, prep_w, batch, seq, 256)
    o, wkv_p = _wkv_prompt(r, lw, k, v, aa, bb, batch, seq, 256)
    yb = _sgu_prompt(pp, lay, sgu_norm_g[0], sgu_norm_b[0], sgu_w[0], sgu_b[0], batch, seq, 512)
    y_prompt = tail(xp, o, r, k, v, g, yb).reshape(batch, seq, d_model)
    shift_p = lay.shift_row(pp.reshape(batch, seq, -1)[:, -1])

    xs = x_sample.reshape(n_dec, d_model)
    ps = _in_proj(xs, norm1_g, w_in_p, n_dec, tn_in)
    r, lw, k, v, aa, bb, g = _prep_sample(ps, state_shift[0], lay, prep_w)
    o, wkv_s = _wkv_step(r, lw, k, v, aa, bb, state_wkv[0])
    yb, vs = _sgu_sample(ps, lay, sgu_norm_g[0], sgu_norm_b[0], sgu_w[0], sgu_b[0])
    y_sample = tail(xs, o, r, k, v, g, yb).reshape(n_dec, 1, d_model)
    shift_s = lay.shift_row(ps)

    return (y_prompt, y_sample, wkv_p[None], shift_p[None], wkv_s[None], shift_s[None],
            vs.reshape(1, n_dec, 1, d_sgu))
```

```python
import functools
import math

import jax
import jax.numpy as jnp
from jax import lax
from jax.experimental import pallas as pl
from jax.experimental.pallas import tpu as pltpu

F32 = jnp.float32
BF16 = jnp.bfloat16

HEAD = 64
LANES = 128
SGU_CHUNK = 128
WKV_CHUNK = 64
RMS_EPS = 1e-5
LN_EPS = 1e-5
GN_EPS = 64e-5
DECAY_SCALE = math.exp(-0.5)
GELU_C = math.sqrt(2.0 / math.pi)
VMEM_LIMIT = 56 * 1024 * 1024


def _params(sem):
    return pltpu.CompilerParams(dimension_semantics=sem, vmem_limit_bytes=VMEM_LIMIT)


def _dot(a, b):
    return jnp.dot(a.astype(BF16), b.astype(BF16), preferred_element_type=F32)


def _dot_nt(a, b):
    return lax.dot_general(a.astype(BF16), b.astype(BF16), (((1,), (1,)), ((), ())),
                           preferred_element_type=F32)


def _dot_tn(a, b):
    return lax.dot_general(a.astype(BF16), b.astype(BF16), (((0,), (0,)), ((), ())),
                           preferred_element_type=F32)


def _split3(x):
    hi = x.astype(BF16)
    r1 = x - hi.astype(F32)
    mid = r1.astype(BF16)
    lo = (r1 - mid.astype(F32)).astype(BF16)
    return hi, mid, lo


def _dot_exact_rhs(x, m):
    hi, mid, lo = _split3(x)
    mb = m.astype(BF16)
    return (jnp.dot(hi, mb, preferred_element_type=F32) + jnp.dot(mid, mb, preferred_element_type=F32)
            + jnp.dot(lo, mb, preferred_element_type=F32))


def _dot_exact_lhs(m, x):
    hi, mid, lo = _split3(x)
    mb = m.astype(BF16)
    return (jnp.dot(mb, hi, preferred_element_type=F32) + jnp.dot(mb, mid, preferred_element_type=F32)
            + jnp.dot(mb, lo, preferred_element_type=F32))


def _sigmoid(x):
    return 1.0 / (1.0 + jnp.exp(-x))


def _head_ones():
    r = lax.broadcasted_iota(jnp.int32, (LANES, LANES), 0) // HEAD
    c = lax.broadcasted_iota(jnp.int32, (LANES, LANES), 1) // HEAD
    return (r == c).astype(F32)


def _head_sum(x, ones):
    parts = [_dot_exact_rhs(x[:, s:s + LANES], ones) for s in range(0, x.shape[1], LANES)]
    return parts[0] if len(parts) == 1 else jnp.concatenate(parts, axis=1)


def _rms(x, g):
    return x * lax.rsqrt(jnp.mean(x * x, axis=-1, keepdims=True) + RMS_EPS) * g


def _in_proj_kernel(x_ref, g_ref, w_ref, o_ref, h_ref):
    @pl.when(pl.program_id(1) == 0)
    def _():
        h_ref[...] = _rms(x_ref[...], g_ref[...]).astype(BF16)

    o_ref[...] = jnp.dot(h_ref[...], w_ref[...].astype(BF16), preferred_element_type=F32)


def _in_proj(x, g, w, tm, tn):
    m, d = x.shape
    n = w.shape[1]
    return pl.pallas_call(
        _in_proj_kernel,
        grid=(m // tm, n // tn),
        in_specs=[pl.BlockSpec((tm, d), lambda i, j: (i, 0)),
                  pl.BlockSpec((1, d), lambda i, j: (0, 0)),
                  pl.BlockSpec((d, tn), lambda i, j: (0, j))],
        out_specs=pl.BlockSpec((tm, tn), lambda i, j: (i, j)),
        out_shape=jax.ShapeDtypeStruct((m, n), F32),
        scratch_shapes=[pltpu.VMEM((tm, d), BF16)],
        compiler_params=_params(("parallel", "arbitrary")),
        name="in_proj",
    )(x, g, w)


def _rwkv_mix(pr, pk, pv, pg, pw, qr, qk, qv, qg, qw, mu_r, mu_k, mu_v, mu_g, mu_w,
              w0, w_up, a0, a_up, g_up, k_k, k_a, outs):
    r_ref, lw_ref, k_ref, v_ref, aa_ref, bb_ref, g_ref = outs
    r = pr + (qr - pr) * mu_r
    k = pk + (qk - pk) * mu_k
    v = pv + (qv - pv) * mu_v
    gl = pg + (qg - pg) * mu_g
    wa = pw + (qw - pw) * mu_w
    lw = -DECAY_SCALE * _sigmoid(w0 + _dot(jnp.tanh(wa), w_up))
    a = _sigmoid(a0 + _dot(wa, a_up))
    gate = _dot(_sigmoid(gl), g_up)
    kk = k * k_k
    ss = _head_sum(kk * kk, _head_ones())
    kk = kk / jnp.maximum(jnp.sqrt(ss), 1e-12)
    r_ref[...] = r
    lw_ref[...] = lw
    k_ref[...] = k * (1.0 + (a - 1.0) * k_a)
    v_ref[...] = v
    aa_ref[...] = -kk
    bb_ref[...] = kk * a
    g_ref[...] = gate


def _shifted(p, carry_ref, first):
    @pl.when(first)
    def _():
        carry_ref[...] = jnp.zeros_like(carry_ref)

    rows = lax.broadcasted_iota(jnp.int32, p.shape, 0)
    q = jnp.where(rows == 0, carry_ref[...], pltpu.roll(p, 1, axis=0))
    carry_ref[...] = p[p.shape[0] - 1:, :]
    return q


def _prep_prompt_kernel(pr, pk, pv, pg, pw, mu_r, mu_k, mu_v, mu_g, mu_w, w0, w_up, a0, a_up, g_up,
                        k_k, k_a, *rest):
    outs, carries = rest[:7], rest[7:]
    first = pl.program_id(1) == 0
    ps = [ref[...] for ref in (pr, pk, pv, pg, pw)]
    qs = [_shifted(p, c, first) for p, c in zip(ps, carries)]
    _rwkv_mix(*ps, *qs, mu_r[...], mu_k[...], mu_v[...], mu_g[...], mu_w[...], w0[...], w_up[...],
              a0[...], a_up[...], g_up[...], k_k[...], k_a[...], outs)


def _prep_sample_kernel(pr, pk, pv, pg, pw, qr, qk, qv, qg, qw, mu_r, mu_k, mu_v, mu_g, mu_w, w0, w_up,
                        a0, a_up, g_up, k_k, k_a, *outs):
    _rwkv_mix(pr[...], pk[...], pv[...], pg[...], pw[...], qr[...], qk[...], qv[...], qg[...], qw[...],
              mu_r[...], mu_k[...], mu_v[...], mu_g[...], mu_w[...], w0[...], w_up[...], a0[...],
              a_up[...], g_up[...], k_k[...], k_a[...], outs)


class _Layout:
    def __init__(self, d_rwkv, d_sgu, lora_w, lora_a, lora_g):
        self.d_rwkv, self.d_sgu = d_rwkv, d_sgu
        self.lora_w, self.lora_a, self.lora_g = lora_w, lora_a, lora_g
        self.gl_w = -(-lora_g // LANES) * LANES
        self.wa_w = lora_w + lora_a
        assert self.wa_w == LANES and d_rwkv % self.gl_w == 0
        self.u0 = 0
        self.vs0 = d_sgu
        self.r0 = 2 * d_sgu
        self.k0 = self.r0 + d_rwkv
        self.v0 = self.k0 + d_rwkv
        self.gl0 = self.v0 + d_rwkv
        self.wa0 = self.gl0 + self.gl_w
        self.used = self.wa0 + self.wa_w
        self.d_shift = 3 * d_rwkv + lora_w + lora_a + lora_g

    def width(self, tn):
        return -(-self.used // tn) * tn

    def rw_pieces(self, a):
        d = self.d_rwkv
        wa = a[..., 3 * d:3 * d + self.wa_w]
        gl = a[..., 3 * d + self.wa_w:self.d_shift]
        pad = [(0, 0)] * (a.ndim - 1) + [(0, self.gl_w - self.lora_g)]
        return a[..., :d], a[..., d:2 * d], a[..., 2 * d:3 * d], jnp.pad(gl, pad), wa

    def relayout_cols(self, w, tn):
        sg = w[:, self.d_shift:]
        r, k, v, gl, wa = self.rw_pieces(w[:, :self.d_shift])
        out = jnp.concatenate([sg, r, k, v, gl, wa], axis=1)
        return jnp.pad(out, ((0, 0), (0, self.width(tn) - self.used)))

    def shift_row(self, p_rows):
        return jnp.concatenate([p_rows[:, self.r0:self.gl0], p_rows[:, self.wa0:self.wa0 + self.wa_w],
                                p_rows[:, self.gl0:self.gl0 + self.lora_g]], axis=1)


def _prep_weights(lay, mu, w0, w_up, a0, a_up, g_up, k_k, k_a):
    d = lay.d_rwkv
    mus = [m[None, :] for m in lay.rw_pieces(mu)]
    w_up_p = jnp.pad(w_up, ((0, lay.lora_a), (0, 0)))
    a_up_p = jnp.pad(a_up, ((lay.lora_w, 0), (0, 0)))
    g_up_p = jnp.pad(g_up, ((0, lay.gl_w - lay.lora_g), (0, 0)))
    return mus + [w0[None, :], w_up_p, a0[None, :], a_up_p, g_up_p, k_k.reshape(1, d), k_a.reshape(1, d)]


def _full(a):
    return pl.BlockSpec(a.shape, lambda *_: (0,) * a.ndim)


def _prep_prompt(p, lay, weights, batch, seq, tp):
    d = lay.d_rwkv
    nt = seq // tp
    row = lambda b, i: b * nt + i
    p_specs = [pl.BlockSpec((tp, d), lambda b, i: (row(b, i), lay.r0 // d)),
               pl.BlockSpec((tp, d), lambda b, i: (row(b, i), lay.k0 // d)),
               pl.BlockSpec((tp, d), lambda b, i: (row(b, i), lay.v0 // d)),
               pl.BlockSpec((tp, lay.gl_w), lambda b, i: (row(b, i), lay.gl0 // lay.gl_w)),
               pl.BlockSpec((tp, lay.wa_w), lambda b, i: (row(b, i), lay.wa0 // lay.wa_w))]
    out_spec = pl.BlockSpec((tp, d), lambda b, i: (row(b, i), 0))
    return pl.pallas_call(
        _prep_prompt_kernel,
        grid=(batch, nt),
        in_specs=p_specs + [_full(w) for w in weights],
        out_specs=[out_spec] * 7,
        out_shape=[jax.ShapeDtypeStruct((batch * seq, d), F32)] * 7,
        scratch_shapes=[pltpu.VMEM((1, d), F32)] * 3 + [pltpu.VMEM((1, lay.gl_w), F32),
                                                        pltpu.VMEM((1, lay.wa_w), F32)],
        compiler_params=_params(("parallel", "arbitrary")),
        name="rwkv_prep_prompt",
    )(p, p, p, p, p, *weights)


def _prep_sample(p, prev, lay, weights):
    d = lay.d_rwkv
    m = p.shape[0]
    p_specs = [pl.BlockSpec((m, d), lambda i: (0, lay.r0 // d)),
               pl.BlockSpec((m, d), lambda i: (0, lay.k0 // d)),
               pl.BlockSpec((m, d), lambda i: (0, lay.v0 // d)),
               pl.BlockSpec((m, lay.gl_w), lambda i: (0, lay.gl0 // lay.gl_w)),
               pl.BlockSpec((m, lay.wa_w), lambda i: (0, lay.wa0 // lay.wa_w))]
    prevs = list(lay.rw_pieces(prev))
    out_spec = pl.BlockSpec((m, d), lambda i: (0, 0))
    return pl.pallas_call(
        _prep_sample_kernel,
        grid=(1,),
        in_specs=p_specs + [_full(q) for q in prevs] + [_full(w) for w in weights],
        out_specs=[out_spec] * 7,
        out_shape=[jax.ShapeDtypeStruct((m, d), F32)] * 7,
        compiler_params=_params(("arbitrary",)),
        name="rwkv_prep_sample",
    )(p, p, p, p, p, *prevs, *weights)


def _pair_stack(x, first_head):
    zero = jnp.zeros_like(x)
    return jnp.concatenate([jnp.where(first_head, x, zero), jnp.where(first_head, zero, x)], axis=0)


def _wkv_block(r, lw, k, v, a, b, states, tri, gram_mask, state_mask):
    c = WKV_CHUNK
    n_chunks = r.shape[0] // c
    n_pairs = r.shape[1] // LANES
    inst = [(j, p) for j in range(n_chunks) for p in range(n_pairs)]
    lane = lax.broadcasted_iota(jnp.int32, (c, LANES), 1)
    h0 = lane < HEAD
    h0x2 = jnp.concatenate([h0, h0], axis=1)

    def cut(x):
        return [x[j * c:(j + 1) * c, p * LANES:(p + 1) * LANES] for j, p in inst]

    cum = _dot_exact_lhs(tri, lw)
    e_out = jnp.exp(-cum)
    cums = cut(cum)
    a_s = cut(a * jnp.exp(cum - lw))
    r_s = cut(r * jnp.exp(cum))
    b_s = cut(b * e_out)
    k_s = cut(k * e_out)
    bs, ks, vs = cut(b), cut(k), cut(v)
    lasts = [x[c - 1:, :] for x in cums]
    e_end = [jnp.exp(last - x) for last, x in zip(lasts, cums)]
    bk_e = [jnp.concatenate([bi * e, ki * e], axis=0) for bi, ki, e in zip(bs, ks, e_end)]
    decay = [jnp.exp(last) for last in lasts]

    grams = [jnp.where(gram_mask,
                       _dot_nt(jnp.concatenate([ai, ri], axis=0),
                               jnp.concatenate([_pair_stack(bi, h0), _pair_stack(ki, h0)], axis=0)), 0.0)
             for ai, ri, bi, ki in zip(a_s, r_s, b_s, k_s)]
    v_st = [_pair_stack(x, h0) for x in vs]
    xs = [jnp.concatenate([ai, _dot(g[:c, LANES:], vi)], axis=1) for ai, g, vi in zip(a_s, grams, v_st)]
    pws = [g[:c, :LANES] for g in grams]
    n = 1
    while True:
        xs = [x + _dot(pw, _pair_stack(x, h0x2)) for x, pw in zip(xs, pws)]
        n *= 2
        if n >= c:
            break
        pws = [_dot(pw, _pair_stack(pw, h0)) for pw in pws]
    qos = [_dot(g[c:, :LANES], _pair_stack(x, h0x2)) for g, x in zip(grams, xs)]
    qp = [jnp.concatenate([ri + qo[:, :LANES], x[:, :LANES]], axis=0) for ri, qo, x in zip(r_s, qos, xs)]
    o2s = [qo[:, LANES:] + _dot(g[c:, LANES:], vi) for qo, g, vi in zip(qos, grams, v_st)]

    outs = {}
    states = list(states)
    for j in range(n_chunks):
        idx = [j * n_pairs + p for p in range(n_pairs)]
        ous = [_dot_nt(qp[i], states[p]) for p, i in enumerate(idx)]
        upds = [_dot_tn(jnp.concatenate([ou[c:] + xs[i][:, LANES:], vs[i]], axis=0), bk_e[i])
                for ou, i in zip(ous, idx)]
        for p, i in enumerate(idx):
            outs[(j, p)] = ous[p][:c] + o2s[i]
            states[p] = states[p] * decay[i] + jnp.where(state_mask, upds[p], 0.0)
    return outs, states


def _wkv_prompt_kernel(r_ref, lw_ref, k_ref, v_ref, a_ref, b_ref, o_ref, sf_ref, s_ref):
    c = WKV_CHUNK
    tb = r_ref.shape[0]
    n_pairs = r_ref.shape[1] // LANES
    t = pl.program_id(2)

    @pl.when(t == 0)
    def _():
        s_ref[...] = jnp.zeros_like(s_ref)

    ri = lax.broadcasted_iota(jnp.int32, (tb, tb), 0)
    ci = lax.broadcasted_iota(jnp.int32, (tb, tb), 1)
    tri = ((ri // c == ci // c) & (ci <= ri)).astype(F32)
    gr = lax.broadcasted_iota(jnp.int32, (2 * c, 2 * LANES), 0)
    gc = lax.broadcasted_iota(jnp.int32, (2 * c, 2 * LANES), 1) % c
    gram_mask = gc <= jnp.where(gr < c, gr - 1, gr - c)
    sr = lax.broadcasted_iota(jnp.int32, (LANES, LANES), 0) // HEAD
    sc = lax.broadcasted_iota(jnp.int32, (LANES, LANES), 1) // HEAD
    state_mask = sr == sc

    outs, states = _wkv_block(r_ref[...], lw_ref[...], k_ref[...], v_ref[...], a_ref[...], b_ref[...],
                              [s_ref[p] for p in range(n_pairs)], tri, gram_mask, state_mask)
    for (j, p), o in outs.items():
        o_ref[j * c:(j + 1) * c, p * LANES:(p + 1) * LANES] = o
    for p, s in enumerate(states):
        s_ref[p] = s

    @pl.when(t == pl.num_programs(2) - 1)
    def _():
        for p, s in enumerate(states):
            sf_ref[0, 2 * p] = s[:HEAD, :HEAD]
            sf_ref[0, 2 * p + 1] = s[HEAD:, HEAD:]


def _wkv_prompt(r, lw, k, v, a, b, batch, seq, tb, n_pairs):
    d = r.shape[1]
    n_heads = d // HEAD
    nt = seq // tb
    width = n_pairs * LANES
    spec = pl.BlockSpec((tb, width), lambda bi, hp, t: (bi * nt + t, hp))
    return pl.pallas_call(
        _wkv_prompt_kernel,
        grid=(batch, d // width, nt),
        in_specs=[spec] * 6,
        out_specs=[spec, pl.BlockSpec((1, 2 * n_pairs, HEAD, HEAD), lambda bi, hp, t: (bi, hp, 0, 0))],
        out_shape=[jax.ShapeDtypeStruct((batch * seq, d), F32),
                   jax.ShapeDtypeStruct((batch, n_heads, HEAD, HEAD), F32)],
        scratch_shapes=[pltpu.VMEM((n_pairs, LANES, LANES), F32)],
        compiler_params=_params(("parallel", "parallel", "arbitrary")),
        name="wkv_prompt",
    )(r, lw, k, v, a, b)


def _split2(x):
    hi = x.astype(BF16)
    return jnp.concatenate([hi, (x - hi.astype(F32)).astype(BF16)], axis=1)


def _wkv_step_kernel(r_ref, lw_ref, k_ref, v_ref, a_ref, b_ref, s_ref, o_ref, sn_ref):
    nb, n_heads = r_ref.shape[0], r_ref.shape[1]
    half = HEAD // 2
    rows = n_heads * half
    q = lax.broadcasted_iota(jnp.int32, (rows, LANES), 0) % half
    l = lax.broadcasted_iota(jnp.int32, (rows, LANES), 1)
    pick = (l % HEAD == 2 * q + l // HEAD).astype(F32)
    ri = lax.broadcasted_iota(jnp.int32, (2 * LANES, LANES), 0) % LANES // HEAD
    ci = lax.broadcasted_iota(jnp.int32, (2 * LANES, LANES), 1) // HEAD
    ones2 = (ri == ci).astype(BF16)

    def half_sum(x):
        return jnp.dot(_split2(x), ones2, preferred_element_type=F32)

    for i in range(nb):
        wide = lambda ref: jnp.broadcast_to(ref[i][:, None, :], (n_heads, half, LANES)).reshape(rows, LANES)
        s = s_ref[i]
        sa = half_sum(s * wide(a_ref))
        vb = half_sum(pick * wide(v_ref))
        s = s * jnp.exp(wide(lw_ref)) + sa * wide(b_ref) + vb * wide(k_ref)
        sn_ref[i] = s
        o = half_sum(s * wide(r_ref)) * pick
        o = jnp.sum(o.reshape(n_heads, half, LANES), axis=1)
        o_ref[i] = o[:, :HEAD] + o[:, HEAD:]


def _wkv_step(r, lw, k, v, a, b, state, nb):
    m, d = r.shape
    n_heads = d // HEAD
    rows = n_heads * HEAD // 2

    def doubled(x):
        x = x.reshape(m, n_heads, HEAD)
        return jnp.concatenate([x, x], axis=-1)

    vec = pl.BlockSpec((nb, n_heads, LANES), lambda i: (i, 0, 0))
    st = pl.BlockSpec((nb, rows, LANES), lambda i: (i, 0, 0))
    o, s = pl.pallas_call(
        _wkv_step_kernel,
        grid=(m // nb,),
        in_specs=[vec] * 6 + [st],
        out_specs=[pl.BlockSpec((nb, n_heads, HEAD), lambda i: (i, 0, 0)), st],
        out_shape=[jax.ShapeDtypeStruct((m, n_heads, HEAD), F32),
                   jax.ShapeDtypeStruct((m, rows, LANES), F32)],
        compiler_params=_params(("parallel",)),
        name="wkv_step",
    )(*[doubled(t) for t in (r, lw, k, v, a, b)], state.reshape(m, rows, LANES))
    return o.reshape(m, d), s.reshape(state.shape)


def _post_kernel(o_ref, r_ref, k_ref, v_ref, g_ref, lg_ref, lb_ref, rk_ref, y_ref):
    ones = _head_ones()
    o = o_ref[...]
    mu = _head_sum(o, ones) * (1.0 / HEAD)
    oc = o - mu
    var = _head_sum(oc * oc, ones) * (1.0 / HEAD)
    y = oc * lax.rsqrt(var + GN_EPS) * lg_ref[...] + lb_ref[...]
    bonus = _head_sum(r_ref[...] * k_ref[...] * rk_ref[...], ones)
    y_ref[...] = ((y + bonus * v_ref[...]) * g_ref[...]).astype(y_ref.dtype)


def _rwkv_post(o, r, k, v, g, lnx_g, lnx_b, r_k, tm):
    m, d = o.shape
    spec = pl.BlockSpec((tm, d), lambda i: (i, 0))
    vec = pl.BlockSpec((1, d), lambda i: (0, 0))
    return pl.pallas_call(
        _post_kernel,
        grid=(m // tm,),
        in_specs=[spec] * 5 + [vec] * 3,
        out_specs=spec,
        out_shape=jax.ShapeDtypeStruct((m, d), BF16),
        compiler_params=_params(("parallel",)),
        name="rwkv_post",
    )(o, r, k, v, g, lnx_g.reshape(1, d), lnx_b.reshape(1, d), r_k.reshape(1, d))


def _gelu(x):
    return 0.5 * x * (1.0 + jnp.tanh(GELU_C * (x + 0.044715 * (x * x * x))))


def _layernorm(x, g, b):
    mu = jnp.mean(x, axis=-1, keepdims=True)
    xc = x - mu
    var = jnp.mean(xc * xc, axis=-1, keepdims=True)
    return xc * lax.rsqrt(var + LN_EPS) * g + b


def _sgu_prompt_kernel(pu_ref, pv_ref, ng_ref, nb_ref, w_ref, bias_ref, y_ref, *, n_chunks):
    c = SGU_CHUNK
    ri = lax.broadcasted_iota(jnp.int32, (c, c), 0)
    ci = lax.broadcasted_iota(jnp.int32, (c, c), 1)
    causal = ci <= ri
    n_groups = w_ref.shape[0]
    ws = [jnp.where(causal, w_ref[g], 0.0).astype(BF16) for g in range(n_groups)]
    for j in range(n_chunks):
        rows = pl.ds(j * c, c)
        u = _gelu(pu_ref[rows, :])
        vs = _layernorm(_gelu(pv_ref[rows, :]), ng_ref[...], nb_ref[...]).astype(BF16)
        for g in range(n_groups):
            cols = slice(g * c, (g + 1) * c)
            mix = jnp.dot(ws[g], vs[:, cols], preferred_element_type=F32) + bias_ref[:, cols]
            y_ref[rows, cols] = (u[:, cols] * mix).astype(y_ref.dtype)


def _sgu_prompt(p, lay, norm_g, norm_b, sgu_w, sgu_b, batch, seq, ts):
    d = lay.d_sgu
    nt = seq // ts
    bias = jnp.repeat(sgu_b.T, SGU_CHUNK, axis=1)
    return pl.pallas_call(
        functools.partial(_sgu_prompt_kernel, n_chunks=ts // SGU_CHUNK),
        grid=(batch * nt,),
        in_specs=[pl.BlockSpec((ts, d), lambda i: (i, lay.u0 // d)),
                  pl.BlockSpec((ts, d), lambda i: (i, lay.vs0 // d)),
                  pl.BlockSpec((1, d), lambda i: (0, 0)),
                  pl.BlockSpec((1, d), lambda i: (0, 0)),
                  _full(sgu_w), _full(bias)],
        out_specs=pl.BlockSpec((ts, d), lambda i: (i, 0)),
        out_shape=jax.ShapeDtypeStruct((batch * seq, d), BF16),
        compiler_params=_params(("parallel",)),
        name="sgu_prompt",
    )(p, p, norm_g[None, :], norm_b[None, :], sgu_w, bias)


def _sgu_sample_kernel(pu_ref, pv_ref, ng_ref, nb_ref, w_ref, bias_ref, y_ref, vs_ref):
    u = _gelu(pu_ref[...])
    vs = _layernorm(_gelu(pv_ref[...]), ng_ref[...], nb_ref[...])
    vs_ref[...] = vs
    y_ref[...] = (u * (w_ref[...] * vs + bias_ref[...])).astype(y_ref.dtype)


def _sgu_sample(p, lay, norm_g, norm_b, sgu_w, sgu_b):
    d = lay.d_sgu
    m = p.shape[0]
    w0 = jnp.repeat(sgu_w[:, 0, 0], SGU_CHUNK)[None, :]
    b0 = jnp.repeat(sgu_b[:, 0], SGU_CHUNK)[None, :]
    vec = pl.BlockSpec((1, d), lambda i: (0, 0))
    out = pl.BlockSpec((m, d), lambda i: (0, 0))
    return pl.pallas_call(
        _sgu_sample_kernel,
        grid=(1,),
        in_specs=[pl.BlockSpec((m, d), lambda i: (0, lay.u0 // d)),
                  pl.BlockSpec((m, d), lambda i: (0, lay.vs0 // d)), vec, vec, vec, vec],
        out_specs=[out, out],
        out_shape=[jax.ShapeDtypeStruct((m, d), BF16), jax.ShapeDtypeStruct((m, d), F32)],
        compiler_params=_params(("arbitrary",)),
        name="sgu_sample",
    )(p, p, norm_g[None, :], norm_b[None, :], w0, b0)


def _out_proj_kernel(x_ref, ya_ref, yb_ref, wa_ref, wb_ref, o_ref):
    o_ref[...] = (x_ref[...] + jnp.dot(ya_ref[...], wa_ref[...].astype(BF16), preferred_element_type=F32)
                  + jnp.dot(yb_ref[...], wb_ref[...].astype(BF16), preferred_element_type=F32))


def _out_proj(x, ya, yb, w, tm, tn):
    m, d = x.shape
    da = ya.shape[1]
    return pl.pallas_call(
        _out_proj_kernel,
        grid=(m // tm, d // tn),
        in_specs=[pl.BlockSpec((tm, tn), lambda i, j: (i, j)),
                  pl.BlockSpec((tm, da), lambda i, j: (i, 0)),
                  pl.BlockSpec((tm, da), lambda i, j: (i, 0)),
                  pl.BlockSpec((da, tn), lambda i, j: (0, j)),
                  pl.BlockSpec((da, tn), lambda i, j: (1, j))],
        out_specs=pl.BlockSpec((tm, tn), lambda i, j: (i, j)),
        out_shape=jax.ShapeDtypeStruct((m, d), F32),
        compiler_params=_params(("parallel", "arbitrary")),
        name="out_proj",
    )(x, ya, yb, w, w)


def _ffn_kernel(x_ref, g2_ref, wu_ref, wd_ref, gf_ref, o_ref, h_ref, a_ref):
    f = pl.program_id(1)

    @pl.when(f == 0)
    def _():
        x = x_ref[...]
        h_ref[...] = _rms(x, g2_ref[...]).astype(BF16)
        o_ref[...] = x
        a_ref[...] = jnp.zeros_like(a_ref)

    down = jnp.dot(a_ref[...], wd_ref[...].astype(BF16), preferred_element_type=F32)
    up = jnp.dot(h_ref[...], wu_ref[...].astype(BF16), preferred_element_type=F32)
    o_ref[...] += down
    a_ref[...] = jnp.square(jnp.maximum(up, 0.0)).astype(BF16)

    @pl.when(f == pl.num_programs(1) - 1)
    def _():
        o_ref[...] = _rms(o_ref[...], gf_ref[...])


def _ffn(x, g2, w_up, w_down, gf, tm, tf):
    m, d = x.shape
    nf = w_up.shape[1] // tf
    return pl.pallas_call(
        _ffn_kernel,
        grid=(m // tm, nf + 1),
        in_specs=[pl.BlockSpec((tm, d), lambda i, f: (i, 0)),
                  pl.BlockSpec((1, d), lambda i, f: (0, 0)),
                  pl.BlockSpec((d, tf), lambda i, f: (0, jnp.minimum(f, nf - 1))),
                  pl.BlockSpec((tf, d), lambda i, f: (jnp.maximum(f - 1, 0), 0)),
                  pl.BlockSpec((1, d), lambda i, f: (0, 0))],
        out_specs=pl.BlockSpec((tm, d), lambda i, f: (i, 0)),
        out_shape=jax.ShapeDtypeStruct((m, d), F32),
        scratch_shapes=[pltpu.VMEM((tm, d), BF16), pltpu.VMEM((tm, tf), BF16)],
        compiler_params=_params(("parallel", "arbitrary")),
        name="ffn",
    )(x, g2[None, :], w_up, w_down, gf[None, :])


def _row_tile(m, cap):
    t = min(m, cap)
    assert m % t == 0
    return t


def kernel(x_prompt, x_sample, state_wkv, state_shift, norm1_g, w_in, mu_shift, w0, w_up, a0, a_up, g_up,
           k_k, k_a, r_k, lnx_g, lnx_b, sgu_norm_g, sgu_norm_b, sgu_w, sgu_b, w_out, norm2_g, w_ffn_up,
           w_ffn_down, norm_f_g):
    batch, seq, d_model = x_prompt.shape
    n_dec, dec_seq, _ = x_sample.shape
    depth = w_in.shape[0]
    assert depth == 1 and dec_seq == 1
    d_rwkv = w0.shape[1]
    d_sgu = sgu_norm_g.shape[1]
    lay = _Layout(d_rwkv, d_sgu, w_up.shape[1], a_up.shape[1], g_up.shape[1])
    tn_in = 512

    w_in_p = lay.relayout_cols(w_in[0], tn_in)
    prep_w = _prep_weights(lay, mu_shift[0], w0[0], w_up[0], a0[0], a_up[0], g_up[0], k_k[0], k_a[0])

    def tail(x, o, r, k, v, g, y_b):
        m = x.shape[0]
        tm = _row_tile(m, 1024)
        y_a = _rwkv_post(o, r, k, v, g, lnx_g[0], lnx_b[0], r_k[0], _row_tile(m, 512))
        x1 = _out_proj(x, y_a, y_b, w_out[0], tm, 512)
        return _ffn(x1, norm2_g[0], w_ffn_up[0], w_ffn_down[0], norm_f_g, tm, 256)

    xp = x_prompt.reshape(batch * seq, d_model)
    pp = _in_proj(xp, norm1_g, w_in_p, _row_tile(batch * seq, 1024), tn_in)
    r, lw, k, v, aa, bb, g = _prep_prompt(pp, lay, prep_w, batch, seq, 256)
    o, wkv_p = _wkv_prompt(r, lw, k, v, aa, bb, batch, seq, 256, 8)
    yb = _sgu_prompt(pp, lay, sgu_norm_g[0], sgu_norm_b[0], sgu_w[0], sgu_b[0], batch, seq, 512)
    y_prompt = tail(xp, o, r, k, v, g, yb).reshape(batch, seq, d_model)
    shift_p = lay.shift_row(pp.reshape(batch, seq, -1)[:, -1])

    xs = x_sample.reshape(n_dec, d_model)
    ps = _in_proj(xs, norm1_g, w_in_p, n_dec, tn_in)
    r, lw, k, v, aa, bb, g = _prep_sample(ps, state_shift[0], lay, prep_w)
    o, wkv_s = _wkv_step(r, lw, k, v, aa, bb, state_wkv[0], _row_tile(n_dec, 4))
    yb, vs = _sgu_sample(ps, lay, sgu_norm_g[0], sgu_norm_b[0], sgu_w[0], sgu_b[0])
    y_sample = tail(xs, o, r, k, v, g, yb).reshape(n_dec, 1, d_model)
    shift_s = lay.shift_row(ps)

    return (y_prompt, y_sample, wkv_p[None], shift_p[None], wkv_s[None], shift_s[None],
            vs.reshape(1, n_dec, 1, d_sgu))
```

```python
import functools
import math

import jax
import jax.numpy as jnp
from jax import lax
from jax.experimental import pallas as pl
from jax.experimental.pallas import tpu as pltpu

F32 = jnp.float32
BF16 = jnp.bfloat16

HEAD = 64
LANES = 128
SGU_CHUNK = 128
WKV_CHUNK = 64
RMS_EPS = 1e-5
LN_EPS = 1e-5
GN_EPS = 64e-5
DECAY_SCALE = math.exp(-0.5)
GELU_C = math.sqrt(2.0 / math.pi)
VMEM_LIMIT = 56 * 1024 * 1024


def _params(sem):
    return pltpu.CompilerParams(dimension_semantics=sem, vmem_limit_bytes=VMEM_LIMIT)


def _dot(a, b):
    return jnp.dot(a.astype(BF16), b.astype(BF16), preferred_element_type=F32)


def _dot_nt(a, b):
    return lax.dot_general(a.astype(BF16), b.astype(BF16), (((1,), (1,)), ((), ())),
                           preferred_element_type=F32)


def _dot_tn(a, b):
    return lax.dot_general(a.astype(BF16), b.astype(BF16), (((0,), (0,)), ((), ())),
                           preferred_element_type=F32)


def _split3(x):
    hi = x.astype(BF16)
    r1 = x - hi.astype(F32)
    mid = r1.astype(BF16)
    lo = (r1 - mid.astype(F32)).astype(BF16)
    return hi, mid, lo


def _dot_exact_rhs(x, m):
    hi, mid, lo = _split3(x)
    mb = m.astype(BF16)
    return (jnp.dot(hi, mb, preferred_element_type=F32) + jnp.dot(mid, mb, preferred_element_type=F32)
            + jnp.dot(lo, mb, preferred_element_type=F32))


def _dot_exact_lhs(m, x):
    hi, mid, lo = _split3(x)
    mb = m.astype(BF16)
    return (jnp.dot(mb, hi, preferred_element_type=F32) + jnp.dot(mb, mid, preferred_element_type=F32)
            + jnp.dot(mb, lo, preferred_element_type=F32))


def _sigmoid(x):
    return 1.0 / (1.0 + jnp.exp(-x))


def _head_ones():
    r = lax.broadcasted_iota(jnp.int32, (LANES, LANES), 0) // HEAD
    c = lax.broadcasted_iota(jnp.int32, (LANES, LANES), 1) // HEAD
    return (r == c).astype(F32)


def _head_sum(x, ones):
    parts = [_dot_exact_rhs(x[:, s:s + LANES], ones) for s in range(0, x.shape[1], LANES)]
    return parts[0] if len(parts) == 1 else jnp.concatenate(parts, axis=1)


def _rms(x, g):
    return x * lax.rsqrt(jnp.mean(x * x, axis=-1, keepdims=True) + RMS_EPS) * g


def _in_proj_kernel(rows_ref, x_ref, g_ref, wt_ref, o_ref, h_ref):
    del rows_ref
    @pl.when(pl.program_id(1) == 0)
    def _():
        h_ref[...] = _rms(x_ref[...], g_ref[...]).astype(BF16)

    o_ref[...] = _dot_nt(h_ref[...], wt_ref[...])


def _in_proj(x, g, wt, src_rows, tm, tn):
    m, d = x.shape
    n_blocks = len(src_rows)
    return pl.pallas_call(
        _in_proj_kernel,
        grid_spec=pltpu.PrefetchScalarGridSpec(
            num_scalar_prefetch=1,
            grid=(m // tm, n_blocks),
            in_specs=[pl.BlockSpec((tm, d), lambda i, j, rows: (i, 0)),
                      pl.BlockSpec((1, d), lambda i, j, rows: (0, 0)),
                      pl.BlockSpec((pl.Element(tn), pl.Element(d)), lambda i, j, rows: (pl.multiple_of(rows[j], 8), 0))],
            out_specs=pl.BlockSpec((tm, tn), lambda i, j, rows: (i, j)),
            scratch_shapes=[pltpu.VMEM((tm, d), BF16)]),
        out_shape=jax.ShapeDtypeStruct((m, n_blocks * tn), F32),
        compiler_params=_params(("parallel", "arbitrary")),
        name="in_proj",
    )(jnp.asarray(src_rows, jnp.int32), x, g, wt)


def _rwkv_mix(pr, pk, pv, pl_, qr, qk, qv, ql, mu_r, mu_k, mu_v, mu_l,
              w0, w_up, a0, a_up, g_up, k_k, k_a, outs):
    r_ref, lw_ref, k_ref, v_ref, aa_ref, bb_ref, g_ref = outs
    n_wa = w_up.shape[0]
    n_gl = g_up.shape[0]
    r = pr + (qr - pr) * mu_r
    k = pk + (qk - pk) * mu_k
    v = pv + (qv - pv) * mu_v
    lo = pl_[:, :n_wa + n_gl]
    lo = lo + (ql[:, :n_wa + n_gl] - lo) * mu_l[:, :n_wa + n_gl]
    wa, gl = lo[:, :n_wa], lo[:, n_wa:]
    lw = -DECAY_SCALE * _sigmoid(w0 + _dot(jnp.tanh(wa), w_up))
    a = _sigmoid(a0 + _dot(wa, a_up))
    gate = _dot(_sigmoid(gl), g_up)
    kk = k * k_k
    ss = _head_sum(kk * kk, _head_ones())
    kk = kk / jnp.maximum(jnp.sqrt(ss), 1e-12)
    r_ref[...] = r
    lw_ref[...] = lw
    k_ref[...] = k * (1.0 + (a - 1.0) * k_a)
    v_ref[...] = v
    aa_ref[...] = -kk
    bb_ref[...] = kk * a
    g_ref[...] = gate


def _shifted(p, carry_ref, first):
    @pl.when(first)
    def _():
        carry_ref[...] = jnp.zeros_like(carry_ref)

    rows = lax.broadcasted_iota(jnp.int32, p.shape, 0)
    q = jnp.where(rows == 0, carry_ref[...], pltpu.roll(p, 1, axis=0))
    carry_ref[...] = p[p.shape[0] - 1:, :]
    return q


def _prep_prompt_kernel(pr, pk, pv, pl_, mu_r, mu_k, mu_v, mu_l, w0, w_up, a0, a_up, g_up, k_k, k_a, *rest):
    outs, carries = rest[:7], rest[7:]
    first = pl.program_id(1) == 0
    ps = [ref[...] for ref in (pr, pk, pv, pl_)]
    qs = [_shifted(p, c, first) for p, c in zip(ps, carries)]
    _rwkv_mix(*ps, *qs, mu_r[...], mu_k[...], mu_v[...], mu_l[...], w0[...], w_up[...],
              a0[...], a_up[...], g_up[...], k_k[...], k_a[...], outs)


def _prep_sample_kernel(pr, pk, pv, pl_, qr, qk, qv, ql, mu_r, mu_k, mu_v, mu_l, w0, w_up,
                        a0, a_up, g_up, k_k, k_a, *outs):
    _rwkv_mix(pr[...], pk[...], pv[...], pl_[...], qr[...], qk[...], qv[...], ql[...],
              mu_r[...], mu_k[...], mu_v[...], mu_l[...], w0[...], w_up[...], a0[...],
              a_up[...], g_up[...], k_k[...], k_a[...], outs)


class _Layout:
    def __init__(self, d_rwkv, d_sgu, lora_w, lora_a, lora_g, tn):
        self.d_rwkv, self.d_sgu, self.tn = d_rwkv, d_sgu, tn
        self.wa_w = lora_w + lora_a
        self.gl_w = -(-lora_g // LANES) * LANES
        self.d_shift = 3 * d_rwkv + self.wa_w + lora_g
        assert self.wa_w == LANES and self.wa_w + self.gl_w <= tn
        assert d_rwkv % tn == 0 and d_sgu % tn == 0
        self.u0 = 0
        self.vs0 = d_sgu
        self.r0 = 2 * d_sgu
        self.k0 = self.r0 + d_rwkv
        self.v0 = self.k0 + d_rwkv
        self.lo0 = self.v0 + d_rwkv
        self.width = self.lo0 + tn
        self.src_rows = (list(range(self.d_shift, self.d_shift + 2 * d_sgu, tn))
                         + list(range(0, 3 * d_rwkv + tn, tn)))

    def rw_pieces(self, a):
        d = self.d_rwkv
        pad = [(0, 0)] * (a.ndim - 1) + [(0, self.tn - (self.d_shift - 3 * d))]
        return a[..., :d], a[..., d:2 * d], a[..., 2 * d:3 * d], jnp.pad(a[..., 3 * d:], pad)

    def shift_row(self, p_rows):
        return p_rows[:, self.r0:self.r0 + self.d_shift]


def _prep_weights(lay, mu, w0, w_up, a0, a_up, g_up, k_k, k_a):
    d = lay.d_rwkv
    lora_w, lora_g = w_up.shape[0], g_up.shape[0]
    mus = [m[None, :] for m in lay.rw_pieces(mu)]
    w_up_p = jnp.pad(w_up, ((0, lay.wa_w - lora_w), (0, 0)))
    a_up_p = jnp.pad(a_up, ((lora_w, 0), (0, 0)))
    g_up_p = jnp.pad(g_up, ((0, lay.gl_w - lora_g), (0, 0)))
    return mus + [w0[None, :], w_up_p, a0[None, :], a_up_p, g_up_p, k_k.reshape(1, d), k_a.reshape(1, d)]


def _full(a):
    return pl.BlockSpec(a.shape, lambda *_: (0,) * a.ndim)


def _prep_prompt(p, lay, weights, batch, seq, tp):
    d = lay.d_rwkv
    nt = seq // tp
    row = lambda b, i: b * nt + i
    p_specs = [pl.BlockSpec((tp, d), lambda b, i: (row(b, i), lay.r0 // d)),
               pl.BlockSpec((tp, d), lambda b, i: (row(b, i), lay.k0 // d)),
               pl.BlockSpec((tp, d), lambda b, i: (row(b, i), lay.v0 // d)),
               pl.BlockSpec((tp, lay.tn), lambda b, i: (row(b, i), lay.lo0 // lay.tn))]
    out_spec = pl.BlockSpec((tp, d), lambda b, i: (row(b, i), 0))
    return pl.pallas_call(
        _prep_prompt_kernel,
        grid=(batch, nt),
        in_specs=p_specs + [_full(w) for w in weights],
        out_specs=[out_spec] * 7,
        out_shape=[jax.ShapeDtypeStruct((batch * seq, d), F32)] * 7,
        scratch_shapes=[pltpu.VMEM((1, d), F32)] * 3 + [pltpu.VMEM((1, lay.tn), F32)],
        compiler_params=_params(("parallel", "arbitrary")),
        name="rwkv_prep_prompt",
    )(p, p, p, p, *weights)


def _prep_sample(p, prev, lay, weights):
    d = lay.d_rwkv
    m = p.shape[0]
    p_specs = [pl.BlockSpec((m, d), lambda i: (0, lay.r0 // d)),
               pl.BlockSpec((m, d), lambda i: (0, lay.k0 // d)),
               pl.BlockSpec((m, d), lambda i: (0, lay.v0 // d)),
               pl.BlockSpec((m, lay.tn), lambda i: (0, lay.lo0 // lay.tn))]
    prevs = list(lay.rw_pieces(prev))
    out_spec = pl.BlockSpec((m, d), lambda i: (0, 0))
    return pl.pallas_call(
        _prep_sample_kernel,
        grid=(1,),
        in_specs=p_specs + [_full(q) for q in prevs] + [_full(w) for w in weights],
        out_specs=[out_spec] * 7,
        out_shape=[jax.ShapeDtypeStruct((m, d), F32)] * 7,
        compiler_params=_params(("arbitrary",)),
        name="rwkv_prep_sample",
    )(p, p, p, p, *prevs, *weights)


def _pair_stack(x, first_head):
    zero = jnp.zeros_like(x)
    return jnp.concatenate([jnp.where(first_head, x, zero), jnp.where(first_head, zero, x)], axis=0)


def _wkv_block(r, lw, k, v, a, b, states, tri, gram_mask, state_mask):
    c = WKV_CHUNK
    n_chunks = r.shape[0] // c
    n_pairs = r.shape[1] // LANES
    inst = [(j, p) for j in range(n_chunks) for p in range(n_pairs)]
    lane = lax.broadcasted_iota(jnp.int32, (c, LANES), 1)
    h0 = lane < HEAD
    h0x2 = jnp.concatenate([h0, h0], axis=1)

    def cut(x):
        return [x[j * c:(j + 1) * c, p * LANES:(p + 1) * LANES] for j, p in inst]

    cum = _dot_exact_lhs(tri, lw)
    e_out = jnp.exp(-cum)
    cums = cut(cum)
    a_s = cut(a * jnp.exp(cum - lw))
    r_s = cut(r * jnp.exp(cum))
    b_s = cut(b * e_out)
    k_s = cut(k * e_out)
    bs, ks, vs = cut(b), cut(k), cut(v)
    lasts = [x[c - 1:, :] for x in cums]
    e_end = [jnp.exp(last - x) for last, x in zip(lasts, cums)]
    bk_e = [jnp.concatenate([bi * e, ki * e], axis=0) for bi, ki, e in zip(bs, ks, e_end)]
    decay = [jnp.exp(last) for last in lasts]

    grams = [jnp.where(gram_mask,
                       _dot_nt(jnp.concatenate([ai, ri], axis=0),
                               jnp.concatenate([_pair_stack(bi, h0), _pair_stack(ki, h0)], axis=0)), 0.0)
             for ai, ri, bi, ki in zip(a_s, r_s, b_s, k_s)]
    v_st = [_pair_stack(x, h0) for x in vs]
    xs = [jnp.concatenate([ai, _dot(g[:c, LANES:], vi)], axis=1) for ai, g, vi in zip(a_s, grams, v_st)]
    pws = [g[:c, :LANES] for g in grams]
    n = 1
    while True:
        xs = [x + _dot(pw, _pair_stack(x, h0x2)) for x, pw in zip(xs, pws)]
        n *= 2
        if n >= c:
            break
        pws = [_dot(pw, _pair_stack(pw, h0)) for pw in pws]
    qos = [_dot(g[c:, :LANES], _pair_stack(x, h0x2)) for g, x in zip(grams, xs)]
    qp = [jnp.concatenate([ri + qo[:, :LANES], x[:, :LANES]], axis=0) for ri, qo, x in zip(r_s, qos, xs)]
    o2s = [qo[:, LANES:] + _dot(g[c:, LANES:], vi) for qo, g, vi in zip(qos, grams, v_st)]

    outs = {}
    states = list(states)
    for j in range(n_chunks):
        idx = [j * n_pairs + p for p in range(n_pairs)]
        ous = [_dot_nt(qp[i], states[p]) for p, i in enumerate(idx)]
        upds = [_dot_tn(jnp.concatenate([ou[c:] + xs[i][:, LANES:], vs[i]], axis=0), bk_e[i])
                for ou, i in zip(ous, idx)]
        for p, i in enumerate(idx):
            outs[(j, p)] = ous[p][:c] + o2s[i]
            states[p] = states[p] * decay[i] + jnp.where(state_mask, upds[p], 0.0)
    return outs, states


def _wkv_prompt_kernel(r_ref, lw_ref, k_ref, v_ref, a_ref, b_ref, o_ref, sf_ref, s_ref):
    c = WKV_CHUNK
    tb = r_ref.shape[0]
    n_pairs = r_ref.shape[1] // LANES
    t = pl.program_id(2)

    @pl.when(t == 0)
    def _():
        s_ref[...] = jnp.zeros_like(s_ref)

    ri = lax.broadcasted_iota(jnp.int32, (tb, tb), 0)
    ci = lax.broadcasted_iota(jnp.int32, (tb, tb), 1)
    tri = ((ri // c == ci // c) & (ci <= ri)).astype(F32)
    gr = lax.broadcasted_iota(jnp.int32, (2 * c, 2 * LANES), 0)
    gc = lax.broadcasted_iota(jnp.int32, (2 * c, 2 * LANES), 1) % c
    gram_mask = gc <= jnp.where(gr < c, gr - 1, gr - c)
    sr = lax.broadcasted_iota(jnp.int32, (LANES, LANES), 0) // HEAD
    sc = lax.broadcasted_iota(jnp.int32, (LANES, LANES), 1) // HEAD
    state_mask = sr == sc

    outs, states = _wkv_block(r_ref[...], lw_ref[...], k_ref[...], v_ref[...], a_ref[...], b_ref[...],
                              [s_ref[p] for p in range(n_pairs)], tri, gram_mask, state_mask)
    for (j, p), o in outs.items():
        o_ref[j * c:(j + 1) * c, p * LANES:(p + 1) * LANES] = o
    for p, s in enumerate(states):
        s_ref[p] = s

    @pl.when(t == pl.num_programs(2) - 1)
    def _():
        for p, s in enumerate(states):
            sf_ref[0, 2 * p] = s[:HEAD, :HEAD]
            sf_ref[0, 2 * p + 1] = s[HEAD:, HEAD:]


def _wkv_prompt(r, lw, k, v, a, b, batch, seq, tb, n_pairs):
    d = r.shape[1]
    n_heads = d // HEAD
    nt = seq // tb
    width = n_pairs * LANES
    spec = pl.BlockSpec((tb, width), lambda bi, hp, t: (bi * nt + t, hp))
    return pl.pallas_call(
        _wkv_prompt_kernel,
        grid=(batch, d // width, nt),
        in_specs=[spec] * 6,
        out_specs=[spec, pl.BlockSpec((1, 2 * n_pairs, HEAD, HEAD), lambda bi, hp, t: (bi, hp, 0, 0))],
        out_shape=[jax.ShapeDtypeStruct((batch * seq, d), F32),
                   jax.ShapeDtypeStruct((batch, n_heads, HEAD, HEAD), F32)],
        scratch_shapes=[pltpu.VMEM((n_pairs, LANES, LANES), F32)],
        compiler_params=_params(("parallel", "parallel", "arbitrary")),
        name="wkv_prompt",
    )(r, lw, k, v, a, b)


STEP_UNROLL = 4


def _wkv_step_kernel(r_ref, lw_ref, k_ref, v_ref, a_ref, b_ref, s_ref, o_ref, sn_ref):
    a, b, k, r = a_ref[...], b_ref[...], k_ref[...], r_ref[...]
    w = jnp.exp(lw_ref[...])

    def body(j, carry):
        for u in range(STEP_UNROLL):
            i = j * STEP_UNROLL + u
            s = s_ref[0, i]
            sa = jnp.sum(s * a, axis=0, keepdims=True)
            s = s * w + sa * b + v_ref[pl.ds(i, 1), :] * k
            sn_ref[0, i] = s
            o_ref[pl.ds(i, 1), :] = jnp.sum(s * r, axis=0, keepdims=True)
        return carry

    lax.fori_loop(0, s_ref.shape[1] // STEP_UNROLL, body, 0)


def _wkv_step(r, lw, k, v, a, b, state):
    m, d = r.shape
    n_heads = d // HEAD
    vec = pl.BlockSpec((HEAD, m), lambda h: (h, 0))
    st = pl.BlockSpec((1, HEAD, HEAD, m), lambda h: (h, 0, 0, 0))
    o, s = pl.pallas_call(
        _wkv_step_kernel,
        grid=(n_heads,),
        in_specs=[vec] * 6 + [st],
        out_specs=[vec, st],
        out_shape=[jax.ShapeDtypeStruct((d, m), F32), jax.ShapeDtypeStruct((n_heads, HEAD, HEAD, m), F32)],
        compiler_params=_params(("parallel",)),
        name="wkv_step",
    )(*[t.T for t in (r, lw, k, v, a, b)], jnp.transpose(state, (1, 2, 3, 0)))
    return o.T, jnp.transpose(s, (3, 0, 1, 2))


def _post_kernel(o_ref, r_ref, k_ref, v_ref, g_ref, lg_ref, lb_ref, rk_ref, y_ref):
    ones = _head_ones()
    o = o_ref[...]
    mu = _head_sum(o, ones) * (1.0 / HEAD)
    oc = o - mu
    var = _head_sum(oc * oc, ones) * (1.0 / HEAD)
    y = oc * lax.rsqrt(var + GN_EPS) * lg_ref[...] + lb_ref[...]
    bonus = _head_sum(r_ref[...] * k_ref[...] * rk_ref[...], ones)
    y_ref[...] = ((y + bonus * v_ref[...]) * g_ref[...]).astype(y_ref.dtype)


def _rwkv_post(o, r, k, v, g, lnx_g, lnx_b, r_k, tm):
    m, d = o.shape
    spec = pl.BlockSpec((tm, d), lambda i: (i, 0))
    vec = pl.BlockSpec((1, d), lambda i: (0, 0))
    return pl.pallas_call(
        _post_kernel,
        grid=(m // tm,),
        in_specs=[spec] * 5 + [vec] * 3,
        out_specs=spec,
        out_shape=jax.ShapeDtypeStruct((m, d), BF16),
        compiler_params=_params(("parallel",)),
        name="rwkv_post",
    )(o, r, k, v, g, lnx_g.reshape(1, d), lnx_b.reshape(1, d), r_k.reshape(1, d))


def _gelu(x):
    return 0.5 * x * (1.0 + jnp.tanh(GELU_C * (x + 0.044715 * (x * x * x))))


def _layernorm(x, g, b):
    mu = jnp.mean(x, axis=-1, keepdims=True)
    xc = x - mu
    var = jnp.mean(xc * xc, axis=-1, keepdims=True)
    return xc * lax.rsqrt(var + LN_EPS) * g + b


def _sgu_prompt_kernel(pu_ref, pv_ref, ng_ref, nb_ref, w_ref, bias_ref, y_ref, *, n_chunks):
    c = SGU_CHUNK
    ri = lax.broadcasted_iota(jnp.int32, (c, c), 0)
    ci = lax.broadcasted_iota(jnp.int32, (c, c), 1)
    causal = ci <= ri
    n_groups = w_ref.shape[0]
    ws = [jnp.where(causal, w_ref[g], 0.0).astype(BF16) for g in range(n_groups)]
    for j in range(n_chunks):
        rows = pl.ds(j * c, c)
        u = _gelu(pu_ref[rows, :])
        vs = _layernorm(_gelu(pv_ref[rows, :]), ng_ref[...], nb_ref[...]).astype(BF16)
        for g in range(n_groups):
            cols = slice(g * c, (g + 1) * c)
            mix = jnp.dot(ws[g], vs[:, cols], preferred_element_type=F32) + bias_ref[:, cols]
            y_ref[rows, cols] = (u[:, cols] * mix).astype(y_ref.dtype)


def _sgu_prompt(p, lay, norm_g, norm_b, sgu_w, sgu_b, batch, seq, ts):
    d = lay.d_sgu
    nt = seq // ts
    bias = jnp.repeat(sgu_b.T, SGU_CHUNK, axis=1)
    return pl.pallas_call(
        functools.partial(_sgu_prompt_kernel, n_chunks=ts // SGU_CHUNK),
        grid=(batch * nt,),
        in_specs=[pl.BlockSpec((ts, d), lambda i: (i, lay.u0 // d)),
                  pl.BlockSpec((ts, d), lambda i: (i, lay.vs0 // d)),
                  pl.BlockSpec((1, d), lambda i: (0, 0)),
                  pl.BlockSpec((1, d), lambda i: (0, 0)),
                  _full(sgu_w), _full(bias)],
        out_specs=pl.BlockSpec((ts, d), lambda i: (i, 0)),
        out_shape=jax.ShapeDtypeStruct((batch * seq, d), BF16),
        compiler_params=_params(("parallel",)),
        name="sgu_prompt",
    )(p, p, norm_g[None, :], norm_b[None, :], sgu_w, bias)


def _sgu_sample_kernel(pu_ref, pv_ref, ng_ref, nb_ref, w_ref, bias_ref, y_ref, vs_ref):
    u = _gelu(pu_ref[...])
    vs = _layernorm(_gelu(pv_ref[...]), ng_ref[...], nb_ref[...])
    vs_ref[...] = vs
    y_ref[...] = (u * (w_ref[...] * vs + bias_ref[...])).astype(y_ref.dtype)


def _sgu_sample(p, lay, norm_g, norm_b, sgu_w, sgu_b):
    d = lay.d_sgu
    m = p.shape[0]
    w0 = jnp.repeat(sgu_w[:, 0, 0], SGU_CHUNK)[None, :]
    b0 = jnp.repeat(sgu_b[:, 0], SGU_CHUNK)[None, :]
    vec = pl.BlockSpec((1, d), lambda i: (0, 0))
    out = pl.BlockSpec((m, d), lambda i: (0, 0))
    return pl.pallas_call(
        _sgu_sample_kernel,
        grid=(1,),
        in_specs=[pl.BlockSpec((m, d), lambda i: (0, lay.u0 // d)),
                  pl.BlockSpec((m, d), lambda i: (0, lay.vs0 // d)), vec, vec, vec, vec],
        out_specs=[out, out],
        out_shape=[jax.ShapeDtypeStruct((m, d), BF16), jax.ShapeDtypeStruct((m, d), F32)],
        compiler_params=_params(("arbitrary",)),
        name="sgu_sample",
    )(p, p, norm_g[None, :], norm_b[None, :], w0, b0)


def _out_proj_kernel(x_ref, ya_ref, yb_ref, wa_ref, wb_ref, o_ref):
    o_ref[...] = (x_ref[...] + jnp.dot(ya_ref[...], wa_ref[...].astype(BF16), preferred_element_type=F32)
                  + jnp.dot(yb_ref[...], wb_ref[...].astype(BF16), preferred_element_type=F32))


def _out_proj(x, ya, yb, w, tm, tn):
    m, d = x.shape
    da = ya.shape[1]
    return pl.pallas_call(
        _out_proj_kernel,
        grid=(m // tm, d // tn),
        in_specs=[pl.BlockSpec((tm, tn), lambda i, j: (i, j)),
                  pl.BlockSpec((tm, da), lambda i, j: (i, 0)),
                  pl.BlockSpec((tm, da), lambda i, j: (i, 0)),
                  pl.BlockSpec((da, tn), lambda i, j: (0, j)),
                  pl.BlockSpec((da, tn), lambda i, j: (1, j))],
        out_specs=pl.BlockSpec((tm, tn), lambda i, j: (i, j)),
        out_shape=jax.ShapeDtypeStruct((m, d), F32),
        compiler_params=_params(("parallel", "arbitrary")),
        name="out_proj",
    )(x, ya, yb, w, w)


def _ffn_kernel(x_ref, g2_ref, wu_ref, wd_ref, gf_ref, o_ref, h_ref, a_ref):
    f = pl.program_id(1)

    @pl.when(f == 0)
    def _():
        x = x_ref[...]
        h_ref[...] = _rms(x, g2_ref[...]).astype(BF16)
        o_ref[...] = x
        a_ref[...] = jnp.zeros_like(a_ref)

    down = jnp.dot(a_ref[...], wd_ref[...].astype(BF16), preferred_element_type=F32)
    up = jnp.dot(h_ref[...], wu_ref[...].astype(BF16), preferred_element_type=F32)
    o_ref[...] += down
    a_ref[...] = jnp.square(jnp.maximum(up, 0.0)).astype(BF16)

    @pl.when(f == pl.num_programs(1) - 1)
    def _():
        o_ref[...] = _rms(o_ref[...], gf_ref[...])


def _ffn(x, g2, w_up, w_down, gf, tm, tf):
    m, d = x.shape
    nf = w_up.shape[1] // tf
    return pl.pallas_call(
        _ffn_kernel,
        grid=(m // tm, nf + 1),
        in_specs=[pl.BlockSpec((tm, d), lambda i, f: (i, 0)),
                  pl.BlockSpec((1, d), lambda i, f: (0, 0)),
                  pl.BlockSpec((d, tf), lambda i, f: (0, jnp.minimum(f, nf - 1))),
                  pl.BlockSpec((tf, d), lambda i, f: (jnp.maximum(f - 1, 0), 0)),
                  pl.BlockSpec((1, d), lambda i, f: (0, 0))],
        out_specs=pl.BlockSpec((tm, d), lambda i, f: (i, 0)),
        out_shape=jax.ShapeDtypeStruct((m, d), F32),
        scratch_shapes=[pltpu.VMEM((tm, d), BF16), pltpu.VMEM((tm, tf), BF16)],
        compiler_params=_params(("parallel", "arbitrary")),
        name="ffn",
    )(x, g2[None, :], w_up, w_down, gf[None, :])


def _row_tile(m, cap):
    t = min(m, cap)
    assert m % t == 0
    return t


def kernel(x_prompt, x_sample, state_wkv, state_shift, norm1_g, w_in, mu_shift, w0, w_up, a0, a_up, g_up,
           k_k, k_a, r_k, lnx_g, lnx_b, sgu_norm_g, sgu_norm_b, sgu_w, sgu_b, w_out, norm2_g, w_ffn_up,
           w_ffn_down, norm_f_g):
    batch, seq, d_model = x_prompt.shape
    n_dec, dec_seq, _ = x_sample.shape
    depth = w_in.shape[0]
    assert depth == 1 and dec_seq == 1
    d_rwkv = w0.shape[1]
    d_sgu = sgu_norm_g.shape[1]
    tn_in = 512
    lay = _Layout(d_rwkv, d_sgu, w_up.shape[1], a_up.shape[1], g_up.shape[1], tn_in)
    w_in_t = w_in[0].T
    prep_w = _prep_weights(lay, mu_shift[0], w0[0], w_up[0], a0[0], a_up[0], g_up[0], k_k[0], k_a[0])

    def tail(x, o, r, k, v, g, y_b):
        m = x.shape[0]
        tm = _row_tile(m, 1024)
        y_a = _rwkv_post(o, r, k, v, g, lnx_g[0], lnx_b[0], r_k[0], _row_tile(m, 512))
        x1 = _out_proj(x, y_a, y_b, w_out[0], tm, 512)
        return _ffn(x1, norm2_g[0], w_ffn_up[0], w_ffn_down[0], norm_f_g, tm, 256)

    xp = x_prompt.reshape(batch * seq, d_model)
    pp = _in_proj(xp, norm1_g, w_in_t, lay.src_rows, _row_tile(batch * seq, 1024), tn_in)
    r, lw, k, v, aa, bb, g = _prep_prompt(pp, lay, prep_w, batch, seq, 256)
    o, wkv_p = _wkv_prompt(r, lw, k, v, aa, bb, batch, seq, 256, 8)
    yb = _sgu_prompt(pp, lay, sgu_norm_g[0], sgu_norm_b[0], sgu_w[0], sgu_b[0], batch, seq, 512)
    y_prompt = tail(xp, o, r, k, v, g, yb).reshape(batch, seq, d_model)
    shift_p = lay.shift_row(pp.reshape(batch, seq, -1)[:, -1])

    xs = x_sample.reshape(n_dec, d_model)
    ps = _in_proj(xs, norm1_g, w_in_t, lay.src_rows, n_dec, tn_in)
    r, lw, k, v, aa, bb, g = _prep_sample(ps, state_shift[0], lay, prep_w)
    o, wkv_s = _wkv_step(r, lw, k, v, aa, bb, state_wkv[0])
    yb, vs = _sgu_sample(ps, lay, sgu_norm_g[0], sgu_norm_b[0], sgu_w[0], sgu_b[0])
    y_sample = tail(xs, o, r, k, v, g, yb).reshape(n_dec, 1, d_model)
    shift_s = lay.shift_row(ps)

    return (y_prompt, y_sample, wkv_p[None], shift_p[None], wkv_s[None], shift_s[None],
            vs.reshape(1, n_dec, 1, d_sgu))
```

```python
import functools
import math

import jax
import jax.numpy as jnp
from jax import lax
from jax.experimental import pallas as pl
from jax.experimental.pallas import tpu as pltpu

F32 = jnp.float32
BF16 = jnp.bfloat16

HEAD = 64
LANES = 128
SGU_CHUNK = 128
WKV_CHUNK = 64
RMS_EPS = 1e-5
LN_EPS = 1e-5
GN_EPS = 64e-5
DECAY_SCALE = math.exp(-0.5)
GELU_C = math.sqrt(2.0 / math.pi)
VMEM_LIMIT = 56 * 1024 * 1024


def _params(sem):
    return pltpu.CompilerParams(dimension_semantics=sem, vmem_limit_bytes=VMEM_LIMIT)


def _dot(a, b):
    return jnp.dot(a.astype(BF16), b.astype(BF16), preferred_element_type=F32)


def _dot_nt(a, b):
    return lax.dot_general(a.astype(BF16), b.astype(BF16), (((1,), (1,)), ((), ())),
                           preferred_element_type=F32)


def _dot_tn(a, b):
    return lax.dot_general(a.astype(BF16), b.astype(BF16), (((0,), (0,)), ((), ())),
                           preferred_element_type=F32)


def _split3(x):
    hi = x.astype(BF16)
    r1 = x - hi.astype(F32)
    mid = r1.astype(BF16)
    lo = (r1 - mid.astype(F32)).astype(BF16)
    return hi, mid, lo


def _dot_exact_lhs(m, x):
    hi, mid, lo = _split3(x)
    mb = m.astype(BF16)
    return (jnp.dot(mb, hi, preferred_element_type=F32) + jnp.dot(mb, mid, preferred_element_type=F32)
            + jnp.dot(mb, lo, preferred_element_type=F32))


def _sigmoid(x):
    return 1.0 / (1.0 + jnp.exp(-x))


def _head_ones():
    r = lax.broadcasted_iota(jnp.int32, (2 * LANES, LANES), 0) % LANES // HEAD
    c = lax.broadcasted_iota(jnp.int32, (2 * LANES, LANES), 1) // HEAD
    return (r == c).astype(BF16)


def _head_sum(x, ones):
    parts = []
    for s in range(0, x.shape[1], LANES):
        xs = x[:, s:s + LANES]
        hi = xs.astype(BF16)
        lo = (xs - hi.astype(F32)).astype(BF16)
        parts.append(jnp.dot(jnp.concatenate([hi, lo], axis=1), ones, preferred_element_type=F32))
    return parts[0] if len(parts) == 1 else jnp.concatenate(parts, axis=1)


def _rms(x, g):
    return x * lax.rsqrt(jnp.mean(x * x, axis=-1, keepdims=True) + RMS_EPS) * g


def _in_proj_kernel(rows_ref, x_ref, g_ref, wt_ref, o_ref, h_ref):
    del rows_ref
    @pl.when(pl.program_id(1) == 0)
    def _():
        h_ref[...] = _rms(x_ref[...], g_ref[...]).astype(BF16)

    o_ref[...] = _dot_nt(h_ref[...], wt_ref[...])


def _in_proj(x, g, wt, src_rows, tm, tn):
    m, d = x.shape
    n_blocks = len(src_rows)
    return pl.pallas_call(
        _in_proj_kernel,
        grid_spec=pltpu.PrefetchScalarGridSpec(
            num_scalar_prefetch=1,
            grid=(m // tm, n_blocks),
            in_specs=[pl.BlockSpec((tm, d), lambda i, j, rows: (i, 0)),
                      pl.BlockSpec((1, d), lambda i, j, rows: (0, 0)),
                      pl.BlockSpec((pl.Element(tn), pl.Element(d)), lambda i, j, rows: (pl.multiple_of(rows[j], 8), 0))],
            out_specs=pl.BlockSpec((tm, tn), lambda i, j, rows: (i, j)),
            scratch_shapes=[pltpu.VMEM((tm, d), BF16)]),
        out_shape=jax.ShapeDtypeStruct((m, n_blocks * tn), F32),
        compiler_params=_params(("parallel", "arbitrary")),
        name="in_proj",
    )(jnp.asarray(src_rows, jnp.int32), x, g, wt)


def _rwkv_mix(pr, pk, pv, pl_, qr, qk, qv, ql, mu_r, mu_k, mu_v, mu_l,
              w0, w_up, a0, a_up, g_up, k_k, k_a):
    n_wa = w_up.shape[0]
    n_gl = g_up.shape[0]
    r = pr + (qr - pr) * mu_r
    k = pk + (qk - pk) * mu_k
    v = pv + (qv - pv) * mu_v
    lo = pl_[:, :n_wa + n_gl]
    lo = lo + (ql[:, :n_wa + n_gl] - lo) * mu_l[:, :n_wa + n_gl]
    wa, gl = lo[:, :n_wa], lo[:, n_wa:]
    lw = -DECAY_SCALE * _sigmoid(w0 + _dot(jnp.tanh(wa), w_up))
    a = _sigmoid(a0 + _dot(wa, a_up))
    gate = _dot(_sigmoid(gl), g_up)
    kk = k * k_k
    ss = _head_sum(kk * kk, _head_ones())
    kk = kk / jnp.maximum(jnp.sqrt(ss), 1e-12)
    return r, lw, k * (1.0 + (a - 1.0) * k_a), v, -kk, kk * a, gate


def _rwkv_out(o, r, k, v, gate, lnx_g, lnx_b, r_k):
    ones = _head_ones()
    mu = _head_sum(o, ones) * (1.0 / HEAD)
    oc = o - mu
    var = _head_sum(oc * oc, ones) * (1.0 / HEAD)
    y = oc * lax.rsqrt(var + GN_EPS) * lnx_g + lnx_b
    return (y + _head_sum(r * k * r_k, ones) * v) * gate


def _shifted(p, carry_ref, first):
    @pl.when(first)
    def _():
        carry_ref[...] = jnp.zeros_like(carry_ref)

    rows = lax.broadcasted_iota(jnp.int32, p.shape, 0)
    q = jnp.where(rows == 0, carry_ref[...], pltpu.roll(p, 1, axis=0))
    carry_ref[...] = p[p.shape[0] - 1:, :]
    return q


def _prep_sample_kernel(*refs):
    ins, outs = refs[:-7], refs[-7:]
    for ref, val in zip(outs, _rwkv_mix(*[ref[...] for ref in ins])):
        ref[...] = val


class _Layout:
    def __init__(self, d_rwkv, d_sgu, lora_w, lora_a, lora_g, tn):
        self.d_rwkv, self.d_sgu, self.tn = d_rwkv, d_sgu, tn
        self.wa_w = lora_w + lora_a
        self.gl_w = -(-lora_g // LANES) * LANES
        self.d_shift = 3 * d_rwkv + self.wa_w + lora_g
        assert self.wa_w == LANES and self.wa_w + self.gl_w <= tn
        assert d_rwkv % tn == 0 and d_sgu % tn == 0
        self.u0 = 0
        self.vs0 = d_sgu
        self.r0 = 2 * d_sgu
        self.k0 = self.r0 + d_rwkv
        self.v0 = self.k0 + d_rwkv
        self.lo0 = self.v0 + d_rwkv
        self.width = self.lo0 + tn
        self.src_rows = (list(range(self.d_shift, self.d_shift + 2 * d_sgu, tn))
                         + list(range(0, 3 * d_rwkv + tn, tn)))

    def rw_pieces(self, a):
        d = self.d_rwkv
        pad = [(0, 0)] * (a.ndim - 1) + [(0, self.tn - (self.d_shift - 3 * d))]
        return a[..., :d], a[..., d:2 * d], a[..., 2 * d:3 * d], jnp.pad(a[..., 3 * d:], pad)

    def shift_row(self, p_rows):
        return p_rows[:, self.r0:self.r0 + self.d_shift]


def _prep_weights(lay, mu, w0, w_up, a0, a_up, g_up, k_k, k_a):
    d = lay.d_rwkv
    lora_w, lora_g = w_up.shape[0], g_up.shape[0]
    mus = [m[None, :] for m in lay.rw_pieces(mu)]
    w_up_p = jnp.pad(w_up, ((0, lay.wa_w - lora_w), (0, 0)))
    a_up_p = jnp.pad(a_up, ((lora_w, 0), (0, 0)))
    g_up_p = jnp.pad(g_up, ((0, lay.gl_w - lora_g), (0, 0)))
    return mus + [w0[None, :], w_up_p, a0[None, :], a_up_p, g_up_p, k_k.reshape(1, d), k_a.reshape(1, d)]


def _full(a):
    return pl.BlockSpec(a.shape, lambda *_: (0,) * a.ndim)


def _prep_sample(p, prev, lay, weights):
    d = lay.d_rwkv
    m = p.shape[0]
    p_specs = [pl.BlockSpec((m, d), lambda i: (0, lay.r0 // d)),
               pl.BlockSpec((m, d), lambda i: (0, lay.k0 // d)),
               pl.BlockSpec((m, d), lambda i: (0, lay.v0 // d)),
               pl.BlockSpec((m, lay.tn), lambda i: (0, lay.lo0 // lay.tn))]
    prevs = list(lay.rw_pieces(prev))
    out_spec = pl.BlockSpec((m, d), lambda i: (0, 0))
    return pl.pallas_call(
        _prep_sample_kernel,
        grid=(1,),
        in_specs=p_specs + [_full(q) for q in prevs] + [_full(w) for w in weights],
        out_specs=[out_spec] * 7,
        out_shape=[jax.ShapeDtypeStruct((m, d), F32)] * 7,
        compiler_params=_params(("arbitrary",)),
        name="rwkv_prep_sample",
    )(p, p, p, p, *prevs, *weights)


def _pair_stack(x, first_head):
    zero = jnp.zeros_like(x)
    return jnp.concatenate([jnp.where(first_head, x, zero), jnp.where(first_head, zero, x)], axis=0)


def _wkv_block(r, lw, k, v, a, b, states, tri, gram_mask, state_mask):
    c = WKV_CHUNK
    n_chunks = r.shape[0] // c
    n_pairs = r.shape[1] // LANES
    inst = [(j, p) for j in range(n_chunks) for p in range(n_pairs)]
    lane = lax.broadcasted_iota(jnp.int32, (c, LANES), 1)
    h0 = lane < HEAD
    h0x2 = jnp.concatenate([h0, h0], axis=1)

    def cut(x):
        return [x[j * c:(j + 1) * c, p * LANES:(p + 1) * LANES] for j, p in inst]

    cum = _dot_exact_lhs(tri, lw)
    e_out = jnp.exp(-cum)
    cums = cut(cum)
    a_s = cut(a * jnp.exp(cum - lw))
    r_s = cut(r * jnp.exp(cum))
    b_s = cut(b * e_out)
    k_s = cut(k * e_out)
    bs, ks, vs = cut(b), cut(k), cut(v)
    lasts = [x[c - 1:, :] for x in cums]
    e_end = [jnp.exp(last - x) for last, x in zip(lasts, cums)]
    bk_e = [jnp.concatenate([bi * e, ki * e], axis=0) for bi, ki, e in zip(bs, ks, e_end)]
    decay = [jnp.exp(last) for last in lasts]

    grams = [jnp.where(gram_mask,
                       _dot_nt(jnp.concatenate([ai, ri], axis=0),
                               jnp.concatenate([_pair_stack(bi, h0), _pair_stack(ki, h0)], axis=0)), 0.0)
             for ai, ri, bi, ki in zip(a_s, r_s, b_s, k_s)]
    v_st = [_pair_stack(x, h0) for x in vs]
    xs = [jnp.concatenate([ai, _dot(g[:c, LANES:], vi)], axis=1) for ai, g, vi in zip(a_s, grams, v_st)]
    pws = [g[:c, :LANES] for g in grams]
    n = 1
    while True:
        xs = [x + _dot(pw, _pair_stack(x, h0x2)) for x, pw in zip(xs, pws)]
        n *= 2
        if n >= c:
            break
        pws = [_dot(pw, _pair_stack(pw, h0)) for pw in pws]
    qos = [_dot(g[c:, :LANES], _pair_stack(x, h0x2)) for g, x in zip(grams, xs)]
    qp = [jnp.concatenate([ri + qo[:, :LANES], x[:, :LANES]], axis=0) for ri, qo, x in zip(r_s, qos, xs)]
    o2s = [qo[:, LANES:] + _dot(g[c:, LANES:], vi) for qo, g, vi in zip(qos, grams, v_st)]

    outs = {}
    states = list(states)
    for j in range(n_chunks):
        idx = [j * n_pairs + p for p in range(n_pairs)]
        ous = [_dot_nt(qp[i], states[p]) for p, i in enumerate(idx)]
        upds = [_dot_tn(jnp.concatenate([ou[c:] + xs[i][:, LANES:], vs[i]], axis=0), bk_e[i])
                for ou, i in zip(ous, idx)]
        for p, i in enumerate(idx):
            outs[(j, p)] = ous[p][:c] + o2s[i]
            states[p] = states[p] * decay[i] + jnp.where(state_mask, upds[p], 0.0)
    return outs, states


def _rwkv_prompt_kernel(*refs):
    p_refs, mix_refs, out_refs = refs[:4], refs[4:15], refs[15:18]
    y_ref, sf_ref = refs[18:20]
    carries, s_ref, o_scr = refs[20:24], refs[24], refs[25]
    c = WKV_CHUNK
    tb = y_ref.shape[0]
    n_pairs = y_ref.shape[1] // LANES
    t = pl.program_id(1)
    first = t == 0

    @pl.when(first)
    def _():
        s_ref[...] = jnp.zeros_like(s_ref)

    ps = [ref[...] for ref in p_refs]
    qs = [_shifted(p, carry, first) for p, carry in zip(ps, carries)]
    r, lw, k, v, aa, bb, gate = _rwkv_mix(*ps, *qs, *[ref[...] for ref in mix_refs])

    ri = lax.broadcasted_iota(jnp.int32, (tb, tb), 0)
    ci = lax.broadcasted_iota(jnp.int32, (tb, tb), 1)
    tri = ((ri // c == ci // c) & (ci <= ri)).astype(F32)
    gr = lax.broadcasted_iota(jnp.int32, (2 * c, 2 * LANES), 0)
    gc = lax.broadcasted_iota(jnp.int32, (2 * c, 2 * LANES), 1) % c
    gram_mask = gc <= jnp.where(gr < c, gr - 1, gr - c)
    sr = lax.broadcasted_iota(jnp.int32, (LANES, LANES), 0) // HEAD
    sc = lax.broadcasted_iota(jnp.int32, (LANES, LANES), 1) // HEAD
    state_mask = sr == sc

    outs, states = _wkv_block(r, lw, k, v, aa, bb, [s_ref[p] for p in range(n_pairs)],
                              tri, gram_mask, state_mask)
    for (j, p), o in outs.items():
        o_scr[j * c:(j + 1) * c, p * LANES:(p + 1) * LANES] = o
    for p, s in enumerate(states):
        s_ref[p] = s
    y = _rwkv_out(o_scr[...], r, k, v, gate, *[ref[...] for ref in out_refs])
    y_ref[...] = y.astype(y_ref.dtype)

    @pl.when(t == pl.num_programs(1) - 1)
    def _():
        for p, s in enumerate(states):
            sf_ref[0, 2 * p] = s[:HEAD, :HEAD]
            sf_ref[0, 2 * p + 1] = s[HEAD:, HEAD:]


def _rwkv_prompt(p, lay, mix_weights, out_weights, batch, seq, tb):
    d = lay.d_rwkv
    n_heads = d // HEAD
    nt = seq // tb
    row = lambda b, i: b * nt + i
    p_specs = [pl.BlockSpec((tb, d), lambda b, i: (row(b, i), lay.r0 // d)),
               pl.BlockSpec((tb, d), lambda b, i: (row(b, i), lay.k0 // d)),
               pl.BlockSpec((tb, d), lambda b, i: (row(b, i), lay.v0 // d)),
               pl.BlockSpec((tb, lay.tn), lambda b, i: (row(b, i), lay.lo0 // lay.tn))]
    weights = list(mix_weights) + list(out_weights)
    return pl.pallas_call(
        _rwkv_prompt_kernel,
        grid=(batch, nt),
        in_specs=p_specs + [_full(w) for w in weights],
        out_specs=[pl.BlockSpec((tb, d), lambda b, i: (row(b, i), 0)),
                   pl.BlockSpec((1, n_heads, HEAD, HEAD), lambda b, i: (b, 0, 0, 0))],
        out_shape=[jax.ShapeDtypeStruct((batch * seq, d), BF16),
                   jax.ShapeDtypeStruct((batch, n_heads, HEAD, HEAD), F32)],
        scratch_shapes=([pltpu.VMEM((1, d), F32)] * 3
                        + [pltpu.VMEM((1, lay.tn), F32), pltpu.VMEM((d // LANES, LANES, LANES), F32),
                           pltpu.VMEM((tb, d), F32)]),
        compiler_params=_params(("parallel", "arbitrary")),
        name="rwkv_prompt",
    )(p, p, p, p, *weights)


STEP_UNROLL = 4


def _wkv_step_kernel(r_ref, lw_ref, k_ref, v_ref, a_ref, b_ref, s_ref, o_ref, sn_ref):
    a, b, k, r = a_ref[...], b_ref[...], k_ref[...], r_ref[...]
    w = jnp.exp(lw_ref[...])

    def body(j, carry):
        for u in range(STEP_UNROLL):
            i = j * STEP_UNROLL + u
            s = s_ref[0, i]
            sa = jnp.sum(s * a, axis=0, keepdims=True)
            s = s * w + sa * b + v_ref[pl.ds(i, 1), :] * k
            sn_ref[0, i] = s
            o_ref[pl.ds(i, 1), :] = jnp.sum(s * r, axis=0, keepdims=True)
        return carry

    lax.fori_loop(0, s_ref.shape[1] // STEP_UNROLL, body, 0)


def _wkv_step(r, lw, k, v, a, b, state):
    m, d = r.shape
    n_heads = d // HEAD
    vec = pl.BlockSpec((HEAD, m), lambda h: (h, 0))
    st = pl.BlockSpec((1, HEAD, HEAD, m), lambda h: (h, 0, 0, 0))
    o, s = pl.pallas_call(
        _wkv_step_kernel,
        grid=(n_heads,),
        in_specs=[vec] * 6 + [st],
        out_specs=[vec, st],
        out_shape=[jax.ShapeDtypeStruct((d, m), F32), jax.ShapeDtypeStruct((n_heads, HEAD, HEAD, m), F32)],
        compiler_params=_params(("parallel",)),
        name="wkv_step",
    )(*[t.T for t in (r, lw, k, v, a, b)], jnp.transpose(state, (1, 2, 3, 0)))
    return o.T, jnp.transpose(s, (3, 0, 1, 2))


def _post_kernel(*refs):
    y_ref = refs[-1]
    y_ref[...] = _rwkv_out(*[ref[...] for ref in refs[:-1]]).astype(y_ref.dtype)


def _rwkv_post(o, r, k, v, g, out_weights):
    m, d = o.shape
    spec = pl.BlockSpec((m, d), lambda i: (0, 0))
    return pl.pallas_call(
        _post_kernel,
        grid=(1,),
        in_specs=[spec] * 5 + [_full(w) for w in out_weights],
        out_specs=spec,
        out_shape=jax.ShapeDtypeStruct((m, d), BF16),
        compiler_params=_params(("arbitrary",)),
        name="rwkv_post",
    )(o, r, k, v, g, *out_weights)


def _gelu(x):
    return 0.5 * x * (1.0 + jnp.tanh(GELU_C * (x + 0.044715 * (x * x * x))))


def _layernorm(x, g, b):
    mu = jnp.mean(x, axis=-1, keepdims=True)
    xc = x - mu
    var = jnp.mean(xc * xc, axis=-1, keepdims=True)
    return xc * lax.rsqrt(var + LN_EPS) * g + b


def _sgu_prompt_kernel(pu_ref, pv_ref, ng_ref, nb_ref, w_ref, bias_ref, y_ref, *, n_chunks):
    c = SGU_CHUNK
    ri = lax.broadcasted_iota(jnp.int32, (c, c), 0)
    ci = lax.broadcasted_iota(jnp.int32, (c, c), 1)
    causal = ci <= ri
    n_groups = w_ref.shape[0]
    ws = [jnp.where(causal, w_ref[g], 0.0).astype(BF16) for g in range(n_groups)]
    for j in range(n_chunks):
        rows = pl.ds(j * c, c)
        u = _gelu(pu_ref[rows, :])
        vs = _layernorm(_gelu(pv_ref[rows, :]), ng_ref[...], nb_ref[...]).astype(BF16)
        for g in range(n_groups):
            cols = slice(g * c, (g + 1) * c)
            mix = jnp.dot(ws[g], vs[:, cols], preferred_element_type=F32) + bias_ref[:, cols]
            y_ref[rows, cols] = (u[:, cols] * mix).astype(y_ref.dtype)


def _sgu_prompt(p, lay, norm_g, norm_b, sgu_w, sgu_b, batch, seq, ts):
    d = lay.d_sgu
    nt = seq // ts
    bias = jnp.repeat(sgu_b.T, SGU_CHUNK, axis=1)
    return pl.pallas_call(
        functools.partial(_sgu_prompt_kernel, n_chunks=ts // SGU_CHUNK),
        grid=(batch * nt,),
        in_specs=[pl.BlockSpec((ts, d), lambda i: (i, lay.u0 // d)),
                  pl.BlockSpec((ts, d), lambda i: (i, lay.vs0 // d)),
                  pl.BlockSpec((1, d), lambda i: (0, 0)),
                  pl.BlockSpec((1, d), lambda i: (0, 0)),
                  _full(sgu_w), _full(bias)],
        out_specs=pl.BlockSpec((ts, d), lambda i: (i, 0)),
        out_shape=jax.ShapeDtypeStruct((batch * seq, d), BF16),
        compiler_params=_params(("parallel",)),
        name="sgu_prompt",
    )(p, p, norm_g[None, :], norm_b[None, :], sgu_w, bias)


def _sgu_sample_kernel(pu_ref, pv_ref, ng_ref, nb_ref, w_ref, bias_ref, y_ref, vs_ref):
    u = _gelu(pu_ref[...])
    vs = _layernorm(_gelu(pv_ref[...]), ng_ref[...], nb_ref[...])
    vs_ref[...] = vs
    y_ref[...] = (u * (w_ref[...] * vs + bias_ref[...])).astype(y_ref.dtype)


def _sgu_sample(p, lay, norm_g, norm_b, sgu_w, sgu_b):
    d = lay.d_sgu
    m = p.shape[0]
    w0 = jnp.repeat(sgu_w[:, 0, 0], SGU_CHUNK)[None, :]
    b0 = jnp.repeat(sgu_b[:, 0], SGU_CHUNK)[None, :]
    vec = pl.BlockSpec((1, d), lambda i: (0, 0))
    out = pl.BlockSpec((m, d), lambda i: (0, 0))
    return pl.pallas_call(
        _sgu_sample_kernel,
        grid=(1,),
        in_specs=[pl.BlockSpec((m, d), lambda i: (0, lay.u0 // d)),
                  pl.BlockSpec((m, d), lambda i: (0, lay.vs0 // d)), vec, vec, vec, vec],
        out_specs=[out, out],
        out_shape=[jax.ShapeDtypeStruct((m, d), BF16), jax.ShapeDtypeStruct((m, d), F32)],
        compiler_params=_params(("arbitrary",)),
        name="sgu_sample",
    )(p, p, norm_g[None, :], norm_b[None, :], w0, b0)


def _out_proj_kernel(x_ref, ya_ref, yb_ref, wa_ref, wb_ref, o_ref):
    o_ref[...] = (x_ref[...] + jnp.dot(ya_ref[...], wa_ref[...].astype(BF16), preferred_element_type=F32)
                  + jnp.dot(yb_ref[...], wb_ref[...].astype(BF16), preferred_element_type=F32))


def _out_proj(x, ya, yb, w, tm, tn):
    m, d = x.shape
    da = ya.shape[1]
    return pl.pallas_call(
        _out_proj_kernel,
        grid=(m // tm, d // tn),
        in_specs=[pl.BlockSpec((tm, tn), lambda i, j: (i, j)),
                  pl.BlockSpec((tm, da), lambda i, j: (i, 0)),
                  pl.BlockSpec((tm, da), lambda i, j: (i, 0)),
                  pl.BlockSpec((da, tn), lambda i, j: (0, j)),
                  pl.BlockSpec((da, tn), lambda i, j: (1, j))],
        out_specs=pl.BlockSpec((tm, tn), lambda i, j: (i, j)),
        out_shape=jax.ShapeDtypeStruct((m, d), F32),
        compiler_params=_params(("parallel", "arbitrary")),
        name="out_proj",
    )(x, ya, yb, w, w)


def _ffn_kernel(x_ref, g2_ref, wu_ref, wd_ref, gf_ref, o_ref, h_ref, a_ref):
    f = pl.program_id(1)

    @pl.when(f == 0)
    def _():
        x = x_ref[...]
        h_ref[...] = _rms(x, g2_ref[...]).astype(BF16)
        o_ref[...] = x
        a_ref[...] = jnp.zeros_like(a_ref)

    down = jnp.dot(a_ref[...], wd_ref[...].astype(BF16), preferred_element_type=F32)
    up = jnp.dot(h_ref[...], wu_ref[...].astype(BF16), preferred_element_type=F32)
    o_ref[...] += down
    a_ref[...] = jnp.square(jnp.maximum(up, 0.0)).astype(BF16)

    @pl.when(f == pl.num_programs(1) - 1)
    def _():
        o_ref[...] = _rms(o_ref[...], gf_ref[...])


def _ffn(x, g2, w_up, w_down, gf, tm, tf):
    m, d = x.shape
    nf = w_up.shape[1] // tf
    return pl.pallas_call(
        _ffn_kernel,
        grid=(m // tm, nf + 1),
        in_specs=[pl.BlockSpec((tm, d), lambda i, f: (i, 0)),
                  pl.BlockSpec((1, d), lambda i, f: (0, 0)),
                  pl.BlockSpec((d, tf), lambda i, f: (0, jnp.minimum(f, nf - 1))),
                  pl.BlockSpec((tf, d), lambda i, f: (jnp.maximum(f - 1, 0), 0)),
                  pl.BlockSpec((1, d), lambda i, f: (0, 0))],
        out_specs=pl.BlockSpec((tm, d), lambda i, f: (i, 0)),
        out_shape=jax.ShapeDtypeStruct((m, d), F32),
        scratch_shapes=[pltpu.VMEM((tm, d), BF16), pltpu.VMEM((tm, tf), BF16)],
        compiler_params=_params(("parallel", "arbitrary")),
        name="ffn",
    )(x, g2[None, :], w_up, w_down, gf[None, :])


def _row_tile(m, cap):
    t = min(m, cap)
    assert m % t == 0
    return t


def kernel(x_prompt, x_sample, state_wkv, state_shift, norm1_g, w_in, mu_shift, w0, w_up, a0, a_up, g_up,
           k_k, k_a, r_k, lnx_g, lnx_b, sgu_norm_g, sgu_norm_b, sgu_w, sgu_b, w_out, norm2_g, w_ffn_up,
           w_ffn_down, norm_f_g):
    batch, seq, d_model = x_prompt.shape
    n_dec, dec_seq, _ = x_sample.shape
    depth = w_in.shape[0]
    assert depth == 1 and dec_seq == 1
    d_rwkv = w0.shape[1]
    d_sgu = sgu_norm_g.shape[1]
    tn_in = 512
    lay = _Layout(d_rwkv, d_sgu, w_up.shape[1], a_up.shape[1], g_up.shape[1], tn_in)
    w_in_t = w_in[0].T
    prep_w = _prep_weights(lay, mu_shift[0], w0[0], w_up[0], a0[0], a_up[0], g_up[0], k_k[0], k_a[0])

    out_w = [w.reshape(1, d_rwkv) for w in (lnx_g[0], lnx_b[0], r_k[0])]

    def tail(x, y_a, y_b):
        tm = _row_tile(x.shape[0], 1024)
        x1 = _out_proj(x, y_a, y_b, w_out[0], tm, 512)
        return _ffn(x1, norm2_g[0], w_ffn_up[0], w_ffn_down[0], norm_f_g, tm, 256)

    xp = x_prompt.reshape(batch * seq, d_model)
    pp = _in_proj(xp, norm1_g, w_in_t, lay.src_rows, _row_tile(batch * seq, 1024), tn_in)
    ya, wkv_p = _rwkv_prompt(pp, lay, prep_w, out_w, batch, seq, 256)
    yb = _sgu_prompt(pp, lay, sgu_norm_g[0], sgu_norm_b[0], sgu_w[0], sgu_b[0], batch, seq, 512)
    y_prompt = tail(xp, ya, yb).reshape(batch, seq, d_model)
    shift_p = lay.shift_row(pp.reshape(batch, seq, -1)[:, -1])

    xs = x_sample.reshape(n_dec, d_model)
    ps = _in_proj(xs, norm1_g, w_in_t, lay.src_rows, n_dec, tn_in)
    r, lw, k, v, aa, bb, g = _prep_sample(ps, state_shift[0], lay, prep_w)
    o, wkv_s = _wkv_step(r, lw, k, v, aa, bb, state_wkv[0])
    yb, vs = _sgu_sample(ps, lay, sgu_norm_g[0], sgu_norm_b[0], sgu_w[0], sgu_b[0])
    ya = _rwkv_post(o, r, k, v, g, out_w)
    y_sample = tail(xs, ya, yb).reshape(n_dec, 1, d_model)
    shift_s = lay.shift_row(ps)

    return (y_prompt, y_sample, wkv_p[None], shift_p[None], wkv_s[None], shift_s[None],
            vs.reshape(1, n_dec, 1, d_sgu))
```

```python
import functools
import math

import jax
import jax.numpy as jnp
from jax import lax
from jax.experimental import pallas as pl
from jax.experimental.pallas import tpu as pltpu

F32 = jnp.float32
BF16 = jnp.bfloat16

HEAD = 64
LANES = 128
SGU_CHUNK = 128
WKV_CHUNK = 64
RMS_EPS = 1e-5
LN_EPS = 1e-5
GN_EPS = 64e-5
DECAY_SCALE = math.exp(-0.5)
GELU_C = math.sqrt(2.0 / math.pi)
VMEM_LIMIT = 58 * 1024 * 1024


def _params(sem):
    return pltpu.CompilerParams(dimension_semantics=sem, vmem_limit_bytes=VMEM_LIMIT)


def _dot(a, b):
    return jnp.dot(a.astype(BF16), b.astype(BF16), preferred_element_type=F32)


def _dot_nt(a, b):
    return lax.dot_general(a.astype(BF16), b.astype(BF16), (((1,), (1,)), ((), ())),
                           preferred_element_type=F32)


def _dot_tn(a, b):
    return lax.dot_general(a.astype(BF16), b.astype(BF16), (((0,), (0,)), ((), ())),
                           preferred_element_type=F32)


def _split3(x):
    hi = x.astype(BF16)
    r1 = x - hi.astype(F32)
    mid = r1.astype(BF16)
    lo = (r1 - mid.astype(F32)).astype(BF16)
    return hi, mid, lo


def _dot_exact_lhs(m, x):
    hi, mid, lo = _split3(x)
    mb = m.astype(BF16)
    return (jnp.dot(mb, hi, preferred_element_type=F32) + jnp.dot(mb, mid, preferred_element_type=F32)
            + jnp.dot(mb, lo, preferred_element_type=F32))


def _sigmoid(x):
    return 1.0 / (1.0 + jnp.exp(-x))


def _head_ones():
    r = lax.broadcasted_iota(jnp.int32, (2 * LANES, LANES), 0) % LANES // HEAD
    c = lax.broadcasted_iota(jnp.int32, (2 * LANES, LANES), 1) // HEAD
    return (r == c).astype(BF16)


def _head_sum(x, ones):
    parts = []
    for s in range(0, x.shape[1], LANES):
        xs = x[:, s:s + LANES]
        hi = xs.astype(BF16)
        lo = (xs - hi.astype(F32)).astype(BF16)
        parts.append(jnp.dot(jnp.concatenate([hi, lo], axis=1), ones, preferred_element_type=F32))
    return parts[0] if len(parts) == 1 else jnp.concatenate(parts, axis=1)


def _rms(x, g):
    return x * lax.rsqrt(jnp.mean(x * x, axis=-1, keepdims=True) + RMS_EPS) * g


def _in_proj_kernel(rows_ref, x_ref, g_ref, wt_ref, o_ref, h_ref):
    del rows_ref
    @pl.when(pl.program_id(1) == 0)
    def _():
        h_ref[...] = _rms(x_ref[...], g_ref[...]).astype(BF16)

    o_ref[...] = _dot_nt(h_ref[...], wt_ref[...])


def _in_proj(x, g, wt, src_rows, tm, tn):
    m, d = x.shape
    n_blocks = len(src_rows)
    return pl.pallas_call(
        _in_proj_kernel,
        grid_spec=pltpu.PrefetchScalarGridSpec(
            num_scalar_prefetch=1,
            grid=(m // tm, n_blocks),
            in_specs=[pl.BlockSpec((tm, d), lambda i, j, rows: (i, 0)),
                      pl.BlockSpec((1, d), lambda i, j, rows: (0, 0)),
                      pl.BlockSpec((pl.Element(tn), pl.Element(d)), lambda i, j, rows: (pl.multiple_of(rows[j], 8), 0))],
            out_specs=pl.BlockSpec((tm, tn), lambda i, j, rows: (i, j)),
            scratch_shapes=[pltpu.VMEM((tm, d), BF16)]),
        out_shape=jax.ShapeDtypeStruct((m, n_blocks * tn), F32),
        compiler_params=_params(("parallel", "arbitrary")),
        name="in_proj",
    )(jnp.asarray(src_rows, jnp.int32), x, g, wt)


def _rwkv_mix(pr, pk, pv, pl_, qr, qk, qv, ql, mu_r, mu_k, mu_v, mu_l,
              w0, w_up, a0, a_up, g_up, k_k, k_a):
    n_wa = w_up.shape[0]
    n_gl = g_up.shape[0]
    r = pr + (qr - pr) * mu_r
    k = pk + (qk - pk) * mu_k
    v = pv + (qv - pv) * mu_v
    lo = pl_[:, :n_wa + n_gl]
    lo = lo + (ql[:, :n_wa + n_gl] - lo) * mu_l[:, :n_wa + n_gl]
    wa, gl = lo[:, :n_wa], lo[:, n_wa:]
    lw = -DECAY_SCALE * _sigmoid(w0 + _dot(jnp.tanh(wa), w_up))
    a = _sigmoid(a0 + _dot(wa, a_up))
    gate = _dot(_sigmoid(gl), g_up)
    kk = k * k_k
    ss = _head_sum(kk * kk, _head_ones())
    kk = kk / jnp.maximum(jnp.sqrt(ss), 1e-12)
    return r, lw, k * (1.0 + (a - 1.0) * k_a), v, -kk, kk * a, gate


def _rwkv_out(o, r, k, v, gate, lnx_g, lnx_b, r_k):
    ones = _head_ones()
    mu = _head_sum(o, ones) * (1.0 / HEAD)
    oc = o - mu
    var = _head_sum(oc * oc, ones) * (1.0 / HEAD)
    y = oc * lax.rsqrt(var + GN_EPS) * lnx_g + lnx_b
    return (y + _head_sum(r * k * r_k, ones) * v) * gate


def _shifted(p, carry_ref, first):
    @pl.when(first)
    def _():
        carry_ref[...] = jnp.zeros_like(carry_ref)

    rows = lax.broadcasted_iota(jnp.int32, p.shape, 0)
    q = jnp.where(rows == 0, carry_ref[...], pltpu.roll(p, 1, axis=0))
    carry_ref[...] = p[p.shape[0] - 1:, :]
    return q


def _prep_sample_kernel(*refs):
    ins, outs = refs[:-7], refs[-7:]
    for ref, val in zip(outs, _rwkv_mix(*[ref[...] for ref in ins])):
        ref[...] = val


class _Layout:
    def __init__(self, d_rwkv, d_sgu, lora_w, lora_a, lora_g, tn):
        self.d_rwkv, self.d_sgu, self.tn = d_rwkv, d_sgu, tn
        self.wa_w = lora_w + lora_a
        self.gl_w = -(-lora_g // LANES) * LANES
        self.d_shift = 3 * d_rwkv + self.wa_w + lora_g
        assert self.wa_w == LANES and self.wa_w + self.gl_w <= tn
        assert d_rwkv % tn == 0 and d_sgu % tn == 0
        self.u0 = 0
        self.vs0 = d_sgu
        self.r0 = 2 * d_sgu
        self.k0 = self.r0 + d_rwkv
        self.v0 = self.k0 + d_rwkv
        self.lo0 = self.v0 + d_rwkv
        self.width = self.lo0 + tn
        self.src_rows = (list(range(self.d_shift, self.d_shift + 2 * d_sgu, tn))
                         + list(range(0, 3 * d_rwkv + tn, tn)))

    def rw_pieces(self, a):
        d = self.d_rwkv
        pad = [(0, 0)] * (a.ndim - 1) + [(0, self.tn - (self.d_shift - 3 * d))]
        return a[..., :d], a[..., d:2 * d], a[..., 2 * d:3 * d], jnp.pad(a[..., 3 * d:], pad)

    def shift_row(self, p_rows):
        return p_rows[:, self.r0:self.r0 + self.d_shift]


def _prep_weights(lay, mu, w0, w_up, a0, a_up, g_up, k_k, k_a):
    d = lay.d_rwkv
    lora_w, lora_g = w_up.shape[0], g_up.shape[0]
    mus = [m[None, :] for m in lay.rw_pieces(mu)]
    w_up_p = jnp.pad(w_up, ((0, lay.wa_w - lora_w), (0, 0)))
    a_up_p = jnp.pad(a_up, ((lora_w, 0), (0, 0)))
    g_up_p = jnp.pad(g_up, ((0, lay.gl_w - lora_g), (0, 0)))
    return mus + [w0[None, :], w_up_p, a0[None, :], a_up_p, g_up_p, k_k.reshape(1, d), k_a.reshape(1, d)]


def _full(a):
    return pl.BlockSpec(a.shape, lambda *_: (0,) * a.ndim)


def _prep_sample(p, prev, lay, weights):
    d = lay.d_rwkv
    m = p.shape[0]
    p_specs = [pl.BlockSpec((m, d), lambda i: (0, lay.r0 // d)),
               pl.BlockSpec((m, d), lambda i: (0, lay.k0 // d)),
               pl.BlockSpec((m, d), lambda i: (0, lay.v0 // d)),
               pl.BlockSpec((m, lay.tn), lambda i: (0, lay.lo0 // lay.tn))]
    prevs = list(lay.rw_pieces(prev))
    out_spec = pl.BlockSpec((m, d), lambda i: (0, 0))
    return pl.pallas_call(
        _prep_sample_kernel,
        grid=(1,),
        in_specs=p_specs + [_full(q) for q in prevs] + [_full(w) for w in weights],
        out_specs=[out_spec] * 7,
        out_shape=[jax.ShapeDtypeStruct((m, d), F32)] * 7,
        compiler_params=_params(("arbitrary",)),
        name="rwkv_prep_sample",
    )(p, p, p, p, *prevs, *weights)


def _pair_stack(x, first_head):
    zero = jnp.zeros_like(x)
    return jnp.concatenate([jnp.where(first_head, x, zero), jnp.where(first_head, zero, x)], axis=0)


def _wkv_block(r, lw, k, v, a, b, states, tri, gram_mask, state_mask):
    c = WKV_CHUNK
    n_chunks = r.shape[0] // c
    n_pairs = r.shape[1] // LANES
    inst = [(j, p) for j in range(n_chunks) for p in range(n_pairs)]
    lane = lax.broadcasted_iota(jnp.int32, (c, LANES), 1)
    h0 = lane < HEAD
    h0x2 = jnp.concatenate([h0, h0], axis=1)

    def cut(x):
        return [x[j * c:(j + 1) * c, p * LANES:(p + 1) * LANES] for j, p in inst]

    cum = _dot_exact_lhs(tri, lw)
    e_out = jnp.exp(-cum)
    cums = cut(cum)
    a_s = cut(a * jnp.exp(cum - lw))
    r_s = cut(r * jnp.exp(cum))
    b_s = cut(b * e_out)
    k_s = cut(k * e_out)
    bs, ks, vs = cut(b), cut(k), cut(v)
    lasts = [x[c - 1:, :] for x in cums]
    e_end = [jnp.exp(last - x) for last, x in zip(lasts, cums)]
    bk_e = [jnp.concatenate([bi * e, ki * e], axis=0) for bi, ki, e in zip(bs, ks, e_end)]
    decay = [jnp.exp(last) for last in lasts]

    grams = [jnp.where(gram_mask,
                       _dot_nt(jnp.concatenate([ai, ri], axis=0),
                               jnp.concatenate([_pair_stack(bi, h0), _pair_stack(ki, h0)], axis=0)), 0.0)
             for ai, ri, bi, ki in zip(a_s, r_s, b_s, k_s)]
    v_st = [_pair_stack(x, h0) for x in vs]
    xs = [jnp.concatenate([ai, _dot(g[:c, LANES:], vi)], axis=1) for ai, g, vi in zip(a_s, grams, v_st)]
    pws = [g[:c, :LANES] for g in grams]
    n = 1
    while True:
        xs = [x + _dot(pw, _pair_stack(x, h0x2)) for x, pw in zip(xs, pws)]
        n *= 2
        if n >= c:
            break
        pws = [_dot(pw, _pair_stack(pw, h0)) for pw in pws]
    qos = [_dot(g[c:, :LANES], _pair_stack(x, h0x2)) for g, x in zip(grams, xs)]
    qp = [jnp.concatenate([ri + qo[:, :LANES], x[:, :LANES]], axis=0) for ri, qo, x in zip(r_s, qos, xs)]
    o2s = [qo[:, LANES:] + _dot(g[c:, LANES:], vi) for qo, g, vi in zip(qos, grams, v_st)]

    outs = {}
    states = list(states)
    for j in range(n_chunks):
        idx = [j * n_pairs + p for p in range(n_pairs)]
        ous = [_dot_nt(qp[i], states[p]) for p, i in enumerate(idx)]
        upds = [_dot_tn(jnp.concatenate([ou[c:] + xs[i][:, LANES:], vs[i]], axis=0), bk_e[i])
                for ou, i in zip(ous, idx)]
        for p, i in enumerate(idx):
            outs[(j, p)] = ous[p][:c] + o2s[i]
            states[p] = states[p] * decay[i] + jnp.where(state_mask, upds[p], 0.0)
    return outs, states


def _rwkv_prompt_kernel(*refs):
    p_refs, mix_refs, out_refs = refs[:4], refs[4:15], refs[15:18]
    y_ref, sf_ref = refs[18:20]
    carries, s_ref, o_scr = refs[20:24], refs[24], refs[25]
    c = WKV_CHUNK
    tb = y_ref.shape[0]
    n_pairs = y_ref.shape[1] // LANES
    t = pl.program_id(1)
    first = t == 0

    @pl.when(first)
    def _():
        s_ref[...] = jnp.zeros_like(s_ref)

    ps = [ref[...] for ref in p_refs]
    qs = [_shifted(p, carry, first) for p, carry in zip(ps, carries)]
    r, lw, k, v, aa, bb, gate = _rwkv_mix(*ps, *qs, *[ref[...] for ref in mix_refs])

    ri = lax.broadcasted_iota(jnp.int32, (tb, tb), 0)
    ci = lax.broadcasted_iota(jnp.int32, (tb, tb), 1)
    tri = ((ri // c == ci // c) & (ci <= ri)).astype(F32)
    gr = lax.broadcasted_iota(jnp.int32, (2 * c, 2 * LANES), 0)
    gc = lax.broadcasted_iota(jnp.int32, (2 * c, 2 * LANES), 1) % c
    gram_mask = gc <= jnp.where(gr < c, gr - 1, gr - c)
    sr = lax.broadcasted_iota(jnp.int32, (LANES, LANES), 0) // HEAD
    sc = lax.broadcasted_iota(jnp.int32, (LANES, LANES), 1) // HEAD
    state_mask = sr == sc

    outs, states = _wkv_block(r, lw, k, v, aa, bb, [s_ref[p] for p in range(n_pairs)],
                              tri, gram_mask, state_mask)
    for (j, p), o in outs.items():
        o_scr[j * c:(j + 1) * c, p * LANES:(p + 1) * LANES] = o
    for p, s in enumerate(states):
        s_ref[p] = s
    y = _rwkv_out(o_scr[...], r, k, v, gate, *[ref[...] for ref in out_refs])
    y_ref[...] = y.astype(y_ref.dtype)

    @pl.when(t == pl.num_programs(1) - 1)
    def _():
        for p, s in enumerate(states):
            sf_ref[0, 2 * p] = s[:HEAD, :HEAD]
            sf_ref[0, 2 * p + 1] = s[HEAD:, HEAD:]


def _rwkv_prompt(p, lay, mix_weights, out_weights, batch, seq, tb):
    d = lay.d_rwkv
    n_heads = d // HEAD
    nt = seq // tb
    row = lambda b, i: b * nt + i
    p_specs = [pl.BlockSpec((tb, d), lambda b, i: (row(b, i), lay.r0 // d)),
               pl.BlockSpec((tb, d), lambda b, i: (row(b, i), lay.k0 // d)),
               pl.BlockSpec((tb, d), lambda b, i: (row(b, i), lay.v0 // d)),
               pl.BlockSpec((tb, lay.tn), lambda b, i: (row(b, i), lay.lo0 // lay.tn))]
    weights = list(mix_weights) + list(out_weights)
    return pl.pallas_call(
        _rwkv_prompt_kernel,
        grid=(batch, nt),
        in_specs=p_specs + [_full(w) for w in weights],
        out_specs=[pl.BlockSpec((tb, d), lambda b, i: (row(b, i), 0)),
                   pl.BlockSpec((1, n_heads, HEAD, HEAD), lambda b, i: (b, 0, 0, 0))],
        out_shape=[jax.ShapeDtypeStruct((batch * seq, d), BF16),
                   jax.ShapeDtypeStruct((batch, n_heads, HEAD, HEAD), F32)],
        scratch_shapes=([pltpu.VMEM((1, d), F32)] * 3
                        + [pltpu.VMEM((1, lay.tn), F32), pltpu.VMEM((d // LANES, LANES, LANES), F32),
                           pltpu.VMEM((tb, d), F32)]),
        compiler_params=_params(("parallel", "arbitrary")),
        name="rwkv_prompt",
    )(p, p, p, p, *weights)


STEP_UNROLL = 4


def _wkv_step_kernel(r_ref, lw_ref, k_ref, v_ref, a_ref, b_ref, s_ref, o_ref, sn_ref):
    a, b, k, r = a_ref[...], b_ref[...], k_ref[...], r_ref[...]
    w = jnp.exp(lw_ref[...])

    def body(j, carry):
        for u in range(STEP_UNROLL):
            i = j * STEP_UNROLL + u
            s = s_ref[0, i]
            sa = jnp.sum(s * a, axis=0, keepdims=True)
            s = s * w + sa * b + v_ref[pl.ds(i, 1), :] * k
            sn_ref[0, i] = s
            o_ref[pl.ds(i, 1), :] = jnp.sum(s * r, axis=0, keepdims=True)
        return carry

    lax.fori_loop(0, s_ref.shape[1] // STEP_UNROLL, body, 0)


def _wkv_step(r, lw, k, v, a, b, state):
    m, d = r.shape
    n_heads = d // HEAD
    vec = pl.BlockSpec((HEAD, m), lambda h: (h, 0))
    st = pl.BlockSpec((1, HEAD, HEAD, m), lambda h: (h, 0, 0, 0))
    o, s = pl.pallas_call(
        _wkv_step_kernel,
        grid=(n_heads,),
        in_specs=[vec] * 6 + [st],
        out_specs=[vec, st],
        out_shape=[jax.ShapeDtypeStruct((d, m), F32), jax.ShapeDtypeStruct((n_heads, HEAD, HEAD, m), F32)],
        compiler_params=_params(("parallel",)),
        name="wkv_step",
    )(*[t.T for t in (r, lw, k, v, a, b)], jnp.transpose(state, (1, 2, 3, 0)))
    return o.T, jnp.transpose(s, (3, 0, 1, 2))


def _post_kernel(*refs):
    y_ref = refs[-1]
    y_ref[...] = _rwkv_out(*[ref[...] for ref in refs[:-1]]).astype(y_ref.dtype)


def _rwkv_post(o, r, k, v, g, out_weights):
    m, d = o.shape
    spec = pl.BlockSpec((m, d), lambda i: (0, 0))
    return pl.pallas_call(
        _post_kernel,
        grid=(1,),
        in_specs=[spec] * 5 + [_full(w) for w in out_weights],
        out_specs=spec,
        out_shape=jax.ShapeDtypeStruct((m, d), BF16),
        compiler_params=_params(("arbitrary",)),
        name="rwkv_post",
    )(o, r, k, v, g, *out_weights)


def _gelu(x):
    return 0.5 * x * (1.0 + jnp.tanh(GELU_C * (x + 0.044715 * (x * x * x))))


def _layernorm(x, g, b):
    mu = jnp.mean(x, axis=-1, keepdims=True)
    xc = x - mu
    var = jnp.mean(xc * xc, axis=-1, keepdims=True)
    return xc * lax.rsqrt(var + LN_EPS) * g + b


def _sgu_prompt_kernel(pu_ref, pv_ref, ng_ref, nb_ref, w_ref, bias_ref, y_ref, *, n_chunks):
    c = SGU_CHUNK
    ri = lax.broadcasted_iota(jnp.int32, (c, c), 0)
    ci = lax.broadcasted_iota(jnp.int32, (c, c), 1)
    causal = ci <= ri
    n_groups = w_ref.shape[0]
    ws = [jnp.where(causal, w_ref[g], 0.0).astype(BF16) for g in range(n_groups)]
    for j in range(n_chunks):
        rows = pl.ds(j * c, c)
        u = _gelu(pu_ref[rows, :])
        vs = _layernorm(_gelu(pv_ref[rows, :]), ng_ref[...], nb_ref[...]).astype(BF16)
        for g in range(n_groups):
            cols = slice(g * c, (g + 1) * c)
            mix = jnp.dot(ws[g], vs[:, cols], preferred_element_type=F32) + bias_ref[:, cols]
            y_ref[rows, cols] = (u[:, cols] * mix).astype(y_ref.dtype)


def _sgu_prompt(p, lay, norm_g, norm_b, sgu_w, sgu_b, batch, seq, ts):
    d = lay.d_sgu
    nt = seq // ts
    bias = jnp.repeat(sgu_b.T, SGU_CHUNK, axis=1)
    return pl.pallas_call(
        functools.partial(_sgu_prompt_kernel, n_chunks=ts // SGU_CHUNK),
        grid=(batch * nt,),
        in_specs=[pl.BlockSpec((ts, d), lambda i: (i, lay.u0 // d)),
                  pl.BlockSpec((ts, d), lambda i: (i, lay.vs0 // d)),
                  pl.BlockSpec((1, d), lambda i: (0, 0)),
                  pl.BlockSpec((1, d), lambda i: (0, 0)),
                  _full(sgu_w), _full(bias)],
        out_specs=pl.BlockSpec((ts, d), lambda i: (i, 0)),
        out_shape=jax.ShapeDtypeStruct((batch * seq, d), BF16),
        compiler_params=_params(("parallel",)),
        name="sgu_prompt",
    )(p, p, norm_g[None, :], norm_b[None, :], sgu_w, bias)


def _sgu_sample_kernel(pu_ref, pv_ref, ng_ref, nb_ref, w_ref, bias_ref, y_ref, vs_ref):
    u = _gelu(pu_ref[...])
    vs = _layernorm(_gelu(pv_ref[...]), ng_ref[...], nb_ref[...])
    vs_ref[...] = vs
    y_ref[...] = (u * (w_ref[...] * vs + bias_ref[...])).astype(y_ref.dtype)


def _sgu_sample(p, lay, norm_g, norm_b, sgu_w, sgu_b):
    d = lay.d_sgu
    m = p.shape[0]
    w0 = jnp.repeat(sgu_w[:, 0, 0], SGU_CHUNK)[None, :]
    b0 = jnp.repeat(sgu_b[:, 0], SGU_CHUNK)[None, :]
    vec = pl.BlockSpec((1, d), lambda i: (0, 0))
    out = pl.BlockSpec((m, d), lambda i: (0, 0))
    return pl.pallas_call(
        _sgu_sample_kernel,
        grid=(1,),
        in_specs=[pl.BlockSpec((m, d), lambda i: (0, lay.u0 // d)),
                  pl.BlockSpec((m, d), lambda i: (0, lay.vs0 // d)), vec, vec, vec, vec],
        out_specs=[out, out],
        out_shape=[jax.ShapeDtypeStruct((m, d), BF16), jax.ShapeDtypeStruct((m, d), F32)],
        compiler_params=_params(("arbitrary",)),
        name="sgu_sample",
    )(p, p, norm_g[None, :], norm_b[None, :], w0, b0)


def _out_proj_kernel(x_ref, ya_ref, yb_ref, wa_ref, wb_ref, o_ref):
    o_ref[...] = (x_ref[...] + jnp.dot(ya_ref[...], wa_ref[...].astype(BF16), preferred_element_type=F32)
                  + jnp.dot(yb_ref[...], wb_ref[...].astype(BF16), preferred_element_type=F32))


def _out_proj(x, ya, yb, w, tm, tn):
    m, d = x.shape
    da = ya.shape[1]
    return pl.pallas_call(
        _out_proj_kernel,
        grid=(m // tm, d // tn),
        in_specs=[pl.BlockSpec((tm, tn), lambda i, j: (i, j)),
                  pl.BlockSpec((tm, da), lambda i, j: (i, 0)),
                  pl.BlockSpec((tm, da), lambda i, j: (i, 0)),
                  pl.BlockSpec((da, tn), lambda i, j: (0, j)),
                  pl.BlockSpec((da, tn), lambda i, j: (1, j))],
        out_specs=pl.BlockSpec((tm, tn), lambda i, j: (i, j)),
        out_shape=jax.ShapeDtypeStruct((m, d), F32),
        compiler_params=_params(("parallel", "arbitrary")),
        name="out_proj",
    )(x, ya, yb, w, w)


def _ffn_rows(f, x_ref, g2_ref, wu, wd, gf_ref, o_ref, h_ref, a_ref):
    @pl.when(f == 0)
    def _():
        x = x_ref[...]
        h_ref[...] = _rms(x, g2_ref[...]).astype(BF16)
        o_ref[...] = x
        a_ref[...] = jnp.zeros_like(a_ref)

    down = jnp.dot(a_ref[...], wd, preferred_element_type=F32)
    up = jnp.dot(h_ref[...], wu, preferred_element_type=F32)
    o_ref[...] += down
    a_ref[...] = jnp.square(jnp.maximum(up, 0.0)).astype(BF16)

    @pl.when(f == pl.num_programs(1) - 1)
    def _():
        o_ref[...] = _rms(o_ref[...], gf_ref[...])


def _ffn_kernel(x_ref, xs_ref, g2_ref, wu_ref, wd_ref, gf_ref, o_ref, os_ref, h_ref, a_ref, hs_ref, as_ref):
    f = pl.program_id(1)
    wu = wu_ref[...].astype(BF16)
    wd = wd_ref[...].astype(BF16)
    _ffn_rows(f, x_ref, g2_ref, wu, wd, gf_ref, o_ref, h_ref, a_ref)

    @pl.when(pl.program_id(0) == 0)
    def _():
        _ffn_rows(f, xs_ref, g2_ref, wu, wd, gf_ref, os_ref, hs_ref, as_ref)


def _ffn(x, xs, g2, w_up, w_down, gf, tm, tf):
    m, d = x.shape
    ms = xs.shape[0]
    nf = w_up.shape[1] // tf
    return pl.pallas_call(
        _ffn_kernel,
        grid=(m // tm, nf + 1),
        in_specs=[pl.BlockSpec((tm, d), lambda i, f: (i, 0)),
                  pl.BlockSpec((ms, d), lambda i, f: (0, 0)),
                  pl.BlockSpec((1, d), lambda i, f: (0, 0)),
                  pl.BlockSpec((d, tf), lambda i, f: (0, jnp.minimum(f, nf - 1))),
                  pl.BlockSpec((tf, d), lambda i, f: (jnp.maximum(f - 1, 0), 0)),
                  pl.BlockSpec((1, d), lambda i, f: (0, 0))],
        out_specs=[pl.BlockSpec((tm, d), lambda i, f: (i, 0)),
                   pl.BlockSpec((ms, d), lambda i, f: (0, 0))],
        out_shape=[jax.ShapeDtypeStruct((m, d), F32), jax.ShapeDtypeStruct((ms, d), F32)],
        scratch_shapes=[pltpu.VMEM((tm, d), BF16), pltpu.VMEM((tm, tf), BF16),
                        pltpu.VMEM((ms, d), BF16), pltpu.VMEM((ms, tf), BF16)],
        compiler_params=_params(("arbitrary", "arbitrary")),
        name="ffn",
    )(x, xs, g2[None, :], w_up, w_down, gf[None, :])


def _row_tile(m, cap):
    t = min(m, cap)
    assert m % t == 0
    return t


def kernel(x_prompt, x_sample, state_wkv, state_shift, norm1_g, w_in, mu_shift, w0, w_up, a0, a_up, g_up,
           k_k, k_a, r_k, lnx_g, lnx_b, sgu_norm_g, sgu_norm_b, sgu_w, sgu_b, w_out, norm2_g, w_ffn_up,
           w_ffn_down, norm_f_g):
    batch, seq, d_model = x_prompt.shape
    n_dec, dec_seq, _ = x_sample.shape
    depth = w_in.shape[0]
    assert depth == 1 and dec_seq == 1
    d_rwkv = w0.shape[1]
    d_sgu = sgu_norm_g.shape[1]
    tn_in = 512
    lay = _Layout(d_rwkv, d_sgu, w_up.shape[1], a_up.shape[1], g_up.shape[1], tn_in)
    w_in_t = w_in[0].T.astype(BF16)
    prep_w = _prep_weights(lay, mu_shift[0], w0[0], w_up[0], a0[0], a_up[0], g_up[0], k_k[0], k_a[0])

    out_w = [w.reshape(1, d_rwkv) for w in (lnx_g[0], lnx_b[0], r_k[0])]

    w_out_b = w_out[0].astype(BF16)

    xp = x_prompt.reshape(batch * seq, d_model)
    pp = _in_proj(xp, norm1_g, w_in_t, lay.src_rows, _row_tile(batch * seq, 2048), tn_in)
    ya, wkv_p = _rwkv_prompt(pp, lay, prep_w, out_w, batch, seq, 256)
    yb = _sgu_prompt(pp, lay, sgu_norm_g[0], sgu_norm_b[0], sgu_w[0], sgu_b[0], batch, seq, 512)
    x1p = _out_proj(xp, ya, yb, w_out_b, _row_tile(batch * seq, 2048), 512)
    shift_p = lay.shift_row(pp.reshape(batch, seq, -1)[:, -1])

    xs = x_sample.reshape(n_dec, d_model)
    ps = _in_proj(xs, norm1_g, w_in_t, lay.src_rows, n_dec, tn_in)
    r, lw, k, v, aa, bb, g = _prep_sample(ps, state_shift[0], lay, prep_w)
    o, wkv_s = _wkv_step(r, lw, k, v, aa, bb, state_wkv[0])
    yb, vs = _sgu_sample(ps, lay, sgu_norm_g[0], sgu_norm_b[0], sgu_w[0], sgu_b[0])
    ya = _rwkv_post(o, r, k, v, g, out_w)
    x1s = _out_proj(xs, ya, yb, w_out_b, n_dec, 512)
    shift_s = lay.shift_row(ps)

    y_prompt, y_sample = _ffn(x1p, x1s, norm2_g[0], w_ffn_up[0], w_ffn_down[0], norm_f_g,
                              _row_tile(batch * seq, 1024), 256)
    y_prompt = y_prompt.reshape(batch, seq, d_model)
    y_sample = y_sample.reshape(n_dec, 1, d_model)
    return (y_prompt, y_sample, wkv_p[None], shift_p[None], wkv_s[None], shift_s[None],
            vs.reshape(1, n_dec, 1, d_sgu))
```

```python
import functools
import math

import jax
import jax.numpy as jnp
from jax import lax
from jax.experimental import pallas as pl
from jax.experimental.pallas import tpu as pltpu

F32 = jnp.float32
BF16 = jnp.bfloat16

HEAD = 64
LANES = 128
SGU_CHUNK = 128
WKV_CHUNK = 64
RMS_EPS = 1e-5
LN_EPS = 1e-5
GN_EPS = 64e-5
DECAY_SCALE = math.exp(-0.5)
GELU_C = math.sqrt(2.0 / math.pi)
VMEM_LIMIT = 58 * 1024 * 1024


def _params(sem):
    return pltpu.CompilerParams(dimension_semantics=sem, vmem_limit_bytes=VMEM_LIMIT)


def _dot(a, b):
    return jnp.dot(a.astype(BF16), b.astype(BF16), preferred_element_type=F32)


def _dot_nt(a, b):
    return lax.dot_general(a.astype(BF16), b.astype(BF16), (((1,), (1,)), ((), ())),
                           preferred_element_type=F32)


def _dot_tn(a, b):
    return lax.dot_general(a.astype(BF16), b.astype(BF16), (((0,), (0,)), ((), ())),
                           preferred_element_type=F32)


def _split3(x):
    hi = x.astype(BF16)
    r1 = x - hi.astype(F32)
    mid = r1.astype(BF16)
    lo = (r1 - mid.astype(F32)).astype(BF16)
    return hi, mid, lo


def _dot_exact_lhs(m, x):
    hi, mid, lo = _split3(x)
    mb = m.astype(BF16)
    return (jnp.dot(mb, hi, preferred_element_type=F32) + jnp.dot(mb, mid, preferred_element_type=F32)
            + jnp.dot(mb, lo, preferred_element_type=F32))


def _sigmoid(x):
    return 1.0 / (1.0 + jnp.exp(-x))


def _head_ones():
    r = lax.broadcasted_iota(jnp.int32, (2 * LANES, LANES), 0) % LANES // HEAD
    c = lax.broadcasted_iota(jnp.int32, (2 * LANES, LANES), 1) // HEAD
    return (r == c).astype(BF16)


def _head_sum(x, ones):
    parts = []
    for s in range(0, x.shape[1], LANES):
        xs = x[:, s:s + LANES]
        hi = xs.astype(BF16)
        lo = (xs - hi.astype(F32)).astype(BF16)
        parts.append(jnp.dot(jnp.concatenate([hi, lo], axis=1), ones, preferred_element_type=F32))
    return parts[0] if len(parts) == 1 else jnp.concatenate(parts, axis=1)


def _rms(x, g):
    return x * lax.rsqrt(jnp.mean(x * x, axis=-1, keepdims=True) + RMS_EPS) * g


def _in_proj_kernel(rows_ref, x_ref, g_ref, wt_ref, o_ref, h_ref):
    del rows_ref
    @pl.when(pl.program_id(1) == 0)
    def _():
        h_ref[...] = _rms(x_ref[...], g_ref[...]).astype(BF16)

    o_ref[...] = _dot_nt(h_ref[...], wt_ref[...])


def _in_proj(x, g, wt, src_rows, tm, tn):
    m, d = x.shape
    n_blocks = len(src_rows)
    return pl.pallas_call(
        _in_proj_kernel,
        grid_spec=pltpu.PrefetchScalarGridSpec(
            num_scalar_prefetch=1,
            grid=(m // tm, n_blocks),
            in_specs=[pl.BlockSpec((tm, d), lambda i, j, rows: (i, 0)),
                      pl.BlockSpec((1, d), lambda i, j, rows: (0, 0)),
                      pl.BlockSpec((pl.Element(tn), pl.Element(d)), lambda i, j, rows: (pl.multiple_of(rows[j], 8), 0))],
            out_specs=pl.BlockSpec((tm, tn), lambda i, j, rows: (i, j)),
            scratch_shapes=[pltpu.VMEM((tm, d), BF16)]),
        out_shape=jax.ShapeDtypeStruct((m, n_blocks * tn), F32),
        compiler_params=_params(("parallel", "arbitrary")),
        name="in_proj",
    )(jnp.asarray(src_rows, jnp.int32), x, g, wt)


def _rwkv_mix(pr, pk, pv, pl_, qr, qk, qv, ql, mu_r, mu_k, mu_v, mu_l,
              w0, w_up, a0, a_up, g_up, k_k, k_a):
    n_wa = w_up.shape[0]
    n_gl = g_up.shape[0]
    r = pr + (qr - pr) * mu_r
    k = pk + (qk - pk) * mu_k
    v = pv + (qv - pv) * mu_v
    lo = pl_[:, :n_wa + n_gl]
    lo = lo + (ql[:, :n_wa + n_gl] - lo) * mu_l[:, :n_wa + n_gl]
    wa, gl = lo[:, :n_wa], lo[:, n_wa:]
    lw = -DECAY_SCALE * _sigmoid(w0 + _dot(jnp.tanh(wa), w_up))
    a = _sigmoid(a0 + _dot(wa, a_up))
    gate = _dot(_sigmoid(gl), g_up)
    kk = k * k_k
    ss = _head_sum(kk * kk, _head_ones())
    kk = kk / jnp.maximum(jnp.sqrt(ss), 1e-12)
    return r, lw, k * (1.0 + (a - 1.0) * k_a), v, -kk, kk * a, gate


def _rwkv_out(o, r, k, v, gate, lnx_g, lnx_b, r_k):
    ones = _head_ones()
    mu = _head_sum(o, ones) * (1.0 / HEAD)
    oc = o - mu
    var = _head_sum(oc * oc, ones) * (1.0 / HEAD)
    y = oc * lax.rsqrt(var + GN_EPS) * lnx_g + lnx_b
    return (y + _head_sum(r * k * r_k, ones) * v) * gate


def _shifted(p, carry_ref, first):
    @pl.when(first)
    def _():
        carry_ref[...] = jnp.zeros_like(carry_ref)

    rows = lax.broadcasted_iota(jnp.int32, p.shape, 0)
    q = jnp.where(rows == 0, carry_ref[...], pltpu.roll(p, 1, axis=0))
    carry_ref[...] = p[p.shape[0] - 1:, :]
    return q


def _prep_sample_kernel(*refs):
    ins, outs = refs[:-7], refs[-7:]
    for ref, val in zip(outs, _rwkv_mix(*[ref[...] for ref in ins])):
        ref[...] = val


class _Layout:
    def __init__(self, d_rwkv, d_sgu, lora_w, lora_a, lora_g, tn):
        self.d_rwkv, self.d_sgu, self.tn = d_rwkv, d_sgu, tn
        self.wa_w = lora_w + lora_a
        self.gl_w = -(-lora_g // LANES) * LANES
        self.d_shift = 3 * d_rwkv + self.wa_w + lora_g
        assert self.wa_w == LANES and self.wa_w + self.gl_w <= tn
        assert d_rwkv % tn == 0 and d_sgu % tn == 0
        self.u0 = 0
        self.vs0 = d_sgu
        self.r0 = 2 * d_sgu
        self.k0 = self.r0 + d_rwkv
        self.v0 = self.k0 + d_rwkv
        self.lo0 = self.v0 + d_rwkv
        self.width = self.lo0 + tn
        self.src_rows = (list(range(self.d_shift, self.d_shift + 2 * d_sgu, tn))
                         + list(range(0, 3 * d_rwkv + tn, tn)))

    def rw_pieces(self, a):
        d = self.d_rwkv
        pad = [(0, 0)] * (a.ndim - 1) + [(0, self.tn - (self.d_shift - 3 * d))]
        return a[..., :d], a[..., d:2 * d], a[..., 2 * d:3 * d], jnp.pad(a[..., 3 * d:], pad)

    def shift_row(self, p_rows):
        return p_rows[:, self.r0:self.r0 + self.d_shift]


def _prep_weights(lay, mu, w0, w_up, a0, a_up, g_up, k_k, k_a):
    d = lay.d_rwkv
    lora_w, lora_g = w_up.shape[0], g_up.shape[0]
    mus = [m[None, :] for m in lay.rw_pieces(mu)]
    w_up_p = jnp.pad(w_up, ((0, lay.wa_w - lora_w), (0, 0)))
    a_up_p = jnp.pad(a_up, ((lora_w, 0), (0, 0)))
    g_up_p = jnp.pad(g_up, ((0, lay.gl_w - lora_g), (0, 0)))
    return mus + [w0[None, :], w_up_p, a0[None, :], a_up_p, g_up_p, k_k.reshape(1, d), k_a.reshape(1, d)]


def _full(a):
    return pl.BlockSpec(a.shape, lambda *_: (0,) * a.ndim)


def _prep_sample(p, prev, lay, weights):
    d = lay.d_rwkv
    m = p.shape[0]
    p_specs = [pl.BlockSpec((m, d), lambda i: (0, lay.r0 // d)),
               pl.BlockSpec((m, d), lambda i: (0, lay.k0 // d)),
               pl.BlockSpec((m, d), lambda i: (0, lay.v0 // d)),
               pl.BlockSpec((m, lay.tn), lambda i: (0, lay.lo0 // lay.tn))]
    prevs = list(lay.rw_pieces(prev))
    out_spec = pl.BlockSpec((m, d), lambda i: (0, 0))
    return pl.pallas_call(
        _prep_sample_kernel,
        grid=(1,),
        in_specs=p_specs + [_full(q) for q in prevs] + [_full(w) for w in weights],
        out_specs=[out_spec] * 7,
        out_shape=[jax.ShapeDtypeStruct((m, d), F32)] * 7,
        compiler_params=_params(("arbitrary",)),
        name="rwkv_prep_sample",
    )(p, p, p, p, *prevs, *weights)


def _pair_stack(x, first_head):
    zero = jnp.zeros_like(x)
    return jnp.concatenate([jnp.where(first_head, x, zero), jnp.where(first_head, zero, x)], axis=0)


def _wkv_block(r, lw, k, v, a, b, states, tri, gram_mask, state_mask):
    c = WKV_CHUNK
    n_chunks = r.shape[0] // c
    n_pairs = r.shape[1] // LANES
    inst = [(j, p) for j in range(n_chunks) for p in range(n_pairs)]
    lane = lax.broadcasted_iota(jnp.int32, (c, LANES), 1)
    h0 = lane < HEAD
    h0x2 = jnp.concatenate([h0, h0], axis=1)

    def cut(x):
        return [x[j * c:(j + 1) * c, p * LANES:(p + 1) * LANES] for j, p in inst]

    cum = _dot_exact_lhs(tri, lw)
    e_out = jnp.exp(-cum)
    cums = cut(cum)
    a_s = cut(a * jnp.exp(cum - lw))
    r_s = cut(r * jnp.exp(cum))
    b_s = cut(b * e_out)
    k_s = cut(k * e_out)
    bs, ks, vs = cut(b), cut(k), cut(v)
    lasts = [x[c - 1:, :] for x in cums]
    e_end = [jnp.exp(last - x) for last, x in zip(lasts, cums)]
    bk_e = [jnp.concatenate([bi * e, ki * e], axis=0) for bi, ki, e in zip(bs, ks, e_end)]
    decay = [jnp.exp(last) for last in lasts]

    grams = [jnp.where(gram_mask,
                       _dot_nt(jnp.concatenate([ai, ri], axis=0),
                               jnp.concatenate([_pair_stack(bi, h0), _pair_stack(ki, h0)], axis=0)), 0.0)
             for ai, ri, bi, ki in zip(a_s, r_s, b_s, k_s)]
    v_st = [_pair_stack(x, h0) for x in vs]
    xs = [jnp.concatenate([ai, _dot(g[:c, LANES:], vi)], axis=1) for ai, g, vi in zip(a_s, grams, v_st)]
    pws = [g[:c, :LANES] for g in grams]
    n = 1
    while True:
        xs = [x + _dot(pw, _pair_stack(x, h0x2)) for x, pw in zip(xs, pws)]
        n *= 2
        if n >= c:
            break
        pws = [_dot(pw, _pair_stack(pw, h0)) for pw in pws]
    qos = [_dot(g[c:, :LANES], _pair_stack(x, h0x2)) for g, x in zip(grams, xs)]
    qp = [jnp.concatenate([ri + qo[:, :LANES], x[:, :LANES]], axis=0) for ri, qo, x in zip(r_s, qos, xs)]
    o2s = [qo[:, LANES:] + _dot(g[c:, LANES:], vi) for qo, g, vi in zip(qos, grams, v_st)]

    outs = {}
    states = list(states)
    for j in range(n_chunks):
        idx = [j * n_pairs + p for p in range(n_pairs)]
        ous = [_dot_nt(qp[i], states[p]) for p, i in enumerate(idx)]
        upds = [_dot_tn(jnp.concatenate([ou[c:] + xs[i][:, LANES:], vs[i]], axis=0), bk_e[i])
                for ou, i in zip(ous, idx)]
        for p, i in enumerate(idx):
            outs[(j, p)] = ous[p][:c] + o2s[i]
            states[p] = states[p] * decay[i] + jnp.where(state_mask, upds[p], 0.0)
    return outs, states


def _rwkv_prompt_kernel(*refs):
    p_refs, mix_refs, out_refs = refs[:4], refs[4:15], refs[15:18]
    y_ref, sf_ref = refs[18:20]
    carries, s_ref, o_scr = refs[20:24], refs[24], refs[25]
    c = WKV_CHUNK
    tb = y_ref.shape[0]
    n_pairs = y_ref.shape[1] // LANES
    t = pl.program_id(1)
    first = t == 0

    @pl.when(first)
    def _():
        s_ref[...] = jnp.zeros_like(s_ref)

    ps = [ref[...] for ref in p_refs]
    qs = [_shifted(p, carry, first) for p, carry in zip(ps, carries)]
    r, lw, k, v, aa, bb, gate = _rwkv_mix(*ps, *qs, *[ref[...] for ref in mix_refs])

    ri = lax.broadcasted_iota(jnp.int32, (tb, tb), 0)
    ci = lax.broadcasted_iota(jnp.int32, (tb, tb), 1)
    tri = ((ri // c == ci // c) & (ci <= ri)).astype(F32)
    gr = lax.broadcasted_iota(jnp.int32, (2 * c, 2 * LANES), 0)
    gc = lax.broadcasted_iota(jnp.int32, (2 * c, 2 * LANES), 1) % c
    gram_mask = gc <= jnp.where(gr < c, gr - 1, gr - c)
    sr = lax.broadcasted_iota(jnp.int32, (LANES, LANES), 0) // HEAD
    sc = lax.broadcasted_iota(jnp.int32, (LANES, LANES), 1) // HEAD
    state_mask = sr == sc

    outs, states = _wkv_block(r, lw, k, v, aa, bb, [s_ref[p] for p in range(n_pairs)],
                              tri, gram_mask, state_mask)
    for (j, p), o in outs.items():
        o_scr[j * c:(j + 1) * c, p * LANES:(p + 1) * LANES] = o
    for p, s in enumerate(states):
        s_ref[p] = s
    y = _rwkv_out(o_scr[...], r, k, v, gate, *[ref[...] for ref in out_refs])
    y_ref[...] = y.astype(y_ref.dtype)

    @pl.when(t == pl.num_programs(1) - 1)
    def _():
        for p, s in enumerate(states):
            sf_ref[0, 2 * p] = s[:HEAD, :HEAD]
            sf_ref[0, 2 * p + 1] = s[HEAD:, HEAD:]


def _rwkv_prompt(p, lay, mix_weights, out_weights, batch, seq, tb):
    d = lay.d_rwkv
    n_heads = d // HEAD
    nt = seq // tb
    row = lambda b, i: b * nt + i
    p_specs = [pl.BlockSpec((tb, d), lambda b, i: (row(b, i), lay.r0 // d)),
               pl.BlockSpec((tb, d), lambda b, i: (row(b, i), lay.k0 // d)),
               pl.BlockSpec((tb, d), lambda b, i: (row(b, i), lay.v0 // d)),
               pl.BlockSpec((tb, lay.tn), lambda b, i: (row(b, i), lay.lo0 // lay.tn))]
    weights = list(mix_weights) + list(out_weights)
    return pl.pallas_call(
        _rwkv_prompt_kernel,
        grid=(batch, nt),
        in_specs=p_specs + [_full(w) for w in weights],
        out_specs=[pl.BlockSpec((tb, d), lambda b, i: (row(b, i), 0)),
                   pl.BlockSpec((1, n_heads, HEAD, HEAD), lambda b, i: (b, 0, 0, 0))],
        out_shape=[jax.ShapeDtypeStruct((batch * seq, d), BF16),
                   jax.ShapeDtypeStruct((batch, n_heads, HEAD, HEAD), F32)],
        scratch_shapes=([pltpu.VMEM((1, d), F32)] * 3
                        + [pltpu.VMEM((1, lay.tn), F32), pltpu.VMEM((d // LANES, LANES, LANES), F32),
                           pltpu.VMEM((tb, d), F32)]),
        compiler_params=_params(("parallel", "arbitrary")),
        name="rwkv_prompt",
    )(p, p, p, p, *weights)


STEP_UNROLL = 4


def _wkv_step_kernel(r_ref, lw_ref, k_ref, v_ref, a_ref, b_ref, s_ref, o_ref, sn_ref):
    a, b, k, r = a_ref[...], b_ref[...], k_ref[...], r_ref[...]
    w = jnp.exp(lw_ref[...])

    def body(j, carry):
        for u in range(STEP_UNROLL):
            i = j * STEP_UNROLL + u
            s = s_ref[0, i]
            sa = jnp.sum(s * a, axis=0, keepdims=True)
            s = s * w + sa * b + v_ref[pl.ds(i, 1), :] * k
            sn_ref[0, i] = s
            o_ref[pl.ds(i, 1), :] = jnp.sum(s * r, axis=0, keepdims=True)
        return carry

    lax.fori_loop(0, s_ref.shape[1] // STEP_UNROLL, body, 0)


def _wkv_step(r, lw, k, v, a, b, state):
    m, d = r.shape
    n_heads = d // HEAD
    vec = pl.BlockSpec((HEAD, m), lambda h: (h, 0))
    st = pl.BlockSpec((1, HEAD, HEAD, m), lambda h: (h, 0, 0, 0))
    o, s = pl.pallas_call(
        _wkv_step_kernel,
        grid=(n_heads,),
        in_specs=[vec] * 6 + [st],
        out_specs=[vec, st],
        out_shape=[jax.ShapeDtypeStruct((d, m), F32), jax.ShapeDtypeStruct((n_heads, HEAD, HEAD, m), F32)],
        compiler_params=_params(("parallel",)),
        name="wkv_step",
    )(*[t.T for t in (r, lw, k, v, a, b)], jnp.transpose(state, (1, 2, 3, 0)))
    return o.T, jnp.transpose(s, (3, 0, 1, 2))


def _post_kernel(*refs):
    y_ref = refs[-1]
    y_ref[...] = _rwkv_out(*[ref[...] for ref in refs[:-1]]).astype(y_ref.dtype)


def _rwkv_post(o, r, k, v, g, out_weights):
    m, d = o.shape
    spec = pl.BlockSpec((m, d), lambda i: (0, 0))
    return pl.pallas_call(
        _post_kernel,
        grid=(1,),
        in_specs=[spec] * 5 + [_full(w) for w in out_weights],
        out_specs=spec,
        out_shape=jax.ShapeDtypeStruct((m, d), BF16),
        compiler_params=_params(("arbitrary",)),
        name="rwkv_post",
    )(o, r, k, v, g, *out_weights)


def _gelu(x):
    return 0.5 * x * (1.0 + jnp.tanh(GELU_C * (x + 0.044715 * (x * x * x))))


def _layernorm(x, g, b):
    mu = jnp.mean(x, axis=-1, keepdims=True)
    xc = x - mu
    var = jnp.mean(xc * xc, axis=-1, keepdims=True)
    return xc * lax.rsqrt(var + LN_EPS) * g + b


def _sgu_prompt_kernel(pu_ref, pv_ref, ng_ref, nb_ref, w_ref, bias_ref, y_ref, *, n_chunks):
    c = SGU_CHUNK
    ri = lax.broadcasted_iota(jnp.int32, (c, c), 0)
    ci = lax.broadcasted_iota(jnp.int32, (c, c), 1)
    causal = ci <= ri
    n_groups = w_ref.shape[0]
    ws = [jnp.where(causal, w_ref[g], 0.0).astype(BF16) for g in range(n_groups)]
    for j in range(n_chunks):
        rows = pl.ds(j * c, c)
        u = _gelu(pu_ref[rows, :])
        vs = _layernorm(_gelu(pv_ref[rows, :]), ng_ref[...], nb_ref[...]).astype(BF16)
        for g in range(n_groups):
            cols = slice(g * c, (g + 1) * c)
            mix = jnp.dot(ws[g], vs[:, cols], preferred_element_type=F32) + bias_ref[:, cols]
            y_ref[rows, cols] = (u[:, cols] * mix).astype(y_ref.dtype)


def _sgu_prompt(p, lay, norm_g, norm_b, sgu_w, sgu_b, batch, seq, ts):
    d = lay.d_sgu
    nt = seq // ts
    bias = jnp.repeat(sgu_b.T, SGU_CHUNK, axis=1)
    return pl.pallas_call(
        functools.partial(_sgu_prompt_kernel, n_chunks=ts // SGU_CHUNK),
        grid=(batch * nt,),
        in_specs=[pl.BlockSpec((ts, d), lambda i: (i, lay.u0 // d)),
                  pl.BlockSpec((ts, d), lambda i: (i, lay.vs0 // d)),
                  pl.BlockSpec((1, d), lambda i: (0, 0)),
                  pl.BlockSpec((1, d), lambda i: (0, 0)),
                  _full(sgu_w), _full(bias)],
        out_specs=pl.BlockSpec((ts, d), lambda i: (i, 0)),
        out_shape=jax.ShapeDtypeStruct((batch * seq, d), BF16),
        compiler_params=_params(("parallel",)),
        name="sgu_prompt",
    )(p, p, norm_g[None, :], norm_b[None, :], sgu_w, bias)


def _sgu_sample_kernel(pu_ref, pv_ref, ng_ref, nb_ref, w_ref, bias_ref, y_ref, vs_ref):
    u = _gelu(pu_ref[...])
    vs = _layernorm(_gelu(pv_ref[...]), ng_ref[...], nb_ref[...])
    vs_ref[...] = vs
    y_ref[...] = (u * (w_ref[...] * vs + bias_ref[...])).astype(y_ref.dtype)


def _sgu_sample(p, lay, norm_g, norm_b, sgu_w, sgu_b):
    d = lay.d_sgu
    m = p.shape[0]
    w0 = jnp.repeat(sgu_w[:, 0, 0], SGU_CHUNK)[None, :]
    b0 = jnp.repeat(sgu_b[:, 0], SGU_CHUNK)[None, :]
    vec = pl.BlockSpec((1, d), lambda i: (0, 0))
    out = pl.BlockSpec((m, d), lambda i: (0, 0))
    return pl.pallas_call(
        _sgu_sample_kernel,
        grid=(1,),
        in_specs=[pl.BlockSpec((m, d), lambda i: (0, lay.u0 // d)),
                  pl.BlockSpec((m, d), lambda i: (0, lay.vs0 // d)), vec, vec, vec, vec],
        out_specs=[out, out],
        out_shape=[jax.ShapeDtypeStruct((m, d), BF16), jax.ShapeDtypeStruct((m, d), F32)],
        compiler_params=_params(("arbitrary",)),
        name="sgu_sample",
    )(p, p, norm_g[None, :], norm_b[None, :], w0, b0)


def _out_proj_kernel(x_ref, ya_ref, yb_ref, wa_ref, wb_ref, o_ref):
    o_ref[...] = (x_ref[...] + jnp.dot(ya_ref[...], wa_ref[...].astype(BF16), preferred_element_type=F32)
                  + jnp.dot(yb_ref[...], wb_ref[...].astype(BF16), preferred_element_type=F32))


def _out_proj(x, ya, yb, w, tm, tn):
    m, d = x.shape
    da = ya.shape[1]
    return pl.pallas_call(
        _out_proj_kernel,
        grid=(m // tm, d // tn),
        in_specs=[pl.BlockSpec((tm, tn), lambda i, j: (i, j)),
                  pl.BlockSpec((tm, da), lambda i, j: (i, 0)),
                  pl.BlockSpec((tm, da), lambda i, j: (i, 0)),
                  pl.BlockSpec((da, tn), lambda i, j: (0, j)),
                  pl.BlockSpec((da, tn), lambda i, j: (1, j))],
        out_specs=pl.BlockSpec((tm, tn), lambda i, j: (i, j)),
        out_shape=jax.ShapeDtypeStruct((m, d), F32),
        compiler_params=_params(("parallel", "arbitrary")),
        name="out_proj",
    )(x, ya, yb, w, w)


def _ffn_kernel(x_ref, xs_ref, g2_ref, wu_ref, wd_ref, gf_ref, o_ref, os_ref, h_ref, a_ref):
    f = pl.program_id(1)
    tm = x_ref.shape[0]

    @pl.when(f == 0)
    def _():
        x, xs = x_ref[...], xs_ref[...]
        h_ref[:tm, :] = _rms(x, g2_ref[...]).astype(BF16)
        h_ref[tm:, :] = _rms(xs, g2_ref[...]).astype(BF16)
        o_ref[...] = x
        os_ref[...] = xs
        a_ref[...] = jnp.zeros_like(a_ref)

    down = jnp.dot(a_ref[...], wd_ref[...].astype(BF16), preferred_element_type=F32)
    up = jnp.dot(h_ref[...], wu_ref[...].astype(BF16), preferred_element_type=F32)
    o_ref[...] += down[:tm]
    os_ref[...] += down[tm:]
    a_ref[...] = jnp.square(jnp.maximum(up, 0.0)).astype(BF16)

    @pl.when(f == pl.num_programs(1) - 1)
    def _():
        o_ref[...] = _rms(o_ref[...], gf_ref[...])
        os_ref[...] = _rms(os_ref[...], gf_ref[...])


def _ffn(x, xs, g2, w_up, w_down, gf, tm, tf):
    m, d = x.shape
    n_blocks = m // tm
    ts = xs.shape[0] // n_blocks
    assert ts * n_blocks == xs.shape[0] and ts % 8 == 0
    nf = w_up.shape[1] // tf
    return pl.pallas_call(
        _ffn_kernel,
        grid=(n_blocks, nf + 1),
        in_specs=[pl.BlockSpec((tm, d), lambda i, f: (i, 0)),
                  pl.BlockSpec((ts, d), lambda i, f: (i, 0)),
                  pl.BlockSpec((1, d), lambda i, f: (0, 0)),
                  pl.BlockSpec((d, tf), lambda i, f: (0, jnp.minimum(f, nf - 1))),
                  pl.BlockSpec((tf, d), lambda i, f: (jnp.maximum(f - 1, 0), 0)),
                  pl.BlockSpec((1, d), lambda i, f: (0, 0))],
        out_specs=[pl.BlockSpec((tm, d), lambda i, f: (i, 0)),
                   pl.BlockSpec((ts, d), lambda i, f: (i, 0))],
        out_shape=[jax.ShapeDtypeStruct((m, d), F32), jax.ShapeDtypeStruct(xs.shape, F32)],
        scratch_shapes=[pltpu.VMEM((tm + ts, d), BF16), pltpu.VMEM((tm + ts, tf), BF16)],
        compiler_params=_params(("parallel", "arbitrary")),
        name="ffn",
    )(x, xs, g2[None, :], w_up, w_down, gf[None, :])


def _row_tile(m, cap):
    t = min(m, cap)
    assert m % t == 0
    return t


def kernel(x_prompt, x_sample, state_wkv, state_shift, norm1_g, w_in, mu_shift, w0, w_up, a0, a_up, g_up,
           k_k, k_a, r_k, lnx_g, lnx_b, sgu_norm_g, sgu_norm_b, sgu_w, sgu_b, w_out, norm2_g, w_ffn_up,
           w_ffn_down, norm_f_g):
    batch, seq, d_model = x_prompt.shape
    n_dec, dec_seq, _ = x_sample.shape
    depth = w_in.shape[0]
    assert depth == 1 and dec_seq == 1
    d_rwkv = w0.shape[1]
    d_sgu = sgu_norm_g.shape[1]
    tn_in = 512
    lay = _Layout(d_rwkv, d_sgu, w_up.shape[1], a_up.shape[1], g_up.shape[1], tn_in)
    w_in_t = w_in[0].T.astype(BF16)
    prep_w = _prep_weights(lay, mu_shift[0], w0[0], w_up[0], a0[0], a_up[0], g_up[0], k_k[0], k_a[0])

    out_w = [w.reshape(1, d_rwkv) for w in (lnx_g[0], lnx_b[0], r_k[0])]

    w_out_b = w_out[0].astype(BF16)

    xp = x_prompt.reshape(batch * seq, d_model)
    pp = _in_proj(xp, norm1_g, w_in_t, lay.src_rows, _row_tile(batch * seq, 2048), tn_in)
    ya, wkv_p = _rwkv_prompt(pp, lay, prep_w, out_w, batch, seq, 256)
    yb = _sgu_prompt(pp, lay, sgu_norm_g[0], sgu_norm_b[0], sgu_w[0], sgu_b[0], batch, seq, 512)
    x1p = _out_proj(xp, ya, yb, w_out_b, _row_tile(batch * seq, 2048), 512)
    shift_p = lay.shift_row(pp.reshape(batch, seq, -1)[:, -1])

    xs = x_sample.reshape(n_dec, d_model)
    ps = _in_proj(xs, norm1_g, w_in_t, lay.src_rows, n_dec, tn_in)
    r, lw, k, v, aa, bb, g = _prep_sample(ps, state_shift[0], lay, prep_w)
    o, wkv_s = _wkv_step(r, lw, k, v, aa, bb, state_wkv[0])
    yb, vs = _sgu_sample(ps, lay, sgu_norm_g[0], sgu_norm_b[0], sgu_w[0], sgu_b[0])
    ya = _rwkv_post(o, r, k, v, g, out_w)
    x1s = _out_proj(xs, ya, yb, w_out_b, n_dec, 512)
    shift_s = lay.shift_row(ps)

    y_prompt, y_sample = _ffn(x1p, x1s, norm2_g[0], w_ffn_up[0], w_ffn_down[0], norm_f_g,
                              _row_tile(batch * seq, 1024), 256)
    y_prompt = y_prompt.reshape(batch, seq, d_model)
    y_sample = y_sample.reshape(n_dec, 1, d_model)
    return (y_prompt, y_sample, wkv_p[None], shift_p[None], wkv_s[None], shift_s[None],
            vs.reshape(1, n_dec, 1, d_sgu))
```

```python
import functools
import math

import jax
import jax.numpy as jnp
from jax import lax
from jax.experimental import pallas as pl
from jax.experimental.pallas import tpu as pltpu

F32 = jnp.float32
BF16 = jnp.bfloat16

HEAD = 64
LANES = 128
SGU_CHUNK = 128
WKV_CHUNK = 64
RMS_EPS = 1e-5
LN_EPS = 1e-5
GN_EPS = 64e-5
DECAY_SCALE = math.exp(-0.5)
GELU_C = math.sqrt(2.0 / math.pi)
VMEM_LIMIT = 58 * 1024 * 1024


def _params(sem):
    return pltpu.CompilerParams(dimension_semantics=sem, vmem_limit_bytes=VMEM_LIMIT)


def _dot(a, b):
    return jnp.dot(a.astype(BF16), b.astype(BF16), preferred_element_type=F32)


def _dot_nt(a, b):
    return lax.dot_general(a.astype(BF16), b.astype(BF16), (((1,), (1,)), ((), ())),
                           preferred_element_type=F32)


def _dot_tn(a, b):
    return lax.dot_general(a.astype(BF16), b.astype(BF16), (((0,), (0,)), ((), ())),
                           preferred_element_type=F32)


def _split3(x):
    hi = x.astype(BF16)
    r1 = x - hi.astype(F32)
    mid = r1.astype(BF16)
    lo = (r1 - mid.astype(F32)).astype(BF16)
    return hi, mid, lo


def _dot_exact_lhs(m, x):
    hi, mid, lo = _split3(x)
    mb = m.astype(BF16)
    return (jnp.dot(mb, hi, preferred_element_type=F32) + jnp.dot(mb, mid, preferred_element_type=F32)
            + jnp.dot(mb, lo, preferred_element_type=F32))


def _sigmoid(x):
    return 1.0 / (1.0 + jnp.exp(-x))


def _head_ones():
    r = lax.broadcasted_iota(jnp.int32, (2 * LANES, LANES), 0) % LANES // HEAD
    c = lax.broadcasted_iota(jnp.int32, (2 * LANES, LANES), 1) // HEAD
    return (r == c).astype(BF16)


def _head_sum(x, ones):
    parts = []
    for s in range(0, x.shape[1], LANES):
        xs = x[:, s:s + LANES]
        hi = xs.astype(BF16)
        lo = (xs - hi.astype(F32)).astype(BF16)
        parts.append(jnp.dot(jnp.concatenate([hi, lo], axis=1), ones, preferred_element_type=F32))
    return parts[0] if len(parts) == 1 else jnp.concatenate(parts, axis=1)


def _rms(x, g):
    return x * lax.rsqrt(jnp.mean(x * x, axis=-1, keepdims=True) + RMS_EPS) * g


def _in_proj_kernel(rows_ref, x_ref, g_ref, wt_ref, o_ref, h_ref):
    del rows_ref
    @pl.when(pl.program_id(1) == 0)
    def _():
        h_ref[...] = _rms(x_ref[...], g_ref[...]).astype(BF16)

    o_ref[...] = _dot_nt(h_ref[...], wt_ref[...])


def _in_proj(x, g, wt, src_rows, tm, tn):
    m, d = x.shape
    n_blocks = len(src_rows)
    return pl.pallas_call(
        _in_proj_kernel,
        grid_spec=pltpu.PrefetchScalarGridSpec(
            num_scalar_prefetch=1,
            grid=(m // tm, n_blocks),
            in_specs=[pl.BlockSpec((tm, d), lambda i, j, rows: (i, 0)),
                      pl.BlockSpec((1, d), lambda i, j, rows: (0, 0)),
                      pl.BlockSpec((pl.Element(tn), pl.Element(d)), lambda i, j, rows: (pl.multiple_of(rows[j], 8), 0))],
            out_specs=pl.BlockSpec((tm, tn), lambda i, j, rows: (i, j)),
            scratch_shapes=[pltpu.VMEM((tm, d), BF16)]),
        out_shape=jax.ShapeDtypeStruct((m, n_blocks * tn), F32),
        compiler_params=_params(("parallel", "arbitrary")),
        name="in_proj",
    )(jnp.asarray(src_rows, jnp.int32), x, g, wt)


def _rwkv_mix(pr, pk, pv, pl_, qr, qk, qv, ql, mu_r, mu_k, mu_v, mu_l,
              w0, w_up, a0, a_up, g_up, k_k, k_a):
    n_wa = w_up.shape[0]
    n_gl = g_up.shape[0]
    r = pr + (qr - pr) * mu_r
    k = pk + (qk - pk) * mu_k
    v = pv + (qv - pv) * mu_v
    lo = pl_[:, :n_wa + n_gl]
    lo = lo + (ql[:, :n_wa + n_gl] - lo) * mu_l[:, :n_wa + n_gl]
    wa, gl = lo[:, :n_wa], lo[:, n_wa:]
    lw = -DECAY_SCALE * _sigmoid(w0 + _dot(jnp.tanh(wa), w_up))
    a = _sigmoid(a0 + _dot(wa, a_up))
    gate = _dot(_sigmoid(gl), g_up)
    kk = k * k_k
    ss = _head_sum(kk * kk, _head_ones())
    kk = kk / jnp.maximum(jnp.sqrt(ss), 1e-12)
    return r, lw, k * (1.0 + (a - 1.0) * k_a), v, -kk, kk * a, gate


def _rwkv_out(o, r, k, v, gate, lnx_g, lnx_b, r_k):
    ones = _head_ones()
    mu = _head_sum(o, ones) * (1.0 / HEAD)
    oc = o - mu
    var = _head_sum(oc * oc, ones) * (1.0 / HEAD)
    y = oc * lax.rsqrt(var + GN_EPS) * lnx_g + lnx_b
    return (y + _head_sum(r * k * r_k, ones) * v) * gate


def _shifted(p, carry_ref, first):
    @pl.when(first)
    def _():
        carry_ref[...] = jnp.zeros_like(carry_ref)

    rows = lax.broadcasted_iota(jnp.int32, p.shape, 0)
    q = jnp.where(rows == 0, carry_ref[...], pltpu.roll(p, 1, axis=0))
    carry_ref[...] = p[p.shape[0] - 1:, :]
    return q


def _prep_sample_kernel(*refs):
    ins, outs = refs[:-7], refs[-7:]
    for ref, val in zip(outs, _rwkv_mix(*[ref[...] for ref in ins])):
        ref[...] = val


class _Layout:
    def __init__(self, d_rwkv, d_sgu, lora_w, lora_a, lora_g, tn):
        self.d_rwkv, self.d_sgu, self.tn = d_rwkv, d_sgu, tn
        self.wa_w = lora_w + lora_a
        self.gl_w = -(-lora_g // LANES) * LANES
        self.d_shift = 3 * d_rwkv + self.wa_w + lora_g
        assert self.wa_w == LANES and self.wa_w + self.gl_w <= tn
        assert d_rwkv % tn == 0 and d_sgu % tn == 0
        self.u0 = 0
        self.vs0 = d_sgu
        self.r0 = 2 * d_sgu
        self.k0 = self.r0 + d_rwkv
        self.v0 = self.k0 + d_rwkv
        self.lo0 = self.v0 + d_rwkv
        self.width = self.lo0 + tn
        self.src_rows = (list(range(self.d_shift, self.d_shift + 2 * d_sgu, tn))
                         + list(range(0, 3 * d_rwkv + tn, tn)))

    def rw_pieces(self, a):
        d = self.d_rwkv
        pad = [(0, 0)] * (a.ndim - 1) + [(0, self.tn - (self.d_shift - 3 * d))]
        return a[..., :d], a[..., d:2 * d], a[..., 2 * d:3 * d], jnp.pad(a[..., 3 * d:], pad)

    def shift_row(self, p_rows):
        return p_rows[:, self.r0:self.r0 + self.d_shift]


def _prep_weights(lay, mu, w0, w_up, a0, a_up, g_up, k_k, k_a):
    d = lay.d_rwkv
    lora_w, lora_g = w_up.shape[0], g_up.shape[0]
    mus = [m[None, :] for m in lay.rw_pieces(mu)]
    w_up_p = jnp.pad(w_up, ((0, lay.wa_w - lora_w), (0, 0)))
    a_up_p = jnp.pad(a_up, ((lora_w, 0), (0, 0)))
    g_up_p = jnp.pad(g_up, ((0, lay.gl_w - lora_g), (0, 0)))
    return mus + [w0[None, :], w_up_p, a0[None, :], a_up_p, g_up_p, k_k.reshape(1, d), k_a.reshape(1, d)]


def _full(a):
    return pl.BlockSpec(a.shape, lambda *_: (0,) * a.ndim)


def _prep_sample(p, prev, lay, weights):
    d = lay.d_rwkv
    m = p.shape[0]
    p_specs = [pl.BlockSpec((m, d), lambda i: (0, lay.r0 // d)),
               pl.BlockSpec((m, d), lambda i: (0, lay.k0 // d)),
               pl.BlockSpec((m, d), lambda i: (0, lay.v0 // d)),
               pl.BlockSpec((m, lay.tn), lambda i: (0, lay.lo0 // lay.tn))]
    prevs = list(lay.rw_pieces(prev))
    out_spec = pl.BlockSpec((m, d), lambda i: (0, 0))
    return pl.pallas_call(
        _prep_sample_kernel,
        grid=(1,),
        in_specs=p_specs + [_full(q) for q in prevs] + [_full(w) for w in weights],
        out_specs=[out_spec] * 7,
        out_shape=[jax.ShapeDtypeStruct((m, d), F32)] * 7,
        compiler_params=_params(("arbitrary",)),
        name="rwkv_prep_sample",
    )(p, p, p, p, *prevs, *weights)


def _pair_stack(x, first_head):
    zero = jnp.zeros_like(x)
    return jnp.concatenate([jnp.where(first_head, x, zero), jnp.where(first_head, zero, x)], axis=0)


def _wkv_block(r, lw, k, v, a, b, states, tri, gram_mask, state_mask):
    c = WKV_CHUNK
    n_chunks = r.shape[0] // c
    n_pairs = r.shape[1] // LANES
    inst = [(j, p) for j in range(n_chunks) for p in range(n_pairs)]
    lane = lax.broadcasted_iota(jnp.int32, (c, LANES), 1)
    h0 = lane < HEAD
    h0x2 = jnp.concatenate([h0, h0], axis=1)

    def cut(x):
        return [x[j * c:(j + 1) * c, p * LANES:(p + 1) * LANES] for j, p in inst]

    cum = _dot_exact_lhs(tri, lw)
    e_out = jnp.exp(-cum)
    cums = cut(cum)
    a_s = cut(a * jnp.exp(cum - lw))
    r_s = cut(r * jnp.exp(cum))
    b_s = cut(b * e_out)
    k_s = cut(k * e_out)
    bs, ks, vs = cut(b), cut(k), cut(v)
    lasts = [x[c - 1:, :] for x in cums]
    e_end = [jnp.exp(last - x) for last, x in zip(lasts, cums)]
    bk_e = [jnp.concatenate([bi * e, ki * e], axis=0) for bi, ki, e in zip(bs, ks, e_end)]
    decay = [jnp.exp(last) for last in lasts]

    grams = [jnp.where(gram_mask,
                       _dot_nt(jnp.concatenate([ai, ri], axis=0),
                               jnp.concatenate([_pair_stack(bi, h0), _pair_stack(ki, h0)], axis=0)), 0.0)
             for ai, ri, bi, ki in zip(a_s, r_s, b_s, k_s)]
    v_st = [_pair_stack(x, h0) for x in vs]
    kvs = [_dot(g[:, LANES:], vi) for g, vi in zip(grams, v_st)]
    xs = [jnp.concatenate([ai, kv[:c]], axis=1) for ai, kv in zip(a_s, kvs)]
    pws = [g[:c, :LANES] for g in grams]
    ns = pws
    pws = [_dot(pw, _pair_stack(pw, h0)) for pw in pws]
    n = 2
    while n < c // 2:
        both = [_dot(jnp.concatenate([pw, nn], axis=0), _pair_stack(pw, h0)) for pw, nn in zip(pws, ns)]
        ns = [nn + pw + bo[c:] for nn, pw, bo in zip(ns, pws, both)]
        pws = [bo[:c] for bo in both]
        n *= 2
    ns = [nn + pw + _dot(nn, _pair_stack(pw, h0)) for nn, pw in zip(ns, pws)]
    xs = [x + _dot(nn, _pair_stack(x, h0x2)) for x, nn in zip(xs, ns)]
    qos = [_dot(g[c:, :LANES], _pair_stack(x, h0x2)) for g, x in zip(grams, xs)]
    qp = [jnp.concatenate([ri + qo[:, :LANES], x[:, :LANES]], axis=0) for ri, qo, x in zip(r_s, qos, xs)]
    o2s = [qo[:, LANES:] + kv[c:] for qo, kv in zip(qos, kvs)]

    outs = {}
    states = list(states)
    for j in range(n_chunks):
        idx = [j * n_pairs + p for p in range(n_pairs)]
        ous = [_dot_nt(qp[i], states[p]) for p, i in enumerate(idx)]
        upds = [_dot_tn(jnp.concatenate([ou[c:] + xs[i][:, LANES:], vs[i]], axis=0), bk_e[i])
                for ou, i in zip(ous, idx)]
        for p, i in enumerate(idx):
            outs[(j, p)] = ous[p][:c] + o2s[i]
            states[p] = states[p] * decay[i] + jnp.where(state_mask, upds[p], 0.0)
    return outs, states


def _rwkv_prompt_kernel(*refs):
    p_refs, mix_refs, out_refs = refs[:4], refs[4:15], refs[15:18]
    y_ref, sf_ref = refs[18:20]
    carries, s_ref, o_scr = refs[20:24], refs[24], refs[25]
    c = WKV_CHUNK
    tb = y_ref.shape[0]
    n_pairs = y_ref.shape[1] // LANES
    t = pl.program_id(1)
    first = t == 0

    @pl.when(first)
    def _():
        s_ref[...] = jnp.zeros_like(s_ref)

    ps = [ref[...] for ref in p_refs]
    qs = [_shifted(p, carry, first) for p, carry in zip(ps, carries)]
    r, lw, k, v, aa, bb, gate = _rwkv_mix(*ps, *qs, *[ref[...] for ref in mix_refs])

    ri = lax.broadcasted_iota(jnp.int32, (tb, tb), 0)
    ci = lax.broadcasted_iota(jnp.int32, (tb, tb), 1)
    tri = ((ri // c == ci // c) & (ci <= ri)).astype(F32)
    gr = lax.broadcasted_iota(jnp.int32, (2 * c, 2 * LANES), 0)
    gc = lax.broadcasted_iota(jnp.int32, (2 * c, 2 * LANES), 1) % c
    gram_mask = gc <= jnp.where(gr < c, gr - 1, gr - c)
    sr = lax.broadcasted_iota(jnp.int32, (LANES, LANES), 0) // HEAD
    sc = lax.broadcasted_iota(jnp.int32, (LANES, LANES), 1) // HEAD
    state_mask = sr == sc

    outs, states = _wkv_block(r, lw, k, v, aa, bb, [s_ref[p] for p in range(n_pairs)],
                              tri, gram_mask, state_mask)
    for (j, p), o in outs.items():
        o_scr[j * c:(j + 1) * c, p * LANES:(p + 1) * LANES] = o
    for p, s in enumerate(states):
        s_ref[p] = s
    y = _rwkv_out(o_scr[...], r, k, v, gate, *[ref[...] for ref in out_refs])
    y_ref[...] = y.astype(y_ref.dtype)

    @pl.when(t == pl.num_programs(1) - 1)
    def _():
        for p, s in enumerate(states):
            sf_ref[0, 2 * p] = s[:HEAD, :HEAD]
            sf_ref[0, 2 * p + 1] = s[HEAD:, HEAD:]


def _rwkv_prompt(p, lay, mix_weights, out_weights, batch, seq, tb):
    d = lay.d_rwkv
    n_heads = d // HEAD
    nt = seq // tb
    row = lambda b, i: b * nt + i
    p_specs = [pl.BlockSpec((tb, d), lambda b, i: (row(b, i), lay.r0 // d)),
               pl.BlockSpec((tb, d), lambda b, i: (row(b, i), lay.k0 // d)),
               pl.BlockSpec((tb, d), lambda b, i: (row(b, i), lay.v0 // d)),
               pl.BlockSpec((tb, lay.tn), lambda b, i: (row(b, i), lay.lo0 // lay.tn))]
    weights = list(mix_weights) + list(out_weights)
    return pl.pallas_call(
        _rwkv_prompt_kernel,
        grid=(batch, nt),
        in_specs=p_specs + [_full(w) for w in weights],
        out_specs=[pl.BlockSpec((tb, d), lambda b, i: (row(b, i), 0)),
                   pl.BlockSpec((1, n_heads, HEAD, HEAD), lambda b, i: (b, 0, 0, 0))],
        out_shape=[jax.ShapeDtypeStruct((batch * seq, d), BF16),
                   jax.ShapeDtypeStruct((batch, n_heads, HEAD, HEAD), F32)],
        scratch_shapes=([pltpu.VMEM((1, d), F32)] * 3
                        + [pltpu.VMEM((1, lay.tn), F32), pltpu.VMEM((d // LANES, LANES, LANES), F32),
                           pltpu.VMEM((tb, d), F32)]),
        compiler_params=_params(("parallel", "arbitrary")),
        name="rwkv_prompt",
    )(p, p, p, p, *weights)


STEP_UNROLL = 4


def _wkv_step_kernel(r_ref, lw_ref, k_ref, v_ref, a_ref, b_ref, s_ref, o_ref, sn_ref):
    a, b, k, r = a_ref[...], b_ref[...], k_ref[...], r_ref[...]
    w = jnp.exp(lw_ref[...])

    def body(j, carry):
        for u in range(STEP_UNROLL):
            i = j * STEP_UNROLL + u
            s = s_ref[0, i]
            sa = jnp.sum(s * a, axis=0, keepdims=True)
            s = s * w + sa * b + v_ref[pl.ds(i, 1), :] * k
            sn_ref[0, i] = s
            o_ref[pl.ds(i, 1), :] = jnp.sum(s * r, axis=0, keepdims=True)
        return carry

    lax.fori_loop(0, s_ref.shape[1] // STEP_UNROLL, body, 0)


def _wkv_step(r, lw, k, v, a, b, state):
    m, d = r.shape
    n_heads = d // HEAD
    vec = pl.BlockSpec((HEAD, m), lambda h: (h, 0))
    st = pl.BlockSpec((1, HEAD, HEAD, m), lambda h: (h, 0, 0, 0))
    o, s = pl.pallas_call(
        _wkv_step_kernel,
        grid=(n_heads,),
        in_specs=[vec] * 6 + [st],
        out_specs=[vec, st],
        out_shape=[jax.ShapeDtypeStruct((d, m), F32), jax.ShapeDtypeStruct((n_heads, HEAD, HEAD, m), F32)],
        compiler_params=_params(("parallel",)),
        name="wkv_step",
    )(*[t.T for t in (r, lw, k, v, a, b)], jnp.transpose(state, (1, 2, 3, 0)))
    return o.T, jnp.transpose(s, (3, 0, 1, 2))


def _post_kernel(*refs):
    y_ref = refs[-1]
    y_ref[...] = _rwkv_out(*[ref[...] for ref in refs[:-1]]).astype(y_ref.dtype)


def _rwkv_post(o, r, k, v, g, out_weights):
    m, d = o.shape
    spec = pl.BlockSpec((m, d), lambda i: (0, 0))
    return pl.pallas_call(
        _post_kernel,
        grid=(1,),
        in_specs=[spec] * 5 + [_full(w) for w in out_weights],
        out_specs=spec,
        out_shape=jax.ShapeDtypeStruct((m, d), BF16),
        compiler_params=_params(("arbitrary",)),
        name="rwkv_post",
    )(o, r, k, v, g, *out_weights)


def _gelu(x):
    return 0.5 * x * (1.0 + jnp.tanh(GELU_C * (x + 0.044715 * (x * x * x))))


def _layernorm(x, g, b):
    mu = jnp.mean(x, axis=-1, keepdims=True)
    xc = x - mu
    var = jnp.mean(xc * xc, axis=-1, keepdims=True)
    return xc * lax.rsqrt(var + LN_EPS) * g + b


def _sgu_prompt_kernel(pu_ref, pv_ref, ng_ref, nb_ref, w_ref, bias_ref, y_ref, *, n_chunks):
    c = SGU_CHUNK
    ri = lax.broadcasted_iota(jnp.int32, (c, c), 0)
    ci = lax.broadcasted_iota(jnp.int32, (c, c), 1)
    causal = ci <= ri
    n_groups = w_ref.shape[0]
    ws = [jnp.where(causal, w_ref[g], 0.0).astype(BF16) for g in range(n_groups)]
    for j in range(n_chunks):
        rows = pl.ds(j * c, c)
        u = _gelu(pu_ref[rows, :])
        vs = _layernorm(_gelu(pv_ref[rows, :]), ng_ref[...], nb_ref[...]).astype(BF16)
        for g in range(n_groups):
            cols = slice(g * c, (g + 1) * c)
            mix = jnp.dot(ws[g], vs[:, cols], preferred_element_type=F32) + bias_ref[:, cols]
            y_ref[rows, cols] = (u[:, cols] * mix).astype(y_ref.dtype)


def _sgu_prompt(p, lay, norm_g, norm_b, sgu_w, sgu_b, batch, seq, ts):
    d = lay.d_sgu
    nt = seq // ts
    bias = jnp.repeat(sgu_b.T, SGU_CHUNK, axis=1)
    return pl.pallas_call(
        functools.partial(_sgu_prompt_kernel, n_chunks=ts // SGU_CHUNK),
        grid=(batch * nt,),
        in_specs=[pl.BlockSpec((ts, d), lambda i: (i, lay.u0 // d)),
                  pl.BlockSpec((ts, d), lambda i: (i, lay.vs0 // d)),
                  pl.BlockSpec((1, d), lambda i: (0, 0)),
                  pl.BlockSpec((1, d), lambda i: (0, 0)),
                  _full(sgu_w), _full(bias)],
        out_specs=pl.BlockSpec((ts, d), lambda i: (i, 0)),
        out_shape=jax.ShapeDtypeStruct((batch * seq, d), BF16),
        compiler_params=_params(("parallel",)),
        name="sgu_prompt",
    )(p, p, norm_g[None, :], norm_b[None, :], sgu_w, bias)


def _sgu_sample_kernel(pu_ref, pv_ref, ng_ref, nb_ref, w_ref, bias_ref, y_ref, vs_ref):
    u = _gelu(pu_ref[...])
    vs = _layernorm(_gelu(pv_ref[...]), ng_ref[...], nb_ref[...])
    vs_ref[...] = vs
    y_ref[...] = (u * (w_ref[...] * vs + bias_ref[...])).astype(y_ref.dtype)


def _sgu_sample(p, lay, norm_g, norm_b, sgu_w, sgu_b):
    d = lay.d_sgu
    m = p.shape[0]
    w0 = jnp.repeat(sgu_w[:, 0, 0], SGU_CHUNK)[None, :]
    b0 = jnp.repeat(sgu_b[:, 0], SGU_CHUNK)[None, :]
    vec = pl.BlockSpec((1, d), lambda i: (0, 0))
    out = pl.BlockSpec((m, d), lambda i: (0, 0))
    return pl.pallas_call(
        _sgu_sample_kernel,
        grid=(1,),
        in_specs=[pl.BlockSpec((m, d), lambda i: (0, lay.u0 // d)),
                  pl.BlockSpec((m, d), lambda i: (0, lay.vs0 // d)), vec, vec, vec, vec],
        out_specs=[out, out],
        out_shape=[jax.ShapeDtypeStruct((m, d), BF16), jax.ShapeDtypeStruct((m, d), F32)],
        compiler_params=_params(("arbitrary",)),
        name="sgu_sample",
    )(p, p, norm_g[None, :], norm_b[None, :], w0, b0)


def _out_proj_kernel(x_ref, ya_ref, yb_ref, wa_ref, wb_ref, o_ref):
    o_ref[...] = (x_ref[...] + jnp.dot(ya_ref[...], wa_ref[...].astype(BF16), preferred_element_type=F32)
                  + jnp.dot(yb_ref[...], wb_ref[...].astype(BF16), preferred_element_type=F32))


def _out_proj(x, ya, yb, w, tm, tn):
    m, d = x.shape
    da = ya.shape[1]
    return pl.pallas_call(
        _out_proj_kernel,
        grid=(m // tm, d // tn),
        in_specs=[pl.BlockSpec((tm, tn), lambda i, j: (i, j)),
                  pl.BlockSpec((tm, da), lambda i, j: (i, 0)),
                  pl.BlockSpec((tm, da), lambda i, j: (i, 0)),
                  pl.BlockSpec((da, tn), lambda i, j: (0, j)),
                  pl.BlockSpec((da, tn), lambda i, j: (1, j))],
        out_specs=pl.BlockSpec((tm, tn), lambda i, j: (i, j)),
        out_shape=jax.ShapeDtypeStruct((m, d), F32),
        compiler_params=_params(("parallel", "arbitrary")),
        name="out_proj",
    )(x, ya, yb, w, w)


def _ffn_kernel(x_ref, xs_ref, g2_ref, wu_ref, wd_ref, gf_ref, o_ref, os_ref, h_ref, a_ref):
    f = pl.program_id(1)
    tm = x_ref.shape[0]

    @pl.when(f == 0)
    def _():
        x, xs = x_ref[...], xs_ref[...]
        h_ref[:tm, :] = _rms(x, g2_ref[...]).astype(BF16)
        h_ref[tm:, :] = _rms(xs, g2_ref[...]).astype(BF16)
        o_ref[...] = x
        os_ref[...] = xs
        a_ref[...] = jnp.zeros_like(a_ref)

    down = jnp.dot(a_ref[...], wd_ref[...].astype(BF16), preferred_element_type=F32)
    up = jnp.dot(h_ref[...], wu_ref[...].astype(BF16), preferred_element_type=F32)
    o_ref[...] += down[:tm]
    os_ref[...] += down[tm:]
    a_ref[...] = jnp.square(jnp.maximum(up, 0.0)).astype(BF16)

    @pl.when(f == pl.num_programs(1) - 1)
    def _():
        o_ref[...] = _rms(o_ref[...], gf_ref[...])
        os_ref[...] = _rms(os_ref[...], gf_ref[...])


def _ffn(x, xs, g2, w_up, w_down, gf, tm, tf):
    m, d = x.shape
    n_blocks = m // tm
    ts = xs.shape[0] // n_blocks
    assert ts * n_blocks == xs.shape[0] and ts % 8 == 0
    nf = w_up.shape[1] // tf
    return pl.pallas_call(
        _ffn_kernel,
        grid=(n_blocks, nf + 1),
        in_specs=[pl.BlockSpec((tm, d), lambda i, f: (i, 0)),
                  pl.BlockSpec((ts, d), lambda i, f: (i, 0)),
                  pl.BlockSpec((1, d), lambda i, f: (0, 0)),
                  pl.BlockSpec((d, tf), lambda i, f: (0, jnp.minimum(f, nf - 1))),
                  pl.BlockSpec((tf, d), lambda i, f: (jnp.maximum(f - 1, 0), 0)),
                  pl.BlockSpec((1, d), lambda i, f: (0, 0))],
        out_specs=[pl.BlockSpec((tm, d), lambda i, f: (i, 0)),
                   pl.BlockSpec((ts, d), lambda i, f: (i, 0))],
        out_shape=[jax.ShapeDtypeStruct((m, d), F32), jax.ShapeDtypeStruct(xs.shape, F32)],
        scratch_shapes=[pltpu.VMEM((tm + ts, d), BF16), pltpu.VMEM((tm + ts, tf), BF16)],
        compiler_params=_params(("parallel", "arbitrary")),
        name="ffn",
    )(x, xs, g2[None, :], w_up, w_down, gf[None, :])


def _row_tile(m, cap):
    t = min(m, cap)
    assert m % t == 0
    return t


def kernel(x_prompt, x_sample, state_wkv, state_shift, norm1_g, w_in, mu_shift, w0, w_up, a0, a_up, g_up,
           k_k, k_a, r_k, lnx_g, lnx_b, sgu_norm_g, sgu_norm_b, sgu_w, sgu_b, w_out, norm2_g, w_ffn_up,
           w_ffn_down, norm_f_g):
    batch, seq, d_model = x_prompt.shape
    n_dec, dec_seq, _ = x_sample.shape
    depth = w_in.shape[0]
    assert depth == 1 and dec_seq == 1
    d_rwkv = w0.shape[1]
    d_sgu = sgu_norm_g.shape[1]
    tn_in = 512
    lay = _Layout(d_rwkv, d_sgu, w_up.shape[1], a_up.shape[1], g_up.shape[1], tn_in)
    w_in_t = w_in[0].T.astype(BF16)
    prep_w = _prep_weights(lay, mu_shift[0], w0[0], w_up[0], a0[0], a_up[0], g_up[0], k_k[0], k_a[0])

    out_w = [w.reshape(1, d_rwkv) for w in (lnx_g[0], lnx_b[0], r_k[0])]

    w_out_b = w_out[0].astype(BF16)

    xp = x_prompt.reshape(batch * seq, d_model)
    pp = _in_proj(xp, norm1_g, w_in_t, lay.src_rows, _row_tile(batch * seq, 2048), tn_in)
    ya, wkv_p = _rwkv_prompt(pp, lay, prep_w, out_w, batch, seq, 256)
    yb = _sgu_prompt(pp, lay, sgu_norm_g[0], sgu_norm_b[0], sgu_w[0], sgu_b[0], batch, seq, 512)
    x1p = _out_proj(xp, ya, yb, w_out_b, _row_tile(batch * seq, 2048), 512)
    shift_p = lay.shift_row(pp.reshape(batch, seq, -1)[:, -1])

    xs = x_sample.reshape(n_dec, d_model)
    ps = _in_proj(xs, norm1_g, w_in_t, lay.src_rows, n_dec, tn_in)
    r, lw, k, v, aa, bb, g = _prep_sample(ps, state_shift[0], lay, prep_w)
    o, wkv_s = _wkv_step(r, lw, k, v, aa, bb, state_wkv[0])
    yb, vs = _sgu_sample(ps, lay, sgu_norm_g[0], sgu_norm_b[0], sgu_w[0], sgu_b[0])
    ya = _rwkv_post(o, r, k, v, g, out_w)
    x1s = _out_proj(xs, ya, yb, w_out_b, n_dec, 512)
    shift_s = lay.shift_row(ps)

    y_prompt, y_sample = _ffn(x1p, x1s, norm2_g[0], w_ffn_up[0], w_ffn_down[0], norm_f_g,
                              _row_tile(batch * seq, 1024), 256)
    y_prompt = y_prompt.reshape(batch, seq, d_model)
    y_sample = y_sample.reshape(n_dec, 1, d_model)
    return (y_prompt, y_sample, wkv_p[None], shift_p[None], wkv_s[None], shift_s[None],
            vs.reshape(1, n_dec, 1, d_sgu))
```

```python
import functools
import math

import jax
import jax.numpy as jnp
from jax import lax
from jax.experimental import pallas as pl
from jax.experimental.pallas import tpu as pltpu

F32 = jnp.float32
BF16 = jnp.bfloat16

HEAD = 64
LANES = 128
SGU_CHUNK = 128
WKV_CHUNK = 64
WKV_GROUP_PAIRS = 4
RMS_EPS = 1e-5
LN_EPS = 1e-5
GN_EPS = 64e-5
DECAY_SCALE = math.exp(-0.5)
GELU_C = math.sqrt(2.0 / math.pi)
VMEM_LIMIT = 58 * 1024 * 1024


def _params(sem):
    return pltpu.CompilerParams(dimension_semantics=sem, vmem_limit_bytes=VMEM_LIMIT)


def _dot(a, b):
    return jnp.dot(a.astype(BF16), b.astype(BF16), preferred_element_type=F32)


def _dot_nt(a, b):
    return lax.dot_general(a.astype(BF16), b.astype(BF16), (((1,), (1,)), ((), ())),
                           preferred_element_type=F32)


def _dot_tn(a, b):
    return lax.dot_general(a.astype(BF16), b.astype(BF16), (((0,), (0,)), ((), ())),
                           preferred_element_type=F32)


def _split3(x):
    hi = x.astype(BF16)
    r1 = x - hi.astype(F32)
    mid = r1.astype(BF16)
    lo = (r1 - mid.astype(F32)).astype(BF16)
    return hi, mid, lo


def _dot_exact_lhs(m, x):
    hi, mid, lo = _split3(x)
    mb = m.astype(BF16)
    return (jnp.dot(mb, hi, preferred_element_type=F32) + jnp.dot(mb, mid, preferred_element_type=F32)
            + jnp.dot(mb, lo, preferred_element_type=F32))


def _sigmoid(x):
    return 1.0 / (1.0 + jnp.exp(-x))


def _head_ones():
    r = lax.broadcasted_iota(jnp.int32, (2 * LANES, LANES), 0) % LANES // HEAD
    c = lax.broadcasted_iota(jnp.int32, (2 * LANES, LANES), 1) // HEAD
    return (r == c).astype(BF16)


def _head_sum(x, ones):
    parts = []
    for s in range(0, x.shape[1], LANES):
        xs = x[:, s:s + LANES]
        hi = xs.astype(BF16)
        lo = (xs - hi.astype(F32)).astype(BF16)
        parts.append(jnp.dot(jnp.concatenate([hi, lo], axis=1), ones, preferred_element_type=F32))
    return parts[0] if len(parts) == 1 else jnp.concatenate(parts, axis=1)


def _rms(x, g):
    return x * lax.rsqrt(jnp.mean(x * x, axis=-1, keepdims=True) + RMS_EPS) * g


def _in_proj_kernel(rows_ref, x_ref, g_ref, wt_ref, o_ref, h_ref):
    del rows_ref
    @pl.when(pl.program_id(1) == 0)
    def _():
        h_ref[...] = _rms(x_ref[...], g_ref[...]).astype(BF16)

    o_ref[...] = _dot_nt(h_ref[...], wt_ref[...])


def _in_proj(x, g, wt, src_rows, tm, tn):
    m, d = x.shape
    n_blocks = len(src_rows)
    return pl.pallas_call(
        _in_proj_kernel,
        grid_spec=pltpu.PrefetchScalarGridSpec(
            num_scalar_prefetch=1,
            grid=(m // tm, n_blocks),
            in_specs=[pl.BlockSpec((tm, d), lambda i, j, rows: (i, 0)),
                      pl.BlockSpec((1, d), lambda i, j, rows: (0, 0)),
                      pl.BlockSpec((pl.Element(tn), pl.Element(d)), lambda i, j, rows: (pl.multiple_of(rows[j], 8), 0))],
            out_specs=pl.BlockSpec((tm, tn), lambda i, j, rows: (i, j)),
            scratch_shapes=[pltpu.VMEM((tm, d), BF16)]),
        out_shape=jax.ShapeDtypeStruct((m, n_blocks * tn), F32),
        compiler_params=_params(("parallel", "arbitrary")),
        name="in_proj",
    )(jnp.asarray(src_rows, jnp.int32), x, g, wt)


def _lo_mix(pl_, ql, mu_l, n_wa, n_gl):
    n = n_wa + n_gl
    lo = pl_[:, :n]
    lo = lo + (ql[:, :n] - lo) * mu_l[:, :n]
    wa = lo[:, :n_wa]
    return jnp.tanh(wa), wa, _sigmoid(lo[:, n_wa:])


def _run(steps):
    try:
        while True:
            next(steps)
    except StopIteration as done:
        return done.value


def _col_mix_steps(pr, pk, pv, qr, qk, qv, mu_r, mu_k, mu_v, tanh_wa, wa, sig_gl, w0, w_up, a0, a_up, g_up,
                   k_k, k_a):
    r = pr + (qr - pr) * mu_r
    yield
    k = pk + (qk - pk) * mu_k
    yield
    v = pv + (qv - pv) * mu_v
    yield
    lw = -DECAY_SCALE * _sigmoid(w0 + _dot(tanh_wa, w_up))
    yield
    a = _sigmoid(a0 + _dot(wa, a_up))
    yield
    gate = _dot(sig_gl, g_up)
    yield
    kk = k * k_k
    ss = _head_sum(kk * kk, _head_ones())
    yield
    kk = kk / jnp.maximum(jnp.sqrt(ss), 1e-12)
    yield
    return r, lw, k * (1.0 + (a - 1.0) * k_a), v, -kk, kk * a, gate


def _rwkv_mix(pr, pk, pv, pl_, qr, qk, qv, ql, mu_r, mu_k, mu_v, mu_l, w0, w_up, a0, a_up, g_up, k_k, k_a):
    lo = _lo_mix(pl_, ql, mu_l, w_up.shape[0], g_up.shape[0])
    return _run(_col_mix_steps(pr, pk, pv, qr, qk, qv, mu_r, mu_k, mu_v, *lo, w0, w_up, a0, a_up, g_up,
                               k_k, k_a))


def _rwkv_out_steps(o, r, k, v, gate, lnx_g, lnx_b, r_k):
    ones = _head_ones()
    mu = _head_sum(o, ones) * (1.0 / HEAD)
    yield
    oc = o - mu
    var = _head_sum(oc * oc, ones) * (1.0 / HEAD)
    yield
    y = oc * lax.rsqrt(var + GN_EPS) * lnx_g + lnx_b
    yield
    bonus = _head_sum(r * k * r_k, ones)
    yield
    return (y + bonus * v) * gate


def _rwkv_out(*args):
    return _run(_rwkv_out_steps(*args))


def _each(fn, *lists):
    out = []
    for args in zip(*lists):
        out.append(fn(*args))
        yield
    return out


def _run_together(work):
    values = [None] * len(work)
    longest = max(n for _, n in work)
    credit = [0.0] * len(work)
    live = set(range(len(work)))
    while live:
        for i, (steps, n) in enumerate(work):
            credit[i] += n / longest
            while i in live and credit[i] >= 1.0:
                credit[i] -= 1.0
                try:
                    next(steps)
                except StopIteration as done:
                    values[i] = done.value
                    live.discard(i)
    return values


def _shifted(p, carry_ref):
    rows = lax.broadcasted_iota(jnp.int32, p.shape, 0)
    q = jnp.where(rows == 0, carry_ref[...], pltpu.roll(p, 1, axis=0))
    carry_ref[...] = p[p.shape[0] - 1:, :]
    return q


def _prep_sample_kernel(*refs):
    ins, outs = refs[:-7], refs[-7:]
    for ref, val in zip(outs, _rwkv_mix(*[ref[...] for ref in ins])):
        ref[...] = val


class _Layout:
    def __init__(self, d_rwkv, d_sgu, lora_w, lora_a, lora_g, tn):
        self.d_rwkv, self.d_sgu, self.tn = d_rwkv, d_sgu, tn
        self.wa_w = lora_w + lora_a
        self.gl_w = -(-lora_g // LANES) * LANES
        self.d_shift = 3 * d_rwkv + self.wa_w + lora_g
        assert self.wa_w == LANES and self.wa_w + self.gl_w <= tn
        assert d_rwkv % tn == 0 and d_sgu % tn == 0
        self.u0 = 0
        self.vs0 = d_sgu
        self.r0 = 2 * d_sgu
        self.k0 = self.r0 + d_rwkv
        self.v0 = self.k0 + d_rwkv
        self.lo0 = self.v0 + d_rwkv
        self.width = self.lo0 + tn
        self.src_rows = (list(range(self.d_shift, self.d_shift + 2 * d_sgu, tn))
                         + list(range(0, 3 * d_rwkv + tn, tn)))

    def rw_pieces(self, a):
        d = self.d_rwkv
        pad = [(0, 0)] * (a.ndim - 1) + [(0, self.tn - (self.d_shift - 3 * d))]
        return a[..., :d], a[..., d:2 * d], a[..., 2 * d:3 * d], jnp.pad(a[..., 3 * d:], pad)

    def shift_row(self, p_rows):
        return p_rows[:, self.r0:self.r0 + self.d_shift]


def _prep_weights(lay, mu, w0, w_up, a0, a_up, g_up, k_k, k_a):
    d = lay.d_rwkv
    lora_w, lora_g = w_up.shape[0], g_up.shape[0]
    mus = [m[None, :] for m in lay.rw_pieces(mu)]
    w_up_p = jnp.pad(w_up, ((0, lay.wa_w - lora_w), (0, 0)))
    a_up_p = jnp.pad(a_up, ((lora_w, 0), (0, 0)))
    g_up_p = jnp.pad(g_up, ((0, lay.gl_w - lora_g), (0, 0)))
    return mus + [w0[None, :], w_up_p, a0[None, :], a_up_p, g_up_p, k_k.reshape(1, d), k_a.reshape(1, d)]


def _full(a):
    return pl.BlockSpec(a.shape, lambda *_: (0,) * a.ndim)


def _prep_sample(p, prev, lay, weights):
    d = lay.d_rwkv
    m = p.shape[0]
    p_specs = [pl.BlockSpec((m, d), lambda i: (0, lay.r0 // d)),
               pl.BlockSpec((m, d), lambda i: (0, lay.k0 // d)),
               pl.BlockSpec((m, d), lambda i: (0, lay.v0 // d)),
               pl.BlockSpec((m, lay.tn), lambda i: (0, lay.lo0 // lay.tn))]
    prevs = list(lay.rw_pieces(prev))
    out_spec = pl.BlockSpec((m, d), lambda i: (0, 0))
    return pl.pallas_call(
        _prep_sample_kernel,
        grid=(1,),
        in_specs=p_specs + [_full(q) for q in prevs] + [_full(w) for w in weights],
        out_specs=[out_spec] * 7,
        out_shape=[jax.ShapeDtypeStruct((m, d), F32)] * 7,
        compiler_params=_params(("arbitrary",)),
        name="rwkv_prep_sample",
    )(p, p, p, p, *prevs, *weights)


def _pair_stack(x, first_head):
    zero = jnp.zeros_like(x)
    return jnp.concatenate([jnp.where(first_head, x, zero), jnp.where(first_head, zero, x)], axis=0)


def _wkv_pre_steps(r, lw, k, v, a, b, tri, gram_mask):
    c = WKV_CHUNK
    n_pairs = r.shape[1] // LANES
    lane = lax.broadcasted_iota(jnp.int32, (c, LANES), 1)
    h0 = lane < HEAD
    h0x2 = jnp.concatenate([h0, h0], axis=1)

    def cut(x):
        return [x[:, p * LANES:(p + 1) * LANES] for p in range(n_pairs)]

    cum = _dot_exact_lhs(tri, lw)
    yield
    e_out = jnp.exp(-cum)
    a_s = cut(a * jnp.exp(cum - lw))
    r_s = cut(r * jnp.exp(cum))
    yield
    b_s = cut(b * e_out)
    k_s = cut(k * e_out)
    yield
    last = cum[c - 1:, :]
    e_end = jnp.exp(last - cum)
    bk_e = cut(jnp.concatenate([b * e_end, k * e_end], axis=0))
    decay = cut(jnp.exp(last))
    vs = cut(v)
    yield

    grams = yield from _each(lambda ai, ri, bi, ki: jnp.where(
        gram_mask, _dot_nt(jnp.concatenate([ai, ri], axis=0),
                           jnp.concatenate([_pair_stack(bi, h0), _pair_stack(ki, h0)], axis=0)), 0.0),
        a_s, r_s, b_s, k_s)
    v_st = [_pair_stack(x, h0) for x in vs]
    kvs = yield from _each(lambda g, vi: _dot(g[:, LANES:], vi), grams, v_st)
    xs = [jnp.concatenate([ai, kv[:c]], axis=1) for ai, kv in zip(a_s, kvs)]
    pws = [g[:c, :LANES] for g in grams]
    ns = pws
    pws = yield from _each(lambda pw: _dot(pw, _pair_stack(pw, h0)), pws)
    n = 2
    while n < c // 2:
        both = yield from _each(
            lambda pw, nn: _dot(jnp.concatenate([pw, nn], axis=0), _pair_stack(pw, h0)), pws, ns)
        ns = [nn + pw + bo[c:] for nn, pw, bo in zip(ns, pws, both)]
        pws = [bo[:c] for bo in both]
        n *= 2
    ns = yield from _each(lambda nn, pw: nn + pw + _dot(nn, _pair_stack(pw, h0)), ns, pws)
    xs = yield from _each(lambda x, nn: x + _dot(nn, _pair_stack(x, h0x2)), xs, ns)
    qos = yield from _each(lambda g, x: _dot(g[c:, :LANES], _pair_stack(x, h0x2)), grams, xs)
    qp = [jnp.concatenate([ri + qo[:, :LANES], x[:, :LANES]], axis=0) for ri, qo, x in zip(r_s, qos, xs)]
    o2 = [qo[:, LANES:] + kv[c:] for qo, kv in zip(qos, kvs)]
    return dict(qp=qp, o2=o2, u2=[x[:, LANES:] for x in xs], v=vs, bk_e=bk_e, decay=decay)


def _wkv_state_steps(pre, states, state_mask):
    c = WKV_CHUNK
    ous = yield from _each(_dot_nt, pre["qp"], states)
    upds = yield from _each(lambda ou, u2, v, bk: _dot_tn(jnp.concatenate([ou[c:] + u2, v], axis=0), bk),
                            ous, pre["u2"], pre["v"], pre["bk_e"])
    for p, (upd, decay) in enumerate(zip(upds, pre["decay"])):
        states[p] = states[p] * decay + jnp.where(state_mask, upd, 0.0)
    return [ou[:c] + o2 for ou, o2 in zip(ous, pre["o2"])]


def _mixer_prompt_kernel(*refs):
    x_ref, g1_ref, wrw_ref, wsg_ref = refs[:4]
    mix_refs, out_refs, sgu_refs = refs[4:15], refs[15:18], refs[18:22]
    y_ref, yb_ref, sf_ref, sh_ref = refs[22:26]
    carries, s_ref, o_scr, p_scr, sg_scr = refs[26:30], refs[30], refs[31], refs[32], refs[33]
    c = WKV_CHUNK
    tb = y_ref.shape[0]
    d = y_ref.shape[1]
    n_pairs = d // LANES
    t = pl.program_id(1)
    first = t == 0
    last = t == pl.num_programs(1) - 1

    @pl.when(first)
    def _():
        s_ref[...] = jnp.zeros_like(s_ref)
        for carry in carries:
            carry[...] = jnp.zeros_like(carry)

    h = _rms(x_ref[...], g1_ref[...]).astype(BF16)
    p_scr[...] = _dot_nt(h, wrw_ref[...])
    p_refs = [p_scr.at[:, j * d:(j + 1) * d] for j in range(3)] + [p_scr.at[:, 3 * d:]]

    @pl.when(last)
    def _():
        sh_ref[0] = p_scr[tb - 1:, :]

    def sg_proj_steps():
        n = wsg_ref.shape[0]
        for j in range(0, n, 512):
            sg_scr[:, j:j + 512] = _dot_nt(h, wsg_ref[j:j + 512, :])
            yield

    sg_ng, sg_nb, sg_w, sg_bias = sgu_refs
    sri = lax.broadcasted_iota(jnp.int32, (SGU_CHUNK, SGU_CHUNK), 0)
    sci = lax.broadcasted_iota(jnp.int32, (SGU_CHUNK, SGU_CHUNK), 1)
    sg_ws = [jnp.where(sci <= sri, sg_w[g], 0.0).astype(BF16) for g in range(sg_w.shape[0])]

    def sgu_steps(rows):
        u = _gelu(sg_scr[rows, :d])
        yield
        vs = _layernorm(_gelu(sg_scr[rows, d:]), sg_ng[...], sg_nb[...]).astype(BF16)
        yield
        for g, w in enumerate(sg_ws):
            cols = slice(g * SGU_CHUNK, (g + 1) * SGU_CHUNK)
            mix = jnp.dot(w, vs[:, cols], preferred_element_type=F32) + sg_bias[:, cols]
            yb_ref[rows, cols] = (u[:, cols] * mix).astype(yb_ref.dtype)
            yield

    mu_r, mu_k, mu_v, mu_l, w0, w_up, a0, a_up, g_up, k_k, k_a = mix_refs
    ri = lax.broadcasted_iota(jnp.int32, (c, c), 0)
    ci = lax.broadcasted_iota(jnp.int32, (c, c), 1)
    tri = (ci <= ri).astype(F32)
    gr = lax.broadcasted_iota(jnp.int32, (2 * c, 2 * LANES), 0)
    gc = lax.broadcasted_iota(jnp.int32, (2 * c, 2 * LANES), 1) % c
    gram_mask = gc <= jnp.where(gr < c, gr - 1, gr - c)
    sr = lax.broadcasted_iota(jnp.int32, (LANES, LANES), 0) // HEAD
    sc = lax.broadcasted_iota(jnp.int32, (LANES, LANES), 1) // HEAD
    state_mask = sr == sc
    states = [s_ref[p] for p in range(n_pairs)]

    def prep_steps(rows):
        pl_ = p_refs[3][rows, :]
        lo = _lo_mix(pl_, _shifted(pl_, carries[3]), mu_l[...], w_up.shape[0], g_up.shape[0])
        yield
        ps = [ref[rows, :] for ref in p_refs[:3]]
        yield
        qs = [_shifted(p, carry) for p, carry in zip(ps, carries[:3])]
        yield
        vals = yield from _col_mix_steps(*ps, *qs, mu_r[...], mu_k[...], mu_v[...], *lo, w0[...], w_up[...],
                                         a0[...], a_up[...], g_up[...], k_k[...], k_a[...])
        return vals

    def state_steps(rows, pre):
        outs = yield from _wkv_state_steps(pre, states, state_mask)
        o_scr[rows, :] = jnp.concatenate(outs, axis=1)

    def post_steps(rows, vals):
        r, _, k, v, _, _, gate = vals
        y = yield from _rwkv_out_steps(o_scr[rows, :], r, k, v, gate, *[ref[...] for ref in out_refs])
        y_ref[rows, :] = y.astype(y_ref.dtype)

    n_chunks = tb // c
    rows = [slice(j * c, (j + 1) * c) for j in range(n_chunks)]
    n_stages = n_chunks + 3
    sgu_at = {min(2 + 2 * i, n_stages - 1): i for i in range(tb // SGU_CHUNK)}
    vals, pre = {}, {}
    for s in range(n_stages):
        work = []
        if s == 0:
            work.append(("sg_proj", 0, sg_proj_steps(), wsg_ref.shape[0] // 512))
        if s in sgu_at:
            i = sgu_at[s]
            work.append(("sgu", i, sgu_steps(slice(i * SGU_CHUNK, (i + 1) * SGU_CHUNK)), 2 + len(sg_ws)))
        if s < n_chunks:
            work.append(("prep", s, prep_steps(rows[s]), 12))
        if 0 <= s - 1 < n_chunks:
            work.append(("pre", s - 1, _wkv_pre_steps(*vals[s - 1][:6], tri, gram_mask), 10 * n_pairs + 5))
        if 0 <= s - 2 < n_chunks:
            work.append(("state", s - 2, state_steps(rows[s - 2], pre[s - 2]), 2 * n_pairs + 1))
        if 0 <= s - 3 < n_chunks:
            work.append(("post", s - 3, post_steps(rows[s - 3], vals[s - 3]), 5))
        done = _run_together([(steps, n) for _, _, steps, n in work])
        for (kind, j, _, _), value in zip(work, done):
            if kind == "prep":
                vals[j] = value
            elif kind == "pre":
                pre[j] = value
    for p, s in enumerate(states):
        s_ref[p] = s

    @pl.when(t == pl.num_programs(1) - 1)
    def _():
        for p in range(n_pairs):
            s = s_ref[p]
            sf_ref[0, 2 * p] = s[:HEAD, :HEAD]
            sf_ref[0, 2 * p + 1] = s[HEAD:, HEAD:]


def _resident(a):
    return pl.BlockSpec(a.shape, lambda *_: (0,) * a.ndim, pipeline_mode=pl.Buffered(1))


def _mixer_prompt(x, norm_g, wt_rw, wt_sg, lay, mix_weights, out_weights, sgu_weights, batch, seq, tb):
    d = lay.d_rwkv
    d_model = x.shape[1]
    assert lay.d_sgu == d and wt_rw.shape[0] == 3 * d + lay.tn and wt_sg.shape[0] == 2 * d
    n_heads = d // HEAD
    nt = seq // tb
    row = lambda b, i: (b * nt + i, 0)
    weights = list(mix_weights) + list(out_weights) + list(sgu_weights)
    y_spec = pl.BlockSpec((tb, d), row)
    return pl.pallas_call(
        _mixer_prompt_kernel,
        grid=(batch, nt),
        in_specs=([pl.BlockSpec((tb, d_model), row), _full(norm_g), _resident(wt_rw), _resident(wt_sg)]
                  + [_full(w) for w in weights]),
        out_specs=[y_spec, y_spec,
                   pl.BlockSpec((1, n_heads, HEAD, HEAD), lambda b, i: (b, 0, 0, 0)),
                   pl.BlockSpec((1, 1, wt_rw.shape[0]), lambda b, i: (b, 0, 0))],
        out_shape=[jax.ShapeDtypeStruct((batch * seq, d), BF16),
                   jax.ShapeDtypeStruct((batch * seq, d), BF16),
                   jax.ShapeDtypeStruct((batch, n_heads, HEAD, HEAD), F32),
                   jax.ShapeDtypeStruct((batch, 1, wt_rw.shape[0]), F32)],
        scratch_shapes=([pltpu.VMEM((1, d), F32)] * 3
                        + [pltpu.VMEM((1, lay.tn), F32), pltpu.VMEM((d // LANES, LANES, LANES), F32),
                           pltpu.VMEM((tb, d), F32), pltpu.VMEM((tb, wt_rw.shape[0]), F32),
                           pltpu.VMEM((tb, wt_sg.shape[0]), F32)]),
        compiler_params=_params(("parallel", "arbitrary")),
        name="mixer_prompt",
    )(x, norm_g, wt_rw, wt_sg, *weights)


STEP_UNROLL = 4


def _wkv_step_kernel(r_ref, lw_ref, k_ref, v_ref, a_ref, b_ref, s_ref, o_ref, sn_ref):
    a, b, k, r = a_ref[...], b_ref[...], k_ref[...], r_ref[...]
    w = jnp.exp(lw_ref[...])

    def body(j, carry):
        for u in range(STEP_UNROLL):
            i = j * STEP_UNROLL + u
            s = s_ref[0, i]
            sa = jnp.sum(s * a, axis=0, keepdims=True)
            s = s * w + sa * b + v_ref[pl.ds(i, 1), :] * k
            sn_ref[0, i] = s
            o_ref[pl.ds(i, 1), :] = jnp.sum(s * r, axis=0, keepdims=True)
        return carry

    lax.fori_loop(0, s_ref.shape[1] // STEP_UNROLL, body, 0)


def _wkv_step(r, lw, k, v, a, b, state):
    m, d = r.shape
    n_heads = d // HEAD
    vec = pl.BlockSpec((HEAD, m), lambda h: (h, 0))
    st = pl.BlockSpec((1, HEAD, HEAD, m), lambda h: (h, 0, 0, 0))
    o, s = pl.pallas_call(
        _wkv_step_kernel,
        grid=(n_heads,),
        in_specs=[vec] * 6 + [st],
        out_specs=[vec, st],
        out_shape=[jax.ShapeDtypeStruct((d, m), F32), jax.ShapeDtypeStruct((n_heads, HEAD, HEAD, m), F32)],
        compiler_params=_params(("parallel",)),
        name="wkv_step",
    )(*[t.T for t in (r, lw, k, v, a, b)], jnp.transpose(state, (1, 2, 3, 0)))
    return o.T, jnp.transpose(s, (3, 0, 1, 2))


def _post_kernel(*refs):
    y_ref = refs[-1]
    y_ref[...] = _rwkv_out(*[ref[...] for ref in refs[:-1]]).astype(y_ref.dtype)


def _rwkv_post(o, r, k, v, g, out_weights):
    m, d = o.shape
    spec = pl.BlockSpec((m, d), lambda i: (0, 0))
    return pl.pallas_call(
        _post_kernel,
        grid=(1,),
        in_specs=[spec] * 5 + [_full(w) for w in out_weights],
        out_specs=spec,
        out_shape=jax.ShapeDtypeStruct((m, d), BF16),
        compiler_params=_params(("arbitrary",)),
        name="rwkv_post",
    )(o, r, k, v, g, *out_weights)


def _gelu(x):
    return 0.5 * x * (1.0 + jnp.tanh(GELU_C * (x + 0.044715 * (x * x * x))))


def _layernorm(x, g, b):
    mu = jnp.mean(x, axis=-1, keepdims=True)
    xc = x - mu
    var = jnp.mean(xc * xc, axis=-1, keepdims=True)
    return xc * lax.rsqrt(var + LN_EPS) * g + b


def _sgu_prompt_kernel(pu_ref, pv_ref, ng_ref, nb_ref, w_ref, bias_ref, y_ref, *, n_chunks):
    c = SGU_CHUNK
    ri = lax.broadcasted_iota(jnp.int32, (c, c), 0)
    ci = lax.broadcasted_iota(jnp.int32, (c, c), 1)
    causal = ci <= ri
    n_groups = w_ref.shape[0]
    ws = [jnp.where(causal, w_ref[g], 0.0).astype(BF16) for g in range(n_groups)]
    for j in range(n_chunks):
        rows = pl.ds(j * c, c)
        u = _gelu(pu_ref[rows, :])
        vs = _layernorm(_gelu(pv_ref[rows, :]), ng_ref[...], nb_ref[...]).astype(BF16)
        for g in range(n_groups):
            cols = slice(g * c, (g + 1) * c)
            mix = jnp.dot(ws[g], vs[:, cols], preferred_element_type=F32) + bias_ref[:, cols]
            y_ref[rows, cols] = (u[:, cols] * mix).astype(y_ref.dtype)


def _sgu_prompt(p, lay, norm_g, norm_b, sgu_w, sgu_b, batch, seq, ts):
    d = lay.d_sgu
    nt = seq // ts
    bias = jnp.repeat(sgu_b.T, SGU_CHUNK, axis=1)
    return pl.pallas_call(
        functools.partial(_sgu_prompt_kernel, n_chunks=ts // SGU_CHUNK),
        grid=(batch * nt,),
        in_specs=[pl.BlockSpec((ts, d), lambda i: (i, lay.u0 // d)),
                  pl.BlockSpec((ts, d), lambda i: (i, lay.vs0 // d)),
                  pl.BlockSpec((1, d), lambda i: (0, 0)),
                  pl.BlockSpec((1, d), lambda i: (0, 0)),
                  _full(sgu_w), _full(bias)],
        out_specs=pl.BlockSpec((ts, d), lambda i: (i, 0)),
        out_shape=jax.ShapeDtypeStruct((batch * seq, d), BF16),
        compiler_params=_params(("parallel",)),
        name="sgu_prompt",
    )(p, p, norm_g[None, :], norm_b[None, :], sgu_w, bias)


def _sgu_sample_kernel(pu_ref, pv_ref, ng_ref, nb_ref, w_ref, bias_ref, y_ref, vs_ref):
    u = _gelu(pu_ref[...])
    vs = _layernorm(_gelu(pv_ref[...]), ng_ref[...], nb_ref[...])
    vs_ref[...] = vs
    y_ref[...] = (u * (w_ref[...] * vs + bias_ref[...])).astype(y_ref.dtype)


def _sgu_sample(p, lay, norm_g, norm_b, sgu_w, sgu_b):
    d = lay.d_sgu
    m = p.shape[0]
    w0 = jnp.repeat(sgu_w[:, 0, 0], SGU_CHUNK)[None, :]
    b0 = jnp.repeat(sgu_b[:, 0], SGU_CHUNK)[None, :]
    vec = pl.BlockSpec((1, d), lambda i: (0, 0))
    out = pl.BlockSpec((m, d), lambda i: (0, 0))
    return pl.pallas_call(
        _sgu_sample_kernel,
        grid=(1,),
        in_specs=[pl.BlockSpec((m, d), lambda i: (0, lay.u0 // d)),
                  pl.BlockSpec((m, d), lambda i: (0, lay.vs0 // d)), vec, vec, vec, vec],
        out_specs=[out, out],
        out_shape=[jax.ShapeDtypeStruct((m, d), BF16), jax.ShapeDtypeStruct((m, d), F32)],
        compiler_params=_params(("arbitrary",)),
        name="sgu_sample",
    )(p, p, norm_g[None, :], norm_b[None, :], w0, b0)


def _out_proj_kernel(x_ref, ya_ref, yb_ref, wa_ref, wb_ref, o_ref):
    o_ref[...] = (x_ref[...] + jnp.dot(ya_ref[...], wa_ref[...].astype(BF16), preferred_element_type=F32)
                  + jnp.dot(yb_ref[...], wb_ref[...].astype(BF16), preferred_element_type=F32))


def _out_proj(x, ya, yb, w, tm, tn):
    m, d = x.shape
    da = ya.shape[1]
    return pl.pallas_call(
        _out_proj_kernel,
        grid=(m // tm, d // tn),
        in_specs=[pl.BlockSpec((tm, tn), lambda i, j: (i, j)),
                  pl.BlockSpec((tm, da), lambda i, j: (i, 0)),
                  pl.BlockSpec((tm, da), lambda i, j: (i, 0)),
                  pl.BlockSpec((da, tn), lambda i, j: (0, j)),
                  pl.BlockSpec((da, tn), lambda i, j: (1, j))],
        out_specs=pl.BlockSpec((tm, tn), lambda i, j: (i, j)),
        out_shape=jax.ShapeDtypeStruct((m, d), F32),
        compiler_params=_params(("parallel", "arbitrary")),
        name="out_proj",
    )(x, ya, yb, w, w)


def _ffn_kernel(x_ref, xs_ref, g2_ref, wu_ref, wd_ref, gf_ref, o_ref, os_ref, h_ref, a_ref):
    f = pl.program_id(1)
    tm = x_ref.shape[0]

    @pl.when(f == 0)
    def _():
        x, xs = x_ref[...], xs_ref[...]
        h_ref[:tm, :] = _rms(x, g2_ref[...]).astype(BF16)
        h_ref[tm:, :] = _rms(xs, g2_ref[...]).astype(BF16)
        o_ref[...] = x
        os_ref[...] = xs
        a_ref[...] = jnp.zeros_like(a_ref)

    down = jnp.dot(a_ref[...], wd_ref[...].astype(BF16), preferred_element_type=F32)
    up = jnp.dot(h_ref[...], wu_ref[...].astype(BF16), preferred_element_type=F32)
    o_ref[...] += down[:tm]
    os_ref[...] += down[tm:]
    a_ref[...] = jnp.square(jnp.maximum(up, 0.0)).astype(BF16)

    @pl.when(f == pl.num_programs(1) - 1)
    def _():
        o_ref[...] = _rms(o_ref[...], gf_ref[...])
        os_ref[...] = _rms(os_ref[...], gf_ref[...])


def _ffn(x, xs, g2, w_up, w_down, gf, tm, tf):
    m, d = x.shape
    n_blocks = m // tm
    ts = xs.shape[0] // n_blocks
    assert ts * n_blocks == xs.shape[0] and ts % 8 == 0
    nf = w_up.shape[1] // tf
    return pl.pallas_call(
        _ffn_kernel,
        grid=(n_blocks, nf + 1),
        in_specs=[pl.BlockSpec((tm, d), lambda i, f: (i, 0)),
                  pl.BlockSpec((ts, d), lambda i, f: (i, 0)),
                  pl.BlockSpec((1, d), lambda i, f: (0, 0)),
                  pl.BlockSpec((d, tf), lambda i, f: (0, jnp.minimum(f, nf - 1))),
                  pl.BlockSpec((tf, d), lambda i, f: (jnp.maximum(f - 1, 0), 0)),
                  pl.BlockSpec((1, d), lambda i, f: (0, 0))],
        out_specs=[pl.BlockSpec((tm, d), lambda i, f: (i, 0)),
                   pl.BlockSpec((ts, d), lambda i, f: (i, 0))],
        out_shape=[jax.ShapeDtypeStruct((m, d), F32), jax.ShapeDtypeStruct(xs.shape, F32)],
        scratch_shapes=[pltpu.VMEM((tm + ts, d), BF16), pltpu.VMEM((tm + ts, tf), BF16)],
        compiler_params=_params(("parallel", "arbitrary")),
        name="ffn",
    )(x, xs, g2[None, :], w_up, w_down, gf[None, :])


def _row_tile(m, cap):
    t = min(m, cap)
    assert m % t == 0
    return t


def kernel(x_prompt, x_sample, state_wkv, state_shift, norm1_g, w_in, mu_shift, w0, w_up, a0, a_up, g_up,
           k_k, k_a, r_k, lnx_g, lnx_b, sgu_norm_g, sgu_norm_b, sgu_w, sgu_b, w_out, norm2_g, w_ffn_up,
           w_ffn_down, norm_f_g):
    batch, seq, d_model = x_prompt.shape
    n_dec, dec_seq, _ = x_sample.shape
    depth = w_in.shape[0]
    assert depth == 1 and dec_seq == 1
    d_rwkv = w0.shape[1]
    d_sgu = sgu_norm_g.shape[1]
    tn_in = 512
    lay = _Layout(d_rwkv, d_sgu, w_up.shape[1], a_up.shape[1], g_up.shape[1], tn_in)
    w_in_t = w_in[0].T.astype(BF16)
    prep_w = _prep_weights(lay, mu_shift[0], w0[0], w_up[0], a0[0], a_up[0], g_up[0], k_k[0], k_a[0])

    out_w = [w.reshape(1, d_rwkv) for w in (lnx_g[0], lnx_b[0], r_k[0])]

    w_out_b = w_out[0].astype(BF16)

    xp = x_prompt.reshape(batch * seq, d_model)
    sgu_bias = jnp.repeat(sgu_b[0].T, SGU_CHUNK, axis=1)
    ya, yb, wkv_p, last_p = _mixer_prompt(
        xp, norm1_g, w_in_t[:3 * d_rwkv + tn_in], w_in_t[lay.d_shift:], lay, prep_w, out_w,
        [sgu_norm_g, sgu_norm_b, sgu_w[0], sgu_bias], batch, seq, 256)
    x1p = _out_proj(xp, ya, yb, w_out_b, _row_tile(batch * seq, 2048), 512)
    shift_p = last_p[:, 0, :lay.d_shift]

    xs = x_sample.reshape(n_dec, d_model)
    ps = _in_proj(xs, norm1_g, w_in_t, lay.src_rows, n_dec, tn_in)
    r, lw, k, v, aa, bb, g = _prep_sample(ps, state_shift[0], lay, prep_w)
    o, wkv_s = _wkv_step(r, lw, k, v, aa, bb, state_wkv[0])
    yb, vs = _sgu_sample(ps, lay, sgu_norm_g[0], sgu_norm_b[0], sgu_w[0], sgu_b[0])
    ya = _rwkv_post(o, r, k, v, g, out_w)
    x1s = _out_proj(xs, ya, yb, w_out_b, n_dec, 512)
    shift_s = lay.shift_row(ps)

    y_prompt, y_sample = _ffn(x1p, x1s, norm2_g[0], w_ffn_up[0], w_ffn_down[0], norm_f_g,
                              _row_tile(batch * seq, 1024), 256)
    y_prompt = y_prompt.reshape(batch, seq, d_model)
    y_sample = y_sample.reshape(n_dec, 1, d_model)
    return (y_prompt, y_sample, wkv_p[None], shift_p[None], wkv_s[None], shift_s[None],
            vs.reshape(1, n_dec, 1, d_sgu))
```

```python
import functools
import math

import jax
import jax.numpy as jnp
from jax import lax
from jax.experimental import pallas as pl
from jax.experimental.pallas import tpu as pltpu

F32 = jnp.float32
BF16 = jnp.bfloat16

HEAD = 64
LANES = 128
SGU_CHUNK = 128
WKV_CHUNK = 64
PROJ_COLS = 512
RMS_EPS = 1e-5
LN_EPS = 1e-5
GN_EPS = 64e-5
DECAY_SCALE = math.exp(-0.5)
GELU_C = math.sqrt(2.0 / math.pi)
VMEM_LIMIT = 58 * 1024 * 1024


def _params(sem):
    return pltpu.CompilerParams(dimension_semantics=sem, vmem_limit_bytes=VMEM_LIMIT)


def _dot(a, b):
    return jnp.dot(a.astype(BF16), b.astype(BF16), preferred_element_type=F32)


def _dot_nt(a, b):
    return lax.dot_general(a.astype(BF16), b.astype(BF16), (((1,), (1,)), ((), ())),
                           preferred_element_type=F32)


def _dot_tn(a, b):
    return lax.dot_general(a.astype(BF16), b.astype(BF16), (((0,), (0,)), ((), ())),
                           preferred_element_type=F32)


def _split3(x):
    hi = x.astype(BF16)
    r1 = x - hi.astype(F32)
    mid = r1.astype(BF16)
    lo = (r1 - mid.astype(F32)).astype(BF16)
    return hi, mid, lo


def _dot_exact_lhs(m, x):
    hi, mid, lo = _split3(x)
    mb = m.astype(BF16)
    return (jnp.dot(mb, hi, preferred_element_type=F32) + jnp.dot(mb, mid, preferred_element_type=F32)
            + jnp.dot(mb, lo, preferred_element_type=F32))


def _sigmoid(x):
    return 1.0 / (1.0 + jnp.exp(-x))


def _head_ones():
    r = lax.broadcasted_iota(jnp.int32, (2 * LANES, LANES), 0) % LANES // HEAD
    c = lax.broadcasted_iota(jnp.int32, (2 * LANES, LANES), 1) // HEAD
    return (r == c).astype(BF16)


def _head_sum(x, ones):
    parts = []
    for s in range(0, x.shape[1], LANES):
        xs = x[:, s:s + LANES]
        hi = xs.astype(BF16)
        lo = (xs - hi.astype(F32)).astype(BF16)
        parts.append(jnp.dot(jnp.concatenate([hi, lo], axis=1), ones, preferred_element_type=F32))
    return parts[0] if len(parts) == 1 else jnp.concatenate(parts, axis=1)


def _rms(x, g):
    return x * lax.rsqrt(jnp.mean(x * x, axis=-1, keepdims=True) + RMS_EPS) * g


def _in_proj_kernel(rows_ref, x_ref, g_ref, wt_ref, o_ref, h_ref):
    del rows_ref
    @pl.when(pl.program_id(1) == 0)
    def _():
        h_ref[...] = _rms(x_ref[...], g_ref[...]).astype(BF16)

    o_ref[...] = _dot_nt(h_ref[...], wt_ref[...])


def _in_proj(x, g, wt, src_rows, tm, tn):
    m, d = x.shape
    n_blocks = len(src_rows)
    return pl.pallas_call(
        _in_proj_kernel,
        grid_spec=pltpu.PrefetchScalarGridSpec(
            num_scalar_prefetch=1,
            grid=(m // tm, n_blocks),
            in_specs=[pl.BlockSpec((tm, d), lambda i, j, rows: (i, 0)),
                      pl.BlockSpec((1, d), lambda i, j, rows: (0, 0)),
                      pl.BlockSpec((pl.Element(tn), pl.Element(d)), lambda i, j, rows: (pl.multiple_of(rows[j], 8), 0))],
            out_specs=pl.BlockSpec((tm, tn), lambda i, j, rows: (i, j)),
            scratch_shapes=[pltpu.VMEM((tm, d), BF16)]),
        out_shape=jax.ShapeDtypeStruct((m, n_blocks * tn), F32),
        compiler_params=_params(("parallel", "arbitrary")),
        name="in_proj",
    )(jnp.asarray(src_rows, jnp.int32), x, g, wt)


def _lo_mix(pl_, ql, mu_l, n_wa, n_gl):
    n = n_wa + n_gl
    lo = pl_[:, :n]
    lo = lo + (ql[:, :n] - lo) * mu_l[:, :n]
    wa = lo[:, :n_wa]
    return jnp.tanh(wa), wa, _sigmoid(lo[:, n_wa:])


def _run(steps):
    try:
        while True:
            next(steps)
    except StopIteration as done:
        return done.value


def _col_mix_steps(pr, pk, pv, qr, qk, qv, mu_r, mu_k, mu_v, tanh_wa, wa, sig_gl, w0, w_up, a0, a_up, g_up,
                   k_k, k_a):
    r = pr + (qr - pr) * mu_r
    yield
    k = pk + (qk - pk) * mu_k
    yield
    v = pv + (qv - pv) * mu_v
    yield
    lw = -DECAY_SCALE * _sigmoid(w0 + _dot(tanh_wa, w_up))
    yield
    a = _sigmoid(a0 + _dot(wa, a_up))
    yield
    gate = _dot(sig_gl, g_up)
    yield
    kk = k * k_k
    ss = _head_sum(kk * kk, _head_ones())
    yield
    kk = kk / jnp.maximum(jnp.sqrt(ss), 1e-12)
    yield
    return r, lw, k * (1.0 + (a - 1.0) * k_a), v, -kk, kk * a, gate


def _rwkv_mix(pr, pk, pv, pl_, qr, qk, qv, ql, mu_r, mu_k, mu_v, mu_l, w0, w_up, a0, a_up, g_up, k_k, k_a):
    lo = _lo_mix(pl_, ql, mu_l, w_up.shape[0], g_up.shape[0])
    return _run(_col_mix_steps(pr, pk, pv, qr, qk, qv, mu_r, mu_k, mu_v, *lo, w0, w_up, a0, a_up, g_up,
                               k_k, k_a))


def _rwkv_out_steps(o, r, k, v, gate, lnx_g, lnx_b, r_k):
    ones = _head_ones()
    mu = _head_sum(o, ones) * (1.0 / HEAD)
    yield
    oc = o - mu
    var = _head_sum(oc * oc, ones) * (1.0 / HEAD)
    yield
    y = oc * lax.rsqrt(var + GN_EPS) * lnx_g + lnx_b
    yield
    bonus = _head_sum(r * k * r_k, ones)
    yield
    return (y + bonus * v) * gate


def _rwkv_out(*args):
    return _run(_rwkv_out_steps(*args))


def _each(fn, *lists):
    out = []
    for args in zip(*lists):
        out.append(fn(*args))
        yield
    return out


def _run_together(work):
    values = [None] * len(work)
    longest = max(n for _, n in work)
    credit = [0.0] * len(work)
    live = set(range(len(work)))
    while live:
        for i, (steps, n) in enumerate(work):
            credit[i] += n / longest
            while i in live and credit[i] >= 1.0:
                credit[i] -= 1.0
                try:
                    next(steps)
                except StopIteration as done:
                    values[i] = done.value
                    live.discard(i)
    return values


def _shifted(p, carry_ref):
    rows = lax.broadcasted_iota(jnp.int32, p.shape, 0)
    q = jnp.where(rows == 0, carry_ref[...], pltpu.roll(p, 1, axis=0))
    carry_ref[...] = p[p.shape[0] - 1:, :]
    return q


def _prep_sample_kernel(*refs):
    ins, outs = refs[:-7], refs[-7:]
    for ref, val in zip(outs, _rwkv_mix(*[ref[...] for ref in ins])):
        ref[...] = val


class _Layout:
    def __init__(self, d_rwkv, d_sgu, lora_w, lora_a, lora_g, tn):
        self.d_rwkv, self.d_sgu, self.tn = d_rwkv, d_sgu, tn
        self.wa_w = lora_w + lora_a
        self.gl_w = -(-lora_g // LANES) * LANES
        self.d_shift = 3 * d_rwkv + self.wa_w + lora_g
        assert self.wa_w == LANES and self.wa_w + self.gl_w <= tn
        assert d_rwkv % tn == 0 and d_sgu % tn == 0
        self.u0 = 0
        self.vs0 = d_sgu
        self.r0 = 2 * d_sgu
        self.k0 = self.r0 + d_rwkv
        self.v0 = self.k0 + d_rwkv
        self.lo0 = self.v0 + d_rwkv
        self.width = self.lo0 + tn
        self.src_rows = (list(range(self.d_shift, self.d_shift + 2 * d_sgu, tn))
                         + list(range(0, 3 * d_rwkv + tn, tn)))

    def rw_pieces(self, a):
        d = self.d_rwkv
        pad = [(0, 0)] * (a.ndim - 1) + [(0, self.tn - (self.d_shift - 3 * d))]
        return a[..., :d], a[..., d:2 * d], a[..., 2 * d:3 * d], jnp.pad(a[..., 3 * d:], pad)

    def shift_row(self, p_rows):
        return p_rows[:, self.r0:self.r0 + self.d_shift]


def _prep_weights(lay, mu, w0, w_up, a0, a_up, g_up, k_k, k_a):
    d = lay.d_rwkv
    lora_w, lora_g = w_up.shape[0], g_up.shape[0]
    mus = [m[None, :] for m in lay.rw_pieces(mu)]
    w_up_p = jnp.pad(w_up, ((0, lay.wa_w - lora_w), (0, 0)))
    a_up_p = jnp.pad(a_up, ((lora_w, 0), (0, 0)))
    g_up_p = jnp.pad(g_up, ((0, lay.gl_w - lora_g), (0, 0)))
    return mus + [w0[None, :], w_up_p, a0[None, :], a_up_p, g_up_p, k_k.reshape(1, d), k_a.reshape(1, d)]


def _full(a):
    return pl.BlockSpec(a.shape, lambda *_: (0,) * a.ndim)


def _prep_sample(p, prev, lay, weights):
    d = lay.d_rwkv
    m = p.shape[0]
    p_specs = [pl.BlockSpec((m, d), lambda i: (0, lay.r0 // d)),
               pl.BlockSpec((m, d), lambda i: (0, lay.k0 // d)),
               pl.BlockSpec((m, d), lambda i: (0, lay.v0 // d)),
               pl.BlockSpec((m, lay.tn), lambda i: (0, lay.lo0 // lay.tn))]
    prevs = list(lay.rw_pieces(prev))
    out_spec = pl.BlockSpec((m, d), lambda i: (0, 0))
    return pl.pallas_call(
        _prep_sample_kernel,
        grid=(1,),
        in_specs=p_specs + [_full(q) for q in prevs] + [_full(w) for w in weights],
        out_specs=[out_spec] * 7,
        out_shape=[jax.ShapeDtypeStruct((m, d), F32)] * 7,
        compiler_params=_params(("arbitrary",)),
        name="rwkv_prep_sample",
    )(p, p, p, p, *prevs, *weights)


def _pair_stack(x, first_head):
    zero = jnp.zeros_like(x)
    return jnp.concatenate([jnp.where(first_head, x, zero), jnp.where(first_head, zero, x)], axis=0)


def _wkv_pre_steps(r, lw, k, v, a, b, tri, gram_mask):
    c = WKV_CHUNK
    n_pairs = r.shape[1] // LANES
    lane = lax.broadcasted_iota(jnp.int32, (c, LANES), 1)
    h0 = lane < HEAD
    h0x2 = jnp.concatenate([h0, h0], axis=1)

    def cut(x):
        return [x[:, p * LANES:(p + 1) * LANES] for p in range(n_pairs)]

    cum = _dot_exact_lhs(tri, lw)
    yield
    e_out = jnp.exp(-cum)
    a_s = cut(a * jnp.exp(cum - lw))
    r_s = cut(r * jnp.exp(cum))
    yield
    b_s = cut(b * e_out)
    k_s = cut(k * e_out)
    yield
    last = cum[c - 1:, :]
    e_end = jnp.exp(last - cum)
    bk_e = cut(jnp.concatenate([b * e_end, k * e_end], axis=0))
    decay = cut(jnp.exp(last))
    vs = cut(v)
    yield

    grams = yield from _each(lambda ai, ri, bi, ki: jnp.where(
        gram_mask, _dot_nt(jnp.concatenate([ai, ri], axis=0),
                           jnp.concatenate([_pair_stack(bi, h0), _pair_stack(ki, h0)], axis=0)), 0.0),
        a_s, r_s, b_s, k_s)
    v_st = [_pair_stack(x, h0) for x in vs]
    kvs = yield from _each(lambda g, vi: _dot(g[:, LANES:], vi), grams, v_st)
    xs = [jnp.concatenate([ai, kv[:c]], axis=1) for ai, kv in zip(a_s, kvs)]
    pws = [g[:c, :LANES] for g in grams]
    ns = pws
    pws = yield from _each(lambda pw: _dot(pw, _pair_stack(pw, h0)), pws)
    n = 2
    while n < c // 2:
        both = yield from _each(
            lambda pw, nn: _dot(jnp.concatenate([pw, nn], axis=0), _pair_stack(pw, h0)), pws, ns)
        ns = [nn + pw + bo[c:] for nn, pw, bo in zip(ns, pws, both)]
        pws = [bo[:c] for bo in both]
        n *= 2
    ns = yield from _each(lambda nn, pw: nn + pw + _dot(nn, _pair_stack(pw, h0)), ns, pws)
    xs = yield from _each(lambda x, nn: x + _dot(nn, _pair_stack(x, h0x2)), xs, ns)
    qos = yield from _each(lambda g, x: _dot(g[c:, :LANES], _pair_stack(x, h0x2)), grams, xs)
    qp = [jnp.concatenate([ri + qo[:, :LANES], x[:, :LANES]], axis=0) for ri, qo, x in zip(r_s, qos, xs)]
    o2 = [qo[:, LANES:] + kv[c:] for qo, kv in zip(qos, kvs)]
    return dict(qp=qp, o2=o2, u2=[x[:, LANES:] for x in xs], v=vs, bk_e=bk_e, decay=decay)


def _wkv_state_steps(pre, states, state_mask):
    c = WKV_CHUNK
    ous = yield from _each(_dot_nt, pre["qp"], states)
    upds = yield from _each(lambda ou, u2, v, bk: _dot_tn(jnp.concatenate([ou[c:] + u2, v], axis=0), bk),
                            ous, pre["u2"], pre["v"], pre["bk_e"])
    for p, (upd, decay) in enumerate(zip(upds, pre["decay"])):
        states[p] = states[p] * decay + jnp.where(state_mask, upd, 0.0)
    return [ou[:c] + o2 for ou, o2 in zip(ous, pre["o2"])]


def _mixer_prompt_kernel(*refs):
    x_ref, g1_ref, w_ref = refs[:3]
    mix_refs, out_refs, sgu_refs = refs[3:14], refs[14:17], refs[17:21]
    y_ref, yb_ref, sf_ref, sh_ref = refs[21:25]
    carries, s_ref, o_scr, p_scr, sg_scr = refs[25:29], refs[29], refs[30], refs[31], refs[32]
    c = WKV_CHUNK
    tb = y_ref.shape[0]
    d = y_ref.shape[1]
    n_pairs = d // LANES
    n_rw, n_sg = p_scr.shape[1], sg_scr.shape[1]
    sg_row0 = w_ref.shape[0] - n_sg
    t = pl.program_id(1)
    first = t == 0
    last = t == pl.num_programs(1) - 1

    @pl.when(first)
    def _():
        s_ref[...] = jnp.zeros_like(s_ref)
        for carry in carries:
            carry[...] = jnp.zeros_like(carry)

    h = _rms(x_ref[...], g1_ref[...]).astype(BF16)
    for j in range(0, n_rw, PROJ_COLS):
        p_scr[:, j:j + PROJ_COLS] = _dot_nt(h, w_ref[j:j + PROJ_COLS, :])
    p_refs = [p_scr.at[:, j * d:(j + 1) * d] for j in range(3)] + [p_scr.at[:, 3 * d:]]

    def sg_proj_steps():
        for j in range(0, n_sg, PROJ_COLS):
            sg_scr[:, j:j + PROJ_COLS] = _dot_nt(h, w_ref[sg_row0 + j:sg_row0 + j + PROJ_COLS, :])
            yield

    sg_ng, sg_nb, sg_w, sg_bias = sgu_refs
    sri = lax.broadcasted_iota(jnp.int32, (SGU_CHUNK, SGU_CHUNK), 0)
    sci = lax.broadcasted_iota(jnp.int32, (SGU_CHUNK, SGU_CHUNK), 1)
    sg_ws = [jnp.where(sci <= sri, sg_w[g], 0.0).astype(BF16) for g in range(sg_w.shape[0])]

    def sgu_steps(rows):
        u = _gelu(sg_scr[rows, :d])
        yield
        vs = _layernorm(_gelu(sg_scr[rows, d:]), sg_ng[...], sg_nb[...]).astype(BF16)
        yield
        for g, w in enumerate(sg_ws):
            cols = slice(g * SGU_CHUNK, (g + 1) * SGU_CHUNK)
            mix = jnp.dot(w, vs[:, cols], preferred_element_type=F32) + sg_bias[:, cols]
            yb_ref[rows, cols] = (u[:, cols] * mix).astype(yb_ref.dtype)
            yield

    mu_r, mu_k, mu_v, mu_l, w0, w_up, a0, a_up, g_up, k_k, k_a = mix_refs
    ri = lax.broadcasted_iota(jnp.int32, (c, c), 0)
    ci = lax.broadcasted_iota(jnp.int32, (c, c), 1)
    tri = (ci <= ri).astype(F32)
    gr = lax.broadcasted_iota(jnp.int32, (2 * c, 2 * LANES), 0)
    gc = lax.broadcasted_iota(jnp.int32, (2 * c, 2 * LANES), 1) % c
    gram_mask = gc <= jnp.where(gr < c, gr - 1, gr - c)
    sr = lax.broadcasted_iota(jnp.int32, (LANES, LANES), 0) // HEAD
    sc = lax.broadcasted_iota(jnp.int32, (LANES, LANES), 1) // HEAD
    state_mask = sr == sc
    states = [s_ref[p] for p in range(n_pairs)]

    def prep_steps(rows):
        pl_ = p_refs[3][rows, :]
        lo = _lo_mix(pl_, _shifted(pl_, carries[3]), mu_l[...], w_up.shape[0], g_up.shape[0])
        yield
        ps = [ref[rows, :] for ref in p_refs[:3]]
        yield
        qs = [_shifted(p, carry) for p, carry in zip(ps, carries[:3])]
        yield
        vals = yield from _col_mix_steps(*ps, *qs, mu_r[...], mu_k[...], mu_v[...], *lo, w0[...], w_up[...],
                                         a0[...], a_up[...], g_up[...], k_k[...], k_a[...])
        return vals

    def state_steps(rows, pre):
        outs = yield from _wkv_state_steps(pre, states, state_mask)
        o_scr[rows, :] = jnp.concatenate(outs, axis=1)

    def post_steps(rows, vals):
        r, _, k, v, _, _, gate = vals
        y = yield from _rwkv_out_steps(o_scr[rows, :], r, k, v, gate, *[ref[...] for ref in out_refs])
        y_ref[rows, :] = y.astype(y_ref.dtype)

    n_chunks = tb // c
    rows = [slice(j * c, (j + 1) * c) for j in range(n_chunks)]
    n_stages = n_chunks + 3
    sgu_at = {min(2 + 2 * j, n_stages - 1): j for j in range(tb // SGU_CHUNK)}
    vals, pre = {}, {}
    for s in range(n_stages):
        work = []
        if s == 0:
            work.append(("sg_proj", 0, sg_proj_steps(), n_sg // PROJ_COLS))
        if s in sgu_at:
            j = sgu_at[s]
            work.append(("sgu", j, sgu_steps(slice(j * SGU_CHUNK, (j + 1) * SGU_CHUNK)), 2 + len(sg_ws)))
        if s < n_chunks:
            work.append(("prep", s, prep_steps(rows[s]), 12))
        if 0 <= s - 1 < n_chunks:
            work.append(("pre", s - 1, _wkv_pre_steps(*vals[s - 1][:6], tri, gram_mask), 10 * n_pairs + 5))
        if 0 <= s - 2 < n_chunks:
            work.append(("state", s - 2, state_steps(rows[s - 2], pre[s - 2]), 2 * n_pairs + 1))
        if 0 <= s - 3 < n_chunks:
            work.append(("post", s - 3, post_steps(rows[s - 3], vals[s - 3]), 5))
        done = _run_together([(steps, n) for _, _, steps, n in work])
        for (kind, j, _, _), value in zip(work, done):
            if kind == "prep":
                vals[j] = value
            elif kind == "pre":
                pre[j] = value
    for p, s in enumerate(states):
        s_ref[p] = s

    @pl.when(last)
    def _():
        sh_ref[0] = p_scr[tb - 1:, :]
        for p in range(n_pairs):
            s = s_ref[p]
            sf_ref[0, 2 * p] = s[:HEAD, :HEAD]
            sf_ref[0, 2 * p + 1] = s[HEAD:, HEAD:]


def _resident(a):
    return pl.BlockSpec(a.shape, lambda *_: (0,) * a.ndim, pipeline_mode=pl.Buffered(1))


def _mixer_prompt(x, norm_g, wt, lay, mix_weights, out_weights, sgu_weights, batch, seq, tb):
    d = lay.d_rwkv
    d_model = x.shape[1]
    n_rw, n_sg = 3 * d + lay.tn, 2 * lay.d_sgu
    assert lay.d_sgu == d and lay.tn % PROJ_COLS == 0 and wt.shape[0] == lay.d_shift + n_sg
    n_heads = d // HEAD
    nt = seq // tb
    row = lambda b, i: (b * nt + i, 0)
    weights = list(mix_weights) + list(out_weights) + list(sgu_weights)
    y_spec = pl.BlockSpec((tb, d), row)
    return pl.pallas_call(
        _mixer_prompt_kernel,
        grid=(batch, nt),
        in_specs=([pl.BlockSpec((tb, d_model), row), _full(norm_g), _resident(wt)]
                  + [_full(w) for w in weights]),
        out_specs=[y_spec, y_spec,
                   pl.BlockSpec((1, n_heads, HEAD, HEAD), lambda b, i: (b, 0, 0, 0)),
                   pl.BlockSpec((1, 1, n_rw), lambda b, i: (b, 0, 0))],
        out_shape=[jax.ShapeDtypeStruct((batch * seq, d), BF16),
                   jax.ShapeDtypeStruct((batch * seq, d), BF16),
                   jax.ShapeDtypeStruct((batch, n_heads, HEAD, HEAD), F32),
                   jax.ShapeDtypeStruct((batch, 1, n_rw), F32)],
        scratch_shapes=([pltpu.VMEM((1, d), F32)] * 3
                        + [pltpu.VMEM((1, lay.tn), F32), pltpu.VMEM((d // LANES, LANES, LANES), F32),
                           pltpu.VMEM((tb, d), F32), pltpu.VMEM((tb, n_rw), F32),
                           pltpu.VMEM((tb, n_sg), F32)]),
        compiler_params=_params(("parallel", "arbitrary")),
        name="mixer_prompt",
    )(x, norm_g, wt, *weights)


STEP_UNROLL = 4


def _wkv_step_kernel(r_ref, lw_ref, k_ref, v_ref, a_ref, b_ref, s_ref, o_ref, sn_ref):
    a, b, k, r = a_ref[...], b_ref[...], k_ref[...], r_ref[...]
    w = jnp.exp(lw_ref[...])

    def body(j, carry):
        for u in range(STEP_UNROLL):
            i = j * STEP_UNROLL + u
            s = s_ref[0, i]
            sa = jnp.sum(s * a, axis=0, keepdims=True)
            s = s * w + sa * b + v_ref[pl.ds(i, 1), :] * k
            sn_ref[0, i] = s
            o_ref[pl.ds(i, 1), :] = jnp.sum(s * r, axis=0, keepdims=True)
        return carry

    lax.fori_loop(0, s_ref.shape[1] // STEP_UNROLL, body, 0)


def _wkv_step(r, lw, k, v, a, b, state):
    m, d = r.shape
    n_heads = d // HEAD
    vec = pl.BlockSpec((HEAD, m), lambda h: (h, 0))
    st = pl.BlockSpec((1, HEAD, HEAD, m), lambda h: (h, 0, 0, 0))
    o, s = pl.pallas_call(
        _wkv_step_kernel,
        grid=(n_heads,),
        in_specs=[vec] * 6 + [st],
        out_specs=[vec, st],
        out_shape=[jax.ShapeDtypeStruct((d, m), F32), jax.ShapeDtypeStruct((n_heads, HEAD, HEAD, m), F32)],
        compiler_params=_params(("parallel",)),
        name="wkv_step",
    )(*[t.T for t in (r, lw, k, v, a, b)], jnp.transpose(state, (1, 2, 3, 0)))
    return o.T, jnp.transpose(s, (3, 0, 1, 2))


def _post_kernel(*refs):
    y_ref = refs[-1]
    y_ref[...] = _rwkv_out(*[ref[...] for ref in refs[:-1]]).astype(y_ref.dtype)


def _rwkv_post(o, r, k, v, g, out_weights):
    m, d = o.shape
    spec = pl.BlockSpec((m, d), lambda i: (0, 0))
    return pl.pallas_call(
        _post_kernel,
        grid=(1,),
        in_specs=[spec] * 5 + [_full(w) for w in out_weights],
        out_specs=spec,
        out_shape=jax.ShapeDtypeStruct((m, d), BF16),
        compiler_params=_params(("arbitrary",)),
        name="rwkv_post",
    )(o, r, k, v, g, *out_weights)


def _gelu(x):
    return 0.5 * x * (1.0 + jnp.tanh(GELU_C * (x + 0.044715 * (x * x * x))))


def _layernorm(x, g, b):
    mu = jnp.mean(x, axis=-1, keepdims=True)
    xc = x - mu
    var = jnp.mean(xc * xc, axis=-1, keepdims=True)
    return xc * lax.rsqrt(var + LN_EPS) * g + b


def _sgu_sample_kernel(pu_ref, pv_ref, ng_ref, nb_ref, w_ref, bias_ref, y_ref, vs_ref):
    u = _gelu(pu_ref[...])
    vs = _layernorm(_gelu(pv_ref[...]), ng_ref[...], nb_ref[...])
    vs_ref[...] = vs
    y_ref[...] = (u * (w_ref[...] * vs + bias_ref[...])).astype(y_ref.dtype)


def _sgu_sample(p, lay, norm_g, norm_b, sgu_w, sgu_b):
    d = lay.d_sgu
    m = p.shape[0]
    w0 = jnp.repeat(sgu_w[:, 0, 0], SGU_CHUNK)[None, :]
    b0 = jnp.repeat(sgu_b[:, 0], SGU_CHUNK)[None, :]
    vec = pl.BlockSpec((1, d), lambda i: (0, 0))
    out = pl.BlockSpec((m, d), lambda i: (0, 0))
    return pl.pallas_call(
        _sgu_sample_kernel,
        grid=(1,),
        in_specs=[pl.BlockSpec((m, d), lambda i: (0, lay.u0 // d)),
                  pl.BlockSpec((m, d), lambda i: (0, lay.vs0 // d)), vec, vec, vec, vec],
        out_specs=[out, out],
        out_shape=[jax.ShapeDtypeStruct((m, d), BF16), jax.ShapeDtypeStruct((m, d), F32)],
        compiler_params=_params(("arbitrary",)),
        name="sgu_sample",
    )(p, p, norm_g[None, :], norm_b[None, :], w0, b0)


def _out_proj_kernel(x_ref, ya_ref, yb_ref, wa_ref, wb_ref, o_ref):
    o_ref[...] = (x_ref[...] + jnp.dot(ya_ref[...], wa_ref[...].astype(BF16), preferred_element_type=F32)
                  + jnp.dot(yb_ref[...], wb_ref[...].astype(BF16), preferred_element_type=F32))


def _out_proj(x, ya, yb, w, tm, tn):
    m, d = x.shape
    da = ya.shape[1]
    return pl.pallas_call(
        _out_proj_kernel,
        grid=(m // tm, d // tn),
        in_specs=[pl.BlockSpec((tm, tn), lambda i, j: (i, j)),
                  pl.BlockSpec((tm, da), lambda i, j: (i, 0)),
                  pl.BlockSpec((tm, da), lambda i, j: (i, 0)),
                  pl.BlockSpec((da, tn), lambda i, j: (0, j)),
                  pl.BlockSpec((da, tn), lambda i, j: (1, j))],
        out_specs=pl.BlockSpec((tm, tn), lambda i, j: (i, j)),
        out_shape=jax.ShapeDtypeStruct((m, d), F32),
        compiler_params=_params(("parallel", "arbitrary")),
        name="out_proj",
    )(x, ya, yb, w, w)


def _ffn_kernel(x_ref, xs_ref, g2_ref, wu_ref, wd_ref, gf_ref, o_ref, os_ref, h_ref, a_ref):
    f = pl.program_id(1)
    tm = x_ref.shape[0]

    @pl.when(f == 0)
    def _():
        x, xs = x_ref[...], xs_ref[...]
        h_ref[:tm, :] = _rms(x, g2_ref[...]).astype(BF16)
        h_ref[tm:, :] = _rms(xs, g2_ref[...]).astype(BF16)
        o_ref[...] = x
        os_ref[...] = xs
        a_ref[...] = jnp.zeros_like(a_ref)

    down = jnp.dot(a_ref[...], wd_ref[...].astype(BF16), preferred_element_type=F32)
    up = jnp.dot(h_ref[...], wu_ref[...].astype(BF16), preferred_element_type=F32)
    o_ref[...] += down[:tm]
    os_ref[...] += down[tm:]
    a_ref[...] = jnp.square(jnp.maximum(up, 0.0)).astype(BF16)

    @pl.when(f == pl.num_programs(1) - 1)
    def _():
        o_ref[...] = _rms(o_ref[...], gf_ref[...])
        os_ref[...] = _rms(os_ref[...], gf_ref[...])


def _ffn(x, xs, g2, w_up, w_down, gf, tm, tf):
    m, d = x.shape
    n_blocks = m // tm
    ts = xs.shape[0] // n_blocks
    assert ts * n_blocks == xs.shape[0] and ts % 8 == 0
    nf = w_up.shape[1] // tf
    return pl.pallas_call(
        _ffn_kernel,
        grid=(n_blocks, nf + 1),
        in_specs=[pl.BlockSpec((tm, d), lambda i, f: (i, 0)),
                  pl.BlockSpec((ts, d), lambda i, f: (i, 0)),
                  pl.BlockSpec((1, d), lambda i, f: (0, 0)),
                  pl.BlockSpec((d, tf), lambda i, f: (0, jnp.minimum(f, nf - 1))),
                  pl.BlockSpec((tf, d), lambda i, f: (jnp.maximum(f - 1, 0), 0)),
                  pl.BlockSpec((1, d), lambda i, f: (0, 0))],
        out_specs=[pl.BlockSpec((tm, d), lambda i, f: (i, 0)),
                   pl.BlockSpec((ts, d), lambda i, f: (i, 0))],
        out_shape=[jax.ShapeDtypeStruct((m, d), F32), jax.ShapeDtypeStruct(xs.shape, F32)],
        scratch_shapes=[pltpu.VMEM((tm + ts, d), BF16), pltpu.VMEM((tm + ts, tf), BF16)],
        compiler_params=_params(("parallel", "arbitrary")),
        name="ffn",
    )(x, xs, g2[None, :], w_up, w_down, gf[None, :])


def _row_tile(m, cap):
    t = min(m, cap)
    assert m % t == 0
    return t


def kernel(x_prompt, x_sample, state_wkv, state_shift, norm1_g, w_in, mu_shift, w0, w_up, a0, a_up, g_up,
           k_k, k_a, r_k, lnx_g, lnx_b, sgu_norm_g, sgu_norm_b, sgu_w, sgu_b, w_out, norm2_g, w_ffn_up,
           w_ffn_down, norm_f_g):
    batch, seq, d_model = x_prompt.shape
    n_dec, dec_seq, _ = x_sample.shape
    depth = w_in.shape[0]
    assert depth == 1 and dec_seq == 1
    d_rwkv = w0.shape[1]
    d_sgu = sgu_norm_g.shape[1]
    tn_in = 512
    lay = _Layout(d_rwkv, d_sgu, w_up.shape[1], a_up.shape[1], g_up.shape[1], tn_in)
    w_in_t = w_in[0].T.astype(BF16)
    prep_w = _prep_weights(lay, mu_shift[0], w0[0], w_up[0], a0[0], a_up[0], g_up[0], k_k[0], k_a[0])

    out_w = [w.reshape(1, d_rwkv) for w in (lnx_g[0], lnx_b[0], r_k[0])]

    w_out_b = w_out[0].astype(BF16)

    xp = x_prompt.reshape(batch * seq, d_model)
    sgu_bias = jnp.repeat(sgu_b[0].T, SGU_CHUNK, axis=1)
    ya, yb, wkv_p, last_p = _mixer_prompt(
        xp, norm1_g, w_in_t, lay, prep_w, out_w, [sgu_norm_g, sgu_norm_b, sgu_w[0], sgu_bias],
        batch, seq, 256)
    x1p = _out_proj(xp, ya, yb, w_out_b, _row_tile(batch * seq, 2048), 512)
    shift_p = last_p[:, 0, :lay.d_shift]

    xs = x_sample.reshape(n_dec, d_model)
    ps = _in_proj(xs, norm1_g, w_in_t, lay.src_rows, n_dec, tn_in)
    r, lw, k, v, aa, bb, g = _prep_sample(ps, state_shift[0], lay, prep_w)
    o, wkv_s = _wkv_step(r, lw, k, v, aa, bb, state_wkv[0])
    yb, vs = _sgu_sample(ps, lay, sgu_norm_g[0], sgu_norm_b[0], sgu_w[0], sgu_b[0])
    ya = _rwkv_post(o, r, k, v, g, out_w)
    x1s = _out_proj(xs, ya, yb, w_out_b, n_dec, 512)
    shift_s = lay.shift_row(ps)

    y_prompt, y_sample = _ffn(x1p, x1s, norm2_g[0], w_ffn_up[0], w_ffn_down[0], norm_f_g,
                              _row_tile(batch * seq, 1024), 256)
    y_prompt = y_prompt.reshape(batch, seq, d_model)
    y_sample = y_sample.reshape(n_dec, 1, d_model)
    return (y_prompt, y_sample, wkv_p[None], shift_p[None], wkv_s[None], shift_s[None],
            vs.reshape(1, n_dec, 1, d_sgu))
```

```python
import functools
import math

import jax
import jax.numpy as jnp
from jax import lax
from jax.experimental import pallas as pl
from jax.experimental.pallas import tpu as pltpu

F32 = jnp.float32
BF16 = jnp.bfloat16

HEAD = 64
LANES = 128
SGU_CHUNK = 128
WKV_CHUNK = 64
PROJ_COLS = 512
RMS_EPS = 1e-5
LN_EPS = 1e-5
GN_EPS = 64e-5
DECAY_SCALE = math.exp(-0.5)
GELU_C = math.sqrt(2.0 / math.pi)
VMEM_LIMIT = 58 * 1024 * 1024


def _params(sem):
    return pltpu.CompilerParams(dimension_semantics=sem, vmem_limit_bytes=VMEM_LIMIT)


def _dot(a, b):
    return jnp.dot(a.astype(BF16), b.astype(BF16), preferred_element_type=F32)


def _dot_nt(a, b):
    return lax.dot_general(a.astype(BF16), b.astype(BF16), (((1,), (1,)), ((), ())),
                           preferred_element_type=F32)


def _dot_tn(a, b):
    return lax.dot_general(a.astype(BF16), b.astype(BF16), (((0,), (0,)), ((), ())),
                           preferred_element_type=F32)


def _split3(x):
    hi = x.astype(BF16)
    r1 = x - hi.astype(F32)
    mid = r1.astype(BF16)
    lo = (r1 - mid.astype(F32)).astype(BF16)
    return hi, mid, lo


def _dot_exact_lhs(m, x):
    hi, mid, lo = _split3(x)
    mb = m.astype(BF16)
    return (jnp.dot(mb, hi, preferred_element_type=F32) + jnp.dot(mb, mid, preferred_element_type=F32)
            + jnp.dot(mb, lo, preferred_element_type=F32))


def _sigmoid(x):
    return 1.0 / (1.0 + jnp.exp(-x))


def _head_ones():
    r = lax.broadcasted_iota(jnp.int32, (2 * LANES, LANES), 0) % LANES // HEAD
    c = lax.broadcasted_iota(jnp.int32, (2 * LANES, LANES), 1) // HEAD
    return (r == c).astype(BF16)


def _head_sum(x, ones):
    parts = []
    for s in range(0, x.shape[1], LANES):
        xs = x[:, s:s + LANES]
        hi = xs.astype(BF16)
        lo = (xs - hi.astype(F32)).astype(BF16)
        parts.append(jnp.dot(jnp.concatenate([hi, lo], axis=1), ones, preferred_element_type=F32))
    return parts[0] if len(parts) == 1 else jnp.concatenate(parts, axis=1)


def _rms(x, g):
    return x * lax.rsqrt(jnp.mean(x * x, axis=-1, keepdims=True) + RMS_EPS) * g


def _in_proj_kernel(rows_ref, x_ref, g_ref, wt_ref, o_ref, h_ref):
    del rows_ref
    @pl.when(pl.program_id(1) == 0)
    def _():
        h_ref[...] = _rms(x_ref[...], g_ref[...]).astype(BF16)

    o_ref[...] = _dot_nt(h_ref[...], wt_ref[...])


def _in_proj(x, g, wt, src_rows, tm, tn):
    m, d = x.shape
    n_blocks = len(src_rows)
    return pl.pallas_call(
        _in_proj_kernel,
        grid_spec=pltpu.PrefetchScalarGridSpec(
            num_scalar_prefetch=1,
            grid=(m // tm, n_blocks),
            in_specs=[pl.BlockSpec((tm, d), lambda i, j, rows: (i, 0)),
                      pl.BlockSpec((1, d), lambda i, j, rows: (0, 0)),
                      pl.BlockSpec((pl.Element(tn), pl.Element(d)), lambda i, j, rows: (pl.multiple_of(rows[j], 8), 0))],
            out_specs=pl.BlockSpec((tm, tn), lambda i, j, rows: (i, j)),
            scratch_shapes=[pltpu.VMEM((tm, d), BF16)]),
        out_shape=jax.ShapeDtypeStruct((m, n_blocks * tn), F32),
        compiler_params=_params(("parallel", "arbitrary")),
        name="in_proj",
    )(jnp.asarray(src_rows, jnp.int32), x, g, wt)


def _lo_mix(pl_, ql, mu_l, n_wa, n_gl):
    n = n_wa + n_gl
    lo = pl_[:, :n]
    lo = lo + (ql[:, :n] - lo) * mu_l[:, :n]
    wa = lo[:, :n_wa]
    return jnp.tanh(wa), wa, _sigmoid(lo[:, n_wa:])


def _run(steps):
    try:
        while True:
            next(steps)
    except StopIteration as done:
        return done.value


def _col_mix_steps(pr, pk, pv, qr, qk, qv, mu_r, mu_k, mu_v, tanh_wa, wa, sig_gl, w0, w_up, a0, a_up, g_up,
                   k_k, k_a):
    r = pr + (qr - pr) * mu_r
    yield
    k = pk + (qk - pk) * mu_k
    yield
    v = pv + (qv - pv) * mu_v
    yield
    lw = -DECAY_SCALE * _sigmoid(w0 + _dot(tanh_wa, w_up))
    yield
    a = _sigmoid(a0 + _dot(wa, a_up))
    yield
    gate = _dot(sig_gl, g_up)
    yield
    kk = k * k_k
    ss = _head_sum(kk * kk, _head_ones())
    yield
    kk = kk / jnp.maximum(jnp.sqrt(ss), 1e-12)
    yield
    return r, lw, k * (1.0 + (a - 1.0) * k_a), v, -kk, kk * a, gate


def _rwkv_mix(pr, pk, pv, pl_, qr, qk, qv, ql, mu_r, mu_k, mu_v, mu_l, w0, w_up, a0, a_up, g_up, k_k, k_a):
    lo = _lo_mix(pl_, ql, mu_l, w_up.shape[0], g_up.shape[0])
    return _run(_col_mix_steps(pr, pk, pv, qr, qk, qv, mu_r, mu_k, mu_v, *lo, w0, w_up, a0, a_up, g_up,
                               k_k, k_a))


def _rwkv_out_steps(o, r, k, v, gate, lnx_g, lnx_b, r_k):
    ones = _head_ones()
    mu = _head_sum(o, ones) * (1.0 / HEAD)
    yield
    oc = o - mu
    var = _head_sum(oc * oc, ones) * (1.0 / HEAD)
    yield
    y = oc * lax.rsqrt(var + GN_EPS) * lnx_g + lnx_b
    yield
    bonus = _head_sum(r * k * r_k, ones)
    yield
    return (y + bonus * v) * gate


def _rwkv_out(*args):
    return _run(_rwkv_out_steps(*args))


def _each(fn, *lists):
    out = []
    for args in zip(*lists):
        out.append(fn(*args))
        yield
    return out


def _run_together(work):
    values = [None] * len(work)
    longest = max(n for _, n in work)
    credit = [0.0] * len(work)
    live = set(range(len(work)))
    while live:
        for i, (steps, n) in enumerate(work):
            credit[i] += n / longest
            while i in live and credit[i] >= 1.0:
                credit[i] -= 1.0
                try:
                    next(steps)
                except StopIteration as done:
                    values[i] = done.value
                    live.discard(i)
    return values


def _shifted(p, carry_ref):
    rows = lax.broadcasted_iota(jnp.int32, p.shape, 0)
    q = jnp.where(rows == 0, carry_ref[...], pltpu.roll(p, 1, axis=0))
    carry_ref[...] = p[p.shape[0] - 1:, :]
    return q


def _prep_sample_kernel(*refs):
    ins, rows, cols = refs[:-10], refs[-10:-6], refs[-6:]
    r, lw, k, v, aa, bb, gate = _rwkv_mix(*[ref[...] for ref in ins])
    for ref, val in zip(rows, (r, k, v, gate)):
        ref[...] = val
    for ref, val in zip(cols, (r, lw, k, v, aa, bb)):
        ref[...] = val.T


class _Layout:
    def __init__(self, d_rwkv, d_sgu, lora_w, lora_a, lora_g, tn):
        self.d_rwkv, self.d_sgu, self.tn = d_rwkv, d_sgu, tn
        self.wa_w = lora_w + lora_a
        self.gl_w = -(-lora_g // LANES) * LANES
        self.d_shift = 3 * d_rwkv + self.wa_w + lora_g
        assert self.wa_w == LANES and self.wa_w + self.gl_w <= tn
        assert d_rwkv % tn == 0 and d_sgu % tn == 0
        self.u0 = 0
        self.vs0 = d_sgu
        self.r0 = 2 * d_sgu
        self.k0 = self.r0 + d_rwkv
        self.v0 = self.k0 + d_rwkv
        self.lo0 = self.v0 + d_rwkv
        self.width = self.lo0 + tn
        self.src_rows = (list(range(self.d_shift, self.d_shift + 2 * d_sgu, tn))
                         + list(range(0, 3 * d_rwkv + tn, tn)))

    def rw_pieces(self, a):
        d = self.d_rwkv
        pad = [(0, 0)] * (a.ndim - 1) + [(0, self.tn - (self.d_shift - 3 * d))]
        return a[..., :d], a[..., d:2 * d], a[..., 2 * d:3 * d], jnp.pad(a[..., 3 * d:], pad)

    def shift_row(self, p_rows):
        return p_rows[:, self.r0:self.r0 + self.d_shift]


def _prep_weights(lay, mu, w0, w_up, a0, a_up, g_up, k_k, k_a):
    d = lay.d_rwkv
    lora_w, lora_g = w_up.shape[0], g_up.shape[0]
    mus = [m[None, :] for m in lay.rw_pieces(mu)]
    w_up_p = jnp.pad(w_up, ((0, lay.wa_w - lora_w), (0, 0)))
    a_up_p = jnp.pad(a_up, ((lora_w, 0), (0, 0)))
    g_up_p = jnp.pad(g_up, ((0, lay.gl_w - lora_g), (0, 0)))
    return mus + [w0[None, :], w_up_p, a0[None, :], a_up_p, g_up_p, k_k.reshape(1, d), k_a.reshape(1, d)]


def _full(a):
    return pl.BlockSpec(a.shape, lambda *_: (0,) * a.ndim)


def _prep_sample(p, prev, lay, weights):
    d = lay.d_rwkv
    m = p.shape[0]
    p_specs = [pl.BlockSpec((m, d), lambda i: (0, lay.r0 // d)),
               pl.BlockSpec((m, d), lambda i: (0, lay.k0 // d)),
               pl.BlockSpec((m, d), lambda i: (0, lay.v0 // d)),
               pl.BlockSpec((m, lay.tn), lambda i: (0, lay.lo0 // lay.tn))]
    prevs = list(lay.rw_pieces(prev))
    row_spec = pl.BlockSpec((m, d), lambda i: (0, 0))
    col_spec = pl.BlockSpec((d, m), lambda i: (0, 0))
    outs = pl.pallas_call(
        _prep_sample_kernel,
        grid=(1,),
        in_specs=p_specs + [_full(q) for q in prevs] + [_full(w) for w in weights],
        out_specs=[row_spec] * 4 + [col_spec] * 6,
        out_shape=[jax.ShapeDtypeStruct((m, d), F32)] * 4 + [jax.ShapeDtypeStruct((d, m), F32)] * 6,
        compiler_params=_params(("arbitrary",)),
        name="rwkv_prep_sample",
    )(p, p, p, p, *prevs, *weights)
    return outs[:4], outs[4:]


def _pair_stack(x, first_head):
    zero = jnp.zeros_like(x)
    return jnp.concatenate([jnp.where(first_head, x, zero), jnp.where(first_head, zero, x)], axis=0)


def _wkv_pre_steps(r, lw, k, v, a, b, tri, gram_mask):
    c = WKV_CHUNK
    n_pairs = r.shape[1] // LANES
    lane = lax.broadcasted_iota(jnp.int32, (c, LANES), 1)
    h0 = lane < HEAD
    h0x2 = jnp.concatenate([h0, h0], axis=1)

    def cut(x):
        return [x[:, p * LANES:(p + 1) * LANES] for p in range(n_pairs)]

    cum = _dot_exact_lhs(tri, lw)
    yield
    e_out = jnp.exp(-cum)
    a_s = cut(a * jnp.exp(cum - lw))
    r_s = cut(r * jnp.exp(cum))
    yield
    b_s = cut(b * e_out)
    k_s = cut(k * e_out)
    yield
    last = cum[c - 1:, :]
    e_end = jnp.exp(last - cum)
    bk_e = cut(jnp.concatenate([b * e_end, k * e_end], axis=0))
    decay = cut(jnp.exp(last))
    vs = cut(v)
    yield

    grams = yield from _each(lambda ai, ri, bi, ki: jnp.where(
        gram_mask, _dot_nt(jnp.concatenate([ai, ri], axis=0),
                           jnp.concatenate([_pair_stack(bi, h0), _pair_stack(ki, h0)], axis=0)), 0.0),
        a_s, r_s, b_s, k_s)
    v_st = [_pair_stack(x, h0) for x in vs]
    kvs = yield from _each(lambda g, vi: _dot(g[:, LANES:], vi), grams, v_st)
    xs = [jnp.concatenate([ai, kv[:c]], axis=1) for ai, kv in zip(a_s, kvs)]
    pws = [g[:c, :LANES] for g in grams]
    ns = pws
    pws = yield from _each(lambda pw: _dot(pw, _pair_stack(pw, h0)), pws)
    n = 2
    while n < c // 2:
        both = yield from _each(
            lambda pw, nn: _dot(jnp.concatenate([pw, nn], axis=0), _pair_stack(pw, h0)), pws, ns)
        ns = [nn + pw + bo[c:] for nn, pw, bo in zip(ns, pws, both)]
        pws = [bo[:c] for bo in both]
        n *= 2
    ns = yield from _each(lambda nn, pw: nn + pw + _dot(nn, _pair_stack(pw, h0)), ns, pws)
    xs = yield from _each(lambda x, nn: x + _dot(nn, _pair_stack(x, h0x2)), xs, ns)
    qos = yield from _each(lambda g, x: _dot(g[c:, :LANES], _pair_stack(x, h0x2)), grams, xs)
    qp = [jnp.concatenate([ri + qo[:, :LANES], x[:, :LANES]], axis=0) for ri, qo, x in zip(r_s, qos, xs)]
    o2 = [qo[:, LANES:] + kv[c:] for qo, kv in zip(qos, kvs)]
    return dict(qp=qp, o2=o2, u2=[x[:, LANES:] for x in xs], v=vs, bk_e=bk_e, decay=decay)


def _wkv_state_steps(pre, states, state_mask):
    c = WKV_CHUNK
    ous = yield from _each(_dot_nt, pre["qp"], states)
    upds = yield from _each(lambda ou, u2, v, bk: _dot_tn(jnp.concatenate([ou[c:] + u2, v], axis=0), bk),
                            ous, pre["u2"], pre["v"], pre["bk_e"])
    for p, (upd, decay) in enumerate(zip(upds, pre["decay"])):
        states[p] = states[p] * decay + jnp.where(state_mask, upd, 0.0)
    return [ou[:c] + o2 for ou, o2 in zip(ous, pre["o2"])]


def _mixer_prompt_kernel(*refs):
    x_ref, g1_ref, w_ref = refs[:3]
    mix_refs, out_refs, sgu_refs = refs[3:14], refs[14:17], refs[17:21]
    y_ref, yb_ref, sf_ref, sh_ref = refs[21:25]
    carries, s_ref, o_scr, p_scr, sg_scr = refs[25:29], refs[29], refs[30], refs[31], refs[32]
    c = WKV_CHUNK
    tb = y_ref.shape[0]
    d = y_ref.shape[1]
    n_pairs = d // LANES
    n_rw, n_sg = p_scr.shape[1], sg_scr.shape[1]
    sg_row0 = w_ref.shape[0] - n_sg
    t = pl.program_id(1)
    first = t == 0
    last = t == pl.num_programs(1) - 1

    @pl.when(first)
    def _():
        s_ref[...] = jnp.zeros_like(s_ref)
        for carry in carries:
            carry[...] = jnp.zeros_like(carry)

    h = _rms(x_ref[...], g1_ref[...]).astype(BF16)
    for j in range(0, n_rw, PROJ_COLS):
        p_scr[:, j:j + PROJ_COLS] = _dot_nt(h, w_ref[j:j + PROJ_COLS, :])
    p_refs = [p_scr.at[:, j * d:(j + 1) * d] for j in range(3)] + [p_scr.at[:, 3 * d:]]

    def sg_proj_steps():
        for j in range(0, n_sg, PROJ_COLS):
            sg_scr[:, j:j + PROJ_COLS] = _dot_nt(h, w_ref[sg_row0 + j:sg_row0 + j + PROJ_COLS, :])
            yield

    sg_ng, sg_nb, sg_w, sg_bias = sgu_refs
    sri = lax.broadcasted_iota(jnp.int32, (SGU_CHUNK, SGU_CHUNK), 0)
    sci = lax.broadcasted_iota(jnp.int32, (SGU_CHUNK, SGU_CHUNK), 1)
    sg_ws = [jnp.where(sci <= sri, sg_w[g], 0.0).astype(BF16) for g in range(sg_w.shape[0])]

    def sgu_steps(rows):
        u = _gelu(sg_scr[rows, :d])
        yield
        vs = _layernorm(_gelu(sg_scr[rows, d:]), sg_ng[...], sg_nb[...]).astype(BF16)
        yield
        for g, w in enumerate(sg_ws):
            cols = slice(g * SGU_CHUNK, (g + 1) * SGU_CHUNK)
            mix = jnp.dot(w, vs[:, cols], preferred_element_type=F32) + sg_bias[:, cols]
            yb_ref[rows, cols] = (u[:, cols] * mix).astype(yb_ref.dtype)
            yield

    mu_r, mu_k, mu_v, mu_l, w0, w_up, a0, a_up, g_up, k_k, k_a = mix_refs
    ri = lax.broadcasted_iota(jnp.int32, (c, c), 0)
    ci = lax.broadcasted_iota(jnp.int32, (c, c), 1)
    tri = (ci <= ri).astype(F32)
    gr = lax.broadcasted_iota(jnp.int32, (2 * c, 2 * LANES), 0)
    gc = lax.broadcasted_iota(jnp.int32, (2 * c, 2 * LANES), 1) % c
    gram_mask = gc <= jnp.where(gr < c, gr - 1, gr - c)
    sr = lax.broadcasted_iota(jnp.int32, (LANES, LANES), 0) // HEAD
    sc = lax.broadcasted_iota(jnp.int32, (LANES, LANES), 1) // HEAD
    state_mask = sr == sc
    states = [s_ref[p] for p in range(n_pairs)]

    def prep_steps(rows):
        pl_ = p_refs[3][rows, :]
        lo = _lo_mix(pl_, _shifted(pl_, carries[3]), mu_l[...], w_up.shape[0], g_up.shape[0])
        yield
        ps = [ref[rows, :] for ref in p_refs[:3]]
        yield
        qs = [_shifted(p, carry) for p, carry in zip(ps, carries[:3])]
        yield
        vals = yield from _col_mix_steps(*ps, *qs, mu_r[...], mu_k[...], mu_v[...], *lo, w0[...], w_up[...],
                                         a0[...], a_up[...], g_up[...], k_k[...], k_a[...])
        return vals

    def state_steps(rows, pre):
        outs = yield from _wkv_state_steps(pre, states, state_mask)
        o_scr[rows, :] = jnp.concatenate(outs, axis=1)

    def post_steps(rows, vals):
        r, _, k, v, _, _, gate = vals
        y = yield from _rwkv_out_steps(o_scr[rows, :], r, k, v, gate, *[ref[...] for ref in out_refs])
        y_ref[rows, :] = y.astype(y_ref.dtype)

    n_chunks = tb // c
    rows = [slice(j * c, (j + 1) * c) for j in range(n_chunks)]
    n_stages = n_chunks + 3
    sgu_at = {min(2 + 2 * j, n_stages - 1): j for j in range(tb // SGU_CHUNK)}
    vals, pre = {}, {}
    for s in range(n_stages):
        work = []
        if s == 0:
            work.append(("sg_proj", 0, sg_proj_steps(), n_sg // PROJ_COLS))
        if s in sgu_at:
            j = sgu_at[s]
            work.append(("sgu", j, sgu_steps(slice(j * SGU_CHUNK, (j + 1) * SGU_CHUNK)), 2 + len(sg_ws)))
        if s < n_chunks:
            work.append(("prep", s, prep_steps(rows[s]), 12))
        if 0 <= s - 1 < n_chunks:
            work.append(("pre", s - 1, _wkv_pre_steps(*vals[s - 1][:6], tri, gram_mask), 10 * n_pairs + 5))
        if 0 <= s - 2 < n_chunks:
            work.append(("state", s - 2, state_steps(rows[s - 2], pre[s - 2]), 2 * n_pairs + 1))
        if 0 <= s - 3 < n_chunks:
            work.append(("post", s - 3, post_steps(rows[s - 3], vals[s - 3]), 5))
        done = _run_together([(steps, n) for _, _, steps, n in work])
        for (kind, j, _, _), value in zip(work, done):
            if kind == "prep":
                vals[j] = value
            elif kind == "pre":
                pre[j] = value
    for p, s in enumerate(states):
        s_ref[p] = s

    @pl.when(last)
    def _():
        sh_ref[0] = p_scr[tb - 1:, :]
        for p in range(n_pairs):
            s = s_ref[p]
            sf_ref[0, 2 * p] = s[:HEAD, :HEAD]
            sf_ref[0, 2 * p + 1] = s[HEAD:, HEAD:]


def _resident(a):
    return pl.BlockSpec(a.shape, lambda *_: (0,) * a.ndim, pipeline_mode=pl.Buffered(1))


def _mixer_prompt(x, norm_g, wt, lay, mix_weights, out_weights, sgu_weights, batch, seq, tb):
    d = lay.d_rwkv
    d_model = x.shape[1]
    n_rw, n_sg = 3 * d + lay.tn, 2 * lay.d_sgu
    assert lay.d_sgu == d and lay.tn % PROJ_COLS == 0 and wt.shape[0] == lay.d_shift + n_sg
    n_heads = d // HEAD
    nt = seq // tb
    row = lambda b, i: (b * nt + i, 0)
    weights = list(mix_weights) + list(out_weights) + list(sgu_weights)
    y_spec = pl.BlockSpec((tb, d), row)
    return pl.pallas_call(
        _mixer_prompt_kernel,
        grid=(batch, nt),
        in_specs=([pl.BlockSpec((tb, d_model), row), _full(norm_g), _resident(wt)]
                  + [_full(w) for w in weights]),
        out_specs=[y_spec, y_spec,
                   pl.BlockSpec((1, n_heads, HEAD, HEAD), lambda b, i: (b, 0, 0, 0)),
                   pl.BlockSpec((1, 1, n_rw), lambda b, i: (b, 0, 0))],
        out_shape=[jax.ShapeDtypeStruct((batch * seq, d), BF16),
                   jax.ShapeDtypeStruct((batch * seq, d), BF16),
                   jax.ShapeDtypeStruct((batch, n_heads, HEAD, HEAD), F32),
                   jax.ShapeDtypeStruct((batch, 1, n_rw), F32)],
        scratch_shapes=([pltpu.VMEM((1, d), F32)] * 3
                        + [pltpu.VMEM((1, lay.tn), F32), pltpu.VMEM((d // LANES, LANES, LANES), F32),
                           pltpu.VMEM((tb, d), F32), pltpu.VMEM((tb, n_rw), F32),
                           pltpu.VMEM((tb, n_sg), F32)]),
        compiler_params=_params(("parallel", "arbitrary")),
        name="mixer_prompt",
    )(x, norm_g, wt, *weights)


STEP_UNROLL = 4


def _wkv_step_kernel(r_ref, lw_ref, k_ref, v_ref, a_ref, b_ref, s_ref, o_ref, sn_ref):
    a, b, k, r = a_ref[...], b_ref[...], k_ref[...], r_ref[...]
    w = jnp.exp(lw_ref[...])

    def body(j, carry):
        for u in range(STEP_UNROLL):
            i = j * STEP_UNROLL + u
            s = s_ref[0, i]
            sa = jnp.sum(s * a, axis=0, keepdims=True)
            s = s * w + sa * b + v_ref[pl.ds(i, 1), :] * k
            sn_ref[0, i] = s
            o_ref[pl.ds(i, 1), :] = jnp.sum(s * r, axis=0, keepdims=True)
        return carry

    lax.fori_loop(0, s_ref.shape[1] // STEP_UNROLL, body, 0)


def _wkv_step(cols, state):
    d, m = cols[0].shape
    n_heads = d // HEAD
    vec = pl.BlockSpec((HEAD, m), lambda h: (h, 0))
    st = pl.BlockSpec((1, HEAD, HEAD, m), lambda h: (h, 0, 0, 0))
    o, s = pl.pallas_call(
        _wkv_step_kernel,
        grid=(n_heads,),
        in_specs=[vec] * 6 + [st],
        out_specs=[vec, st],
        out_shape=[jax.ShapeDtypeStruct((d, m), F32), jax.ShapeDtypeStruct((n_heads, HEAD, HEAD, m), F32)],
        compiler_params=_params(("parallel",)),
        name="wkv_step",
    )(*cols, jnp.transpose(state, (1, 2, 3, 0)))
    return o, jnp.transpose(s, (3, 0, 1, 2))


def _post_kernel(ot_ref, *refs):
    y_ref = refs[-1]
    y_ref[...] = _rwkv_out(ot_ref[...].T, *[ref[...] for ref in refs[:-1]]).astype(y_ref.dtype)


def _rwkv_post(o_t, r, k, v, g, out_weights):
    m, d = r.shape
    spec = pl.BlockSpec((m, d), lambda i: (0, 0))
    return pl.pallas_call(
        _post_kernel,
        grid=(1,),
        in_specs=[pl.BlockSpec((d, m), lambda i: (0, 0))] + [spec] * 4 + [_full(w) for w in out_weights],
        out_specs=spec,
        out_shape=jax.ShapeDtypeStruct((m, d), BF16),
        compiler_params=_params(("arbitrary",)),
        name="rwkv_post",
    )(o_t, r, k, v, g, *out_weights)


def _gelu(x):
    return 0.5 * x * (1.0 + jnp.tanh(GELU_C * (x + 0.044715 * (x * x * x))))


def _layernorm(x, g, b):
    mu = jnp.mean(x, axis=-1, keepdims=True)
    xc = x - mu
    var = jnp.mean(xc * xc, axis=-1, keepdims=True)
    return xc * lax.rsqrt(var + LN_EPS) * g + b


def _sgu_sample_kernel(pu_ref, pv_ref, ng_ref, nb_ref, w_ref, bias_ref, y_ref, vs_ref):
    u = _gelu(pu_ref[...])
    vs = _layernorm(_gelu(pv_ref[...]), ng_ref[...], nb_ref[...])
    vs_ref[...] = vs
    y_ref[...] = (u * (w_ref[...] * vs + bias_ref[...])).astype(y_ref.dtype)


def _sgu_sample(p, lay, norm_g, norm_b, sgu_w, sgu_b):
    d = lay.d_sgu
    m = p.shape[0]
    w0 = jnp.repeat(sgu_w[:, 0, 0], SGU_CHUNK)[None, :]
    b0 = jnp.repeat(sgu_b[:, 0], SGU_CHUNK)[None, :]
    vec = pl.BlockSpec((1, d), lambda i: (0, 0))
    out = pl.BlockSpec((m, d), lambda i: (0, 0))
    return pl.pallas_call(
        _sgu_sample_kernel,
        grid=(1,),
        in_specs=[pl.BlockSpec((m, d), lambda i: (0, lay.u0 // d)),
                  pl.BlockSpec((m, d), lambda i: (0, lay.vs0 // d)), vec, vec, vec, vec],
        out_specs=[out, out],
        out_shape=[jax.ShapeDtypeStruct((m, d), BF16), jax.ShapeDtypeStruct((m, d), F32)],
        compiler_params=_params(("arbitrary",)),
        name="sgu_sample",
    )(p, p, norm_g[None, :], norm_b[None, :], w0, b0)


def _out_proj_kernel(x_ref, ya_ref, yb_ref, wa_ref, wb_ref, o_ref):
    o_ref[...] = (x_ref[...] + jnp.dot(ya_ref[...], wa_ref[...].astype(BF16), preferred_element_type=F32)
                  + jnp.dot(yb_ref[...], wb_ref[...].astype(BF16), preferred_element_type=F32))


def _out_proj(x, ya, yb, w, tm, tn):
    m, d = x.shape
    da = ya.shape[1]
    return pl.pallas_call(
        _out_proj_kernel,
        grid=(m // tm, d // tn),
        in_specs=[pl.BlockSpec((tm, tn), lambda i, j: (i, j)),
                  pl.BlockSpec((tm, da), lambda i, j: (i, 0)),
                  pl.BlockSpec((tm, da), lambda i, j: (i, 0)),
                  pl.BlockSpec((da, tn), lambda i, j: (0, j)),
                  pl.BlockSpec((da, tn), lambda i, j: (1, j))],
        out_specs=pl.BlockSpec((tm, tn), lambda i, j: (i, j)),
        out_shape=jax.ShapeDtypeStruct((m, d), F32),
        compiler_params=_params(("parallel", "arbitrary")),
        name="out_proj",
    )(x, ya, yb, w, w)


def _ffn_kernel(x_ref, xs_ref, g2_ref, wu_ref, wd_ref, gf_ref, o_ref, os_ref, h_ref, a_ref):
    f = pl.program_id(1)
    last = pl.num_programs(1) - 1
    tm = x_ref.shape[0]

    def up():
        a = jnp.dot(h_ref[...], wu_ref[...].astype(BF16), preferred_element_type=F32)
        return jnp.square(jnp.maximum(a, 0.0)).astype(BF16)

    def down():
        return jnp.dot(a_ref[...], wd_ref[...].astype(BF16), preferred_element_type=F32)

    @pl.when(f == 0)
    def _():
        x, xs = x_ref[...], xs_ref[...]
        h_ref[:tm, :] = _rms(x, g2_ref[...]).astype(BF16)
        h_ref[tm:, :] = _rms(xs, g2_ref[...]).astype(BF16)
        o_ref[...] = x
        os_ref[...] = xs
        a_ref[...] = up()

    @pl.when((f > 0) & (f < last))
    def _():
        acc = down()
        a_new = up()
        o_ref[...] += acc[:tm]
        os_ref[...] += acc[tm:]
        a_ref[...] = a_new

    @pl.when(f == last)
    def _():
        acc = down()
        o_ref[...] = _rms(o_ref[...] + acc[:tm], gf_ref[...])
        os_ref[...] = _rms(os_ref[...] + acc[tm:], gf_ref[...])


def _ffn(x, xs, g2, w_up, w_down, gf, tm, tf):
    m, d = x.shape
    n_blocks = m // tm
    ts = xs.shape[0] // n_blocks
    assert ts * n_blocks == xs.shape[0] and ts % 8 == 0
    nf = w_up.shape[1] // tf
    return pl.pallas_call(
        _ffn_kernel,
        grid=(n_blocks, nf + 1),
        in_specs=[pl.BlockSpec((tm, d), lambda i, f: (i, 0)),
                  pl.BlockSpec((ts, d), lambda i, f: (i, 0)),
                  pl.BlockSpec((1, d), lambda i, f: (0, 0)),
                  pl.BlockSpec((d, tf), lambda i, f: (0, jnp.minimum(f, nf - 1))),
                  pl.BlockSpec((tf, d), lambda i, f: (jnp.maximum(f - 1, 0), 0)),
                  pl.BlockSpec((1, d), lambda i, f: (0, 0))],
        out_specs=[pl.BlockSpec((tm, d), lambda i, f: (i, 0)),
                   pl.BlockSpec((ts, d), lambda i, f: (i, 0))],
        out_shape=[jax.ShapeDtypeStruct((m, d), F32), jax.ShapeDtypeStruct(xs.shape, F32)],
        scratch_shapes=[pltpu.VMEM((tm + ts, d), BF16), pltpu.VMEM((tm + ts, tf), BF16)],
        compiler_params=_params(("parallel", "arbitrary")),
        name="ffn",
    )(x, xs, g2[None, :], w_up, w_down, gf[None, :])


def _row_tile(m, cap):
    t = min(m, cap)
    assert m % t == 0
    return t


def kernel(x_prompt, x_sample, state_wkv, state_shift, norm1_g, w_in, mu_shift, w0, w_up, a0, a_up, g_up,
           k_k, k_a, r_k, lnx_g, lnx_b, sgu_norm_g, sgu_norm_b, sgu_w, sgu_b, w_out, norm2_g, w_ffn_up,
           w_ffn_down, norm_f_g):
    batch, seq, d_model = x_prompt.shape
    n_dec, dec_seq, _ = x_sample.shape
    depth = w_in.shape[0]
    assert depth == 1 and dec_seq == 1
    d_rwkv = w0.shape[1]
    d_sgu = sgu_norm_g.shape[1]
    tn_in = 512
    lay = _Layout(d_rwkv, d_sgu, w_up.shape[1], a_up.shape[1], g_up.shape[1], tn_in)
    w_in_t = w_in[0].T.astype(BF16)
    prep_w = _prep_weights(lay, mu_shift[0], w0[0], w_up[0], a0[0], a_up[0], g_up[0], k_k[0], k_a[0])

    out_w = [w.reshape(1, d_rwkv) for w in (lnx_g[0], lnx_b[0], r_k[0])]

    w_out_b = w_out[0].astype(BF16)

    xp = x_prompt.reshape(batch * seq, d_model)
    sgu_bias = jnp.repeat(sgu_b[0].T, SGU_CHUNK, axis=1)
    ya, yb, wkv_p, last_p = _mixer_prompt(
        xp, norm1_g, w_in_t, lay, prep_w, out_w, [sgu_norm_g, sgu_norm_b, sgu_w[0], sgu_bias],
        batch, seq, 256)
    x1p = _out_proj(xp, ya, yb, w_out_b, _row_tile(batch * seq, 2048), 512)
    shift_p = last_p[:, 0, :lay.d_shift]

    xs = x_sample.reshape(n_dec, d_model)
    ps = _in_proj(xs, norm1_g, w_in_t, lay.src_rows, n_dec, tn_in)
    (r, k, v, g), step_cols = _prep_sample(ps, state_shift[0], lay, prep_w)
    o_t, wkv_s = _wkv_step(step_cols, state_wkv[0])
    yb, vs = _sgu_sample(ps, lay, sgu_norm_g[0], sgu_norm_b[0], sgu_w[0], sgu_b[0])
    ya = _rwkv_post(o_t, r, k, v, g, out_w)
    x1s = _out_proj(xs, ya, yb, w_out_b, n_dec, 512)
    shift_s = lay.shift_row(ps)

    y_prompt, y_sample = _ffn(x1p, x1s, norm2_g[0], w_ffn_up[0], w_ffn_down[0], norm_f_g,
                              _row_tile(batch * seq, 1024), 256)
    y_prompt = y_prompt.reshape(batch, seq, d_model)
    y_sample = y_sample.reshape(n_dec, 1, d_model)
    return (y_prompt, y_sample, wkv_p[None], shift_p[None], wkv_s[None], shift_s[None],
            vs.reshape(1, n_dec, 1, d_sgu))
```

```python
import functools
import math

import jax
import jax.numpy as jnp
from jax import lax
from jax.experimental import pallas as pl
from jax.experimental.pallas import tpu as pltpu

F32 = jnp.float32
BF16 = jnp.bfloat16

HEAD = 64
LANES = 128
SGU_CHUNK = 128
WKV_CHUNK = 64
PROJ_COLS = 512
RMS_EPS = 1e-5
LN_EPS = 1e-5
GN_EPS = 64e-5
DECAY_SCALE = math.exp(-0.5)
GELU_C = math.sqrt(2.0 / math.pi)
VMEM_LIMIT = 58 * 1024 * 1024


def _params(sem):
    return pltpu.CompilerParams(dimension_semantics=sem, vmem_limit_bytes=VMEM_LIMIT)


def _dot(a, b):
    return jnp.dot(a.astype(BF16), b.astype(BF16), preferred_element_type=F32)


def _dot_nt(a, b):
    return lax.dot_general(a.astype(BF16), b.astype(BF16), (((1,), (1,)), ((), ())),
                           preferred_element_type=F32)


def _dot_tn(a, b):
    return lax.dot_general(a.astype(BF16), b.astype(BF16), (((0,), (0,)), ((), ())),
                           preferred_element_type=F32)


def _split3(x):
    hi = x.astype(BF16)
    r1 = x - hi.astype(F32)
    mid = r1.astype(BF16)
    lo = (r1 - mid.astype(F32)).astype(BF16)
    return hi, mid, lo


def _dot_exact_lhs(m, x):
    hi, mid, lo = _split3(x)
    mb = m.astype(BF16)
    return (jnp.dot(mb, hi, preferred_element_type=F32) + jnp.dot(mb, mid, preferred_element_type=F32)
            + jnp.dot(mb, lo, preferred_element_type=F32))


def _sigmoid(x):
    return 1.0 / (1.0 + jnp.exp(-x))


def _head_ones():
    r = lax.broadcasted_iota(jnp.int32, (2 * LANES, LANES), 0) % LANES // HEAD
    c = lax.broadcasted_iota(jnp.int32, (2 * LANES, LANES), 1) // HEAD
    return (r == c).astype(BF16)


def _head_sum(x, ones):
    parts = []
    for s in range(0, x.shape[1], LANES):
        xs = x[:, s:s + LANES]
        hi = xs.astype(BF16)
        lo = (xs - hi.astype(F32)).astype(BF16)
        parts.append(jnp.dot(jnp.concatenate([hi, lo], axis=1), ones, preferred_element_type=F32))
    return parts[0] if len(parts) == 1 else jnp.concatenate(parts, axis=1)


def _rms(x, g):
    return x * lax.rsqrt(jnp.mean(x * x, axis=-1, keepdims=True) + RMS_EPS) * g


def _in_proj_kernel(rows_ref, x_ref, g_ref, wt_ref, o_ref, h_ref):
    del rows_ref
    @pl.when(pl.program_id(1) == 0)
    def _():
        h_ref[...] = _rms(x_ref[...], g_ref[...]).astype(BF16)

    o_ref[...] = _dot_nt(h_ref[...], wt_ref[...])


def _in_proj(x, g, wt, src_rows, tm, tn):
    m, d = x.shape
    n_blocks = len(src_rows)
    return pl.pallas_call(
        _in_proj_kernel,
        grid_spec=pltpu.PrefetchScalarGridSpec(
            num_scalar_prefetch=1,
            grid=(m // tm, n_blocks),
            in_specs=[pl.BlockSpec((tm, d), lambda i, j, rows: (i, 0)),
                      pl.BlockSpec((1, d), lambda i, j, rows: (0, 0)),
                      pl.BlockSpec((pl.Element(tn), pl.Element(d)), lambda i, j, rows: (pl.multiple_of(rows[j], 8), 0))],
            out_specs=pl.BlockSpec((tm, tn), lambda i, j, rows: (i, j)),
            scratch_shapes=[pltpu.VMEM((tm, d), BF16)]),
        out_shape=jax.ShapeDtypeStruct((m, n_blocks * tn), F32),
        compiler_params=_params(("parallel", "arbitrary")),
        name="in_proj",
    )(jnp.asarray(src_rows, jnp.int32), x, g, wt)


def _lo_mix(pl_, ql, mu_l, n_wa, n_gl):
    n = n_wa + n_gl
    lo = pl_[:, :n]
    lo = lo + (ql[:, :n] - lo) * mu_l[:, :n]
    wa = lo[:, :n_wa]
    return jnp.tanh(wa), wa, _sigmoid(lo[:, n_wa:])


def _run(steps):
    try:
        while True:
            next(steps)
    except StopIteration as done:
        return done.value


def _col_mix_steps(pr, pk, pv, qr, qk, qv, mu_r, mu_k, mu_v, tanh_wa, wa, sig_gl, w0, w_up, a0, a_up, g_up,
                   k_k, k_a):
    r = pr + (qr - pr) * mu_r
    yield
    k = pk + (qk - pk) * mu_k
    yield
    v = pv + (qv - pv) * mu_v
    yield
    lw = -DECAY_SCALE * _sigmoid(w0 + _dot(tanh_wa, w_up))
    yield
    a = _sigmoid(a0 + _dot(wa, a_up))
    yield
    gate = _dot(sig_gl, g_up)
    yield
    kk = k * k_k
    ss = _head_sum(kk * kk, _head_ones())
    yield
    kk = kk / jnp.maximum(jnp.sqrt(ss), 1e-12)
    yield
    return r, lw, k * (1.0 + (a - 1.0) * k_a), v, -kk, kk * a, gate


def _rwkv_mix(pr, pk, pv, pl_, qr, qk, qv, ql, mu_r, mu_k, mu_v, mu_l, w0, w_up, a0, a_up, g_up, k_k, k_a):
    lo = _lo_mix(pl_, ql, mu_l, w_up.shape[0], g_up.shape[0])
    return _run(_col_mix_steps(pr, pk, pv, qr, qk, qv, mu_r, mu_k, mu_v, *lo, w0, w_up, a0, a_up, g_up,
                               k_k, k_a))


def _rwkv_out_steps(o, r, k, v, gate, lnx_g, lnx_b, r_k):
    ones = _head_ones()
    mu = _head_sum(o, ones) * (1.0 / HEAD)
    yield
    oc = o - mu
    var = _head_sum(oc * oc, ones) * (1.0 / HEAD)
    yield
    y = oc * lax.rsqrt(var + GN_EPS) * lnx_g + lnx_b
    yield
    bonus = _head_sum(r * k * r_k, ones)
    yield
    return (y + bonus * v) * gate


def _rwkv_out(*args):
    return _run(_rwkv_out_steps(*args))


def _each(fn, *lists):
    out = []
    for args in zip(*lists):
        out.append(fn(*args))
        yield
    return out


def _run_together(work):
    values = [None] * len(work)
    longest = max(n for _, n in work)
    credit = [0.0] * len(work)
    live = set(range(len(work)))
    while live:
        for i, (steps, n) in enumerate(work):
            credit[i] += n / longest
            while i in live and credit[i] >= 1.0:
                credit[i] -= 1.0
                try:
                    next(steps)
                except StopIteration as done:
                    values[i] = done.value
                    live.discard(i)
    return values


def _shifted(p, carry_ref):
    rows = lax.broadcasted_iota(jnp.int32, p.shape, 0)
    q = jnp.where(rows == 0, carry_ref[...], pltpu.roll(p, 1, axis=0))
    carry_ref[...] = p[p.shape[0] - 1:, :]
    return q


def _prep_sample_kernel(*refs):
    ins, rows, cols = refs[:-10], refs[-10:-6], refs[-6:]
    r, lw, k, v, aa, bb, gate = _rwkv_mix(*[ref[...] for ref in ins])
    for ref, val in zip(rows, (r, k, v, gate)):
        ref[...] = val
    for ref, val in zip(cols, (r, lw, k, v, aa, bb)):
        ref[...] = val.T


class _Layout:
    def __init__(self, d_rwkv, d_sgu, lora_w, lora_a, lora_g, tn):
        self.d_rwkv, self.d_sgu, self.tn = d_rwkv, d_sgu, tn
        self.wa_w = lora_w + lora_a
        self.gl_w = -(-lora_g // LANES) * LANES
        self.d_shift = 3 * d_rwkv + self.wa_w + lora_g
        assert self.wa_w == LANES and self.wa_w + self.gl_w <= tn
        assert d_rwkv % tn == 0 and d_sgu % tn == 0
        self.u0 = 0
        self.vs0 = d_sgu
        self.r0 = 2 * d_sgu
        self.k0 = self.r0 + d_rwkv
        self.v0 = self.k0 + d_rwkv
        self.lo0 = self.v0 + d_rwkv
        self.width = self.lo0 + tn
        self.src_rows = (list(range(self.d_shift, self.d_shift + 2 * d_sgu, tn))
                         + list(range(0, 3 * d_rwkv + tn, tn)))

    def rw_pieces(self, a):
        d = self.d_rwkv
        pad = [(0, 0)] * (a.ndim - 1) + [(0, self.tn - (self.d_shift - 3 * d))]
        return a[..., :d], a[..., d:2 * d], a[..., 2 * d:3 * d], jnp.pad(a[..., 3 * d:], pad)

    def shift_row(self, p_rows):
        return p_rows[:, self.r0:self.r0 + self.d_shift]


def _prep_weights(lay, mu, w0, w_up, a0, a_up, g_up, k_k, k_a):
    d = lay.d_rwkv
    lora_w, lora_g = w_up.shape[0], g_up.shape[0]
    mus = [m[None, :] for m in lay.rw_pieces(mu)]
    w_up_p = jnp.pad(w_up, ((0, lay.wa_w - lora_w), (0, 0)))
    a_up_p = jnp.pad(a_up, ((lora_w, 0), (0, 0)))
    g_up_p = jnp.pad(g_up, ((0, lay.gl_w - lora_g), (0, 0)))
    return mus + [w0[None, :], w_up_p, a0[None, :], a_up_p, g_up_p, k_k.reshape(1, d), k_a.reshape(1, d)]


def _full(a):
    return pl.BlockSpec(a.shape, lambda *_: (0,) * a.ndim)


def _prep_sample(p, prev, lay, weights):
    d = lay.d_rwkv
    m = p.shape[0]
    p_specs = [pl.BlockSpec((m, d), lambda i: (0, lay.r0 // d)),
               pl.BlockSpec((m, d), lambda i: (0, lay.k0 // d)),
               pl.BlockSpec((m, d), lambda i: (0, lay.v0 // d)),
               pl.BlockSpec((m, lay.tn), lambda i: (0, lay.lo0 // lay.tn))]
    prevs = list(lay.rw_pieces(prev))
    row_spec = pl.BlockSpec((m, d), lambda i: (0, 0))
    col_spec = pl.BlockSpec((d, m), lambda i: (0, 0))
    outs = pl.pallas_call(
        _prep_sample_kernel,
        grid=(1,),
        in_specs=p_specs + [_full(q) for q in prevs] + [_full(w) for w in weights],
        out_specs=[row_spec] * 4 + [col_spec] * 6,
        out_shape=[jax.ShapeDtypeStruct((m, d), F32)] * 4 + [jax.ShapeDtypeStruct((d, m), F32)] * 6,
        compiler_params=_params(("arbitrary",)),
        name="rwkv_prep_sample",
    )(p, p, p, p, *prevs, *weights)
    return outs[:4], outs[4:]


def _pair_stack(x, first_head):
    zero = jnp.zeros_like(x)
    return jnp.concatenate([jnp.where(first_head, x, zero), jnp.where(first_head, zero, x)], axis=0)


def _wkv_pre_steps(r, lw, k, v, a, b, tri, gram_mask):
    c = WKV_CHUNK
    n_pairs = r.shape[1] // LANES
    lane = lax.broadcasted_iota(jnp.int32, (c, LANES), 1)
    h0 = lane < HEAD
    h0x2 = jnp.concatenate([h0, h0], axis=1)

    def cut(x):
        return [x[:, p * LANES:(p + 1) * LANES] for p in range(n_pairs)]

    cum = _dot_exact_lhs(tri, lw)
    yield
    e_out = jnp.exp(-cum)
    a_s = cut(a * jnp.exp(cum - lw))
    r_s = cut(r * jnp.exp(cum))
    yield
    b_s = cut(b * e_out)
    k_s = cut(k * e_out)
    yield
    last = cum[c - 1:, :]
    e_end = jnp.exp(last - cum)
    bk_e = cut(jnp.concatenate([b * e_end, k * e_end], axis=0))
    decay = cut(jnp.exp(last))
    vs = cut(v)
    yield

    grams = yield from _each(lambda ai, ri, bi, ki: jnp.where(
        gram_mask, _dot_nt(jnp.concatenate([ai, ri], axis=0),
                           jnp.concatenate([_pair_stack(bi, h0), _pair_stack(ki, h0)], axis=0)), 0.0),
        a_s, r_s, b_s, k_s)
    v_st = [_pair_stack(x, h0) for x in vs]
    kvs = yield from _each(lambda g, vi: _dot(g[:, LANES:], vi), grams, v_st)
    xs = [jnp.concatenate([ai, kv[:c]], axis=1) for ai, kv in zip(a_s, kvs)]
    pws = [g[:c, :LANES] for g in grams]
    ns = pws
    pws = yield from _each(lambda pw: _dot(pw, _pair_stack(pw, h0)), pws)
    n = 2
    while n < c // 2:
        both = yield from _each(
            lambda pw, nn: _dot(jnp.concatenate([pw, nn], axis=0), _pair_stack(pw, h0)), pws, ns)
        ns = [nn + pw + bo[c:] for nn, pw, bo in zip(ns, pws, both)]
        pws = [bo[:c] for bo in both]
        n *= 2
    ns = yield from _each(lambda nn, pw: nn + pw + _dot(nn, _pair_stack(pw, h0)), ns, pws)
    xs = yield from _each(lambda x, nn: x + _dot(nn, _pair_stack(x, h0x2)), xs, ns)
    qos = yield from _each(lambda g, x: _dot(g[c:, :LANES], _pair_stack(x, h0x2)), grams, xs)
    qp = [jnp.concatenate([ri + qo[:, :LANES], x[:, :LANES]], axis=0) for ri, qo, x in zip(r_s, qos, xs)]
    o2 = [qo[:, LANES:] + kv[c:] for qo, kv in zip(qos, kvs)]
    return dict(qp=qp, o2=o2, u2=[x[:, LANES:] for x in xs], v=vs, bk_e=bk_e, decay=decay)


def _wkv_state_steps(pre, states, state_mask):
    c = WKV_CHUNK
    ous = yield from _each(_dot_nt, pre["qp"], states)
    upds = yield from _each(lambda ou, u2, v, bk: _dot_tn(jnp.concatenate([ou[c:] + u2, v], axis=0), bk),
                            ous, pre["u2"], pre["v"], pre["bk_e"])
    for p, (upd, decay) in enumerate(zip(upds, pre["decay"])):
        states[p] = states[p] * decay + jnp.where(state_mask, upd, 0.0)
    return [ou[:c] + o2 for ou, o2 in zip(ous, pre["o2"])]


def _mixer_prompt_kernel(*refs):
    x_ref, g1_ref, w_ref = refs[:3]
    mix_refs, out_refs, sgu_refs = refs[3:14], refs[14:17], refs[17:21]
    y_ref, yb_ref, sf_ref, sh_ref = refs[21:25]
    carries, s_ref, o_scr, p_scr, sg_scr = refs[25:29], refs[29], refs[30], refs[31], refs[32]
    c = WKV_CHUNK
    tb = y_ref.shape[0]
    d = y_ref.shape[1]
    n_pairs = d // LANES
    n_rw, n_sg = p_scr.shape[1], sg_scr.shape[1]
    sg_row0 = w_ref.shape[0] - n_sg
    t = pl.program_id(1)
    first = t == 0
    last = t == pl.num_programs(1) - 1

    @pl.when(first)
    def _():
        s_ref[...] = jnp.zeros_like(s_ref)
        for carry in carries:
            carry[...] = jnp.zeros_like(carry)

    h = _rms(x_ref[...], g1_ref[...]).astype(BF16)
    for j in range(0, n_rw, PROJ_COLS):
        p_scr[:, j:j + PROJ_COLS] = _dot_nt(h, w_ref[j:j + PROJ_COLS, :])
    p_refs = [p_scr.at[:, j * d:(j + 1) * d] for j in range(3)] + [p_scr.at[:, 3 * d:]]

    def sg_proj_steps():
        for j in range(0, n_sg, PROJ_COLS):
            sg_scr[:, j:j + PROJ_COLS] = _dot_nt(h, w_ref[sg_row0 + j:sg_row0 + j + PROJ_COLS, :])
            yield

    sg_ng, sg_nb, sg_w, sg_bias = sgu_refs
    sri = lax.broadcasted_iota(jnp.int32, (SGU_CHUNK, SGU_CHUNK), 0)
    sci = lax.broadcasted_iota(jnp.int32, (SGU_CHUNK, SGU_CHUNK), 1)
    sg_ws = [jnp.where(sci <= sri, sg_w[g], 0.0).astype(BF16) for g in range(sg_w.shape[0])]

    def sgu_steps(rows):
        u = _gelu(sg_scr[rows, :d])
        yield
        vs = _layernorm(_gelu(sg_scr[rows, d:]), sg_ng[...], sg_nb[...]).astype(BF16)
        yield
        for g, w in enumerate(sg_ws):
            cols = slice(g * SGU_CHUNK, (g + 1) * SGU_CHUNK)
            mix = jnp.dot(w, vs[:, cols], preferred_element_type=F32) + sg_bias[:, cols]
            yb_ref[rows, cols] = (u[:, cols] * mix).astype(yb_ref.dtype)
            yield

    mu_r, mu_k, mu_v, mu_l, w0, w_up, a0, a_up, g_up, k_k, k_a = mix_refs
    ri = lax.broadcasted_iota(jnp.int32, (c, c), 0)
    ci = lax.broadcasted_iota(jnp.int32, (c, c), 1)
    tri = (ci <= ri).astype(F32)
    gr = lax.broadcasted_iota(jnp.int32, (2 * c, 2 * LANES), 0)
    gc = lax.broadcasted_iota(jnp.int32, (2 * c, 2 * LANES), 1) % c
    gram_mask = gc <= jnp.where(gr < c, gr - 1, gr - c)
    sr = lax.broadcasted_iota(jnp.int32, (LANES, LANES), 0) // HEAD
    sc = lax.broadcasted_iota(jnp.int32, (LANES, LANES), 1) // HEAD
    state_mask = sr == sc
    states = [s_ref[p] for p in range(n_pairs)]

    def prep_steps(rows):
        pl_ = p_refs[3][rows, :]
        lo = _lo_mix(pl_, _shifted(pl_, carries[3]), mu_l[...], w_up.shape[0], g_up.shape[0])
        yield
        ps = [ref[rows, :] for ref in p_refs[:3]]
        yield
        qs = [_shifted(p, carry) for p, carry in zip(ps, carries[:3])]
        yield
        vals = yield from _col_mix_steps(*ps, *qs, mu_r[...], mu_k[...], mu_v[...], *lo, w0[...], w_up[...],
                                         a0[...], a_up[...], g_up[...], k_k[...], k_a[...])
        return vals

    def state_steps(rows, pre):
        outs = yield from _wkv_state_steps(pre, states, state_mask)
        o_scr[rows, :] = jnp.concatenate(outs, axis=1)

    def post_steps(rows, vals):
        r, _, k, v, _, _, gate = vals
        y = yield from _rwkv_out_steps(o_scr[rows, :], r, k, v, gate, *[ref[...] for ref in out_refs])
        y_ref[rows, :] = y.astype(y_ref.dtype)

    n_chunks = tb // c
    rows = [slice(j * c, (j + 1) * c) for j in range(n_chunks)]
    n_stages = n_chunks + 3
    sgu_at = {min(2 + 2 * j, n_stages - 1): j for j in range(tb // SGU_CHUNK)}
    vals, pre = {}, {}
    for s in range(n_stages):
        work = []
        if s == 0:
            work.append(("sg_proj", 0, sg_proj_steps(), n_sg // PROJ_COLS))
        if s in sgu_at:
            j = sgu_at[s]
            work.append(("sgu", j, sgu_steps(slice(j * SGU_CHUNK, (j + 1) * SGU_CHUNK)), 2 + len(sg_ws)))
        if s < n_chunks:
            work.append(("prep", s, prep_steps(rows[s]), 12))
        if 0 <= s - 1 < n_chunks:
            work.append(("pre", s - 1, _wkv_pre_steps(*vals[s - 1][:6], tri, gram_mask), 10 * n_pairs + 5))
        if 0 <= s - 2 < n_chunks:
            work.append(("state", s - 2, state_steps(rows[s - 2], pre[s - 2]), 2 * n_pairs + 1))
        if 0 <= s - 3 < n_chunks:
            work.append(("post", s - 3, post_steps(rows[s - 3], vals[s - 3]), 5))
        done = _run_together([(steps, n) for _, _, steps, n in work])
        for (kind, j, _, _), value in zip(work, done):
            if kind == "prep":
                vals[j] = value
            elif kind == "pre":
                pre[j] = value
    for p, s in enumerate(states):
        s_ref[p] = s

    @pl.when(last)
    def _():
        sh_ref[0] = p_scr[tb - 1:, :]
        for p in range(n_pairs):
            s = s_ref[p]
            sf_ref[0, 2 * p] = s[:HEAD, :HEAD]
            sf_ref[0, 2 * p + 1] = s[HEAD:, HEAD:]


def _resident(a):
    return pl.BlockSpec(a.shape, lambda *_: (0,) * a.ndim, pipeline_mode=pl.Buffered(1))


def _mixer_prompt(x, norm_g, wt, lay, mix_weights, out_weights, sgu_weights, batch, seq, tb):
    d = lay.d_rwkv
    d_model = x.shape[1]
    n_rw, n_sg = 3 * d + lay.tn, 2 * lay.d_sgu
    assert lay.d_sgu == d and lay.tn % PROJ_COLS == 0 and wt.shape[0] == lay.d_shift + n_sg
    n_heads = d // HEAD
    nt = seq // tb
    row = lambda b, i: (b * nt + i, 0)
    weights = list(mix_weights) + list(out_weights) + list(sgu_weights)
    y_spec = pl.BlockSpec((tb, d), row)
    return pl.pallas_call(
        _mixer_prompt_kernel,
        grid=(batch, nt),
        in_specs=([pl.BlockSpec((tb, d_model), row), _full(norm_g), _resident(wt)]
                  + [_full(w) for w in weights]),
        out_specs=[y_spec, y_spec,
                   pl.BlockSpec((1, n_heads, HEAD, HEAD), lambda b, i: (b, 0, 0, 0)),
                   pl.BlockSpec((1, 1, n_rw), lambda b, i: (b, 0, 0))],
        out_shape=[jax.ShapeDtypeStruct((batch * seq, d), BF16),
                   jax.ShapeDtypeStruct((batch * seq, d), BF16),
                   jax.ShapeDtypeStruct((batch, n_heads, HEAD, HEAD), F32),
                   jax.ShapeDtypeStruct((batch, 1, n_rw), F32)],
        scratch_shapes=([pltpu.VMEM((1, d), F32)] * 3
                        + [pltpu.VMEM((1, lay.tn), F32), pltpu.VMEM((d // LANES, LANES, LANES), F32),
                           pltpu.VMEM((tb, d), F32), pltpu.VMEM((tb, n_rw), F32),
                           pltpu.VMEM((tb, n_sg), F32)]),
        compiler_params=_params(("parallel", "arbitrary")),
        name="mixer_prompt",
    )(x, norm_g, wt, *weights)


STEP_UNROLL = 4


def _wkv_step_kernel(r_ref, lw_ref, k_ref, v_ref, a_ref, b_ref, s_ref, o_ref, sn_ref):
    a, b, k, r = a_ref[...], b_ref[...], k_ref[...], r_ref[...]
    w = jnp.exp(lw_ref[...])

    def body(j, carry):
        for u in range(STEP_UNROLL):
            i = j * STEP_UNROLL + u
            s = s_ref[0, i]
            sa = jnp.sum(s * a, axis=0, keepdims=True)
            s = s * w + sa * b + v_ref[pl.ds(i, 1), :] * k
            sn_ref[0, i] = s
            o_ref[pl.ds(i, 1), :] = jnp.sum(s * r, axis=0, keepdims=True)
        return carry

    lax.fori_loop(0, s_ref.shape[1] // STEP_UNROLL, body, 0)


def _wkv_step(cols, state):
    d, m = cols[0].shape
    n_heads = d // HEAD
    vec = pl.BlockSpec((HEAD, m), lambda h: (h, 0))
    st = pl.BlockSpec((1, HEAD, HEAD, m), lambda h: (h, 0, 0, 0))
    o, s = pl.pallas_call(
        _wkv_step_kernel,
        grid=(n_heads,),
        in_specs=[vec] * 6 + [st],
        out_specs=[vec, st],
        out_shape=[jax.ShapeDtypeStruct((d, m), F32), jax.ShapeDtypeStruct((n_heads, HEAD, HEAD, m), F32)],
        compiler_params=_params(("parallel",)),
        name="wkv_step",
    )(*cols, jnp.transpose(state, (1, 2, 3, 0)))
    return o, jnp.transpose(s, (3, 0, 1, 2))


def _post_kernel(ot_ref, *refs):
    y_ref = refs[-1]
    y_ref[...] = _rwkv_out(ot_ref[...].T, *[ref[...] for ref in refs[:-1]]).astype(y_ref.dtype)


def _rwkv_post(o_t, r, k, v, g, out_weights):
    m, d = r.shape
    spec = pl.BlockSpec((m, d), lambda i: (0, 0))
    return pl.pallas_call(
        _post_kernel,
        grid=(1,),
        in_specs=[pl.BlockSpec((d, m), lambda i: (0, 0))] + [spec] * 4 + [_full(w) for w in out_weights],
        out_specs=spec,
        out_shape=jax.ShapeDtypeStruct((m, d), BF16),
        compiler_params=_params(("arbitrary",)),
        name="rwkv_post",
    )(o_t, r, k, v, g, *out_weights)


def _gelu(x):
    return 0.5 * x * (1.0 + jnp.tanh(GELU_C * (x + 0.044715 * (x * x * x))))


def _layernorm(x, g, b):
    mu = jnp.mean(x, axis=-1, keepdims=True)
    xc = x - mu
    var = jnp.mean(xc * xc, axis=-1, keepdims=True)
    return xc * lax.rsqrt(var + LN_EPS) * g + b


def _sgu_sample_kernel(pu_ref, pv_ref, ng_ref, nb_ref, w_ref, bias_ref, y_ref, vs_ref):
    u = _gelu(pu_ref[...])
    vs = _layernorm(_gelu(pv_ref[...]), ng_ref[...], nb_ref[...])
    vs_ref[...] = vs
    y_ref[...] = (u * (w_ref[...] * vs + bias_ref[...])).astype(y_ref.dtype)


def _sgu_sample(p, lay, norm_g, norm_b, sgu_w, sgu_b):
    d = lay.d_sgu
    m = p.shape[0]
    w0 = jnp.repeat(sgu_w[:, 0, 0], SGU_CHUNK)[None, :]
    b0 = jnp.repeat(sgu_b[:, 0], SGU_CHUNK)[None, :]
    vec = pl.BlockSpec((1, d), lambda i: (0, 0))
    out = pl.BlockSpec((m, d), lambda i: (0, 0))
    return pl.pallas_call(
        _sgu_sample_kernel,
        grid=(1,),
        in_specs=[pl.BlockSpec((m, d), lambda i: (0, lay.u0 // d)),
                  pl.BlockSpec((m, d), lambda i: (0, lay.vs0 // d)), vec, vec, vec, vec],
        out_specs=[out, out],
        out_shape=[jax.ShapeDtypeStruct((m, d), BF16), jax.ShapeDtypeStruct((m, d), F32)],
        compiler_params=_params(("arbitrary",)),
        name="sgu_sample",
    )(p, p, norm_g[None, :], norm_b[None, :], w0, b0)


def _out_proj_kernel(x_ref, ya_ref, yb_ref, wa_ref, wb_ref, o_ref):
    o_ref[...] = (x_ref[...] + jnp.dot(ya_ref[...], wa_ref[...].astype(BF16), preferred_element_type=F32)
                  + jnp.dot(yb_ref[...], wb_ref[...].astype(BF16), preferred_element_type=F32))


def _out_proj(x, ya, yb, w, tm, tn):
    m, d = x.shape
    da = ya.shape[1]
    return pl.pallas_call(
        _out_proj_kernel,
        grid=(m // tm, d // tn),
        in_specs=[pl.BlockSpec((tm, tn), lambda i, j: (i, j)),
                  pl.BlockSpec((tm, da), lambda i, j: (i, 0)),
                  pl.BlockSpec((tm, da), lambda i, j: (i, 0)),
                  pl.BlockSpec((da, tn), lambda i, j: (0, j)),
                  pl.BlockSpec((da, tn), lambda i, j: (1, j))],
        out_specs=pl.BlockSpec((tm, tn), lambda i, j: (i, j)),
        out_shape=jax.ShapeDtypeStruct((m, d), F32),
        compiler_params=_params(("parallel", "arbitrary")),
        name="out_proj",
    )(x, ya, yb, w, w)


def _ffn_kernel(x_ref, xs_ref, g2_ref, wu_ref, wd_ref, gf_ref, o_ref, os_ref, h_ref, a_ref):
    f = pl.program_id(1)
    last = pl.num_programs(1) - 1
    tm = x_ref.shape[0]

    def up():
        a = jnp.dot(h_ref[...], wu_ref[...].astype(BF16), preferred_element_type=F32)
        return jnp.square(jnp.maximum(a, 0.0)).astype(BF16)

    def down():
        return jnp.dot(a_ref[...], wd_ref[...].astype(BF16), preferred_element_type=F32)

    @pl.when(f == 0)
    def _():
        x, xs = x_ref[...], xs_ref[...]
        h_ref[:tm, :] = _rms(x, g2_ref[...]).astype(BF16)
        h_ref[tm:, :] = _rms(xs, g2_ref[...]).astype(BF16)
        o_ref[...] = x
        os_ref[...] = xs
        a_ref[...] = up()

    @pl.when((f > 0) & (f < last))
    def _():
        acc = down()
        a_new = up()
        o_ref[...] += acc[:tm]
        os_ref[...] += acc[tm:]
        a_ref[...] = a_new

    @pl.when(f == last)
    def _():
        acc = down()
        o_ref[...] = _rms(o_ref[...] + acc[:tm], gf_ref[...])
        os_ref[...] = _rms(os_ref[...] + acc[tm:], gf_ref[...])


def _ffn(x, xs, g2, w_up, w_down, gf, tm, tf):
    m, d = x.shape
    n_blocks = m // tm
    ts = xs.shape[0] // n_blocks
    assert ts * n_blocks == xs.shape[0] and ts % 8 == 0
    nf = w_up.shape[1] // tf
    w_up = w_up.reshape(d, nf, tf).transpose(1, 0, 2).astype(BF16)
    return pl.pallas_call(
        _ffn_kernel,
        grid=(n_blocks, nf + 1),
        in_specs=[pl.BlockSpec((tm, d), lambda i, f: (i, 0)),
                  pl.BlockSpec((ts, d), lambda i, f: (i, 0)),
                  pl.BlockSpec((1, d), lambda i, f: (0, 0)),
                  pl.BlockSpec((None, d, tf), lambda i, f: (jnp.minimum(f, nf - 1), 0, 0)),
                  pl.BlockSpec((tf, d), lambda i, f: (jnp.maximum(f - 1, 0), 0)),
                  pl.BlockSpec((1, d), lambda i, f: (0, 0))],
        out_specs=[pl.BlockSpec((tm, d), lambda i, f: (i, 0)),
                   pl.BlockSpec((ts, d), lambda i, f: (i, 0))],
        out_shape=[jax.ShapeDtypeStruct((m, d), F32), jax.ShapeDtypeStruct(xs.shape, F32)],
        scratch_shapes=[pltpu.VMEM((tm + ts, d), BF16), pltpu.VMEM((tm + ts, tf), BF16)],
        compiler_params=_params(("parallel", "arbitrary")),
        name="ffn",
    )(x, xs, g2[None, :], w_up, w_down, gf[None, :])


def _row_tile(m, cap):
    t = min(m, cap)
    assert m % t == 0
    return t


def kernel(x_prompt, x_sample, state_wkv, state_shift, norm1_g, w_in, mu_shift, w0, w_up, a0, a_up, g_up,
           k_k, k_a, r_k, lnx_g, lnx_b, sgu_norm_g, sgu_norm_b, sgu_w, sgu_b, w_out, norm2_g, w_ffn_up,
           w_ffn_down, norm_f_g):
    batch, seq, d_model = x_prompt.shape
    n_dec, dec_seq, _ = x_sample.shape
    depth = w_in.shape[0]
    assert depth == 1 and dec_seq == 1
    d_rwkv = w0.shape[1]
    d_sgu = sgu_norm_g.shape[1]
    tn_in = 512
    lay = _Layout(d_rwkv, d_sgu, w_up.shape[1], a_up.shape[1], g_up.shape[1], tn_in)
    w_in_t = w_in[0].T.astype(BF16)
    prep_w = _prep_weights(lay, mu_shift[0], w0[0], w_up[0], a0[0], a_up[0], g_up[0], k_k[0], k_a[0])

    out_w = [w.reshape(1, d_rwkv) for w in (lnx_g[0], lnx_b[0], r_k[0])]

    w_out_b = w_out[0].astype(BF16)

    xp = x_prompt.reshape(batch * seq, d_model)
    sgu_bias = jnp.repeat(sgu_b[0].T, SGU_CHUNK, axis=1)
    ya, yb, wkv_p, last_p = _mixer_prompt(
        xp, norm1_g, w_in_t, lay, prep_w, out_w, [sgu_norm_g, sgu_norm_b, sgu_w[0], sgu_bias],
        batch, seq, 256)
    x1p = _out_proj(xp, ya, yb, w_out_b, _row_tile(batch * seq, 2048), 512)
    shift_p = last_p[:, 0, :lay.d_shift]

    xs = x_sample.reshape(n_dec, d_model)
    ps = _in_proj(xs, norm1_g, w_in_t, lay.src_rows, n_dec, tn_in)
    (r, k, v, g), step_cols = _prep_sample(ps, state_shift[0], lay, prep_w)
    o_t, wkv_s = _wkv_step(step_cols, state_wkv[0])
    yb, vs = _sgu_sample(ps, lay, sgu_norm_g[0], sgu_norm_b[0], sgu_w[0], sgu_b[0])
    ya = _rwkv_post(o_t, r, k, v, g, out_w)
    x1s = _out_proj(xs, ya, yb, w_out_b, n_dec, 512)
    shift_s = lay.shift_row(ps)

    y_prompt, y_sample = _ffn(x1p, x1s, norm2_g[0], w_ffn_up[0], w_ffn_down[0], norm_f_g,
                              _row_tile(batch * seq, 1024), 256)
    y_prompt = y_prompt.reshape(batch, seq, d_model)
    y_sample = y_sample.reshape(n_dec, 1, d_model)
    return (y_prompt, y_sample, wkv_p[None], shift_p[None], wkv_s[None], shift_s[None],
            vs.reshape(1, n_dec, 1, d_sgu))
```

```python
import functools
import math

import jax
import jax.numpy as jnp
from jax import lax
from jax.experimental import pallas as pl
from jax.experimental.pallas import tpu as pltpu

F32 = jnp.float32
BF16 = jnp.bfloat16

HEAD = 64
LANES = 128
GROUP_HEADS = 2
GROUP = GROUP_HEADS * HEAD
SGU_CHUNK = 128
WKV_CHUNK = 64
PROJ_COLS = 512
MIXER_ROWS = 512
OUT_PROJ_ROWS = 2048
OUT_PROJ_COLS = 512
FFN_ROWS = 1024
FFN_HIDDEN = 256
RMS_EPS = 1e-5
LN_EPS = 1e-5
GN_EPS = 64e-5
DECAY_SCALE = math.exp(-0.5)
GELU_C = math.sqrt(2.0 / math.pi)
VMEM_LIMIT = 58 * 1024 * 1024


def _params(sem):
    return pltpu.CompilerParams(dimension_semantics=sem, vmem_limit_bytes=VMEM_LIMIT)


def _dot(a, b):
    return jnp.dot(a.astype(BF16), b.astype(BF16), preferred_element_type=F32)


def _dot_nt(a, b):
    return lax.dot_general(a.astype(BF16), b.astype(BF16), (((1,), (1,)), ((), ())),
                           preferred_element_type=F32)


def _dot_tn(a, b):
    return lax.dot_general(a.astype(BF16), b.astype(BF16), (((0,), (0,)), ((), ())),
                           preferred_element_type=F32)


def _split3(x):
    hi = x.astype(BF16)
    r1 = x - hi.astype(F32)
    mid = r1.astype(BF16)
    lo = (r1 - mid.astype(F32)).astype(BF16)
    return hi, mid, lo


def _dot_exact_lhs(m, x):
    hi, mid, lo = _split3(x)
    mb = m.astype(BF16)
    return (jnp.dot(mb, hi, preferred_element_type=F32) + jnp.dot(mb, mid, preferred_element_type=F32)
            + jnp.dot(mb, lo, preferred_element_type=F32))


def _sigmoid(x):
    return 1.0 / (1.0 + jnp.exp(-x))


def _head_ones():
    r = lax.broadcasted_iota(jnp.int32, (2 * LANES, LANES), 0) % LANES // HEAD
    c = lax.broadcasted_iota(jnp.int32, (2 * LANES, LANES), 1) // HEAD
    return (r == c).astype(BF16)


def _head_sum(x, ones):
    parts = []
    for s in range(0, x.shape[1], LANES):
        xs = x[:, s:s + LANES]
        hi = xs.astype(BF16)
        lo = (xs - hi.astype(F32)).astype(BF16)
        parts.append(jnp.dot(jnp.concatenate([hi, lo], axis=1), ones, preferred_element_type=F32))
    return parts[0] if len(parts) == 1 else jnp.concatenate(parts, axis=1)


def _rms(x, g):
    return x * lax.rsqrt(jnp.mean(x * x, axis=-1, keepdims=True) + RMS_EPS) * g


def _in_proj_kernel(rows_ref, x_ref, g_ref, wt_ref, o_ref, h_ref):
    del rows_ref
    @pl.when(pl.program_id(1) == 0)
    def _():
        h_ref[...] = _rms(x_ref[...], g_ref[...]).astype(BF16)

    o_ref[...] = _dot_nt(h_ref[...], wt_ref[...])


def _in_proj(x, g, wt, src_rows, tm, tn):
    m, d = x.shape
    n_blocks = len(src_rows)
    return pl.pallas_call(
        _in_proj_kernel,
        grid_spec=pltpu.PrefetchScalarGridSpec(
            num_scalar_prefetch=1,
            grid=(m // tm, n_blocks),
            in_specs=[pl.BlockSpec((tm, d), lambda i, j, rows: (i, 0)),
                      pl.BlockSpec((1, d), lambda i, j, rows: (0, 0)),
                      pl.BlockSpec((pl.Element(tn), pl.Element(d)), lambda i, j, rows: (pl.multiple_of(rows[j], 8), 0))],
            out_specs=pl.BlockSpec((tm, tn), lambda i, j, rows: (i, j)),
            scratch_shapes=[pltpu.VMEM((tm, d), BF16)]),
        out_shape=jax.ShapeDtypeStruct((m, n_blocks * tn), F32),
        compiler_params=_params(("parallel", "arbitrary")),
        name="in_proj",
    )(jnp.asarray(src_rows, jnp.int32), x, g, wt)


def _lo_mix(pl_, ql, mu_l, n_wa, n_gl):
    n = n_wa + n_gl
    lo = pl_[:, :n]
    lo = lo + (ql[:, :n] - lo) * mu_l[:, :n]
    wa = lo[:, :n_wa]
    return jnp.tanh(wa), wa, _sigmoid(lo[:, n_wa:])


def _run(steps):
    try:
        while True:
            next(steps)
    except StopIteration as done:
        return done.value


def _col_mix_steps(pr, pk, pv, qr, qk, qv, mu_r, mu_k, mu_v, tanh_wa, wa, sig_gl, w0, w_up, a0, a_up, g_up,
                   k_k, k_a):
    r = pr + (qr - pr) * mu_r
    yield
    k = pk + (qk - pk) * mu_k
    yield
    v = pv + (qv - pv) * mu_v
    yield
    lw = -DECAY_SCALE * _sigmoid(w0 + _dot(tanh_wa, w_up))
    yield
    a = _sigmoid(a0 + _dot(wa, a_up))
    yield
    gate = _dot(sig_gl, g_up)
    yield
    kk = k * k_k
    ss = _head_sum(kk * kk, _head_ones())
    yield
    kk = kk / jnp.maximum(jnp.sqrt(ss), 1e-12)
    yield
    return r, lw, k * (1.0 + (a - 1.0) * k_a), v, -kk, kk * a, gate


def _rwkv_mix(pr, pk, pv, pl_, qr, qk, qv, ql, mu_r, mu_k, mu_v, mu_l, w0, w_up, a0, a_up, g_up, k_k, k_a):
    lo = _lo_mix(pl_, ql, mu_l, w_up.shape[0], g_up.shape[0])
    return _run(_col_mix_steps(pr, pk, pv, qr, qk, qv, mu_r, mu_k, mu_v, *lo, w0, w_up, a0, a_up, g_up,
                               k_k, k_a))


def _rwkv_out_steps(o, r, k, v, gate, lnx_g, lnx_b, r_k):
    ones = _head_ones()
    mu = _head_sum(o, ones) * (1.0 / HEAD)
    yield
    oc = o - mu
    var = _head_sum(oc * oc, ones) * (1.0 / HEAD)
    yield
    y = oc * lax.rsqrt(var + GN_EPS) * lnx_g + lnx_b
    yield
    bonus = _head_sum(r * k * r_k, ones)
    yield
    return (y + bonus * v) * gate


def _rwkv_out(*args):
    return _run(_rwkv_out_steps(*args))


def _each(fn, *lists):
    out = []
    for args in zip(*lists):
        out.append(fn(*args))
        yield
    return out


def _run_together(work):
    values = [None] * len(work)
    longest = max(n for _, n in work)
    credit = [0.0] * len(work)
    live = set(range(len(work)))
    while live:
        for i, (steps, n) in enumerate(work):
            credit[i] += n / longest
            while i in live and credit[i] >= 1.0:
                credit[i] -= 1.0
                try:
                    next(steps)
                except StopIteration as done:
                    values[i] = done.value
                    live.discard(i)
    return values


def _shifted(p, carry_ref):
    rows = lax.broadcasted_iota(jnp.int32, p.shape, 0)
    q = jnp.where(rows == 0, carry_ref[...], pltpu.roll(p, 1, axis=0))
    carry_ref[...] = p[p.shape[0] - 1:, :]
    return q


def _prep_sample_kernel(*refs):
    ins, rows, cols = refs[:-10], refs[-10:-6], refs[-6:]
    r, lw, k, v, aa, bb, gate = _rwkv_mix(*[ref[...] for ref in ins])
    for ref, val in zip(rows, (r, k, v, gate)):
        ref[...] = val
    for ref, val in zip(cols, (r, lw, k, v, aa, bb)):
        ref[...] = val.T


class _Layout:
    def __init__(self, d_rwkv, d_sgu, lora_w, lora_a, lora_g, tn):
        self.d_rwkv, self.d_sgu, self.tn = d_rwkv, d_sgu, tn
        self.wa_w = lora_w + lora_a
        self.gl_w = -(-lora_g // LANES) * LANES
        self.d_shift = 3 * d_rwkv + self.wa_w + lora_g
        assert self.wa_w == LANES and self.wa_w + self.gl_w <= tn
        assert d_rwkv % tn == 0 and d_sgu % tn == 0
        self.u0 = 0
        self.vs0 = d_sgu
        self.r0 = 2 * d_sgu
        self.k0 = self.r0 + d_rwkv
        self.v0 = self.k0 + d_rwkv
        self.lo0 = self.v0 + d_rwkv
        self.width = self.lo0 + tn
        self.src_rows = (list(range(self.d_shift, self.d_shift + 2 * d_sgu, tn))
                         + list(range(0, 3 * d_rwkv + tn, tn)))

    def rw_pieces(self, a):
        d = self.d_rwkv
        pad = [(0, 0)] * (a.ndim - 1) + [(0, self.tn - (self.d_shift - 3 * d))]
        return a[..., :d], a[..., d:2 * d], a[..., 2 * d:3 * d], jnp.pad(a[..., 3 * d:], pad)

    def shift_row(self, p_rows):
        return p_rows[:, self.r0:self.r0 + self.d_shift]


def _prep_weights(lay, mu, w0, w_up, a0, a_up, g_up, k_k, k_a):
    d = lay.d_rwkv
    lora_w, lora_g = w_up.shape[0], g_up.shape[0]
    mus = [m[None, :] for m in lay.rw_pieces(mu)]
    w_up_p = jnp.pad(w_up, ((0, lay.wa_w - lora_w), (0, 0)))
    a_up_p = jnp.pad(a_up, ((lora_w, 0), (0, 0)))
    g_up_p = jnp.pad(g_up, ((0, lay.gl_w - lora_g), (0, 0)))
    return mus + [w0[None, :], w_up_p, a0[None, :], a_up_p, g_up_p, k_k.reshape(1, d), k_a.reshape(1, d)]


def _full(a):
    return pl.BlockSpec(a.shape, lambda *_: (0,) * a.ndim)


def _prep_sample(p, prev, lay, weights):
    d = lay.d_rwkv
    m = p.shape[0]
    p_specs = [pl.BlockSpec((m, d), lambda i: (0, lay.r0 // d)),
               pl.BlockSpec((m, d), lambda i: (0, lay.k0 // d)),
               pl.BlockSpec((m, d), lambda i: (0, lay.v0 // d)),
               pl.BlockSpec((m, lay.tn), lambda i: (0, lay.lo0 // lay.tn))]
    prevs = list(lay.rw_pieces(prev))
    row_spec = pl.BlockSpec((m, d), lambda i: (0, 0))
    col_spec = pl.BlockSpec((d, m), lambda i: (0, 0))
    outs = pl.pallas_call(
        _prep_sample_kernel,
        grid=(1,),
        in_specs=p_specs + [_full(q) for q in prevs] + [_full(w) for w in weights],
        out_specs=[row_spec] * 4 + [col_spec] * 6,
        out_shape=[jax.ShapeDtypeStruct((m, d), F32)] * 4 + [jax.ShapeDtypeStruct((d, m), F32)] * 6,
        compiler_params=_params(("arbitrary",)),
        name="rwkv_prep_sample",
    )(p, p, p, p, *prevs, *weights)
    return outs[:4], outs[4:]


def _head_stack(x):
    head = lax.broadcasted_iota(jnp.int32, x.shape, 1) % GROUP // HEAD
    return jnp.concatenate([jnp.where(head == h, x, 0.0) for h in range(GROUP_HEADS)], axis=0)


def _wkv_pre_steps(r, lw, k, v, a, b, tri, gram_mask):
    c = WKV_CHUNK
    n_groups = r.shape[1] // GROUP

    def cut(x):
        return [x[:, g * GROUP:(g + 1) * GROUP] for g in range(n_groups)]

    cum = _dot_exact_lhs(tri, lw)
    yield
    e_out = jnp.exp(-cum)
    a_s = cut(a * jnp.exp(cum - lw))
    r_s = cut(r * jnp.exp(cum))
    yield
    b_s = cut(b * e_out)
    k_s = cut(k * e_out)
    yield
    last = cum[c - 1:, :]
    e_end = jnp.exp(last - cum)
    bk_e = cut(jnp.concatenate([b * e_end, k * e_end], axis=0))
    decay = cut(jnp.exp(last))
    vs = cut(v)
    yield

    grams = yield from _each(lambda ai, ri, bi, ki: jnp.where(
        gram_mask, _dot_nt(jnp.concatenate([ai, ri], axis=0),
                           jnp.concatenate([_head_stack(bi), _head_stack(ki)], axis=0)), 0.0),
        a_s, r_s, b_s, k_s)
    v_st = [_head_stack(x) for x in vs]
    kvs = yield from _each(lambda g, vi: _dot(g[:, GROUP:], vi), grams, v_st)
    xs = [jnp.concatenate([ai, kv[:c]], axis=1) for ai, kv in zip(a_s, kvs)]
    pws = [g[:c, :GROUP] for g in grams]
    ns = pws
    pws = yield from _each(lambda pw: _dot(pw, _head_stack(pw)), pws)
    n = 2
    while n < c // 2:
        both = yield from _each(
            lambda pw, nn: _dot(jnp.concatenate([pw, nn], axis=0), _head_stack(pw)), pws, ns)
        ns = [nn + pw + bo[c:] for nn, pw, bo in zip(ns, pws, both)]
        pws = [bo[:c] for bo in both]
        n *= 2
    ns = yield from _each(lambda nn, pw: nn + pw + _dot(nn, _head_stack(pw)), ns, pws)
    xs = yield from _each(lambda x, nn: x + _dot(nn, _head_stack(x)), xs, ns)
    qos = yield from _each(lambda g, x: _dot(g[c:, :GROUP], _head_stack(x)), grams, xs)
    qp = [jnp.concatenate([ri + qo[:, :GROUP], x[:, :GROUP]], axis=0) for ri, qo, x in zip(r_s, qos, xs)]
    o2 = [qo[:, GROUP:] + kv[c:] for qo, kv in zip(qos, kvs)]
    return dict(qp=qp, o2=o2, u2=[x[:, GROUP:] for x in xs], v=vs, bk_e=bk_e, decay=decay)


def _wkv_state_steps(pre, states, state_mask):
    c = WKV_CHUNK
    ous = yield from _each(_dot_nt, pre["qp"], states)
    upds = yield from _each(lambda ou, u2, v, bk: _dot_tn(jnp.concatenate([ou[c:] + u2, v], axis=0), bk),
                            ous, pre["u2"], pre["v"], pre["bk_e"])
    for p, (upd, decay) in enumerate(zip(upds, pre["decay"])):
        states[p] = states[p] * decay + jnp.where(state_mask, upd, 0.0)
    return [ou[:c] + o2 for ou, o2 in zip(ous, pre["o2"])]


def _mixer_prompt_kernel(*refs):
    x_ref, g1_ref, w_ref = refs[:3]
    mix_refs, out_refs, sgu_refs = refs[3:14], refs[14:17], refs[17:21]
    y_ref, yb_ref, sf_ref, sh_ref = refs[21:25]
    carries, s_ref, o_scr, p_scr, sg_scr = refs[25:29], refs[29], refs[30], refs[31], refs[32]
    c = WKV_CHUNK
    tb = y_ref.shape[0]
    d = y_ref.shape[1]
    n_groups = d // GROUP
    n_rw, n_sg = p_scr.shape[1], sg_scr.shape[1]
    sg_row0 = w_ref.shape[0] - n_sg
    t = pl.program_id(1)
    first = t == 0
    last = t == pl.num_programs(1) - 1

    @pl.when(first)
    def _():
        s_ref[...] = jnp.zeros_like(s_ref)
        for carry in carries:
            carry[...] = jnp.zeros_like(carry)

    h = _rms(x_ref[...], g1_ref[...]).astype(BF16)
    for j in range(0, n_rw, PROJ_COLS):
        p_scr[:, j:j + PROJ_COLS] = _dot_nt(h, w_ref[j:j + PROJ_COLS, :])
    p_refs = [p_scr.at[:, j * d:(j + 1) * d] for j in range(3)] + [p_scr.at[:, 3 * d:]]

    def sg_proj_steps():
        for j in range(0, n_sg, PROJ_COLS):
            sg_scr[:, j:j + PROJ_COLS] = _dot_nt(h, w_ref[sg_row0 + j:sg_row0 + j + PROJ_COLS, :])
            yield

    sg_ng, sg_nb, sg_w, sg_bias = sgu_refs
    sri = lax.broadcasted_iota(jnp.int32, (SGU_CHUNK, SGU_CHUNK), 0)
    sci = lax.broadcasted_iota(jnp.int32, (SGU_CHUNK, SGU_CHUNK), 1)
    sg_ws = [jnp.where(sci <= sri, sg_w[g], 0.0).astype(BF16) for g in range(sg_w.shape[0])]

    def sgu_steps(rows):
        u = _gelu(sg_scr[rows, :d])
        yield
        vs = _layernorm(_gelu(sg_scr[rows, d:]), sg_ng[...], sg_nb[...]).astype(BF16)
        yield
        for g, w in enumerate(sg_ws):
            cols = slice(g * SGU_CHUNK, (g + 1) * SGU_CHUNK)
            mix = jnp.dot(w, vs[:, cols], preferred_element_type=F32) + sg_bias[:, cols]
            yb_ref[rows, cols] = (u[:, cols] * mix).astype(yb_ref.dtype)
            yield

    mu_r, mu_k, mu_v, mu_l, w0, w_up, a0, a_up, g_up, k_k, k_a = mix_refs
    ri = lax.broadcasted_iota(jnp.int32, (c, c), 0)
    ci = lax.broadcasted_iota(jnp.int32, (c, c), 1)
    tri = (ci <= ri).astype(F32)
    gr = lax.broadcasted_iota(jnp.int32, (2 * c, 2 * GROUP), 0)
    gc = lax.broadcasted_iota(jnp.int32, (2 * c, 2 * GROUP), 1) % c
    gram_mask = gc <= jnp.where(gr < c, gr - 1, gr - c)
    sr = lax.broadcasted_iota(jnp.int32, (GROUP, GROUP), 0) // HEAD
    sc = lax.broadcasted_iota(jnp.int32, (GROUP, GROUP), 1) // HEAD
    state_mask = sr == sc
    states = [s_ref[g] for g in range(n_groups)]

    def prep_steps(rows):
        pl_ = p_refs[3][rows, :]
        lo = _lo_mix(pl_, _shifted(pl_, carries[3]), mu_l[...], w_up.shape[0], g_up.shape[0])
        yield
        ps = [ref[rows, :] for ref in p_refs[:3]]
        yield
        qs = [_shifted(p, carry) for p, carry in zip(ps, carries[:3])]
        yield
        vals = yield from _col_mix_steps(*ps, *qs, mu_r[...], mu_k[...], mu_v[...], *lo, w0[...], w_up[...],
                                         a0[...], a_up[...], g_up[...], k_k[...], k_a[...])
        return vals

    def state_steps(rows, pre):
        outs = yield from _wkv_state_steps(pre, states, state_mask)
        o_scr[rows, :] = jnp.concatenate(outs, axis=1)

    def post_steps(rows, vals):
        r, _, k, v, _, _, gate = vals
        y = yield from _rwkv_out_steps(o_scr[rows, :], r, k, v, gate, *[ref[...] for ref in out_refs])
        y_ref[rows, :] = y.astype(y_ref.dtype)

    n_chunks = tb // c
    rows = [slice(j * c, (j + 1) * c) for j in range(n_chunks)]
    n_stages = n_chunks + 3
    sgu_at = {min(2 + 2 * j, n_stages - 1): j for j in range(tb // SGU_CHUNK)}
    vals, pre = {}, {}
    for s in range(n_stages):
        work = []
        if s == 0:
            work.append(("sg_proj", 0, sg_proj_steps(), n_sg // PROJ_COLS))
        if s in sgu_at:
            j = sgu_at[s]
            work.append(("sgu", j, sgu_steps(slice(j * SGU_CHUNK, (j + 1) * SGU_CHUNK)), 2 + len(sg_ws)))
        if s < n_chunks:
            work.append(("prep", s, prep_steps(rows[s]), 12))
        if 0 <= s - 1 < n_chunks:
            work.append(("pre", s - 1, _wkv_pre_steps(*vals[s - 1][:6], tri, gram_mask), 10 * n_groups + 5))
        if 0 <= s - 2 < n_chunks:
            work.append(("state", s - 2, state_steps(rows[s - 2], pre[s - 2]), 2 * n_groups + 1))
        if 0 <= s - 3 < n_chunks:
            work.append(("post", s - 3, post_steps(rows[s - 3], vals[s - 3]), 5))
        done = _run_together([(steps, n) for _, _, steps, n in work])
        for (kind, j, _, _), value in zip(work, done):
            if kind == "prep":
                vals[j] = value
            elif kind == "pre":
                pre[j] = value
    for g, s in enumerate(states):
        s_ref[g] = s

    @pl.when(last)
    def _():
        sh_ref[0] = p_scr[tb - 1:, :]
        for g in range(n_groups):
            for h in range(GROUP_HEADS):
                sf_ref[0, g * GROUP_HEADS + h] = s_ref[g, h * HEAD:(h + 1) * HEAD, h * HEAD:(h + 1) * HEAD]


def _resident(a):
    return pl.BlockSpec(a.shape, lambda *_: (0,) * a.ndim, pipeline_mode=pl.Buffered(1))


def _mixer_prompt(x, norm_g, wt, lay, mix_weights, out_weights, sgu_weights, batch, seq, tb):
    d = lay.d_rwkv
    d_model = x.shape[1]
    n_rw, n_sg = 3 * d + lay.tn, 2 * lay.d_sgu
    assert lay.d_sgu == d and lay.tn % PROJ_COLS == 0 and wt.shape[0] == lay.d_shift + n_sg
    n_heads = d // HEAD
    nt = seq // tb
    row = lambda b, i: (b * nt + i, 0)
    weights = list(mix_weights) + list(out_weights) + list(sgu_weights)
    y_spec = pl.BlockSpec((tb, d), row)
    return pl.pallas_call(
        _mixer_prompt_kernel,
        grid=(batch, nt),
        in_specs=([pl.BlockSpec((tb, d_model), row), _full(norm_g), _resident(wt)]
                  + [_full(w) for w in weights]),
        out_specs=[y_spec, y_spec,
                   pl.BlockSpec((1, n_heads, HEAD, HEAD), lambda b, i: (b, 0, 0, 0)),
                   pl.BlockSpec((1, 1, n_rw), lambda b, i: (b, 0, 0))],
        out_shape=[jax.ShapeDtypeStruct((batch * seq, d), BF16),
                   jax.ShapeDtypeStruct((batch * seq, d), BF16),
                   jax.ShapeDtypeStruct((batch, n_heads, HEAD, HEAD), F32),
                   jax.ShapeDtypeStruct((batch, 1, n_rw), F32)],
        scratch_shapes=([pltpu.VMEM((1, d), F32)] * 3
                        + [pltpu.VMEM((1, lay.tn), F32), pltpu.VMEM((d // GROUP, GROUP, GROUP), F32),
                           pltpu.VMEM((tb, d), F32), pltpu.VMEM((tb, n_rw), F32),
                           pltpu.VMEM((tb, n_sg), F32)]),
        compiler_params=_params(("parallel", "arbitrary")),
        name="mixer_prompt",
    )(x, norm_g, wt, *weights)


STEP_UNROLL = 8


def _wkv_step_kernel(r_ref, lw_ref, k_ref, v_ref, a_ref, b_ref, s_ref, o_ref, sn_ref):
    a, b, k, r = a_ref[...], b_ref[...], k_ref[...], r_ref[...]
    w = jnp.exp(lw_ref[...])

    def body(j, carry):
        for u in range(STEP_UNROLL):
            i = j * STEP_UNROLL + u
            s = s_ref[0, i]
            sa = jnp.sum(s * a, axis=0, keepdims=True)
            s = s * w + sa * b + v_ref[pl.ds(i, 1), :] * k
            sn_ref[0, i] = s
            o_ref[pl.ds(i, 1), :] = jnp.sum(s * r, axis=0, keepdims=True)
        return carry

    lax.fori_loop(0, s_ref.shape[1] // STEP_UNROLL, body, 0)


def _wkv_step(cols, state):
    d, m = cols[0].shape
    n_heads = d // HEAD
    vec = pl.BlockSpec((HEAD, m), lambda h: (h, 0))
    st = pl.BlockSpec((1, HEAD, HEAD, m), lambda h: (h, 0, 0, 0))
    o, s = pl.pallas_call(
        _wkv_step_kernel,
        grid=(n_heads,),
        in_specs=[vec] * 6 + [st],
        out_specs=[vec, st],
        out_shape=[jax.ShapeDtypeStruct((d, m), F32), jax.ShapeDtypeStruct((n_heads, HEAD, HEAD, m), F32)],
        compiler_params=_params(("parallel",)),
        name="wkv_step",
    )(*cols, jnp.transpose(state, (1, 2, 3, 0)))
    return o, jnp.transpose(s, (3, 0, 1, 2))


def _post_kernel(ot_ref, *refs):
    y_ref = refs[-1]
    y_ref[...] = _rwkv_out(ot_ref[...].T, *[ref[...] for ref in refs[:-1]]).astype(y_ref.dtype)


def _rwkv_post(o_t, r, k, v, g, out_weights):
    m, d = r.shape
    spec = pl.BlockSpec((m, d), lambda i: (0, 0))
    return pl.pallas_call(
        _post_kernel,
        grid=(1,),
        in_specs=[pl.BlockSpec((d, m), lambda i: (0, 0))] + [spec] * 4 + [_full(w) for w in out_weights],
        out_specs=spec,
        out_shape=jax.ShapeDtypeStruct((m, d), BF16),
        compiler_params=_params(("arbitrary",)),
        name="rwkv_post",
    )(o_t, r, k, v, g, *out_weights)


def _gelu(x):
    return 0.5 * x * (1.0 + jnp.tanh(GELU_C * (x + 0.044715 * (x * x * x))))


def _layernorm(x, g, b):
    mu = jnp.mean(x, axis=-1, keepdims=True)
    xc = x - mu
    var = jnp.mean(xc * xc, axis=-1, keepdims=True)
    return xc * lax.rsqrt(var + LN_EPS) * g + b


def _sgu_sample_kernel(pu_ref, pv_ref, ng_ref, nb_ref, w_ref, bias_ref, y_ref, vs_ref):
    u = _gelu(pu_ref[...])
    vs = _layernorm(_gelu(pv_ref[...]), ng_ref[...], nb_ref[...])
    vs_ref[...] = vs
    y_ref[...] = (u * (w_ref[...] * vs + bias_ref[...])).astype(y_ref.dtype)


def _sgu_sample(p, lay, norm_g, norm_b, sgu_w, sgu_b):
    d = lay.d_sgu
    m = p.shape[0]
    w0 = jnp.repeat(sgu_w[:, 0, 0], SGU_CHUNK)[None, :]
    b0 = jnp.repeat(sgu_b[:, 0], SGU_CHUNK)[None, :]
    vec = pl.BlockSpec((1, d), lambda i: (0, 0))
    out = pl.BlockSpec((m, d), lambda i: (0, 0))
    return pl.pallas_call(
        _sgu_sample_kernel,
        grid=(1,),
        in_specs=[pl.BlockSpec((m, d), lambda i: (0, lay.u0 // d)),
                  pl.BlockSpec((m, d), lambda i: (0, lay.vs0 // d)), vec, vec, vec, vec],
        out_specs=[out, out],
        out_shape=[jax.ShapeDtypeStruct((m, d), BF16), jax.ShapeDtypeStruct((m, d), F32)],
        compiler_params=_params(("arbitrary",)),
        name="sgu_sample",
    )(p, p, norm_g[None, :], norm_b[None, :], w0, b0)


def _out_proj_kernel(x_ref, ya_ref, yb_ref, wa_ref, wb_ref, o_ref):
    o_ref[...] = (x_ref[...] + jnp.dot(ya_ref[...], wa_ref[...].astype(BF16), preferred_element_type=F32)
                  + jnp.dot(yb_ref[...], wb_ref[...].astype(BF16), preferred_element_type=F32))


def _out_proj(x, ya, yb, w, tm, tn):
    m, d = x.shape
    da = ya.shape[1]
    return pl.pallas_call(
        _out_proj_kernel,
        grid=(m // tm, d // tn),
        in_specs=[pl.BlockSpec((tm, tn), lambda i, j: (i, j)),
                  pl.BlockSpec((tm, da), lambda i, j: (i, 0)),
                  pl.BlockSpec((tm, da), lambda i, j: (i, 0)),
                  pl.BlockSpec((da, tn), lambda i, j: (0, j)),
                  pl.BlockSpec((da, tn), lambda i, j: (1, j))],
        out_specs=pl.BlockSpec((tm, tn), lambda i, j: (i, j)),
        out_shape=jax.ShapeDtypeStruct((m, d), F32),
        compiler_params=_params(("parallel", "arbitrary")),
        name="out_proj",
    )(x, ya, yb, w, w)


def _ffn_kernel(x_ref, xs_ref, g2_ref, wu_ref, wd_ref, gf_ref, o_ref, os_ref, h_ref, a_ref):
    f = pl.program_id(1)
    last = pl.num_programs(1) - 1
    tm = x_ref.shape[0]

    def up():
        a = jnp.dot(h_ref[...], wu_ref[...].astype(BF16), preferred_element_type=F32)
        return jnp.square(jnp.maximum(a, 0.0)).astype(BF16)

    def down():
        return jnp.dot(a_ref[...], wd_ref[...].astype(BF16), preferred_element_type=F32)

    @pl.when(f == 0)
    def _():
        x, xs = x_ref[...], xs_ref[...]
        h_ref[:tm, :] = _rms(x, g2_ref[...]).astype(BF16)
        h_ref[tm:, :] = _rms(xs, g2_ref[...]).astype(BF16)
        o_ref[...] = x
        os_ref[...] = xs
        a_ref[...] = up()

    @pl.when((f > 0) & (f < last))
    def _():
        acc = down()
        a_new = up()
        o_ref[...] += acc[:tm]
        os_ref[...] += acc[tm:]
        a_ref[...] = a_new

    @pl.when(f == last)
    def _():
        acc = down()
        o_ref[...] = _rms(o_ref[...] + acc[:tm], gf_ref[...])
        os_ref[...] = _rms(os_ref[...] + acc[tm:], gf_ref[...])


def _ffn(x, xs, g2, w_up, w_down, gf, tm, tf):
    m, d = x.shape
    n_blocks = m // tm
    ts = xs.shape[0] // n_blocks
    assert ts * n_blocks == xs.shape[0] and ts % 8 == 0
    nf = w_up.shape[1] // tf
    return pl.pallas_call(
        _ffn_kernel,
        grid=(n_blocks, nf + 1),
        in_specs=[pl.BlockSpec((tm, d), lambda i, f: (i, 0)),
                  pl.BlockSpec((ts, d), lambda i, f: (i, 0)),
                  pl.BlockSpec((1, d), lambda i, f: (0, 0)),
                  pl.BlockSpec((d, tf), lambda i, f: (0, jnp.minimum(f, nf - 1))),
                  pl.BlockSpec((tf, d), lambda i, f: (jnp.maximum(f - 1, 0), 0)),
                  pl.BlockSpec((1, d), lambda i, f: (0, 0))],
        out_specs=[pl.BlockSpec((tm, d), lambda i, f: (i, 0)),
                   pl.BlockSpec((ts, d), lambda i, f: (i, 0))],
        out_shape=[jax.ShapeDtypeStruct((m, d), F32), jax.ShapeDtypeStruct(xs.shape, F32)],
        scratch_shapes=[pltpu.VMEM((tm + ts, d), BF16), pltpu.VMEM((tm + ts, tf), BF16)],
        compiler_params=_params(("parallel", "arbitrary")),
        name="ffn",
    )(x, xs, g2[None, :], w_up, w_down, gf[None, :])


def _row_tile(m, cap):
    t = min(m, cap)
    assert m % t == 0
    return t


def kernel(x_prompt, x_sample, state_wkv, state_shift, norm1_g, w_in, mu_shift, w0, w_up, a0, a_up, g_up,
           k_k, k_a, r_k, lnx_g, lnx_b, sgu_norm_g, sgu_norm_b, sgu_w, sgu_b, w_out, norm2_g, w_ffn_up,
           w_ffn_down, norm_f_g):
    batch, seq, d_model = x_prompt.shape
    n_dec, dec_seq, _ = x_sample.shape
    depth = w_in.shape[0]
    assert depth == 1 and dec_seq == 1
    d_rwkv = w0.shape[1]
    d_sgu = sgu_norm_g.shape[1]
    lay = _Layout(d_rwkv, d_sgu, w_up.shape[1], a_up.shape[1], g_up.shape[1], PROJ_COLS)
    w_in_t = w_in[0].T.astype(BF16)
    prep_w = _prep_weights(lay, mu_shift[0], w0[0], w_up[0], a0[0], a_up[0], g_up[0], k_k[0], k_a[0])

    out_w = [w.reshape(1, d_rwkv) for w in (lnx_g[0], lnx_b[0], r_k[0])]

    w_out_b = w_out[0].astype(BF16)

    xp = x_prompt.reshape(batch * seq, d_model)
    sgu_bias = jnp.repeat(sgu_b[0].T, SGU_CHUNK, axis=1)
    ya, yb, wkv_p, last_p = _mixer_prompt(
        xp, norm1_g, w_in_t, lay, prep_w, out_w, [sgu_norm_g, sgu_norm_b, sgu_w[0], sgu_bias],
        batch, seq, _row_tile(seq, MIXER_ROWS))
    x1p = _out_proj(xp, ya, yb, w_out_b, _row_tile(batch * seq, OUT_PROJ_ROWS), OUT_PROJ_COLS)
    shift_p = last_p[:, 0, :lay.d_shift]

    xs = x_sample.reshape(n_dec, d_model)
    ps = _in_proj(xs, norm1_g, w_in_t, lay.src_rows, n_dec, PROJ_COLS)
    (r, k, v, g), step_cols = _prep_sample(ps, state_shift[0], lay, prep_w)
    o_t, wkv_s = _wkv_step(step_cols, state_wkv[0])
    yb, vs = _sgu_sample(ps, lay, sgu_norm_g[0], sgu_norm_b[0], sgu_w[0], sgu_b[0])
    ya = _rwkv_post(o_t, r, k, v, g, out_w)
    x1s = _out_proj(xs, ya, yb, w_out_b, n_dec, OUT_PROJ_COLS)
    shift_s = lay.shift_row(ps)

    y_prompt, y_sample = _ffn(x1p, x1s, norm2_g[0], w_ffn_up[0], w_ffn_down[0], norm_f_g,
                              _row_tile(batch * seq, FFN_ROWS), FFN_HIDDEN)
    y_prompt = y_prompt.reshape(batch, seq, d_model)
    y_sample = y_sample.reshape(n_dec, 1, d_model)
    return (y_prompt, y_sample, wkv_p[None], shift_p[None], wkv_s[None], shift_s[None],
            vs.reshape(1, n_dec, 1, d_sgu))
```

```python
import functools
import math

import jax
import jax.numpy as jnp
from jax import lax
from jax.experimental import pallas as pl
from jax.experimental.pallas import tpu as pltpu

F32 = jnp.float32
BF16 = jnp.bfloat16

HEAD = 64
LANES = 128
GROUP_HEADS = 2
GROUP = GROUP_HEADS * HEAD
SGU_CHUNK = 128
WKV_CHUNK = 64
PROJ_COLS = 512
MIXER_ROWS = 512
OUT_PROJ_ROWS = 2048
OUT_PROJ_COLS = 512
FFN_ROWS = 1024
FFN_HIDDEN = 256
RMS_EPS = 1e-5
LN_EPS = 1e-5
GN_EPS = 64e-5
DECAY_SCALE = math.exp(-0.5)
GELU_C = math.sqrt(2.0 / math.pi)
VMEM_LIMIT = 58 * 1024 * 1024


def _params(sem):
    return pltpu.CompilerParams(dimension_semantics=sem, vmem_limit_bytes=VMEM_LIMIT)


def _dot(a, b):
    return jnp.dot(a.astype(BF16), b.astype(BF16), preferred_element_type=F32)


def _dot_nt(a, b):
    return lax.dot_general(a.astype(BF16), b.astype(BF16), (((1,), (1,)), ((), ())),
                           preferred_element_type=F32)


def _dot_tn(a, b):
    return lax.dot_general(a.astype(BF16), b.astype(BF16), (((0,), (0,)), ((), ())),
                           preferred_element_type=F32)


def _split3(x):
    hi = x.astype(BF16)
    r1 = x - hi.astype(F32)
    mid = r1.astype(BF16)
    lo = (r1 - mid.astype(F32)).astype(BF16)
    return hi, mid, lo


def _dot_exact_lhs(m, x):
    hi, mid, lo = _split3(x)
    mb = m.astype(BF16)
    return (jnp.dot(mb, hi, preferred_element_type=F32) + jnp.dot(mb, mid, preferred_element_type=F32)
            + jnp.dot(mb, lo, preferred_element_type=F32))


def _sigmoid(x):
    return 1.0 / (1.0 + jnp.exp(-x))


def _head_ones():
    r = lax.broadcasted_iota(jnp.int32, (2 * LANES, LANES), 0) % LANES // HEAD
    c = lax.broadcasted_iota(jnp.int32, (2 * LANES, LANES), 1) // HEAD
    return (r == c).astype(BF16)


def _head_sum(x, ones):
    parts = []
    for s in range(0, x.shape[1], LANES):
        xs = x[:, s:s + LANES]
        hi = xs.astype(BF16)
        lo = (xs - hi.astype(F32)).astype(BF16)
        parts.append(jnp.dot(jnp.concatenate([hi, lo], axis=1), ones, preferred_element_type=F32))
    return parts[0] if len(parts) == 1 else jnp.concatenate(parts, axis=1)


def _rms(x, g):
    return x * lax.rsqrt(jnp.mean(x * x, axis=-1, keepdims=True) + RMS_EPS) * g


def _in_proj_kernel(rows_ref, x_ref, g_ref, wt_ref, o_ref, h_ref):
    del rows_ref
    @pl.when(pl.program_id(1) == 0)
    def _():
        h_ref[...] = _rms(x_ref[...], g_ref[...]).astype(BF16)

    o_ref[...] = _dot_nt(h_ref[...], wt_ref[...])


def _in_proj(x, g, wt, src_rows, tm, tn):
    m, d = x.shape
    n_blocks = len(src_rows)
    return pl.pallas_call(
        _in_proj_kernel,
        grid_spec=pltpu.PrefetchScalarGridSpec(
            num_scalar_prefetch=1,
            grid=(m // tm, n_blocks),
            in_specs=[pl.BlockSpec((tm, d), lambda i, j, rows: (i, 0)),
                      pl.BlockSpec((1, d), lambda i, j, rows: (0, 0)),
                      pl.BlockSpec((pl.Element(tn), pl.Element(d)), lambda i, j, rows: (pl.multiple_of(rows[j], 8), 0))],
            out_specs=pl.BlockSpec((tm, tn), lambda i, j, rows: (i, j)),
            scratch_shapes=[pltpu.VMEM((tm, d), BF16)]),
        out_shape=jax.ShapeDtypeStruct((m, n_blocks * tn), F32),
        compiler_params=_params(("parallel", "arbitrary")),
        name="in_proj",
    )(jnp.asarray(src_rows, jnp.int32), x, g, wt)


def _lo_mix(pl_, ql, mu_l, n_wa, n_gl):
    n = n_wa + n_gl
    lo = pl_[:, :n]
    lo = lo + (ql[:, :n] - lo) * mu_l[:, :n]
    wa = lo[:, :n_wa]
    return jnp.tanh(wa), wa, _sigmoid(lo[:, n_wa:])


def _run(steps):
    try:
        while True:
            next(steps)
    except StopIteration as done:
        return done.value


def _col_mix_steps(pr, pk, pv, qr, qk, qv, mu_r, mu_k, mu_v, tanh_wa, wa, sig_gl, w0, w_up, a0, a_up, g_up,
                   k_k, k_a):
    r = pr + (qr - pr) * mu_r
    yield
    k = pk + (qk - pk) * mu_k
    yield
    v = pv + (qv - pv) * mu_v
    yield
    lw = -DECAY_SCALE * _sigmoid(w0 + _dot(tanh_wa, w_up))
    yield
    a = _sigmoid(a0 + _dot(wa, a_up))
    yield
    gate = _dot(sig_gl, g_up)
    yield
    kk = k * k_k
    ss = _head_sum(kk * kk, _head_ones())
    yield
    kk = kk / jnp.maximum(jnp.sqrt(ss), 1e-12)
    yield
    return r, lw, k * (1.0 + (a - 1.0) * k_a), v, -kk, kk * a, gate


def _rwkv_mix(pr, pk, pv, pl_, qr, qk, qv, ql, mu_r, mu_k, mu_v, mu_l, w0, w_up, a0, a_up, g_up, k_k, k_a):
    lo = _lo_mix(pl_, ql, mu_l, w_up.shape[0], g_up.shape[0])
    return _run(_col_mix_steps(pr, pk, pv, qr, qk, qv, mu_r, mu_k, mu_v, *lo, w0, w_up, a0, a_up, g_up,
                               k_k, k_a))


def _rwkv_out_steps(o, r, k, v, gate, lnx_g, lnx_b, r_k):
    ones = _head_ones()
    mu = _head_sum(o, ones) * (1.0 / HEAD)
    yield
    oc = o - mu
    var = _head_sum(oc * oc, ones) * (1.0 / HEAD)
    yield
    y = oc * lax.rsqrt(var + GN_EPS) * lnx_g + lnx_b
    yield
    bonus = _head_sum(r * k * r_k, ones)
    yield
    return (y + bonus * v) * gate


def _rwkv_out(*args):
    return _run(_rwkv_out_steps(*args))


def _each(fn, *lists):
    out = []
    for args in zip(*lists):
        out.append(fn(*args))
        yield
    return out


def _run_together(work):
    values = [None] * len(work)
    longest = max(n for _, n in work)
    credit = [0.0] * len(work)
    live = set(range(len(work)))
    while live:
        for i, (steps, n) in enumerate(work):
            credit[i] += n / longest
            while i in live and credit[i] >= 1.0:
                credit[i] -= 1.0
                try:
                    next(steps)
                except StopIteration as done:
                    values[i] = done.value
                    live.discard(i)
    return values


def _shifted(p, carry_ref):
    rows = lax.broadcasted_iota(jnp.int32, p.shape, 0)
    q = jnp.where(rows == 0, carry_ref[...], pltpu.roll(p, 1, axis=0))
    carry_ref[...] = p[p.shape[0] - 1:, :]
    return q


def _prep_sample_kernel(*refs):
    ins, rows, cols = refs[:-10], refs[-10:-6], refs[-6:]
    r, lw, k, v, aa, bb, gate = _rwkv_mix(*[ref[...] for ref in ins])
    for ref, val in zip(rows, (r, k, v, gate)):
        ref[...] = val
    for ref, val in zip(cols, (r, lw, k, v, aa, bb)):
        ref[...] = val.T


class _Layout:
    def __init__(self, d_rwkv, d_sgu, lora_w, lora_a, lora_g, tn):
        self.d_rwkv, self.d_sgu, self.tn = d_rwkv, d_sgu, tn
        self.wa_w = lora_w + lora_a
        self.gl_w = -(-lora_g // LANES) * LANES
        self.d_shift = 3 * d_rwkv + self.wa_w + lora_g
        assert self.wa_w == LANES and self.wa_w + self.gl_w <= tn
        assert d_rwkv % tn == 0 and d_sgu % tn == 0
        self.u0 = 0
        self.vs0 = d_sgu
        self.r0 = 2 * d_sgu
        self.k0 = self.r0 + d_rwkv
        self.v0 = self.k0 + d_rwkv
        self.lo0 = self.v0 + d_rwkv
        self.width = self.lo0 + tn
        self.src_rows = (list(range(self.d_shift, self.d_shift + 2 * d_sgu, tn))
                         + list(range(0, 3 * d_rwkv + tn, tn)))

    def rw_pieces(self, a):
        d = self.d_rwkv
        pad = [(0, 0)] * (a.ndim - 1) + [(0, self.tn - (self.d_shift - 3 * d))]
        return a[..., :d], a[..., d:2 * d], a[..., 2 * d:3 * d], jnp.pad(a[..., 3 * d:], pad)

    def shift_row(self, p_rows):
        return p_rows[:, self.r0:self.r0 + self.d_shift]


def _prep_weights(lay, mu, w0, w_up, a0, a_up, g_up, k_k, k_a):
    d = lay.d_rwkv
    lora_w, lora_g = w_up.shape[0], g_up.shape[0]
    mus = [m[None, :] for m in lay.rw_pieces(mu)]
    w_up_p = jnp.pad(w_up, ((0, lay.wa_w - lora_w), (0, 0)))
    a_up_p = jnp.pad(a_up, ((lora_w, 0), (0, 0)))
    g_up_p = jnp.pad(g_up, ((0, lay.gl_w - lora_g), (0, 0)))
    return mus + [w0[None, :], w_up_p, a0[None, :], a_up_p, g_up_p, k_k.reshape(1, d), k_a.reshape(1, d)]


def _full(a):
    return pl.BlockSpec(a.shape, lambda *_: (0,) * a.ndim)


def _prep_sample(p, prev, lay, weights):
    d = lay.d_rwkv
    m = p.shape[0]
    p_specs = [pl.BlockSpec((m, d), lambda i: (0, lay.r0 // d)),
               pl.BlockSpec((m, d), lambda i: (0, lay.k0 // d)),
               pl.BlockSpec((m, d), lambda i: (0, lay.v0 // d)),
               pl.BlockSpec((m, lay.tn), lambda i: (0, lay.lo0 // lay.tn))]
    prevs = list(lay.rw_pieces(prev))
    row_spec = pl.BlockSpec((m, d), lambda i: (0, 0))
    col_spec = pl.BlockSpec((d, m), lambda i: (0, 0))
    outs = pl.pallas_call(
        _prep_sample_kernel,
        grid=(1,),
        in_specs=p_specs + [_full(q) for q in prevs] + [_full(w) for w in weights],
        out_specs=[row_spec] * 4 + [col_spec] * 6,
        out_shape=[jax.ShapeDtypeStruct((m, d), F32)] * 4 + [jax.ShapeDtypeStruct((d, m), F32)] * 6,
        compiler_params=_params(("arbitrary",)),
        name="rwkv_prep_sample",
    )(p, p, p, p, *prevs, *weights)
    return outs[:4], outs[4:]


def _head_stack(x):
    head = lax.broadcasted_iota(jnp.int32, x.shape, 1) % GROUP // HEAD
    return jnp.concatenate([jnp.where(head == h, x, 0.0) for h in range(GROUP_HEADS)], axis=0)


def _wkv_pre_steps(r, lw, k, v, a, b, tri, gram_mask):
    c = WKV_CHUNK
    n_groups = r.shape[1] // GROUP

    def cut(x):
        return [x[:, g * GROUP:(g + 1) * GROUP] for g in range(n_groups)]

    cum = _dot_exact_lhs(tri, lw)
    yield
    e_out = jnp.exp(-cum)
    a_s = cut(a * jnp.exp(cum - lw))
    r_s = cut(r * jnp.exp(cum))
    yield
    b_s = cut(b * e_out)
    k_s = cut(k * e_out)
    yield
    last = cum[c - 1:, :]
    e_end = jnp.exp(last - cum)
    bk_e = cut(jnp.concatenate([b * e_end, k * e_end], axis=0))
    decay = cut(jnp.exp(last))
    vs = cut(v)
    yield

    grams = yield from _each(lambda ai, ri, bi, ki: jnp.where(
        gram_mask, _dot_nt(jnp.concatenate([ai, ri], axis=0),
                           jnp.concatenate([_head_stack(bi), _head_stack(ki)], axis=0)), 0.0),
        a_s, r_s, b_s, k_s)
    v_st = [_head_stack(x) for x in vs]
    kvs = yield from _each(lambda g, vi: _dot(g[:, GROUP:], vi), grams, v_st)
    xs = [jnp.concatenate([ai, kv[:c]], axis=1) for ai, kv in zip(a_s, kvs)]
    pws = [g[:c, :GROUP] for g in grams]
    ns = pws
    pws = yield from _each(lambda pw: _dot(pw, _head_stack(pw)), pws)
    n = 2
    while n < c // 2:
        both = yield from _each(
            lambda pw, nn: _dot(jnp.concatenate([pw, nn], axis=0), _head_stack(pw)), pws, ns)
        ns = [nn + pw + bo[c:] for nn, pw, bo in zip(ns, pws, both)]
        pws = [bo[:c] for bo in both]
        n *= 2
    ns = yield from _each(lambda nn, pw: nn + pw + _dot(nn, _head_stack(pw)), ns, pws)
    xs = yield from _each(lambda x, nn: x + _dot(nn, _head_stack(x)), xs, ns)
    qos = yield from _each(lambda g, x: _dot(g[c:, :GROUP], _head_stack(x)), grams, xs)
    qp = [jnp.concatenate([ri + qo[:, :GROUP], x[:, :GROUP]], axis=0) for ri, qo, x in zip(r_s, qos, xs)]
    o2 = [qo[:, GROUP:] + kv[c:] for qo, kv in zip(qos, kvs)]
    return dict(qp=qp, o2=o2, u2=[x[:, GROUP:] for x in xs], v=vs, bk_e=bk_e, decay=decay)


def _wkv_state_steps(pre, states, state_mask):
    c = WKV_CHUNK
    ous = yield from _each(_dot_nt, pre["qp"], states)
    upds = yield from _each(lambda ou, u2, v, bk: _dot_tn(jnp.concatenate([ou[c:] + u2, v], axis=0), bk),
                            ous, pre["u2"], pre["v"], pre["bk_e"])
    for p, (upd, decay) in enumerate(zip(upds, pre["decay"])):
        states[p] = states[p] * decay + jnp.where(state_mask, upd, 0.0)
    return [ou[:c] + o2 for ou, o2 in zip(ous, pre["o2"])]


def _mixer_prompt_kernel(*refs):
    x_ref, g1_ref, w_ref = refs[:3]
    mix_refs, out_refs, sgu_refs = refs[3:14], refs[14:17], refs[17:21]
    y_ref, yb_ref, sf_ref, sh_ref = refs[21:25]
    carries, s_ref, o_scr, p_scr, sg_scr = refs[25:29], refs[29], refs[30], refs[31], refs[32]
    c = WKV_CHUNK
    tb = y_ref.shape[0]
    d = y_ref.shape[1]
    n_groups = d // GROUP
    n_rw, n_sg = p_scr.shape[1], sg_scr.shape[1]
    sg_row0 = w_ref.shape[0] - n_sg
    t = pl.program_id(1)
    first = t == 0
    last = t == pl.num_programs(1) - 1

    @pl.when(first)
    def _():
        s_ref[...] = jnp.zeros_like(s_ref)
        for carry in carries:
            carry[...] = jnp.zeros_like(carry)

    h = _rms(x_ref[...], g1_ref[...]).astype(BF16)
    for j in range(0, n_rw, PROJ_COLS):
        p_scr[:, j:j + PROJ_COLS] = _dot_nt(h, w_ref[j:j + PROJ_COLS, :])
    p_refs = [p_scr.at[:, j * d:(j + 1) * d] for j in range(3)] + [p_scr.at[:, 3 * d:]]

    def sg_proj_steps():
        for j in range(0, n_sg, PROJ_COLS):
            sg_scr[:, j:j + PROJ_COLS] = _dot_nt(h, w_ref[sg_row0 + j:sg_row0 + j + PROJ_COLS, :])
            yield

    sg_ng, sg_nb, sg_w, sg_bias = sgu_refs
    sri = lax.broadcasted_iota(jnp.int32, (SGU_CHUNK, SGU_CHUNK), 0)
    sci = lax.broadcasted_iota(jnp.int32, (SGU_CHUNK, SGU_CHUNK), 1)
    sg_ws = [jnp.where(sci <= sri, sg_w[g], 0.0).astype(BF16) for g in range(sg_w.shape[0])]

    def sgu_steps(rows):
        u = _gelu(sg_scr[rows, :d])
        yield
        vs = _layernorm(_gelu(sg_scr[rows, d:]), sg_ng[...], sg_nb[...]).astype(BF16)
        yield
        for g, w in enumerate(sg_ws):
            cols = slice(g * SGU_CHUNK, (g + 1) * SGU_CHUNK)
            mix = jnp.dot(w, vs[:, cols], preferred_element_type=F32) + sg_bias[:, cols]
            yb_ref[rows, cols] = (u[:, cols] * mix).astype(yb_ref.dtype)
            yield

    mu_r, mu_k, mu_v, mu_l, w0, w_up, a0, a_up, g_up, k_k, k_a = mix_refs
    ri = lax.broadcasted_iota(jnp.int32, (c, c), 0)
    ci = lax.broadcasted_iota(jnp.int32, (c, c), 1)
    tri = (ci <= ri).astype(F32)
    gr = lax.broadcasted_iota(jnp.int32, (2 * c, 2 * GROUP), 0)
    gc = lax.broadcasted_iota(jnp.int32, (2 * c, 2 * GROUP), 1) % c
    gram_mask = gc <= jnp.where(gr < c, gr - 1, gr - c)
    sr = lax.broadcasted_iota(jnp.int32, (GROUP, GROUP), 0) // HEAD
    sc = lax.broadcasted_iota(jnp.int32, (GROUP, GROUP), 1) // HEAD
    state_mask = sr == sc
    states = [s_ref[g] for g in range(n_groups)]

    def prep_steps(rows):
        pl_ = p_refs[3][rows, :]
        lo = _lo_mix(pl_, _shifted(pl_, carries[3]), mu_l[...], w_up.shape[0], g_up.shape[0])
        yield
        ps = [ref[rows, :] for ref in p_refs[:3]]
        yield
        qs = [_shifted(p, carry) for p, carry in zip(ps, carries[:3])]
        yield
        vals = yield from _col_mix_steps(*ps, *qs, mu_r[...], mu_k[...], mu_v[...], *lo, w0[...], w_up[...],
                                         a0[...], a_up[...], g_up[...], k_k[...], k_a[...])
        return vals

    def state_steps(rows, pre):
        outs = yield from _wkv_state_steps(pre, states, state_mask)
        o_scr[rows, :] = jnp.concatenate(outs, axis=1)

    def post_steps(rows, vals):
        r, _, k, v, _, _, gate = vals
        y = yield from _rwkv_out_steps(o_scr[rows, :], r, k, v, gate, *[ref[...] for ref in out_refs])
        y_ref[rows, :] = y.astype(y_ref.dtype)

    n_chunks = tb // c
    rows = [slice(j * c, (j + 1) * c) for j in range(n_chunks)]
    n_stages = n_chunks + 3
    sgu_at = {min(2 + 2 * j, n_stages - 1): j for j in range(tb // SGU_CHUNK)}
    vals, pre = {}, {}
    for s in range(n_stages):
        work = []
        if s == 0:
            work.append(("sg_proj", 0, sg_proj_steps(), n_sg // PROJ_COLS))
        if s in sgu_at:
            j = sgu_at[s]
            work.append(("sgu", j, sgu_steps(slice(j * SGU_CHUNK, (j + 1) * SGU_CHUNK)), 2 + len(sg_ws)))
        if s < n_chunks:
            work.append(("prep", s, prep_steps(rows[s]), 12))
        if 0 <= s - 1 < n_chunks:
            work.append(("pre", s - 1, _wkv_pre_steps(*vals[s - 1][:6], tri, gram_mask), 10 * n_groups + 5))
        if 0 <= s - 2 < n_chunks:
            work.append(("state", s - 2, state_steps(rows[s - 2], pre[s - 2]), 2 * n_groups + 1))
        if 0 <= s - 3 < n_chunks:
            work.append(("post", s - 3, post_steps(rows[s - 3], vals[s - 3]), 5))
        done = _run_together([(steps, n) for _, _, steps, n in work])
        for (kind, j, _, _), value in zip(work, done):
            if kind == "prep":
                vals[j] = value
            elif kind == "pre":
                pre[j] = value
    for g, s in enumerate(states):
        s_ref[g] = s

    @pl.when(last)
    def _():
        sh_ref[0] = p_scr[tb - 1:, :]
        for g in range(n_groups):
            for h in range(GROUP_HEADS):
                sf_ref[0, g * GROUP_HEADS + h] = s_ref[g, h * HEAD:(h + 1) * HEAD, h * HEAD:(h + 1) * HEAD]


def _resident(a):
    return pl.BlockSpec(a.shape, lambda *_: (0,) * a.ndim, pipeline_mode=pl.Buffered(1))


def _mixer_prompt(x, norm_g, wt, lay, mix_weights, out_weights, sgu_weights, batch, seq, tb):
    d = lay.d_rwkv
    d_model = x.shape[1]
    n_rw, n_sg = 3 * d + lay.tn, 2 * lay.d_sgu
    assert lay.d_sgu == d and lay.tn % PROJ_COLS == 0 and wt.shape[0] == lay.d_shift + n_sg
    n_heads = d // HEAD
    nt = seq // tb
    row = lambda b, i: (b * nt + i, 0)
    weights = list(mix_weights) + list(out_weights) + list(sgu_weights)
    y_spec = pl.BlockSpec((tb, d), row)
    return pl.pallas_call(
        _mixer_prompt_kernel,
        grid=(batch, nt),
        in_specs=([pl.BlockSpec((tb, d_model), row), _full(norm_g), _resident(wt)]
                  + [_full(w) for w in weights]),
        out_specs=[y_spec, y_spec,
                   pl.BlockSpec((1, n_heads, HEAD, HEAD), lambda b, i: (b, 0, 0, 0)),
                   pl.BlockSpec((1, 1, n_rw), lambda b, i: (b, 0, 0))],
        out_shape=[jax.ShapeDtypeStruct((batch * seq, d), BF16),
                   jax.ShapeDtypeStruct((batch * seq, d), BF16),
                   jax.ShapeDtypeStruct((batch, n_heads, HEAD, HEAD), F32),
                   jax.ShapeDtypeStruct((batch, 1, n_rw), F32)],
        scratch_shapes=([pltpu.VMEM((1, d), F32)] * 3
                        + [pltpu.VMEM((1, lay.tn), F32), pltpu.VMEM((d // GROUP, GROUP, GROUP), F32),
                           pltpu.VMEM((tb, d), F32), pltpu.VMEM((tb, n_rw), F32),
                           pltpu.VMEM((tb, n_sg), F32)]),
        compiler_params=_params(("parallel", "arbitrary")),
        name="mixer_prompt",
    )(x, norm_g, wt, *weights)


STEP_HEADS = 2
STEP_UNROLL = 8


def _wkv_step_kernel(r_ref, lw_ref, k_ref, v_ref, a_ref, b_ref, s_ref, o_ref, sn_ref):
    for h in range(s_ref.shape[0]):
        feat = slice(h * HEAD, (h + 1) * HEAD)
        a, b, k, r = a_ref[feat, :], b_ref[feat, :], k_ref[feat, :], r_ref[feat, :]
        w = jnp.exp(lw_ref[feat, :])

        def body(j, carry, h=h, a=a, b=b, k=k, r=r, w=w):
            for u in range(STEP_UNROLL):
                i = j * STEP_UNROLL + u
                s = s_ref[h, i]
                sa = jnp.sum(s * a, axis=0, keepdims=True)
                s = s * w + sa * b + v_ref[pl.ds(h * HEAD + i, 1), :] * k
                sn_ref[h, i] = s
                o_ref[pl.ds(h * HEAD + i, 1), :] = jnp.sum(s * r, axis=0, keepdims=True)
            return carry

        lax.fori_loop(0, HEAD // STEP_UNROLL, body, 0)


def _wkv_step(cols, state):
    d, m = cols[0].shape
    n_heads = d // HEAD
    vec = pl.BlockSpec((STEP_HEADS * HEAD, m), lambda h: (h, 0))
    st = pl.BlockSpec((STEP_HEADS, HEAD, HEAD, m), lambda h: (h, 0, 0, 0))
    o, s = pl.pallas_call(
        _wkv_step_kernel,
        grid=(n_heads // STEP_HEADS,),
        in_specs=[vec] * 6 + [st],
        out_specs=[vec, st],
        out_shape=[jax.ShapeDtypeStruct((d, m), F32), jax.ShapeDtypeStruct((n_heads, HEAD, HEAD, m), F32)],
        compiler_params=_params(("parallel",)),
        name="wkv_step",
    )(*cols, jnp.transpose(state, (1, 2, 3, 0)))
    return o, jnp.transpose(s, (3, 0, 1, 2))


def _post_kernel(ot_ref, *refs):
    y_ref = refs[-1]
    y_ref[...] = _rwkv_out(ot_ref[...].T, *[ref[...] for ref in refs[:-1]]).astype(y_ref.dtype)


def _rwkv_post(o_t, r, k, v, g, out_weights):
    m, d = r.shape
    spec = pl.BlockSpec((m, d), lambda i: (0, 0))
    return pl.pallas_call(
        _post_kernel,
        grid=(1,),
        in_specs=[pl.BlockSpec((d, m), lambda i: (0, 0))] + [spec] * 4 + [_full(w) for w in out_weights],
        out_specs=spec,
        out_shape=jax.ShapeDtypeStruct((m, d), BF16),
        compiler_params=_params(("arbitrary",)),
        name="rwkv_post",
    )(o_t, r, k, v, g, *out_weights)


def _gelu(x):
    return 0.5 * x * (1.0 + jnp.tanh(GELU_C * (x + 0.044715 * (x * x * x))))


def _layernorm(x, g, b):
    mu = jnp.mean(x, axis=-1, keepdims=True)
    xc = x - mu
    var = jnp.mean(xc * xc, axis=-1, keepdims=True)
    return xc * lax.rsqrt(var + LN_EPS) * g + b


def _sgu_sample_kernel(pu_ref, pv_ref, ng_ref, nb_ref, w_ref, bias_ref, y_ref, vs_ref):
    u = _gelu(pu_ref[...])
    vs = _layernorm(_gelu(pv_ref[...]), ng_ref[...], nb_ref[...])
    vs_ref[...] = vs
    y_ref[...] = (u * (w_ref[...] * vs + bias_ref[...])).astype(y_ref.dtype)


def _sgu_sample(p, lay, norm_g, norm_b, sgu_w, sgu_b):
    d = lay.d_sgu
    m = p.shape[0]
    w0 = jnp.repeat(sgu_w[:, 0, 0], SGU_CHUNK)[None, :]
    b0 = jnp.repeat(sgu_b[:, 0], SGU_CHUNK)[None, :]
    vec = pl.BlockSpec((1, d), lambda i: (0, 0))
    out = pl.BlockSpec((m, d), lambda i: (0, 0))
    return pl.pallas_call(
        _sgu_sample_kernel,
        grid=(1,),
        in_specs=[pl.BlockSpec((m, d), lambda i: (0, lay.u0 // d)),
                  pl.BlockSpec((m, d), lambda i: (0, lay.vs0 // d)), vec, vec, vec, vec],
        out_specs=[out, out],
        out_shape=[jax.ShapeDtypeStruct((m, d), BF16), jax.ShapeDtypeStruct((m, d), F32)],
        compiler_params=_params(("arbitrary",)),
        name="sgu_sample",
    )(p, p, norm_g[None, :], norm_b[None, :], w0, b0)


def _out_proj_kernel(x_ref, ya_ref, yb_ref, wa_ref, wb_ref, o_ref):
    o_ref[...] = (x_ref[...] + jnp.dot(ya_ref[...], wa_ref[...].astype(BF16), preferred_element_type=F32)
                  + jnp.dot(yb_ref[...], wb_ref[...].astype(BF16), preferred_element_type=F32))


def _out_proj(x, ya, yb, w, tm, tn):
    m, d = x.shape
    da = ya.shape[1]
    return pl.pallas_call(
        _out_proj_kernel,
        grid=(m // tm, d // tn),
        in_specs=[pl.BlockSpec((tm, tn), lambda i, j: (i, j)),
                  pl.BlockSpec((tm, da), lambda i, j: (i, 0)),
                  pl.BlockSpec((tm, da), lambda i, j: (i, 0)),
                  pl.BlockSpec((da, tn), lambda i, j: (0, j)),
                  pl.BlockSpec((da, tn), lambda i, j: (1, j))],
        out_specs=pl.BlockSpec((tm, tn), lambda i, j: (i, j)),
        out_shape=jax.ShapeDtypeStruct((m, d), F32),
        compiler_params=_params(("parallel", "arbitrary")),
        name="out_proj",
    )(x, ya, yb, w, w)


def _ffn_kernel(x_ref, xs_ref, g2_ref, wu_ref, wd_ref, gf_ref, o_ref, os_ref, h_ref, a_ref):
    f = pl.program_id(1)
    last = pl.num_programs(1) - 1
    tm = x_ref.shape[0]

    def up():
        a = jnp.dot(h_ref[...], wu_ref[...].astype(BF16), preferred_element_type=F32)
        return jnp.square(jnp.maximum(a, 0.0)).astype(BF16)

    def down():
        return jnp.dot(a_ref[...], wd_ref[...].astype(BF16), preferred_element_type=F32)

    @pl.when(f == 0)
    def _():
        x, xs = x_ref[...], xs_ref[...]
        h_ref[:tm, :] = _rms(x, g2_ref[...]).astype(BF16)
        h_ref[tm:, :] = _rms(xs, g2_ref[...]).astype(BF16)
        o_ref[...] = x
        os_ref[...] = xs
        a_ref[...] = up()

    @pl.when((f > 0) & (f < last))
    def _():
        acc = down()
        a_new = up()
        o_ref[...] += acc[:tm]
        os_ref[...] += acc[tm:]
        a_ref[...] = a_new

    @pl.when(f == last)
    def _():
        acc = down()
        o_ref[...] = _rms(o_ref[...] + acc[:tm], gf_ref[...])
        os_ref[...] = _rms(os_ref[...] + acc[tm:], gf_ref[...])


def _ffn(x, xs, g2, w_up, w_down, gf, tm, tf):
    m, d = x.shape
    n_blocks = m // tm
    ts = xs.shape[0] // n_blocks
    assert ts * n_blocks == xs.shape[0] and ts % 8 == 0
    nf = w_up.shape[1] // tf
    return pl.pallas_call(
        _ffn_kernel,
        grid=(n_blocks, nf + 1),
        in_specs=[pl.BlockSpec((tm, d), lambda i, f: (i, 0)),
                  pl.BlockSpec((ts, d), lambda i, f: (i, 0)),
                  pl.BlockSpec((1, d), lambda i, f: (0, 0)),
                  pl.BlockSpec((d, tf), lambda i, f: (0, jnp.minimum(f, nf - 1))),
                  pl.BlockSpec((tf, d), lambda i, f: (jnp.maximum(f - 1, 0), 0)),
                  pl.BlockSpec((1, d), lambda i, f: (0, 0))],
        out_specs=[pl.BlockSpec((tm, d), lambda i, f: (i, 0)),
                   pl.BlockSpec((ts, d), lambda i, f: (i, 0))],
        out_shape=[jax.ShapeDtypeStruct((m, d), F32), jax.ShapeDtypeStruct(xs.shape, F32)],
        scratch_shapes=[pltpu.VMEM((tm + ts, d), BF16), pltpu.VMEM((tm + ts, tf), BF16)],
        compiler_params=_params(("parallel", "arbitrary")),
        name="ffn",
    )(x, xs, g2[None, :], w_up, w_down, gf[None, :])


def _row_tile(m, cap):
    t = min(m, cap)
    assert m % t == 0
    return t


def kernel(x_prompt, x_sample, state_wkv, state_shift, norm1_g, w_in, mu_shift, w0, w_up, a0, a_up, g_up,
           k_k, k_a, r_k, lnx_g, lnx_b, sgu_norm_g, sgu_norm_b, sgu_w, sgu_b, w_out, norm2_g, w_ffn_up,
           w_ffn_down, norm_f_g):
    batch, seq, d_model = x_prompt.shape
    n_dec, dec_seq, _ = x_sample.shape
    depth = w_in.shape[0]
    assert depth == 1 and dec_seq == 1
    d_rwkv = w0.shape[1]
    d_sgu = sgu_norm_g.shape[1]
    lay = _Layout(d_rwkv, d_sgu, w_up.shape[1], a_up.shape[1], g_up.shape[1], PROJ_COLS)
    w_in_t = w_in[0].T.astype(BF16)
    prep_w = _prep_weights(lay, mu_shift[0], w0[0], w_up[0], a0[0], a_up[0], g_up[0], k_k[0], k_a[0])

    out_w = [w.reshape(1, d_rwkv) for w in (lnx_g[0], lnx_b[0], r_k[0])]

    xp = x_prompt.reshape(batch * seq, d_model)
    sgu_bias = jnp.repeat(sgu_b[0].T, SGU_CHUNK, axis=1)
    ya, yb, wkv_p, last_p = _mixer_prompt(
        xp, norm1_g, w_in_t, lay, prep_w, out_w, [sgu_norm_g, sgu_norm_b, sgu_w[0], sgu_bias],
        batch, seq, _row_tile(seq, MIXER_ROWS))
    x1p = _out_proj(xp, ya, yb, w_out[0], _row_tile(batch * seq, OUT_PROJ_ROWS), OUT_PROJ_COLS)
    shift_p = last_p[:, 0, :lay.d_shift]

    xs = x_sample.reshape(n_dec, d_model)
    ps = _in_proj(xs, norm1_g, w_in_t, lay.src_rows, n_dec, PROJ_COLS)
    (r, k, v, g), step_cols = _prep_sample(ps, state_shift[0], lay, prep_w)
    o_t, wkv_s = _wkv_step(step_cols, state_wkv[0])
    yb, vs = _sgu_sample(ps, lay, sgu_norm_g[0], sgu_norm_b[0], sgu_w[0], sgu_b[0])
    ya = _rwkv_post(o_t, r, k, v, g, out_w)
    x1s = _out_proj(xs, ya, yb, w_out[0], n_dec, OUT_PROJ_COLS)
    shift_s = lay.shift_row(ps)

    y_prompt, y_sample = _ffn(x1p, x1s, norm2_g[0], w_ffn_up[0], w_ffn_down[0], norm_f_g,
                              _row_tile(batch * seq, FFN_ROWS), FFN_HIDDEN)
    y_prompt = y_prompt.reshape(batch, seq, d_model)
    y_sample = y_sample.reshape(n_dec, 1, d_model)
    return (y_prompt, y_sample, wkv_p[None], shift_p[None], wkv_s[None], shift_s[None],
            vs.reshape(1, n_dec, 1, d_sgu))
```

```python
import functools
import math

import jax
import jax.numpy as jnp
from jax import lax
from jax.experimental import pallas as pl
from jax.experimental.pallas import tpu as pltpu

F32 = jnp.float32
BF16 = jnp.bfloat16

HEAD = 64
LANES = 128
GROUP_HEADS = 2
GROUP = GROUP_HEADS * HEAD
SGU_CHUNK = 128
WKV_CHUNK = 64
PROJ_COLS = 512
MIXER_ROWS = 512
OUT_PROJ_ROWS = 2048
OUT_PROJ_COLS = 512
FFN_ROWS = 1024
FFN_HIDDEN = 256
RMS_EPS = 1e-5
LN_EPS = 1e-5
GN_EPS = 64e-5
DECAY_SCALE = math.exp(-0.5)
GELU_C = math.sqrt(2.0 / math.pi)
VMEM_LIMIT = 62 * 1024 * 1024


def _params(sem):
    return pltpu.CompilerParams(dimension_semantics=sem, vmem_limit_bytes=VMEM_LIMIT)


def _dot(a, b):
    return jnp.dot(a.astype(BF16), b.astype(BF16), preferred_element_type=F32)


def _dot_nt(a, b):
    return lax.dot_general(a.astype(BF16), b.astype(BF16), (((1,), (1,)), ((), ())),
                           preferred_element_type=F32)


def _dot_tn(a, b):
    return lax.dot_general(a.astype(BF16), b.astype(BF16), (((0,), (0,)), ((), ())),
                           preferred_element_type=F32)


def _split3(x):
    hi = x.astype(BF16)
    r1 = x - hi.astype(F32)
    mid = r1.astype(BF16)
    lo = (r1 - mid.astype(F32)).astype(BF16)
    return hi, mid, lo


def _dot_exact_lhs(m, x):
    hi, mid, lo = _split3(x)
    mb = m.astype(BF16)
    return (jnp.dot(mb, hi, preferred_element_type=F32) + jnp.dot(mb, mid, preferred_element_type=F32)
            + jnp.dot(mb, lo, preferred_element_type=F32))


def _sigmoid(x):
    return 1.0 / (1.0 + jnp.exp(-x))


def _head_ones():
    r = lax.broadcasted_iota(jnp.int32, (2 * LANES, LANES), 0) % LANES // HEAD
    c = lax.broadcasted_iota(jnp.int32, (2 * LANES, LANES), 1) // HEAD
    return (r == c).astype(BF16)


def _head_sum(x, ones):
    parts = []
    for s in range(0, x.shape[1], LANES):
        xs = x[:, s:s + LANES]
        hi = xs.astype(BF16)
        lo = (xs - hi.astype(F32)).astype(BF16)
        parts.append(jnp.dot(jnp.concatenate([hi, lo], axis=1), ones, preferred_element_type=F32))
    return parts[0] if len(parts) == 1 else jnp.concatenate(parts, axis=1)


def _rms(x, g):
    return x * lax.rsqrt(jnp.mean(x * x, axis=-1, keepdims=True) + RMS_EPS) * g


def _lo_mix(pl_, ql, mu_l, n_wa, n_gl):
    n = n_wa + n_gl
    lo = pl_[:, :n]
    lo = lo + (ql[:, :n] - lo) * mu_l[:, :n]
    wa = lo[:, :n_wa]
    return jnp.tanh(wa), wa, _sigmoid(lo[:, n_wa:])


def _run(steps):
    try:
        while True:
            next(steps)
    except StopIteration as done:
        return done.value


def _col_mix_steps(pr, pk, pv, qr, qk, qv, mu_r, mu_k, mu_v, tanh_wa, wa, sig_gl, w0, w_up, a0, a_up, g_up,
                   k_k, k_a):
    r = pr + (qr - pr) * mu_r
    yield
    k = pk + (qk - pk) * mu_k
    yield
    v = pv + (qv - pv) * mu_v
    yield
    lw = -DECAY_SCALE * _sigmoid(w0 + _dot(tanh_wa, w_up))
    yield
    a = _sigmoid(a0 + _dot(wa, a_up))
    yield
    gate = _dot(sig_gl, g_up)
    yield
    kk = k * k_k
    ss = _head_sum(kk * kk, _head_ones())
    yield
    kk = kk / jnp.maximum(jnp.sqrt(ss), 1e-12)
    yield
    return r, lw, k * (1.0 + (a - 1.0) * k_a), v, -kk, kk * a, gate


def _rwkv_mix(pr, pk, pv, pl_, qr, qk, qv, ql, mu_r, mu_k, mu_v, mu_l, w0, w_up, a0, a_up, g_up, k_k, k_a):
    lo = _lo_mix(pl_, ql, mu_l, w_up.shape[0], g_up.shape[0])
    return _run(_col_mix_steps(pr, pk, pv, qr, qk, qv, mu_r, mu_k, mu_v, *lo, w0, w_up, a0, a_up, g_up,
                               k_k, k_a))


def _rwkv_out_steps(o, r, k, v, gate, lnx_g, lnx_b, r_k):
    ones = _head_ones()
    mu = _head_sum(o, ones) * (1.0 / HEAD)
    yield
    oc = o - mu
    var = _head_sum(oc * oc, ones) * (1.0 / HEAD)
    yield
    y = oc * lax.rsqrt(var + GN_EPS) * lnx_g + lnx_b
    yield
    bonus = _head_sum(r * k * r_k, ones)
    yield
    return (y + bonus * v) * gate


def _rwkv_out(*args):
    return _run(_rwkv_out_steps(*args))


def _each(fn, *lists):
    out = []
    for args in zip(*lists):
        out.append(fn(*args))
        yield
    return out


def _run_together(work):
    values = [None] * len(work)
    longest = max(n for _, n in work)
    credit = [0.0] * len(work)
    live = set(range(len(work)))
    while live:
        for i, (steps, n) in enumerate(work):
            credit[i] += n / longest
            while i in live and credit[i] >= 1.0:
                credit[i] -= 1.0
                try:
                    next(steps)
                except StopIteration as done:
                    values[i] = done.value
                    live.discard(i)
    return values


def _shifted(p, carry_ref):
    rows = lax.broadcasted_iota(jnp.int32, p.shape, 0)
    q = jnp.where(rows == 0, carry_ref[...], pltpu.roll(p, 1, axis=0))
    carry_ref[...] = p[p.shape[0] - 1:, :]
    return q


def _prep_sample_kernel(*refs):
    ins, rows, cols = refs[:-10], refs[-10:-6], refs[-6:]
    r, lw, k, v, aa, bb, gate = _rwkv_mix(*[ref[...] for ref in ins])
    for ref, val in zip(rows, (r, k, v, gate)):
        ref[...] = val
    for ref, val in zip(cols, (r, lw, k, v, aa, bb)):
        ref[...] = val.T


class _Layout:
    def __init__(self, d_rwkv, d_sgu, lora_w, lora_a, lora_g, tn):
        self.d_rwkv, self.d_sgu, self.tn = d_rwkv, d_sgu, tn
        self.wa_w = lora_w + lora_a
        self.gl_w = -(-lora_g // LANES) * LANES
        self.d_shift = 3 * d_rwkv + self.wa_w + lora_g
        assert self.wa_w == LANES and self.wa_w + self.gl_w <= tn and d_rwkv % tn == 0

    def rw_pieces(self, a):
        d = self.d_rwkv
        pad = [(0, 0)] * (a.ndim - 1) + [(0, self.tn - (self.d_shift - 3 * d))]
        return a[..., :d], a[..., d:2 * d], a[..., 2 * d:3 * d], jnp.pad(a[..., 3 * d:], pad)


def _prep_weights(lay, mu, w0, w_up, a0, a_up, g_up, k_k, k_a):
    d = lay.d_rwkv
    lora_w, lora_g = w_up.shape[0], g_up.shape[0]
    mus = [m[None, :] for m in lay.rw_pieces(mu)]
    w_up_p = jnp.pad(w_up, ((0, lay.wa_w - lora_w), (0, 0)))
    a_up_p = jnp.pad(a_up, ((lora_w, 0), (0, 0)))
    g_up_p = jnp.pad(g_up, ((0, lay.gl_w - lora_g), (0, 0)))
    return mus + [w0[None, :], w_up_p, a0[None, :], a_up_p, g_up_p, k_k.reshape(1, d), k_a.reshape(1, d)]


def _full(a):
    return pl.BlockSpec(a.shape, lambda *_: (0,) * a.ndim)


def _prep_sample(p, prev, lay, weights):
    d = lay.d_rwkv
    m = p.shape[0]
    p_specs = [pl.BlockSpec((m, d), lambda i: (0, 0)),
               pl.BlockSpec((m, d), lambda i: (0, 1)),
               pl.BlockSpec((m, d), lambda i: (0, 2)),
               pl.BlockSpec((m, lay.tn), lambda i: (0, 3 * d // lay.tn))]
    prevs = list(lay.rw_pieces(prev))
    row_spec = pl.BlockSpec((m, d), lambda i: (0, 0))
    col_spec = pl.BlockSpec((d, m), lambda i: (0, 0))
    outs = pl.pallas_call(
        _prep_sample_kernel,
        grid=(1,),
        in_specs=p_specs + [_full(q) for q in prevs] + [_full(w) for w in weights],
        out_specs=[row_spec] * 4 + [col_spec] * 6,
        out_shape=[jax.ShapeDtypeStruct((m, d), F32)] * 4 + [jax.ShapeDtypeStruct((d, m), F32)] * 6,
        compiler_params=_params(("arbitrary",)),
        name="rwkv_prep_sample",
    )(p, p, p, p, *prevs, *weights)
    return outs[:4], outs[4:]


def _head_stack(x):
    head = lax.broadcasted_iota(jnp.int32, x.shape, 1) % GROUP // HEAD
    return jnp.concatenate([jnp.where(head == h, x, 0.0) for h in range(GROUP_HEADS)], axis=0)


def _wkv_pre_steps(r, lw, k, v, a, b, tri, gram_mask):
    c = WKV_CHUNK
    n_groups = r.shape[1] // GROUP

    def cut(x):
        return [x[:, g * GROUP:(g + 1) * GROUP] for g in range(n_groups)]

    cum = _dot_exact_lhs(tri, lw)
    yield
    e_out = jnp.exp(-cum)
    a_s = cut(a * jnp.exp(cum - lw))
    r_s = cut(r * jnp.exp(cum))
    yield
    b_s = cut(b * e_out)
    k_s = cut(k * e_out)
    yield
    last = cum[c - 1:, :]
    e_end = jnp.exp(last - cum)
    bk_e = cut(jnp.concatenate([b * e_end, k * e_end], axis=0))
    decay = cut(jnp.exp(last))
    vs = cut(v)
    yield

    grams = yield from _each(lambda ai, ri, bi, ki: jnp.where(
        gram_mask, _dot_nt(jnp.concatenate([ai, ri], axis=0),
                           jnp.concatenate([_head_stack(bi), _head_stack(ki)], axis=0)), 0.0),
        a_s, r_s, b_s, k_s)
    v_st = [_head_stack(x) for x in vs]
    kvs = yield from _each(lambda g, vi: _dot(g[:, GROUP:], vi), grams, v_st)
    xs = [jnp.concatenate([ai, kv[:c]], axis=1) for ai, kv in zip(a_s, kvs)]
    pws = [g[:c, :GROUP] for g in grams]
    ns = pws
    pws = yield from _each(lambda pw: _dot(pw, _head_stack(pw)), pws)
    n = 2
    while n < c // 2:
        both = yield from _each(
            lambda pw, nn: _dot(jnp.concatenate([pw, nn], axis=0), _head_stack(pw)), pws, ns)
        ns = [nn + pw + bo[c:] for nn, pw, bo in zip(ns, pws, both)]
        pws = [bo[:c] for bo in both]
        n *= 2
    ns = yield from _each(lambda nn, pw: nn + pw + _dot(nn, _head_stack(pw)), ns, pws)
    xs = yield from _each(lambda x, nn: x + _dot(nn, _head_stack(x)), xs, ns)
    qos = yield from _each(lambda g, x: _dot(g[c:, :GROUP], _head_stack(x)), grams, xs)
    qp = [jnp.concatenate([ri + qo[:, :GROUP], x[:, :GROUP]], axis=0) for ri, qo, x in zip(r_s, qos, xs)]
    o2 = [qo[:, GROUP:] + kv[c:] for qo, kv in zip(qos, kvs)]
    return dict(qp=qp, o2=o2, u2=[x[:, GROUP:] for x in xs], v=vs, bk_e=bk_e, decay=decay)


def _wkv_state_steps(pre, states, state_mask):
    c = WKV_CHUNK
    ous = yield from _each(_dot_nt, pre["qp"], states)
    upds = yield from _each(lambda ou, u2, v, bk: _dot_tn(jnp.concatenate([ou[c:] + u2, v], axis=0), bk),
                            ous, pre["u2"], pre["v"], pre["bk_e"])
    for p, (upd, decay) in enumerate(zip(upds, pre["decay"])):
        states[p] = states[p] * decay + jnp.where(state_mask, upd, 0.0)
    return [ou[:c] + o2 for ou, o2 in zip(ous, pre["o2"])]


def _mixer_prompt_kernel(*refs):
    x_ref, xs_ref, g1_ref, w_ref = refs[:4]
    mix_refs, out_refs, sgu_refs = refs[4:15], refs[15:18], refs[18:22]
    y_ref, yb_ref, sf_ref, sh_ref, psr_ref, pss_ref = refs[22:28]
    carries, s_ref, o_scr, p_scr, sg_scr = refs[28:32], refs[32], refs[33], refs[34], refs[35]
    c = WKV_CHUNK
    tb = y_ref.shape[0]
    d = y_ref.shape[1]
    n_groups = d // GROUP
    n_rw, n_sg = p_scr.shape[1], sg_scr.shape[1]
    sg_row0 = w_ref.shape[0] - n_sg
    t = pl.program_id(1)
    first = t == 0
    last = t == pl.num_programs(1) - 1

    @pl.when(first)
    def _():
        s_ref[...] = jnp.zeros_like(s_ref)
        for carry in carries:
            carry[...] = jnp.zeros_like(carry)

    @pl.when(first & (pl.program_id(0) == 0))
    def _():
        hs = _rms(xs_ref[...], g1_ref[...]).astype(BF16)
        for j in range(0, n_rw, PROJ_COLS):
            psr_ref[:, j:j + PROJ_COLS] = _dot_nt(hs, w_ref[j:j + PROJ_COLS, :])
        for j in range(0, n_sg, PROJ_COLS):
            pss_ref[:, j:j + PROJ_COLS] = _dot_nt(hs, w_ref[sg_row0 + j:sg_row0 + j + PROJ_COLS, :])

    h = _rms(x_ref[...], g1_ref[...]).astype(BF16)
    for j in range(0, n_rw, PROJ_COLS):
        p_scr[:, j:j + PROJ_COLS] = _dot_nt(h, w_ref[j:j + PROJ_COLS, :])
    p_refs = [p_scr.at[:, j * d:(j + 1) * d] for j in range(3)] + [p_scr.at[:, 3 * d:]]

    def sg_proj_steps():
        for j in range(0, n_sg, PROJ_COLS):
            sg_scr[:, j:j + PROJ_COLS] = _dot_nt(h, w_ref[sg_row0 + j:sg_row0 + j + PROJ_COLS, :])
            yield

    sg_ng, sg_nb, sg_w, sg_bias = sgu_refs
    sri = lax.broadcasted_iota(jnp.int32, (SGU_CHUNK, SGU_CHUNK), 0)
    sci = lax.broadcasted_iota(jnp.int32, (SGU_CHUNK, SGU_CHUNK), 1)
    sg_ws = [jnp.where(sci <= sri, sg_w[g], 0.0).astype(BF16) for g in range(sg_w.shape[0])]

    def sgu_steps(rows):
        u = _gelu(sg_scr[rows, :d])
        yield
        vs = _layernorm(_gelu(sg_scr[rows, d:]), sg_ng[...], sg_nb[...]).astype(BF16)
        yield
        for g, w in enumerate(sg_ws):
            cols = slice(g * SGU_CHUNK, (g + 1) * SGU_CHUNK)
            mix = jnp.dot(w, vs[:, cols], preferred_element_type=F32) + sg_bias[:, cols]
            yb_ref[rows, cols] = (u[:, cols] * mix).astype(yb_ref.dtype)
            yield

    mu_r, mu_k, mu_v, mu_l, w0, w_up, a0, a_up, g_up, k_k, k_a = mix_refs
    ri = lax.broadcasted_iota(jnp.int32, (c, c), 0)
    ci = lax.broadcasted_iota(jnp.int32, (c, c), 1)
    tri = (ci <= ri).astype(F32)
    gr = lax.broadcasted_iota(jnp.int32, (2 * c, 2 * GROUP), 0)
    gc = lax.broadcasted_iota(jnp.int32, (2 * c, 2 * GROUP), 1) % c
    gram_mask = gc <= jnp.where(gr < c, gr - 1, gr - c)
    sr = lax.broadcasted_iota(jnp.int32, (GROUP, GROUP), 0) // HEAD
    sc = lax.broadcasted_iota(jnp.int32, (GROUP, GROUP), 1) // HEAD
    state_mask = sr == sc
    states = [s_ref[g] for g in range(n_groups)]

    def prep_steps(rows):
        pl_ = p_refs[3][rows, :]
        lo = _lo_mix(pl_, _shifted(pl_, carries[3]), mu_l[...], w_up.shape[0], g_up.shape[0])
        yield
        ps = [ref[rows, :] for ref in p_refs[:3]]
        yield
        qs = [_shifted(p, carry) for p, carry in zip(ps, carries[:3])]
        yield
        vals = yield from _col_mix_steps(*ps, *qs, mu_r[...], mu_k[...], mu_v[...], *lo, w0[...], w_up[...],
                                         a0[...], a_up[...], g_up[...], k_k[...], k_a[...])
        return vals

    def state_steps(rows, pre):
        outs = yield from _wkv_state_steps(pre, states, state_mask)
        o_scr[rows, :] = jnp.concatenate(outs, axis=1)

    def post_steps(rows, vals):
        r, _, k, v, _, _, gate = vals
        y = yield from _rwkv_out_steps(o_scr[rows, :], r, k, v, gate, *[ref[...] for ref in out_refs])
        y_ref[rows, :] = y.astype(y_ref.dtype)

    n_chunks = tb // c
    rows = [slice(j * c, (j + 1) * c) for j in range(n_chunks)]
    n_stages = n_chunks + 3
    sgu_at = {min(2 + 2 * j, n_stages - 1): j for j in range(tb // SGU_CHUNK)}
    vals, pre = {}, {}
    for s in range(n_stages):
        work = []
        if s == 0:
            work.append(("sg_proj", 0, sg_proj_steps(), n_sg // PROJ_COLS))
        if s in sgu_at:
            j = sgu_at[s]
            work.append(("sgu", j, sgu_steps(slice(j * SGU_CHUNK, (j + 1) * SGU_CHUNK)), 2 + len(sg_ws)))
        if s < n_chunks:
            work.append(("prep", s, prep_steps(rows[s]), 12))
        if 0 <= s - 1 < n_chunks:
            work.append(("pre", s - 1, _wkv_pre_steps(*vals[s - 1][:6], tri, gram_mask), 10 * n_groups + 5))
        if 0 <= s - 2 < n_chunks:
            work.append(("state", s - 2, state_steps(rows[s - 2], pre[s - 2]), 2 * n_groups + 1))
        if 0 <= s - 3 < n_chunks:
            work.append(("post", s - 3, post_steps(rows[s - 3], vals[s - 3]), 5))
        done = _run_together([(steps, n) for _, _, steps, n in work])
        for (kind, j, _, _), value in zip(work, done):
            if kind == "prep":
                vals[j] = value
            elif kind == "pre":
                pre[j] = value
    for g, s in enumerate(states):
        s_ref[g] = s

    @pl.when(last)
    def _():
        sh_ref[0] = p_scr[tb - 1:, :]
        for g in range(n_groups):
            for h in range(GROUP_HEADS):
                sf_ref[0, g * GROUP_HEADS + h] = s_ref[g, h * HEAD:(h + 1) * HEAD, h * HEAD:(h + 1) * HEAD]


def _resident(a):
    return pl.BlockSpec(a.shape, lambda *_: (0,) * a.ndim, pipeline_mode=pl.Buffered(1))


def _mixer_prompt(x, xs, norm_g, wt, lay, mix_weights, out_weights, sgu_weights, batch, seq, tb):
    d = lay.d_rwkv
    d_model = x.shape[1]
    ms = xs.shape[0]
    n_rw, n_sg = 3 * d + lay.tn, 2 * lay.d_sgu
    assert lay.d_sgu == d and lay.tn % PROJ_COLS == 0 and wt.shape[0] == lay.d_shift + n_sg
    n_heads = d // HEAD
    nt = seq // tb
    row = lambda b, i: (b * nt + i, 0)
    weights = list(mix_weights) + list(out_weights) + list(sgu_weights)
    y_spec = pl.BlockSpec((tb, d), row)
    return pl.pallas_call(
        _mixer_prompt_kernel,
        grid=(batch, nt),
        in_specs=([pl.BlockSpec((tb, d_model), row), _resident(xs), _full(norm_g), _resident(wt)]
                  + [_full(w) for w in weights]),
        out_specs=[y_spec, y_spec,
                   pl.BlockSpec((1, n_heads, HEAD, HEAD), lambda b, i: (b, 0, 0, 0)),
                   pl.BlockSpec((1, 1, n_rw), lambda b, i: (b, 0, 0)),
                   pl.BlockSpec((ms, n_rw), lambda b, i: (0, 0), pipeline_mode=pl.Buffered(1)),
                   pl.BlockSpec((ms, n_sg), lambda b, i: (0, 0), pipeline_mode=pl.Buffered(1))],
        out_shape=[jax.ShapeDtypeStruct((batch * seq, d), BF16),
                   jax.ShapeDtypeStruct((batch * seq, d), BF16),
                   jax.ShapeDtypeStruct((batch, n_heads, HEAD, HEAD), F32),
                   jax.ShapeDtypeStruct((batch, 1, n_rw), F32),
                   jax.ShapeDtypeStruct((ms, n_rw), F32),
                   jax.ShapeDtypeStruct((ms, n_sg), F32)],
        scratch_shapes=([pltpu.VMEM((1, d), F32)] * 3
                        + [pltpu.VMEM((1, lay.tn), F32), pltpu.VMEM((d // GROUP, GROUP, GROUP), F32),
                           pltpu.VMEM((tb, d), F32), pltpu.VMEM((tb, n_rw), F32),
                           pltpu.VMEM((tb, n_sg), F32)]),
        compiler_params=_params(("arbitrary", "arbitrary")),
        name="mixer_prompt",
    )(x, xs, norm_g, wt, *weights)


STEP_HEADS = 2
STEP_UNROLL = 8


def _wkv_step_kernel(r_ref, lw_ref, k_ref, v_ref, a_ref, b_ref, s_ref, o_ref, sn_ref):
    for h in range(s_ref.shape[0]):
        feat = slice(h * HEAD, (h + 1) * HEAD)
        a, b, k, r = a_ref[feat, :], b_ref[feat, :], k_ref[feat, :], r_ref[feat, :]
        w = jnp.exp(lw_ref[feat, :])

        def body(j, carry, h=h, a=a, b=b, k=k, r=r, w=w):
            for u in range(STEP_UNROLL):
                i = j * STEP_UNROLL + u
                s = s_ref[h, i]
                sa = jnp.sum(s * a, axis=0, keepdims=True)
                s = s * w + sa * b + v_ref[pl.ds(h * HEAD + i, 1), :] * k
                sn_ref[h, i] = s
                o_ref[pl.ds(h * HEAD + i, 1), :] = jnp.sum(s * r, axis=0, keepdims=True)
            return carry

        lax.fori_loop(0, HEAD // STEP_UNROLL, body, 0)


def _wkv_step(cols, state):
    d, m = cols[0].shape
    n_heads = d // HEAD
    vec = pl.BlockSpec((STEP_HEADS * HEAD, m), lambda h: (h, 0))
    st = pl.BlockSpec((STEP_HEADS, HEAD, HEAD, m), lambda h: (h, 0, 0, 0))
    o, s = pl.pallas_call(
        _wkv_step_kernel,
        grid=(n_heads // STEP_HEADS,),
        in_specs=[vec] * 6 + [st],
        out_specs=[vec, st],
        out_shape=[jax.ShapeDtypeStruct((d, m), F32), jax.ShapeDtypeStruct((n_heads, HEAD, HEAD, m), F32)],
        compiler_params=_params(("parallel",)),
        name="wkv_step",
    )(*cols, jnp.transpose(state, (1, 2, 3, 0)))
    return o, jnp.transpose(s, (3, 0, 1, 2))


def _post_kernel(ot_ref, *refs):
    y_ref = refs[-1]
    y_ref[...] = _rwkv_out(ot_ref[...].T, *[ref[...] for ref in refs[:-1]]).astype(y_ref.dtype)


def _rwkv_post(o_t, r, k, v, g, out_weights):
    m, d = r.shape
    spec = pl.BlockSpec((m, d), lambda i: (0, 0))
    return pl.pallas_call(
        _post_kernel,
        grid=(1,),
        in_specs=[pl.BlockSpec((d, m), lambda i: (0, 0))] + [spec] * 4 + [_full(w) for w in out_weights],
        out_specs=spec,
        out_shape=jax.ShapeDtypeStruct((m, d), BF16),
        compiler_params=_params(("arbitrary",)),
        name="rwkv_post",
    )(o_t, r, k, v, g, *out_weights)


def _gelu(x):
    return 0.5 * x * (1.0 + jnp.tanh(GELU_C * (x + 0.044715 * (x * x * x))))


def _layernorm(x, g, b):
    mu = jnp.mean(x, axis=-1, keepdims=True)
    xc = x - mu
    var = jnp.mean(xc * xc, axis=-1, keepdims=True)
    return xc * lax.rsqrt(var + LN_EPS) * g + b


def _sgu_sample_kernel(pu_ref, pv_ref, ng_ref, nb_ref, w_ref, bias_ref, y_ref, vs_ref):
    u = _gelu(pu_ref[...])
    vs = _layernorm(_gelu(pv_ref[...]), ng_ref[...], nb_ref[...])
    vs_ref[...] = vs
    y_ref[...] = (u * (w_ref[...] * vs + bias_ref[...])).astype(y_ref.dtype)


def _sgu_sample(p, lay, norm_g, norm_b, sgu_w, sgu_b):
    d = lay.d_sgu
    m = p.shape[0]
    w0 = jnp.repeat(sgu_w[:, 0, 0], SGU_CHUNK)[None, :]
    b0 = jnp.repeat(sgu_b[:, 0], SGU_CHUNK)[None, :]
    vec = pl.BlockSpec((1, d), lambda i: (0, 0))
    out = pl.BlockSpec((m, d), lambda i: (0, 0))
    return pl.pallas_call(
        _sgu_sample_kernel,
        grid=(1,),
        in_specs=[pl.BlockSpec((m, d), lambda i: (0, 0)),
                  pl.BlockSpec((m, d), lambda i: (0, 1)), vec, vec, vec, vec],
        out_specs=[out, out],
        out_shape=[jax.ShapeDtypeStruct((m, d), BF16), jax.ShapeDtypeStruct((m, d), F32)],
        compiler_params=_params(("arbitrary",)),
        name="sgu_sample",
    )(p, p, norm_g[None, :], norm_b[None, :], w0, b0)


def _out_proj_kernel(x_ref, ya_ref, yb_ref, wa_ref, wb_ref, o_ref):
    o_ref[...] = (x_ref[...] + jnp.dot(ya_ref[...], wa_ref[...].astype(BF16), preferred_element_type=F32)
                  + jnp.dot(yb_ref[...], wb_ref[...].astype(BF16), preferred_element_type=F32))


def _out_proj(x, ya, yb, w, tm, tn):
    m, d = x.shape
    da = ya.shape[1]
    return pl.pallas_call(
        _out_proj_kernel,
        grid=(m // tm, d // tn),
        in_specs=[pl.BlockSpec((tm, tn), lambda i, j: (i, j)),
                  pl.BlockSpec((tm, da), lambda i, j: (i, 0)),
                  pl.BlockSpec((tm, da), lambda i, j: (i, 0)),
                  pl.BlockSpec((da, tn), lambda i, j: (0, j)),
                  pl.BlockSpec((da, tn), lambda i, j: (1, j))],
        out_specs=pl.BlockSpec((tm, tn), lambda i, j: (i, j)),
        out_shape=jax.ShapeDtypeStruct((m, d), F32),
        compiler_params=_params(("parallel", "arbitrary")),
        name="out_proj",
    )(x, ya, yb, w, w)


def _ffn_kernel(x_ref, xs_ref, g2_ref, wu_ref, wd_ref, gf_ref, o_ref, os_ref, h_ref, a_ref):
    f = pl.program_id(1)
    last = pl.num_programs(1) - 1
    tm = x_ref.shape[0]

    def up():
        a = jnp.dot(h_ref[...], wu_ref[...].astype(BF16), preferred_element_type=F32)
        return jnp.square(jnp.maximum(a, 0.0)).astype(BF16)

    def down():
        return jnp.dot(a_ref[...], wd_ref[...].astype(BF16), preferred_element_type=F32)

    @pl.when(f == 0)
    def _():
        x, xs = x_ref[...], xs_ref[...]
        h_ref[:tm, :] = _rms(x, g2_ref[...]).astype(BF16)
        h_ref[tm:, :] = _rms(xs, g2_ref[...]).astype(BF16)
        o_ref[...] = x
        os_ref[...] = xs
        a_ref[...] = up()

    @pl.when((f > 0) & (f < last))
    def _():
        acc = down()
        a_new = up()
        o_ref[...] += acc[:tm]
        os_ref[...] += acc[tm:]
        a_ref[...] = a_new

    @pl.when(f == last)
    def _():
        acc = down()
        o_ref[...] = _rms(o_ref[...] + acc[:tm], gf_ref[...])
        os_ref[...] = _rms(os_ref[...] + acc[tm:], gf_ref[...])


def _ffn(x, xs, g2, w_up, w_down, gf, tm, tf):
    m, d = x.shape
    n_blocks = m // tm
    ts = xs.shape[0] // n_blocks
    assert ts * n_blocks == xs.shape[0] and ts % 8 == 0
    nf = w_up.shape[1] // tf
    return pl.pallas_call(
        _ffn_kernel,
        grid=(n_blocks, nf + 1),
        in_specs=[pl.BlockSpec((tm, d), lambda i, f: (i, 0)),
                  pl.BlockSpec((ts, d), lambda i, f: (i, 0)),
                  pl.BlockSpec((1, d), lambda i, f: (0, 0)),
                  pl.BlockSpec((d, tf), lambda i, f: (0, jnp.minimum(f, nf - 1))),
                  pl.BlockSpec((tf, d), lambda i, f: (jnp.maximum(f - 1, 0), 0)),
                  pl.BlockSpec((1, d), lambda i, f: (0, 0))],
        out_specs=[pl.BlockSpec((tm, d), lambda i, f: (i, 0)),
                   pl.BlockSpec((ts, d), lambda i, f: (i, 0))],
        out_shape=[jax.ShapeDtypeStruct((m, d), F32), jax.ShapeDtypeStruct(xs.shape, F32)],
        scratch_shapes=[pltpu.VMEM((tm + ts, d), BF16), pltpu.VMEM((tm + ts, tf), BF16)],
        compiler_params=_params(("parallel", "arbitrary")),
        name="ffn",
    )(x, xs, g2[None, :], w_up, w_down, gf[None, :])


def _row_tile(m, cap):
    t = min(m, cap)
    assert m % t == 0
    return t


def kernel(x_prompt, x_sample, state_wkv, state_shift, norm1_g, w_in, mu_shift, w0, w_up, a0, a_up, g_up,
           k_k, k_a, r_k, lnx_g, lnx_b, sgu_norm_g, sgu_norm_b, sgu_w, sgu_b, w_out, norm2_g, w_ffn_up,
           w_ffn_down, norm_f_g):
    batch, seq, d_model = x_prompt.shape
    n_dec, dec_seq, _ = x_sample.shape
    depth = w_in.shape[0]
    assert depth == 1 and dec_seq == 1
    d_rwkv = w0.shape[1]
    d_sgu = sgu_norm_g.shape[1]
    lay = _Layout(d_rwkv, d_sgu, w_up.shape[1], a_up.shape[1], g_up.shape[1], PROJ_COLS)
    w_in_t = w_in[0].T.astype(BF16)
    prep_w = _prep_weights(lay, mu_shift[0], w0[0], w_up[0], a0[0], a_up[0], g_up[0], k_k[0], k_a[0])

    out_w = [w.reshape(1, d_rwkv) for w in (lnx_g[0], lnx_b[0], r_k[0])]

    xp = x_prompt.reshape(batch * seq, d_model)
    xs = x_sample.reshape(n_dec, d_model)
    sgu_bias = jnp.repeat(sgu_b[0].T, SGU_CHUNK, axis=1)
    ya, yb, wkv_p, last_p, ps_rw, ps_sg = _mixer_prompt(
        xp, xs, norm1_g, w_in_t, lay, prep_w, out_w, [sgu_norm_g, sgu_norm_b, sgu_w[0], sgu_bias],
        batch, seq, _row_tile(seq, MIXER_ROWS))
    x1p = _out_proj(xp, ya, yb, w_out[0], _row_tile(batch * seq, OUT_PROJ_ROWS), OUT_PROJ_COLS)
    shift_p = last_p[:, 0, :lay.d_shift]

    (r, k, v, g), step_cols = _prep_sample(ps_rw, state_shift[0], lay, prep_w)
    o_t, wkv_s = _wkv_step(step_cols, state_wkv[0])
    yb, vs = _sgu_sample(ps_sg, lay, sgu_norm_g[0], sgu_norm_b[0], sgu_w[0], sgu_b[0])
    ya = _rwkv_post(o_t, r, k, v, g, out_w)
    x1s = _out_proj(xs, ya, yb, w_out[0], n_dec, OUT_PROJ_COLS)
    shift_s = ps_rw[:, :lay.d_shift]

    y_prompt, y_sample = _ffn(x1p, x1s, norm2_g[0], w_ffn_up[0], w_ffn_down[0], norm_f_g,
                              _row_tile(batch * seq, FFN_ROWS), FFN_HIDDEN)
    y_prompt = y_prompt.reshape(batch, seq, d_model)
    y_sample = y_sample.reshape(n_dec, 1, d_model)
    return (y_prompt, y_sample, wkv_p[None], shift_p[None], wkv_s[None], shift_s[None],
            vs.reshape(1, n_dec, 1, d_sgu))
```

```python
import functools
import math

import jax
import jax.numpy as jnp
from jax import lax
from jax.experimental import pallas as pl
from jax.experimental.pallas import tpu as pltpu

F32 = jnp.float32
BF16 = jnp.bfloat16

HEAD = 64
LANES = 128
GROUP_HEADS = 2
GROUP = GROUP_HEADS * HEAD
SGU_CHUNK = 128
WKV_CHUNK = 64
PROJ_COLS = 512
MIXER_ROWS = 512
OUT_PROJ_ROWS = 2048
OUT_PROJ_COLS = 512
FFN_ROWS = 1024
FFN_HIDDEN = 512
RMS_EPS = 1e-5
LN_EPS = 1e-5
GN_EPS = 64e-5
DECAY_SCALE = math.exp(-0.5)
GELU_C = math.sqrt(2.0 / math.pi)
VMEM_LIMIT = 62 * 1024 * 1024


def _params(sem):
    return pltpu.CompilerParams(dimension_semantics=sem, vmem_limit_bytes=VMEM_LIMIT)


def _dot(a, b):
    return jnp.dot(a.astype(BF16), b.astype(BF16), preferred_element_type=F32)


def _dot_nt(a, b):
    return lax.dot_general(a.astype(BF16), b.astype(BF16), (((1,), (1,)), ((), ())),
                           preferred_element_type=F32)


def _dot_tn(a, b):
    return lax.dot_general(a.astype(BF16), b.astype(BF16), (((0,), (0,)), ((), ())),
                           preferred_element_type=F32)


def _split3(x):
    hi = x.astype(BF16)
    r1 = x - hi.astype(F32)
    mid = r1.astype(BF16)
    lo = (r1 - mid.astype(F32)).astype(BF16)
    return hi, mid, lo


def _dot_exact_lhs(m, x):
    hi, mid, lo = _split3(x)
    mb = m.astype(BF16)
    return (jnp.dot(mb, hi, preferred_element_type=F32) + jnp.dot(mb, mid, preferred_element_type=F32)
            + jnp.dot(mb, lo, preferred_element_type=F32))


def _sigmoid(x):
    return 1.0 / (1.0 + jnp.exp(-x))


def _head_ones():
    r = lax.broadcasted_iota(jnp.int32, (2 * LANES, LANES), 0) % LANES // HEAD
    c = lax.broadcasted_iota(jnp.int32, (2 * LANES, LANES), 1) // HEAD
    return (r == c).astype(BF16)


def _head_sum(x, ones):
    parts = []
    for s in range(0, x.shape[1], LANES):
        xs = x[:, s:s + LANES]
        hi = xs.astype(BF16)
        lo = (xs - hi.astype(F32)).astype(BF16)
        parts.append(jnp.dot(jnp.concatenate([hi, lo], axis=1), ones, preferred_element_type=F32))
    return parts[0] if len(parts) == 1 else jnp.concatenate(parts, axis=1)


def _rms(x, g):
    return x * lax.rsqrt(jnp.mean(x * x, axis=-1, keepdims=True) + RMS_EPS) * g


def _lo_mix(pl_, ql, mu_l, n_wa, n_gl):
    n = n_wa + n_gl
    lo = pl_[:, :n]
    lo = lo + (ql[:, :n] - lo) * mu_l[:, :n]
    wa = lo[:, :n_wa]
    return jnp.tanh(wa), wa, _sigmoid(lo[:, n_wa:])


def _run(steps):
    try:
        while True:
            next(steps)
    except StopIteration as done:
        return done.value


def _col_mix_steps(pr, pk, pv, qr, qk, qv, mu_r, mu_k, mu_v, tanh_wa, wa, sig_gl, w0, w_up, a0, a_up, g_up,
                   k_k, k_a):
    r = pr + (qr - pr) * mu_r
    yield
    k = pk + (qk - pk) * mu_k
    yield
    v = pv + (qv - pv) * mu_v
    yield
    lw = -DECAY_SCALE * _sigmoid(w0 + _dot(tanh_wa, w_up))
    yield
    a = _sigmoid(a0 + _dot(wa, a_up))
    yield
    gate = _dot(sig_gl, g_up)
    yield
    kk = k * k_k
    ss = _head_sum(kk * kk, _head_ones())
    yield
    kk = kk / jnp.maximum(jnp.sqrt(ss), 1e-12)
    yield
    return r, lw, k * (1.0 + (a - 1.0) * k_a), v, -kk, kk * a, gate


def _rwkv_mix(pr, pk, pv, pl_, qr, qk, qv, ql, mu_r, mu_k, mu_v, mu_l, w0, w_up, a0, a_up, g_up, k_k, k_a):
    lo = _lo_mix(pl_, ql, mu_l, w_up.shape[0], g_up.shape[0])
    return _run(_col_mix_steps(pr, pk, pv, qr, qk, qv, mu_r, mu_k, mu_v, *lo, w0, w_up, a0, a_up, g_up,
                               k_k, k_a))


def _rwkv_out_steps(o, r, k, v, gate, lnx_g, lnx_b, r_k):
    ones = _head_ones()
    mu = _head_sum(o, ones) * (1.0 / HEAD)
    yield
    oc = o - mu
    var = _head_sum(oc * oc, ones) * (1.0 / HEAD)
    yield
    y = oc * lax.rsqrt(var + GN_EPS) * lnx_g + lnx_b
    yield
    bonus = _head_sum(r * k * r_k, ones)
    yield
    return (y + bonus * v) * gate


def _rwkv_out(*args):
    return _run(_rwkv_out_steps(*args))


def _each(fn, *lists):
    out = []
    for args in zip(*lists):
        out.append(fn(*args))
        yield
    return out


def _run_together(work):
    values = [None] * len(work)
    longest = max(n for _, n in work)
    credit = [0.0] * len(work)
    live = set(range(len(work)))
    while live:
        for i, (steps, n) in enumerate(work):
            credit[i] += n / longest
            while i in live and credit[i] >= 1.0:
                credit[i] -= 1.0
                try:
                    next(steps)
                except StopIteration as done:
                    values[i] = done.value
                    live.discard(i)
    return values


def _shifted(p, carry_ref):
    rows = lax.broadcasted_iota(jnp.int32, p.shape, 0)
    q = jnp.where(rows == 0, carry_ref[...], pltpu.roll(p, 1, axis=0))
    carry_ref[...] = p[p.shape[0] - 1:, :]
    return q


def _prep_sample_kernel(*refs):
    ins, rows, cols = refs[:-10], refs[-10:-6], refs[-6:]
    r, lw, k, v, aa, bb, gate = _rwkv_mix(*[ref[...] for ref in ins])
    for ref, val in zip(rows, (r, k, v, gate)):
        ref[...] = val
    for ref, val in zip(cols, (r, lw, k, v, aa, bb)):
        ref[...] = val.T


class _Layout:
    def __init__(self, d_rwkv, d_sgu, lora_w, lora_a, lora_g, tn):
        self.d_rwkv, self.d_sgu, self.tn = d_rwkv, d_sgu, tn
        self.wa_w = lora_w + lora_a
        self.gl_w = -(-lora_g // LANES) * LANES
        self.d_shift = 3 * d_rwkv + self.wa_w + lora_g
        assert self.wa_w == LANES and self.wa_w + self.gl_w <= tn and d_rwkv % tn == 0

    def rw_pieces(self, a):
        d = self.d_rwkv
        pad = [(0, 0)] * (a.ndim - 1) + [(0, self.tn - (self.d_shift - 3 * d))]
        return a[..., :d], a[..., d:2 * d], a[..., 2 * d:3 * d], jnp.pad(a[..., 3 * d:], pad)


def _prep_weights(lay, mu, w0, w_up, a0, a_up, g_up, k_k, k_a):
    d = lay.d_rwkv
    lora_w, lora_g = w_up.shape[0], g_up.shape[0]
    mus = [m[None, :] for m in lay.rw_pieces(mu)]
    w_up_p = jnp.pad(w_up, ((0, lay.wa_w - lora_w), (0, 0)))
    a_up_p = jnp.pad(a_up, ((lora_w, 0), (0, 0)))
    g_up_p = jnp.pad(g_up, ((0, lay.gl_w - lora_g), (0, 0)))
    return mus + [w0[None, :], w_up_p, a0[None, :], a_up_p, g_up_p, k_k.reshape(1, d), k_a.reshape(1, d)]


def _full(a):
    return pl.BlockSpec(a.shape, lambda *_: (0,) * a.ndim)


def _prep_sample(p, prev, lay, weights):
    d = lay.d_rwkv
    m = p.shape[0]
    p_specs = [pl.BlockSpec((m, d), lambda i: (0, 0)),
               pl.BlockSpec((m, d), lambda i: (0, 1)),
               pl.BlockSpec((m, d), lambda i: (0, 2)),
               pl.BlockSpec((m, lay.tn), lambda i: (0, 3 * d // lay.tn))]
    prevs = list(lay.rw_pieces(prev))
    row_spec = pl.BlockSpec((m, d), lambda i: (0, 0))
    col_spec = pl.BlockSpec((d, m), lambda i: (0, 0))
    outs = pl.pallas_call(
        _prep_sample_kernel,
        grid=(1,),
        in_specs=p_specs + [_full(q) for q in prevs] + [_full(w) for w in weights],
        out_specs=[row_spec] * 4 + [col_spec] * 6,
        out_shape=[jax.ShapeDtypeStruct((m, d), F32)] * 4 + [jax.ShapeDtypeStruct((d, m), F32)] * 6,
        compiler_params=_params(("arbitrary",)),
        name="rwkv_prep_sample",
    )(p, p, p, p, *prevs, *weights)
    return outs[:4], outs[4:]


def _head_stack(x):
    head = lax.broadcasted_iota(jnp.int32, x.shape, 1) % GROUP // HEAD
    return jnp.concatenate([jnp.where(head == h, x, 0.0) for h in range(GROUP_HEADS)], axis=0)


def _wkv_pre_steps(r, lw, k, v, a, b, tri, gram_mask):
    c = WKV_CHUNK
    n_groups = r.shape[1] // GROUP

    def cut(x):
        return [x[:, g * GROUP:(g + 1) * GROUP] for g in range(n_groups)]

    cum = _dot_exact_lhs(tri, lw)
    yield
    e_out = jnp.exp(-cum)
    a_s = cut(a * jnp.exp(cum - lw))
    r_s = cut(r * jnp.exp(cum))
    yield
    b_s = cut(b * e_out)
    k_s = cut(k * e_out)
    yield
    last = cum[c - 1:, :]
    e_end = jnp.exp(last - cum)
    bk_e = cut(jnp.concatenate([b * e_end, k * e_end], axis=0))
    decay = cut(jnp.exp(last))
    vs = cut(v)
    yield

    grams = yield from _each(lambda ai, ri, bi, ki: jnp.where(
        gram_mask, _dot_nt(jnp.concatenate([ai, ri], axis=0),
                           jnp.concatenate([_head_stack(bi), _head_stack(ki)], axis=0)), 0.0),
        a_s, r_s, b_s, k_s)
    v_st = [_head_stack(x) for x in vs]
    kvs = yield from _each(lambda g, vi: _dot(g[:, GROUP:], vi), grams, v_st)
    xs = [jnp.concatenate([ai, kv[:c]], axis=1) for ai, kv in zip(a_s, kvs)]
    pws = [g[:c, :GROUP] for g in grams]
    ns = pws
    pws = yield from _each(lambda pw: _dot(pw, _head_stack(pw)), pws)
    n = 2
    while n < c // 2:
        both = yield from _each(
            lambda pw, nn: _dot(jnp.concatenate([pw, nn], axis=0), _head_stack(pw)), pws, ns)
        ns = [nn + pw + bo[c:] for nn, pw, bo in zip(ns, pws, both)]
        pws = [bo[:c] for bo in both]
        n *= 2
    ns = yield from _each(lambda nn, pw: nn + pw + _dot(nn, _head_stack(pw)), ns, pws)
    xs = yield from _each(lambda x, nn: x + _dot(nn, _head_stack(x)), xs, ns)
    qos = yield from _each(lambda g, x: _dot(g[c:, :GROUP], _head_stack(x)), grams, xs)
    qp = [jnp.concatenate([ri + qo[:, :GROUP], x[:, :GROUP]], axis=0) for ri, qo, x in zip(r_s, qos, xs)]
    o2 = [qo[:, GROUP:] + kv[c:] for qo, kv in zip(qos, kvs)]
    return dict(qp=qp, o2=o2, u2=[x[:, GROUP:] for x in xs], v=vs, bk_e=bk_e, decay=decay)


def _wkv_state_steps(pre, states, state_mask):
    c = WKV_CHUNK
    ous = yield from _each(_dot_nt, pre["qp"], states)
    upds = yield from _each(lambda ou, u2, v, bk: _dot_tn(jnp.concatenate([ou[c:] + u2, v], axis=0), bk),
                            ous, pre["u2"], pre["v"], pre["bk_e"])
    for p, (upd, decay) in enumerate(zip(upds, pre["decay"])):
        states[p] = states[p] * decay + jnp.where(state_mask, upd, 0.0)
    return [ou[:c] + o2 for ou, o2 in zip(ous, pre["o2"])]


def _mixer_prompt_kernel(*refs):
    x_ref, xs_ref, g1_ref, w_ref = refs[:4]
    mix_refs, out_refs, sgu_refs = refs[4:15], refs[15:18], refs[18:22]
    y_ref, yb_ref, sf_ref, sh_ref, psr_ref, pss_ref = refs[22:28]
    carries, s_ref, o_scr, p_scr, sg_scr = refs[28:32], refs[32], refs[33], refs[34], refs[35]
    c = WKV_CHUNK
    tb = y_ref.shape[0]
    d = y_ref.shape[1]
    n_groups = d // GROUP
    n_rw, n_sg = p_scr.shape[1], sg_scr.shape[1]
    sg_row0 = w_ref.shape[0] - n_sg
    t = pl.program_id(1)
    first = t == 0
    last = t == pl.num_programs(1) - 1

    @pl.when(first)
    def _():
        s_ref[...] = jnp.zeros_like(s_ref)
        for carry in carries:
            carry[...] = jnp.zeros_like(carry)

    @pl.when(first & (pl.program_id(0) == 0))
    def _():
        hs = _rms(xs_ref[...], g1_ref[...]).astype(BF16)
        for j in range(0, n_rw, PROJ_COLS):
            psr_ref[:, j:j + PROJ_COLS] = _dot_nt(hs, w_ref[j:j + PROJ_COLS, :])
        for j in range(0, n_sg, PROJ_COLS):
            pss_ref[:, j:j + PROJ_COLS] = _dot_nt(hs, w_ref[sg_row0 + j:sg_row0 + j + PROJ_COLS, :])

    h = _rms(x_ref[...], g1_ref[...]).astype(BF16)
    for j in range(0, n_rw, PROJ_COLS):
        p_scr[:, j:j + PROJ_COLS] = _dot_nt(h, w_ref[j:j + PROJ_COLS, :])
    p_refs = [p_scr.at[:, j * d:(j + 1) * d] for j in range(3)] + [p_scr.at[:, 3 * d:]]

    def sg_proj_steps():
        for j in range(0, n_sg, PROJ_COLS):
            sg_scr[:, j:j + PROJ_COLS] = _dot_nt(h, w_ref[sg_row0 + j:sg_row0 + j + PROJ_COLS, :])
            yield

    sg_ng, sg_nb, sg_w, sg_bias = sgu_refs
    sri = lax.broadcasted_iota(jnp.int32, (SGU_CHUNK, SGU_CHUNK), 0)
    sci = lax.broadcasted_iota(jnp.int32, (SGU_CHUNK, SGU_CHUNK), 1)
    sg_ws = [jnp.where(sci <= sri, sg_w[g], 0.0).astype(BF16) for g in range(sg_w.shape[0])]

    def sgu_steps(rows):
        u = _gelu(sg_scr[rows, :d])
        yield
        vs = _layernorm(_gelu(sg_scr[rows, d:]), sg_ng[...], sg_nb[...]).astype(BF16)
        yield
        for g, w in enumerate(sg_ws):
            cols = slice(g * SGU_CHUNK, (g + 1) * SGU_CHUNK)
            mix = jnp.dot(w, vs[:, cols], preferred_element_type=F32) + sg_bias[:, cols]
            yb_ref[rows, cols] = (u[:, cols] * mix).astype(yb_ref.dtype)
            yield

    mu_r, mu_k, mu_v, mu_l, w0, w_up, a0, a_up, g_up, k_k, k_a = mix_refs
    ri = lax.broadcasted_iota(jnp.int32, (c, c), 0)
    ci = lax.broadcasted_iota(jnp.int32, (c, c), 1)
    tri = (ci <= ri).astype(F32)
    gr = lax.broadcasted_iota(jnp.int32, (2 * c, 2 * GROUP), 0)
    gc = lax.broadcasted_iota(jnp.int32, (2 * c, 2 * GROUP), 1) % c
    gram_mask = gc <= jnp.where(gr < c, gr - 1, gr - c)
    sr = lax.broadcasted_iota(jnp.int32, (GROUP, GROUP), 0) // HEAD
    sc = lax.broadcasted_iota(jnp.int32, (GROUP, GROUP), 1) // HEAD
    state_mask = sr == sc
    states = [s_ref[g] for g in range(n_groups)]

    def prep_steps(rows):
        pl_ = p_refs[3][rows, :]
        lo = _lo_mix(pl_, _shifted(pl_, carries[3]), mu_l[...], w_up.shape[0], g_up.shape[0])
        yield
        ps = [ref[rows, :] for ref in p_refs[:3]]
        yield
        qs = [_shifted(p, carry) for p, carry in zip(ps, carries[:3])]
        yield
        vals = yield from _col_mix_steps(*ps, *qs, mu_r[...], mu_k[...], mu_v[...], *lo, w0[...], w_up[...],
                                         a0[...], a_up[...], g_up[...], k_k[...], k_a[...])
        return vals

    def state_steps(rows, pre):
        outs = yield from _wkv_state_steps(pre, states, state_mask)
        o_scr[rows, :] = jnp.concatenate(outs, axis=1)

    def post_steps(rows, vals):
        r, _, k, v, _, _, gate = vals
        y = yield from _rwkv_out_steps(o_scr[rows, :], r, k, v, gate, *[ref[...] for ref in out_refs])
        y_ref[rows, :] = y.astype(y_ref.dtype)

    n_chunks = tb // c
    rows = [slice(j * c, (j + 1) * c) for j in range(n_chunks)]
    n_stages = n_chunks + 3
    sgu_at = {min(2 + 2 * j, n_stages - 1): j for j in range(tb // SGU_CHUNK)}
    vals, pre = {}, {}
    for s in range(n_stages):
        work = []
        if s == 0:
            work.append(("sg_proj", 0, sg_proj_steps(), n_sg // PROJ_COLS))
        if s in sgu_at:
            j = sgu_at[s]
            work.append(("sgu", j, sgu_steps(slice(j * SGU_CHUNK, (j + 1) * SGU_CHUNK)), 2 + len(sg_ws)))
        if s < n_chunks:
            work.append(("prep", s, prep_steps(rows[s]), 12))
        if 0 <= s - 1 < n_chunks:
            work.append(("pre", s - 1, _wkv_pre_steps(*vals[s - 1][:6], tri, gram_mask), 10 * n_groups + 5))
        if 0 <= s - 2 < n_chunks:
            work.append(("state", s - 2, state_steps(rows[s - 2], pre[s - 2]), 2 * n_groups + 1))
        if 0 <= s - 3 < n_chunks:
            work.append(("post", s - 3, post_steps(rows[s - 3], vals[s - 3]), 5))
        done = _run_together([(steps, n) for _, _, steps, n in work])
        for (kind, j, _, _), value in zip(work, done):
            if kind == "prep":
                vals[j] = value
            elif kind == "pre":
                pre[j] = value
    for g, s in enumerate(states):
        s_ref[g] = s

    @pl.when(last)
    def _():
        sh_ref[0] = p_scr[tb - 1:, :]
        for g in range(n_groups):
            for h in range(GROUP_HEADS):
                sf_ref[0, g * GROUP_HEADS + h] = s_ref[g, h * HEAD:(h + 1) * HEAD, h * HEAD:(h + 1) * HEAD]


def _resident(a):
    return pl.BlockSpec(a.shape, lambda *_: (0,) * a.ndim, pipeline_mode=pl.Buffered(1))


def _mixer_prompt(x, xs, norm_g, wt, lay, mix_weights, out_weights, sgu_weights, batch, seq, tb):
    d = lay.d_rwkv
    d_model = x.shape[1]
    ms = xs.shape[0]
    n_rw, n_sg = 3 * d + lay.tn, 2 * lay.d_sgu
    assert lay.d_sgu == d and lay.tn % PROJ_COLS == 0 and wt.shape[0] == lay.d_shift + n_sg
    n_heads = d // HEAD
    nt = seq // tb
    row = lambda b, i: (b * nt + i, 0)
    weights = list(mix_weights) + list(out_weights) + list(sgu_weights)
    y_spec = pl.BlockSpec((tb, d), row)
    return pl.pallas_call(
        _mixer_prompt_kernel,
        grid=(batch, nt),
        in_specs=([pl.BlockSpec((tb, d_model), row), _resident(xs), _full(norm_g), _resident(wt)]
                  + [_full(w) for w in weights]),
        out_specs=[y_spec, y_spec,
                   pl.BlockSpec((1, n_heads, HEAD, HEAD), lambda b, i: (b, 0, 0, 0)),
                   pl.BlockSpec((1, 1, n_rw), lambda b, i: (b, 0, 0)),
                   pl.BlockSpec((ms, n_rw), lambda b, i: (0, 0), pipeline_mode=pl.Buffered(1)),
                   pl.BlockSpec((ms, n_sg), lambda b, i: (0, 0), pipeline_mode=pl.Buffered(1))],
        out_shape=[jax.ShapeDtypeStruct((batch * seq, d), BF16),
                   jax.ShapeDtypeStruct((batch * seq, d), BF16),
                   jax.ShapeDtypeStruct((batch, n_heads, HEAD, HEAD), F32),
                   jax.ShapeDtypeStruct((batch, 1, n_rw), F32),
                   jax.ShapeDtypeStruct((ms, n_rw), F32),
                   jax.ShapeDtypeStruct((ms, n_sg), F32)],
        scratch_shapes=([pltpu.VMEM((1, d), F32)] * 3
                        + [pltpu.VMEM((1, lay.tn), F32), pltpu.VMEM((d // GROUP, GROUP, GROUP), F32),
                           pltpu.VMEM((tb, d), F32), pltpu.VMEM((tb, n_rw), F32),
                           pltpu.VMEM((tb, n_sg), F32)]),
        compiler_params=_params(("arbitrary", "arbitrary")),
        name="mixer_prompt",
    )(x, xs, norm_g, wt, *weights)


STEP_HEADS = 2
STEP_UNROLL = 8


def _wkv_step_kernel(r_ref, lw_ref, k_ref, v_ref, a_ref, b_ref, s_ref, o_ref, sn_ref):
    for h in range(s_ref.shape[0]):
        feat = slice(h * HEAD, (h + 1) * HEAD)
        a, b, k, r = a_ref[feat, :], b_ref[feat, :], k_ref[feat, :], r_ref[feat, :]
        w = jnp.exp(lw_ref[feat, :])

        def body(j, carry, h=h, a=a, b=b, k=k, r=r, w=w):
            for u in range(STEP_UNROLL):
                i = j * STEP_UNROLL + u
                s = s_ref[h, i]
                sa = jnp.sum(s * a, axis=0, keepdims=True)
                s = s * w + sa * b + v_ref[pl.ds(h * HEAD + i, 1), :] * k
                sn_ref[h, i] = s
                o_ref[pl.ds(h * HEAD + i, 1), :] = jnp.sum(s * r, axis=0, keepdims=True)
            return carry

        lax.fori_loop(0, HEAD // STEP_UNROLL, body, 0)


def _wkv_step(cols, state):
    d, m = cols[0].shape
    n_heads = d // HEAD
    vec = pl.BlockSpec((STEP_HEADS * HEAD, m), lambda h: (h, 0))
    st = pl.BlockSpec((STEP_HEADS, HEAD, HEAD, m), lambda h: (h, 0, 0, 0))
    o, s = pl.pallas_call(
        _wkv_step_kernel,
        grid=(n_heads // STEP_HEADS,),
        in_specs=[vec] * 6 + [st],
        out_specs=[vec, st],
        out_shape=[jax.ShapeDtypeStruct((d, m), F32), jax.ShapeDtypeStruct((n_heads, HEAD, HEAD, m), F32)],
        compiler_params=_params(("parallel",)),
        name="wkv_step",
    )(*cols, jnp.transpose(state, (1, 2, 3, 0)))
    return o, jnp.transpose(s, (3, 0, 1, 2))


def _post_kernel(ot_ref, *refs):
    y_ref = refs[-1]
    y_ref[...] = _rwkv_out(ot_ref[...].T, *[ref[...] for ref in refs[:-1]]).astype(y_ref.dtype)


def _rwkv_post(o_t, r, k, v, g, out_weights):
    m, d = r.shape
    spec = pl.BlockSpec((m, d), lambda i: (0, 0))
    return pl.pallas_call(
        _post_kernel,
        grid=(1,),
        in_specs=[pl.BlockSpec((d, m), lambda i: (0, 0))] + [spec] * 4 + [_full(w) for w in out_weights],
        out_specs=spec,
        out_shape=jax.ShapeDtypeStruct((m, d), BF16),
        compiler_params=_params(("arbitrary",)),
        name="rwkv_post",
    )(o_t, r, k, v, g, *out_weights)


def _gelu(x):
    return 0.5 * x * (1.0 + jnp.tanh(GELU_C * (x + 0.044715 * (x * x * x))))


def _layernorm(x, g, b):
    mu = jnp.mean(x, axis=-1, keepdims=True)
    xc = x - mu
    var = jnp.mean(xc * xc, axis=-1, keepdims=True)
    return xc * lax.rsqrt(var + LN_EPS) * g + b


def _sgu_sample_kernel(pu_ref, pv_ref, ng_ref, nb_ref, w_ref, bias_ref, y_ref, vs_ref):
    u = _gelu(pu_ref[...])
    vs = _layernorm(_gelu(pv_ref[...]), ng_ref[...], nb_ref[...])
    vs_ref[...] = vs
    y_ref[...] = (u * (w_ref[...] * vs + bias_ref[...])).astype(y_ref.dtype)


def _sgu_sample(p, lay, norm_g, norm_b, sgu_w, sgu_b):
    d = lay.d_sgu
    m = p.shape[0]
    w0 = jnp.repeat(sgu_w[:, 0, 0], SGU_CHUNK)[None, :]
    b0 = jnp.repeat(sgu_b[:, 0], SGU_CHUNK)[None, :]
    vec = pl.BlockSpec((1, d), lambda i: (0, 0))
    out = pl.BlockSpec((m, d), lambda i: (0, 0))
    return pl.pallas_call(
        _sgu_sample_kernel,
        grid=(1,),
        in_specs=[pl.BlockSpec((m, d), lambda i: (0, 0)),
                  pl.BlockSpec((m, d), lambda i: (0, 1)), vec, vec, vec, vec],
        out_specs=[out, out],
        out_shape=[jax.ShapeDtypeStruct((m, d), BF16), jax.ShapeDtypeStruct((m, d), F32)],
        compiler_params=_params(("arbitrary",)),
        name="sgu_sample",
    )(p, p, norm_g[None, :], norm_b[None, :], w0, b0)


def _out_proj_kernel(x_ref, ya_ref, yb_ref, wa_ref, wb_ref, o_ref):
    o_ref[...] = (x_ref[...] + jnp.dot(ya_ref[...], wa_ref[...].astype(BF16), preferred_element_type=F32)
                  + jnp.dot(yb_ref[...], wb_ref[...].astype(BF16), preferred_element_type=F32))


def _out_proj(x, ya, yb, w, tm, tn):
    m, d = x.shape
    da = ya.shape[1]
    return pl.pallas_call(
        _out_proj_kernel,
        grid=(m // tm, d // tn),
        in_specs=[pl.BlockSpec((tm, tn), lambda i, j: (i, j)),
                  pl.BlockSpec((tm, da), lambda i, j: (i, 0)),
                  pl.BlockSpec((tm, da), lambda i, j: (i, 0)),
                  pl.BlockSpec((da, tn), lambda i, j: (0, j)),
                  pl.BlockSpec((da, tn), lambda i, j: (1, j))],
        out_specs=pl.BlockSpec((tm, tn), lambda i, j: (i, j)),
        out_shape=jax.ShapeDtypeStruct((m, d), F32),
        compiler_params=_params(("parallel", "arbitrary")),
        name="out_proj",
    )(x, ya, yb, w, w)


def _ffn_kernel(x_ref, xs_ref, g2_ref, wu_ref, wd_ref, gf_ref, o_ref, os_ref, h_ref, a_ref):
    f = pl.program_id(1)
    last = pl.num_programs(1) - 1
    tm = x_ref.shape[0]

    def up():
        a = jnp.dot(h_ref[...], wu_ref[...].astype(BF16), preferred_element_type=F32)
        return jnp.square(jnp.maximum(a, 0.0)).astype(BF16)

    def down():
        return jnp.dot(a_ref[...], wd_ref[...].astype(BF16), preferred_element_type=F32)

    @pl.when(f == 0)
    def _():
        x, xs = x_ref[...], xs_ref[...]
        h_ref[:tm, :] = _rms(x, g2_ref[...]).astype(BF16)
        h_ref[tm:, :] = _rms(xs, g2_ref[...]).astype(BF16)
        o_ref[...] = x
        os_ref[...] = xs
        a_ref[...] = up()

    @pl.when((f > 0) & (f < last))
    def _():
        acc = down()
        a_new = up()
        o_ref[...] += acc[:tm]
        os_ref[...] += acc[tm:]
        a_ref[...] = a_new

    @pl.when(f == last)
    def _():
        acc = down()
        o_ref[...] = _rms(o_ref[...] + acc[:tm], gf_ref[...])
        os_ref[...] = _rms(os_ref[...] + acc[tm:], gf_ref[...])


def _ffn(x, xs, g2, w_up, w_down, gf, tm, tf):
    m, d = x.shape
    n_blocks = m // tm
    ts = xs.shape[0] // n_blocks
    assert ts * n_blocks == xs.shape[0] and ts % 8 == 0
    nf = w_up.shape[1] // tf
    return pl.pallas_call(
        _ffn_kernel,
        grid=(n_blocks, nf + 1),
        in_specs=[pl.BlockSpec((tm, d), lambda i, f: (i, 0)),
                  pl.BlockSpec((ts, d), lambda i, f: (i, 0)),
                  pl.BlockSpec((1, d), lambda i, f: (0, 0)),
                  pl.BlockSpec((d, tf), lambda i, f: (0, jnp.minimum(f, nf - 1))),
                  pl.BlockSpec((tf, d), lambda i, f: (jnp.maximum(f - 1, 0), 0)),
                  pl.BlockSpec((1, d), lambda i, f: (0, 0))],
        out_specs=[pl.BlockSpec((tm, d), lambda i, f: (i, 0)),
                   pl.BlockSpec((ts, d), lambda i, f: (i, 0))],
        out_shape=[jax.ShapeDtypeStruct((m, d), F32), jax.ShapeDtypeStruct(xs.shape, F32)],
        scratch_shapes=[pltpu.VMEM((tm + ts, d), BF16), pltpu.VMEM((tm + ts, tf), BF16)],
        compiler_params=_params(("parallel", "arbitrary")),
        name="ffn",
    )(x, xs, g2[None, :], w_up, w_down, gf[None, :])


def _row_tile(m, cap):
    t = min(m, cap)
    assert m % t == 0
    return t


def kernel(x_prompt, x_sample, state_wkv, state_shift, norm1_g, w_in, mu_shift, w0, w_up, a0, a_up, g_up,
           k_k, k_a, r_k, lnx_g, lnx_b, sgu_norm_g, sgu_norm_b, sgu_w, sgu_b, w_out, norm2_g, w_ffn_up,
           w_ffn_down, norm_f_g):
    batch, seq, d_model = x_prompt.shape
    n_dec, dec_seq, _ = x_sample.shape
    depth = w_in.shape[0]
    assert depth == 1 and dec_seq == 1
    d_rwkv = w0.shape[1]
    d_sgu = sgu_norm_g.shape[1]
    lay = _Layout(d_rwkv, d_sgu, w_up.shape[1], a_up.shape[1], g_up.shape[1], PROJ_COLS)
    w_in_t = w_in[0].T.astype(BF16)
    prep_w = _prep_weights(lay, mu_shift[0], w0[0], w_up[0], a0[0], a_up[0], g_up[0], k_k[0], k_a[0])

    out_w = [w.reshape(1, d_rwkv) for w in (lnx_g[0], lnx_b[0], r_k[0])]

    xp = x_prompt.reshape(batch * seq, d_model)
    xs = x_sample.reshape(n_dec, d_model)
    sgu_bias = jnp.repeat(sgu_b[0].T, SGU_CHUNK, axis=1)
    ya, yb, wkv_p, last_p, ps_rw, ps_sg = _mixer_prompt(
        xp, xs, norm1_g, w_in_t, lay, prep_w, out_w, [sgu_norm_g, sgu_norm_b, sgu_w[0], sgu_bias],
        batch, seq, _row_tile(seq, MIXER_ROWS))
    x1p = _out_proj(xp, ya, yb, w_out[0], _row_tile(batch * seq, OUT_PROJ_ROWS), OUT_PROJ_COLS)
    shift_p = last_p[:, 0, :lay.d_shift]

    (r, k, v, g), step_cols = _prep_sample(ps_rw, state_shift[0], lay, prep_w)
    o_t, wkv_s = _wkv_step(step_cols, state_wkv[0])
    yb, vs = _sgu_sample(ps_sg, lay, sgu_norm_g[0], sgu_norm_b[0], sgu_w[0], sgu_b[0])
    ya = _rwkv_post(o_t, r, k, v, g, out_w)
    x1s = _out_proj(xs, ya, yb, w_out[0], n_dec, OUT_PROJ_COLS)
    shift_s = ps_rw[:, :lay.d_shift]

    y_prompt, y_sample = _ffn(x1p, x1s, norm2_g[0], w_ffn_up[0], w_ffn_down[0], norm_f_g,
                              _row_tile(batch * seq, FFN_ROWS), FFN_HIDDEN)
    y_prompt = y_prompt.reshape(batch, seq, d_model)
    y_sample = y_sample.reshape(n_dec, 1, d_model)
    return (y_prompt, y_sample, wkv_p[None], shift_p[None], wkv_s[None], shift_s[None],
            vs.reshape(1, n_dec, 1, d_sgu))
```

```python
import functools
import math

import jax
import jax.numpy as jnp
from jax import lax
from jax.experimental import pallas as pl
from jax.experimental.pallas import tpu as pltpu

F32 = jnp.float32
BF16 = jnp.bfloat16

HEAD = 64
LANES = 128
GROUP_HEADS = 2
GROUP = GROUP_HEADS * HEAD
SGU_CHUNK = 128
WKV_CHUNK = 64
PROJ_COLS = 512
MIXER_ROWS = 512
OUT_PROJ_ROWS = 512
FFN_ROWS = 1024
FFN_HIDDEN = 512
RMS_EPS = 1e-5
LN_EPS = 1e-5
GN_EPS = 64e-5
DECAY_SCALE = math.exp(-0.5)
GELU_C = math.sqrt(2.0 / math.pi)
VMEM_LIMIT = 62 * 1024 * 1024


def _params(sem):
    return pltpu.CompilerParams(dimension_semantics=sem, vmem_limit_bytes=VMEM_LIMIT)


def _dot(a, b):
    return jnp.dot(a.astype(BF16), b.astype(BF16), preferred_element_type=F32)


def _dot_nt(a, b):
    return lax.dot_general(a.astype(BF16), b.astype(BF16), (((1,), (1,)), ((), ())),
                           preferred_element_type=F32)


def _dot_tn(a, b):
    return lax.dot_general(a.astype(BF16), b.astype(BF16), (((0,), (0,)), ((), ())),
                           preferred_element_type=F32)


def _split3(x):
    hi = x.astype(BF16)
    r1 = x - hi.astype(F32)
    mid = r1.astype(BF16)
    lo = (r1 - mid.astype(F32)).astype(BF16)
    return hi, mid, lo


def _dot_exact_lhs(m, x):
    hi, mid, lo = _split3(x)
    mb = m.astype(BF16)
    return (jnp.dot(mb, hi, preferred_element_type=F32) + jnp.dot(mb, mid, preferred_element_type=F32)
            + jnp.dot(mb, lo, preferred_element_type=F32))


def _sigmoid(x):
    return 1.0 / (1.0 + jnp.exp(-x))


def _head_ones():
    r = lax.broadcasted_iota(jnp.int32, (2 * LANES, LANES), 0) % LANES // HEAD
    c = lax.broadcasted_iota(jnp.int32, (2 * LANES, LANES), 1) // HEAD
    return (r == c).astype(BF16)


def _head_sum(x, ones):
    parts = []
    for s in range(0, x.shape[1], LANES):
        xs = x[:, s:s + LANES]
        hi = xs.astype(BF16)
        lo = (xs - hi.astype(F32)).astype(BF16)
        parts.append(jnp.dot(jnp.concatenate([hi, lo], axis=1), ones, preferred_element_type=F32))
    return parts[0] if len(parts) == 1 else jnp.concatenate(parts, axis=1)


def _rms(x, g):
    return x * lax.rsqrt(jnp.mean(x * x, axis=-1, keepdims=True) + RMS_EPS) * g


def _lo_mix(pl_, ql, mu_l, n_wa, n_gl):
    n = n_wa + n_gl
    lo = pl_[:, :n]
    lo = lo + (ql[:, :n] - lo) * mu_l[:, :n]
    wa = lo[:, :n_wa]
    return jnp.tanh(wa), wa, _sigmoid(lo[:, n_wa:])


def _run(steps):
    try:
        while True:
            next(steps)
    except StopIteration as done:
        return done.value


def _col_mix_steps(pr, pk, pv, qr, qk, qv, mu_r, mu_k, mu_v, tanh_wa, wa, sig_gl, w0, w_up, a0, a_up, g_up,
                   k_k, k_a):
    r = pr + (qr - pr) * mu_r
    yield
    k = pk + (qk - pk) * mu_k
    yield
    v = pv + (qv - pv) * mu_v
    yield
    lw = -DECAY_SCALE * _sigmoid(w0 + _dot(tanh_wa, w_up))
    yield
    a = _sigmoid(a0 + _dot(wa, a_up))
    yield
    gate = _dot(sig_gl, g_up)
    yield
    kk = k * k_k
    ss = _head_sum(kk * kk, _head_ones())
    yield
    kk = kk / jnp.maximum(jnp.sqrt(ss), 1e-12)
    yield
    return r, lw, k * (1.0 + (a - 1.0) * k_a), v, -kk, kk * a, gate


def _rwkv_mix(pr, pk, pv, pl_, qr, qk, qv, ql, mu_r, mu_k, mu_v, mu_l, w0, w_up, a0, a_up, g_up, k_k, k_a):
    lo = _lo_mix(pl_, ql, mu_l, w_up.shape[0], g_up.shape[0])
    return _run(_col_mix_steps(pr, pk, pv, qr, qk, qv, mu_r, mu_k, mu_v, *lo, w0, w_up, a0, a_up, g_up,
                               k_k, k_a))


def _rwkv_out_steps(o, r, k, v, gate, lnx_g, lnx_b, r_k):
    ones = _head_ones()
    mu = _head_sum(o, ones) * (1.0 / HEAD)
    yield
    oc = o - mu
    var = _head_sum(oc * oc, ones) * (1.0 / HEAD)
    yield
    y = oc * lax.rsqrt(var + GN_EPS) * lnx_g + lnx_b
    yield
    bonus = _head_sum(r * k * r_k, ones)
    yield
    return (y + bonus * v) * gate


def _rwkv_out(*args):
    return _run(_rwkv_out_steps(*args))


def _each(fn, *lists):
    out = []
    for args in zip(*lists):
        out.append(fn(*args))
        yield
    return out


def _run_together(work):
    values = [None] * len(work)
    longest = max(n for _, n in work)
    credit = [0.0] * len(work)
    live = set(range(len(work)))
    while live:
        for i, (steps, n) in enumerate(work):
            credit[i] += n / longest
            while i in live and credit[i] >= 1.0:
                credit[i] -= 1.0
                try:
                    next(steps)
                except StopIteration as done:
                    values[i] = done.value
                    live.discard(i)
    return values


def _shifted(p, carry_ref):
    rows = lax.broadcasted_iota(jnp.int32, p.shape, 0)
    q = jnp.where(rows == 0, carry_ref[...], pltpu.roll(p, 1, axis=0))
    carry_ref[...] = p[p.shape[0] - 1:, :]
    return q


def _prep_sample_kernel(*refs):
    ins, rows, cols = refs[:-10], refs[-10:-6], refs[-6:]
    r, lw, k, v, aa, bb, gate = _rwkv_mix(*[ref[...] for ref in ins])
    for ref, val in zip(rows, (r, k, v, gate)):
        ref[...] = val
    for ref, val in zip(cols, (r, lw, k, v, aa, bb)):
        ref[...] = val.T


class _Layout:
    def __init__(self, d_rwkv, d_sgu, lora_w, lora_a, lora_g, tn):
        self.d_rwkv, self.d_sgu, self.tn = d_rwkv, d_sgu, tn
        self.wa_w = lora_w + lora_a
        self.gl_w = -(-lora_g // LANES) * LANES
        self.d_shift = 3 * d_rwkv + self.wa_w + lora_g
        assert self.wa_w == LANES and self.wa_w + self.gl_w <= tn and d_rwkv % tn == 0

    def rw_pieces(self, a):
        d = self.d_rwkv
        pad = [(0, 0)] * (a.ndim - 1) + [(0, self.tn - (self.d_shift - 3 * d))]
        return a[..., :d], a[..., d:2 * d], a[..., 2 * d:3 * d], jnp.pad(a[..., 3 * d:], pad)


def _prep_weights(lay, mu, w0, w_up, a0, a_up, g_up, k_k, k_a):
    d = lay.d_rwkv
    lora_w, lora_g = w_up.shape[0], g_up.shape[0]
    mus = [m[None, :] for m in lay.rw_pieces(mu)]
    w_up_p = jnp.pad(w_up, ((0, lay.wa_w - lora_w), (0, 0)))
    a_up_p = jnp.pad(a_up, ((lora_w, 0), (0, 0)))
    g_up_p = jnp.pad(g_up, ((0, lay.gl_w - lora_g), (0, 0)))
    return mus + [w0[None, :], w_up_p, a0[None, :], a_up_p, g_up_p, k_k.reshape(1, d), k_a.reshape(1, d)]


def _full(a):
    return pl.BlockSpec(a.shape, lambda *_: (0,) * a.ndim)


def _prep_sample(p, prev, lay, weights):
    d = lay.d_rwkv
    m = p.shape[0]
    p_specs = [pl.BlockSpec((m, d), lambda i: (0, 0)),
               pl.BlockSpec((m, d), lambda i: (0, 1)),
               pl.BlockSpec((m, d), lambda i: (0, 2)),
               pl.BlockSpec((m, lay.tn), lambda i: (0, 3 * d // lay.tn))]
    prevs = list(lay.rw_pieces(prev))
    row_spec = pl.BlockSpec((m, d), lambda i: (0, 0))
    col_spec = pl.BlockSpec((d, m), lambda i: (0, 0))
    outs = pl.pallas_call(
        _prep_sample_kernel,
        grid=(1,),
        in_specs=p_specs + [_full(q) for q in prevs] + [_full(w) for w in weights],
        out_specs=[row_spec] * 4 + [col_spec] * 6,
        out_shape=[jax.ShapeDtypeStruct((m, d), F32)] * 4 + [jax.ShapeDtypeStruct((d, m), F32)] * 6,
        compiler_params=_params(("arbitrary",)),
        name="rwkv_prep_sample",
    )(p, p, p, p, *prevs, *weights)
    return outs[:4], outs[4:]


def _head_stack(x):
    head = lax.broadcasted_iota(jnp.int32, x.shape, 1) % GROUP // HEAD
    return jnp.concatenate([jnp.where(head == h, x, 0.0) for h in range(GROUP_HEADS)], axis=0)


def _wkv_pre_steps(r, lw, k, v, a, b, tri, gram_mask):
    c = WKV_CHUNK
    n_groups = r.shape[1] // GROUP

    def cut(x):
        return [x[:, g * GROUP:(g + 1) * GROUP] for g in range(n_groups)]

    cum = _dot_exact_lhs(tri, lw)
    yield
    e_out = jnp.exp(-cum)
    a_s = cut(a * jnp.exp(cum - lw))
    r_s = cut(r * jnp.exp(cum))
    yield
    b_s = cut(b * e_out)
    k_s = cut(k * e_out)
    yield
    last = cum[c - 1:, :]
    e_end = jnp.exp(last - cum)
    bk_e = cut(jnp.concatenate([b * e_end, k * e_end], axis=0))
    decay = cut(jnp.exp(last))
    vs = cut(v)
    yield

    grams = yield from _each(lambda ai, ri, bi, ki: jnp.where(
        gram_mask, _dot_nt(jnp.concatenate([ai, ri], axis=0),
                           jnp.concatenate([_head_stack(bi), _head_stack(ki)], axis=0)), 0.0),
        a_s, r_s, b_s, k_s)
    v_st = [_head_stack(x) for x in vs]
    kvs = yield from _each(lambda g, vi: _dot(g[:, GROUP:], vi), grams, v_st)
    xs = [jnp.concatenate([ai, kv[:c]], axis=1) for ai, kv in zip(a_s, kvs)]
    pws = [g[:c, :GROUP] for g in grams]
    ns = pws
    pws = yield from _each(lambda pw: _dot(pw, _head_stack(pw)), pws)
    n = 2
    while n < c // 2:
        both = yield from _each(
            lambda pw, nn: _dot(jnp.concatenate([pw, nn], axis=0), _head_stack(pw)), pws, ns)
        ns = [nn + pw + bo[c:] for nn, pw, bo in zip(ns, pws, both)]
        pws = [bo[:c] for bo in both]
        n *= 2
    ns = yield from _each(lambda nn, pw: nn + pw + _dot(nn, _head_stack(pw)), ns, pws)
    xs = yield from _each(lambda x, nn: x + _dot(nn, _head_stack(x)), xs, ns)
    qos = yield from _each(lambda g, x: _dot(g[c:, :GROUP], _head_stack(x)), grams, xs)
    qp = [jnp.concatenate([ri + qo[:, :GROUP], x[:, :GROUP]], axis=0) for ri, qo, x in zip(r_s, qos, xs)]
    o2 = [qo[:, GROUP:] + kv[c:] for qo, kv in zip(qos, kvs)]
    return dict(qp=qp, o2=o2, u2=[x[:, GROUP:] for x in xs], v=vs, bk_e=bk_e, decay=decay)


def _wkv_state_steps(pre, states, state_mask):
    c = WKV_CHUNK
    ous = yield from _each(_dot_nt, pre["qp"], states)
    upds = yield from _each(lambda ou, u2, v, bk: _dot_tn(jnp.concatenate([ou[c:] + u2, v], axis=0), bk),
                            ous, pre["u2"], pre["v"], pre["bk_e"])
    for p, (upd, decay) in enumerate(zip(upds, pre["decay"])):
        states[p] = states[p] * decay + jnp.where(state_mask, upd, 0.0)
    return [ou[:c] + o2 for ou, o2 in zip(ous, pre["o2"])]


def _mixer_prompt_kernel(*refs):
    x_ref, xs_ref, g1_ref, w_ref = refs[:4]
    mix_refs, out_refs, sgu_refs = refs[4:15], refs[15:18], refs[18:22]
    y_ref, yb_ref, sf_ref, sh_ref, psr_ref, pss_ref = refs[22:28]
    carries, s_ref, o_scr, p_scr, sg_scr = refs[28:32], refs[32], refs[33], refs[34], refs[35]
    c = WKV_CHUNK
    tb = y_ref.shape[0]
    d = y_ref.shape[1]
    n_groups = d // GROUP
    n_rw, n_sg = p_scr.shape[1], sg_scr.shape[1]
    sg_row0 = w_ref.shape[0] - n_sg
    t = pl.program_id(1)
    first = t == 0
    last = t == pl.num_programs(1) - 1

    @pl.when(first)
    def _():
        s_ref[...] = jnp.zeros_like(s_ref)
        for carry in carries:
            carry[...] = jnp.zeros_like(carry)

    @pl.when(first & (pl.program_id(0) == 0))
    def _():
        hs = _rms(xs_ref[...], g1_ref[...]).astype(BF16)
        for j in range(0, n_rw, PROJ_COLS):
            psr_ref[:, j:j + PROJ_COLS] = _dot_nt(hs, w_ref[j:j + PROJ_COLS, :])
        for j in range(0, n_sg, PROJ_COLS):
            pss_ref[:, j:j + PROJ_COLS] = _dot_nt(hs, w_ref[sg_row0 + j:sg_row0 + j + PROJ_COLS, :])

    h = _rms(x_ref[...], g1_ref[...]).astype(BF16)
    for j in range(0, n_rw, PROJ_COLS):
        p_scr[:, j:j + PROJ_COLS] = _dot_nt(h, w_ref[j:j + PROJ_COLS, :])
    p_refs = [p_scr.at[:, j * d:(j + 1) * d] for j in range(3)] + [p_scr.at[:, 3 * d:]]

    def sg_proj_steps():
        for j in range(0, n_sg, PROJ_COLS):
            sg_scr[:, j:j + PROJ_COLS] = _dot_nt(h, w_ref[sg_row0 + j:sg_row0 + j + PROJ_COLS, :])
            yield

    sg_ng, sg_nb, sg_w, sg_bias = sgu_refs
    sri = lax.broadcasted_iota(jnp.int32, (SGU_CHUNK, SGU_CHUNK), 0)
    sci = lax.broadcasted_iota(jnp.int32, (SGU_CHUNK, SGU_CHUNK), 1)
    sg_ws = [jnp.where(sci <= sri, sg_w[g], 0.0).astype(BF16) for g in range(sg_w.shape[0])]

    def sgu_steps(rows):
        u = _gelu(sg_scr[rows, :d])
        yield
        vs = _layernorm(_gelu(sg_scr[rows, d:]), sg_ng[...], sg_nb[...]).astype(BF16)
        yield
        for g, w in enumerate(sg_ws):
            cols = slice(g * SGU_CHUNK, (g + 1) * SGU_CHUNK)
            mix = jnp.dot(w, vs[:, cols], preferred_element_type=F32) + sg_bias[:, cols]
            yb_ref[rows, cols] = (u[:, cols] * mix).astype(yb_ref.dtype)
            yield

    mu_r, mu_k, mu_v, mu_l, w0, w_up, a0, a_up, g_up, k_k, k_a = mix_refs
    ri = lax.broadcasted_iota(jnp.int32, (c, c), 0)
    ci = lax.broadcasted_iota(jnp.int32, (c, c), 1)
    tri = (ci <= ri).astype(F32)
    gr = lax.broadcasted_iota(jnp.int32, (2 * c, 2 * GROUP), 0)
    gc = lax.broadcasted_iota(jnp.int32, (2 * c, 2 * GROUP), 1) % c
    gram_mask = gc <= jnp.where(gr < c, gr - 1, gr - c)
    sr = lax.broadcasted_iota(jnp.int32, (GROUP, GROUP), 0) // HEAD
    sc = lax.broadcasted_iota(jnp.int32, (GROUP, GROUP), 1) // HEAD
    state_mask = sr == sc
    states = [s_ref[g] for g in range(n_groups)]

    def prep_steps(rows):
        pl_ = p_refs[3][rows, :]
        lo = _lo_mix(pl_, _shifted(pl_, carries[3]), mu_l[...], w_up.shape[0], g_up.shape[0])
        yield
        ps = [ref[rows, :] for ref in p_refs[:3]]
        yield
        qs = [_shifted(p, carry) for p, carry in zip(ps, carries[:3])]
        yield
        vals = yield from _col_mix_steps(*ps, *qs, mu_r[...], mu_k[...], mu_v[...], *lo, w0[...], w_up[...],
                                         a0[...], a_up[...], g_up[...], k_k[...], k_a[...])
        return vals

    def state_steps(rows, pre):
        outs = yield from _wkv_state_steps(pre, states, state_mask)
        o_scr[rows, :] = jnp.concatenate(outs, axis=1)

    def post_steps(rows, vals):
        r, _, k, v, _, _, gate = vals
        y = yield from _rwkv_out_steps(o_scr[rows, :], r, k, v, gate, *[ref[...] for ref in out_refs])
        y_ref[rows, :] = y.astype(y_ref.dtype)

    n_chunks = tb // c
    rows = [slice(j * c, (j + 1) * c) for j in range(n_chunks)]
    n_stages = n_chunks + 3
    sgu_at = {min(2 + 2 * j, n_stages - 1): j for j in range(tb // SGU_CHUNK)}
    vals, pre = {}, {}
    for s in range(n_stages):
        work = []
        if s == 0:
            work.append(("sg_proj", 0, sg_proj_steps(), n_sg // PROJ_COLS))
        if s in sgu_at:
            j = sgu_at[s]
            work.append(("sgu", j, sgu_steps(slice(j * SGU_CHUNK, (j + 1) * SGU_CHUNK)), 2 + len(sg_ws)))
        if s < n_chunks:
            work.append(("prep", s, prep_steps(rows[s]), 12))
        if 0 <= s - 1 < n_chunks:
            work.append(("pre", s - 1, _wkv_pre_steps(*vals[s - 1][:6], tri, gram_mask), 10 * n_groups + 5))
        if 0 <= s - 2 < n_chunks:
            work.append(("state", s - 2, state_steps(rows[s - 2], pre[s - 2]), 2 * n_groups + 1))
        if 0 <= s - 3 < n_chunks:
            work.append(("post", s - 3, post_steps(rows[s - 3], vals[s - 3]), 5))
        done = _run_together([(steps, n) for _, _, steps, n in work])
        for (kind, j, _, _), value in zip(work, done):
            if kind == "prep":
                vals[j] = value
            elif kind == "pre":
                pre[j] = value
    for g, s in enumerate(states):
        s_ref[g] = s

    @pl.when(last)
    def _():
        sh_ref[0] = p_scr[tb - 1:, :]
        for g in range(n_groups):
            for h in range(GROUP_HEADS):
                sf_ref[0, g * GROUP_HEADS + h] = s_ref[g, h * HEAD:(h + 1) * HEAD, h * HEAD:(h + 1) * HEAD]


def _resident(a):
    return pl.BlockSpec(a.shape, lambda *_: (0,) * a.ndim, pipeline_mode=pl.Buffered(1))


def _mixer_prompt(x, xs, norm_g, wt, lay, mix_weights, out_weights, sgu_weights, batch, seq, tb):
    d = lay.d_rwkv
    d_model = x.shape[1]
    ms = xs.shape[0]
    n_rw, n_sg = 3 * d + lay.tn, 2 * lay.d_sgu
    assert lay.d_sgu == d and lay.tn % PROJ_COLS == 0 and wt.shape[0] == lay.d_shift + n_sg
    n_heads = d // HEAD
    nt = seq // tb
    row = lambda b, i: (b * nt + i, 0)
    weights = list(mix_weights) + list(out_weights) + list(sgu_weights)
    y_spec = pl.BlockSpec((tb, d), row)
    return pl.pallas_call(
        _mixer_prompt_kernel,
        grid=(batch, nt),
        in_specs=([pl.BlockSpec((tb, d_model), row), _resident(xs), _full(norm_g), _resident(wt)]
                  + [_full(w) for w in weights]),
        out_specs=[y_spec, y_spec,
                   pl.BlockSpec((1, n_heads, HEAD, HEAD), lambda b, i: (b, 0, 0, 0)),
                   pl.BlockSpec((1, 1, n_rw), lambda b, i: (b, 0, 0)),
                   pl.BlockSpec((ms, n_rw), lambda b, i: (0, 0), pipeline_mode=pl.Buffered(1)),
                   pl.BlockSpec((ms, n_sg), lambda b, i: (0, 0), pipeline_mode=pl.Buffered(1))],
        out_shape=[jax.ShapeDtypeStruct((batch * seq, d), BF16),
                   jax.ShapeDtypeStruct((batch * seq, d), BF16),
                   jax.ShapeDtypeStruct((batch, n_heads, HEAD, HEAD), F32),
                   jax.ShapeDtypeStruct((batch, 1, n_rw), F32),
                   jax.ShapeDtypeStruct((ms, n_rw), F32),
                   jax.ShapeDtypeStruct((ms, n_sg), F32)],
        scratch_shapes=([pltpu.VMEM((1, d), F32)] * 3
                        + [pltpu.VMEM((1, lay.tn), F32), pltpu.VMEM((d // GROUP, GROUP, GROUP), F32),
                           pltpu.VMEM((tb, d), F32), pltpu.VMEM((tb, n_rw), F32),
                           pltpu.VMEM((tb, n_sg), F32)]),
        compiler_params=_params(("arbitrary", "arbitrary")),
        name="mixer_prompt",
    )(x, xs, norm_g, wt, *weights)


STEP_HEADS = 2
STEP_UNROLL = 8


def _wkv_step_kernel(r_ref, lw_ref, k_ref, v_ref, a_ref, b_ref, s_ref, o_ref, sn_ref):
    for h in range(s_ref.shape[0]):
        feat = slice(h * HEAD, (h + 1) * HEAD)
        a, b, k, r = a_ref[feat, :], b_ref[feat, :], k_ref[feat, :], r_ref[feat, :]
        w = jnp.exp(lw_ref[feat, :])

        def body(j, carry, h=h, a=a, b=b, k=k, r=r, w=w):
            for u in range(STEP_UNROLL):
                i = j * STEP_UNROLL + u
                s = s_ref[h, i]
                sa = jnp.sum(s * a, axis=0, keepdims=True)
                s = s * w + sa * b + v_ref[pl.ds(h * HEAD + i, 1), :] * k
                sn_ref[h, i] = s
                o_ref[pl.ds(h * HEAD + i, 1), :] = jnp.sum(s * r, axis=0, keepdims=True)
            return carry

        lax.fori_loop(0, HEAD // STEP_UNROLL, body, 0)


def _wkv_step(cols, state):
    d, m = cols[0].shape
    n_heads = d // HEAD
    vec = pl.BlockSpec((STEP_HEADS * HEAD, m), lambda h: (h, 0))
    st = pl.BlockSpec((STEP_HEADS, HEAD, HEAD, m), lambda h: (h, 0, 0, 0))
    o, s = pl.pallas_call(
        _wkv_step_kernel,
        grid=(n_heads // STEP_HEADS,),
        in_specs=[vec] * 6 + [st],
        out_specs=[vec, st],
        out_shape=[jax.ShapeDtypeStruct((d, m), F32), jax.ShapeDtypeStruct((n_heads, HEAD, HEAD, m), F32)],
        compiler_params=_params(("parallel",)),
        name="wkv_step",
    )(*cols, jnp.transpose(state, (1, 2, 3, 0)))
    return o, jnp.transpose(s, (3, 0, 1, 2))


def _post_kernel(ot_ref, *refs):
    y_ref = refs[-1]
    y_ref[...] = _rwkv_out(ot_ref[...].T, *[ref[...] for ref in refs[:-1]]).astype(y_ref.dtype)


def _rwkv_post(o_t, r, k, v, g, out_weights):
    m, d = r.shape
    spec = pl.BlockSpec((m, d), lambda i: (0, 0))
    return pl.pallas_call(
        _post_kernel,
        grid=(1,),
        in_specs=[pl.BlockSpec((d, m), lambda i: (0, 0))] + [spec] * 4 + [_full(w) for w in out_weights],
        out_specs=spec,
        out_shape=jax.ShapeDtypeStruct((m, d), BF16),
        compiler_params=_params(("arbitrary",)),
        name="rwkv_post",
    )(o_t, r, k, v, g, *out_weights)


def _gelu(x):
    return 0.5 * x * (1.0 + jnp.tanh(GELU_C * (x + 0.044715 * (x * x * x))))


def _layernorm(x, g, b):
    mu = jnp.mean(x, axis=-1, keepdims=True)
    xc = x - mu
    var = jnp.mean(xc * xc, axis=-1, keepdims=True)
    return xc * lax.rsqrt(var + LN_EPS) * g + b


def _sgu_sample_kernel(pu_ref, pv_ref, ng_ref, nb_ref, w_ref, bias_ref, y_ref, vs_ref):
    u = _gelu(pu_ref[...])
    vs = _layernorm(_gelu(pv_ref[...]), ng_ref[...], nb_ref[...])
    vs_ref[...] = vs
    y_ref[...] = (u * (w_ref[...] * vs + bias_ref[...])).astype(y_ref.dtype)


def _sgu_sample(p, lay, norm_g, norm_b, sgu_w, sgu_b):
    d = lay.d_sgu
    m = p.shape[0]
    w0 = jnp.repeat(sgu_w[:, 0, 0], SGU_CHUNK)[None, :]
    b0 = jnp.repeat(sgu_b[:, 0], SGU_CHUNK)[None, :]
    vec = pl.BlockSpec((1, d), lambda i: (0, 0))
    out = pl.BlockSpec((m, d), lambda i: (0, 0))
    return pl.pallas_call(
        _sgu_sample_kernel,
        grid=(1,),
        in_specs=[pl.BlockSpec((m, d), lambda i: (0, 0)),
                  pl.BlockSpec((m, d), lambda i: (0, 1)), vec, vec, vec, vec],
        out_specs=[out, out],
        out_shape=[jax.ShapeDtypeStruct((m, d), BF16), jax.ShapeDtypeStruct((m, d), F32)],
        compiler_params=_params(("arbitrary",)),
        name="sgu_sample",
    )(p, p, norm_g[None, :], norm_b[None, :], w0, b0)


def _out_proj_kernel(x_ref, ya_ref, yb_ref, w_ref, o_ref):
    da = ya_ref.shape[1]
    o_ref[...] = (x_ref[...] + jnp.dot(ya_ref[...], w_ref[:da, :].astype(BF16), preferred_element_type=F32)
                  + jnp.dot(yb_ref[...], w_ref[da:, :].astype(BF16), preferred_element_type=F32))


def _out_proj(x, ya, yb, w, tm):
    m, d = x.shape
    da = ya.shape[1]
    return pl.pallas_call(
        _out_proj_kernel,
        grid=(m // tm,),
        in_specs=[pl.BlockSpec((tm, d), lambda i: (i, 0)),
                  pl.BlockSpec((tm, da), lambda i: (i, 0)),
                  pl.BlockSpec((tm, da), lambda i: (i, 0)),
                  _resident(w)],
        out_specs=pl.BlockSpec((tm, d), lambda i: (i, 0)),
        out_shape=jax.ShapeDtypeStruct((m, d), F32),
        compiler_params=_params(("parallel",)),
        name="out_proj",
    )(x, ya, yb, w)


def _ffn_kernel(x_ref, xs_ref, g2_ref, wu_ref, wd_ref, gf_ref, o_ref, os_ref, h_ref, a_ref):
    f = pl.program_id(1)
    last = pl.num_programs(1) - 1
    tm = x_ref.shape[0]

    def up():
        a = jnp.dot(h_ref[...], wu_ref[...].astype(BF16), preferred_element_type=F32)
        return jnp.square(jnp.maximum(a, 0.0)).astype(BF16)

    def down():
        return jnp.dot(a_ref[...], wd_ref[...].astype(BF16), preferred_element_type=F32)

    @pl.when(f == 0)
    def _():
        x, xs = x_ref[...], xs_ref[...]
        h_ref[:tm, :] = _rms(x, g2_ref[...]).astype(BF16)
        h_ref[tm:, :] = _rms(xs, g2_ref[...]).astype(BF16)
        o_ref[...] = x
        os_ref[...] = xs
        a_ref[...] = up()

    @pl.when((f > 0) & (f < last))
    def _():
        acc = down()
        a_new = up()
        o_ref[...] += acc[:tm]
        os_ref[...] += acc[tm:]
        a_ref[...] = a_new

    @pl.when(f == last)
    def _():
        acc = down()
        o_ref[...] = _rms(o_ref[...] + acc[:tm], gf_ref[...])
        os_ref[...] = _rms(os_ref[...] + acc[tm:], gf_ref[...])


def _ffn(x, xs, g2, w_up, w_down, gf, tm, tf):
    m, d = x.shape
    n_blocks = m // tm
    ts = xs.shape[0] // n_blocks
    assert ts * n_blocks == xs.shape[0] and ts % 8 == 0
    nf = w_up.shape[1] // tf
    return pl.pallas_call(
        _ffn_kernel,
        grid=(n_blocks, nf + 1),
        in_specs=[pl.BlockSpec((tm, d), lambda i, f: (i, 0)),
                  pl.BlockSpec((ts, d), lambda i, f: (i, 0)),
                  pl.BlockSpec((1, d), lambda i, f: (0, 0)),
                  pl.BlockSpec((d, tf), lambda i, f: (0, jnp.minimum(f, nf - 1))),
                  pl.BlockSpec((tf, d), lambda i, f: (jnp.maximum(f - 1, 0), 0)),
                  pl.BlockSpec((1, d), lambda i, f: (0, 0))],
        out_specs=[pl.BlockSpec((tm, d), lambda i, f: (i, 0)),
                   pl.BlockSpec((ts, d), lambda i, f: (i, 0))],
        out_shape=[jax.ShapeDtypeStruct((m, d), F32), jax.ShapeDtypeStruct(xs.shape, F32)],
        scratch_shapes=[pltpu.VMEM((tm + ts, d), BF16), pltpu.VMEM((tm + ts, tf), BF16)],
        compiler_params=_params(("parallel", "arbitrary")),
        name="ffn",
    )(x, xs, g2[None, :], w_up, w_down, gf[None, :])


def _row_tile(m, cap):
    t = min(m, cap)
    assert m % t == 0
    return t


def kernel(x_prompt, x_sample, state_wkv, state_shift, norm1_g, w_in, mu_shift, w0, w_up, a0, a_up, g_up,
           k_k, k_a, r_k, lnx_g, lnx_b, sgu_norm_g, sgu_norm_b, sgu_w, sgu_b, w_out, norm2_g, w_ffn_up,
           w_ffn_down, norm_f_g):
    batch, seq, d_model = x_prompt.shape
    n_dec, dec_seq, _ = x_sample.shape
    depth = w_in.shape[0]
    assert depth == 1 and dec_seq == 1
    d_rwkv = w0.shape[1]
    d_sgu = sgu_norm_g.shape[1]
    lay = _Layout(d_rwkv, d_sgu, w_up.shape[1], a_up.shape[1], g_up.shape[1], PROJ_COLS)
    w_in_t = w_in[0].T.astype(BF16)
    prep_w = _prep_weights(lay, mu_shift[0], w0[0], w_up[0], a0[0], a_up[0], g_up[0], k_k[0], k_a[0])

    out_w = [w.reshape(1, d_rwkv) for w in (lnx_g[0], lnx_b[0], r_k[0])]

    xp = x_prompt.reshape(batch * seq, d_model)
    xs = x_sample.reshape(n_dec, d_model)
    sgu_bias = jnp.repeat(sgu_b[0].T, SGU_CHUNK, axis=1)
    ya, yb, wkv_p, last_p, ps_rw, ps_sg = _mixer_prompt(
        xp, xs, norm1_g, w_in_t, lay, prep_w, out_w, [sgu_norm_g, sgu_norm_b, sgu_w[0], sgu_bias],
        batch, seq, _row_tile(seq, MIXER_ROWS))
    x1p = _out_proj(xp, ya, yb, w_out[0], _row_tile(batch * seq, OUT_PROJ_ROWS))
    shift_p = last_p[:, 0, :lay.d_shift]

    (r, k, v, g), step_cols = _prep_sample(ps_rw, state_shift[0], lay, prep_w)
    o_t, wkv_s = _wkv_step(step_cols, state_wkv[0])
    yb, vs = _sgu_sample(ps_sg, lay, sgu_norm_g[0], sgu_norm_b[0], sgu_w[0], sgu_b[0])
    ya = _rwkv_post(o_t, r, k, v, g, out_w)
    x1s = _out_proj(xs, ya, yb, w_out[0], n_dec)
    shift_s = ps_rw[:, :lay.d_shift]

    y_prompt, y_sample = _ffn(x1p, x1s, norm2_g[0], w_ffn_up[0], w_ffn_down[0], norm_f_g,
                              _row_tile(batch * seq, FFN_ROWS), FFN_HIDDEN)
    y_prompt = y_prompt.reshape(batch, seq, d_model)
    y_sample = y_sample.reshape(n_dec, 1, d_model)
    return (y_prompt, y_sample, wkv_p[None], shift_p[None], wkv_s[None], shift_s[None],
            vs.reshape(1, n_dec, 1, d_sgu))
```

```python
import math

import jax
import jax.numpy as jnp
from jax import lax
from jax.experimental import pallas as pl
from jax.experimental.pallas import tpu as pltpu

F32 = jnp.float32
BF16 = jnp.bfloat16

HEAD = 64
LANES = 128
GROUP_HEADS = 2
GROUP = GROUP_HEADS * HEAD
SGU_CHUNK = 128
WKV_CHUNK = 64
PROJ_COLS = 512
MIXER_ROWS = 512
OUT_PROJ_ROWS = 512
FFN_ROWS = 1024
FFN_HIDDEN = 512
RMS_EPS = 1e-5
LN_EPS = 1e-5
GN_EPS = 64e-5
DECAY_SCALE = math.exp(-0.5)
GELU_C = math.sqrt(2.0 / math.pi)
V7X_VMEM_BYTES = 64 * 1024 * 1024
VMEM_LIMIT = V7X_VMEM_BYTES - 2 * 1024 * 1024


def _params(sem):
    return pltpu.CompilerParams(dimension_semantics=sem, vmem_limit_bytes=VMEM_LIMIT)


def _dot(a, b):
    return jnp.dot(a.astype(BF16), b.astype(BF16), preferred_element_type=F32)


def _dot_nt(a, b):
    return lax.dot_general(a.astype(BF16), b.astype(BF16), (((1,), (1,)), ((), ())),
                           preferred_element_type=F32)


def _dot_tn(a, b):
    return lax.dot_general(a.astype(BF16), b.astype(BF16), (((0,), (0,)), ((), ())),
                           preferred_element_type=F32)


def _split3(x):
    hi = x.astype(BF16)
    r1 = x - hi.astype(F32)
    mid = r1.astype(BF16)
    lo = (r1 - mid.astype(F32)).astype(BF16)
    return hi, mid, lo


def _dot_exact_lhs(m, x):
    hi, mid, lo = _split3(x)
    mb = m.astype(BF16)
    return (jnp.dot(mb, hi, preferred_element_type=F32) + jnp.dot(mb, mid, preferred_element_type=F32)
            + jnp.dot(mb, lo, preferred_element_type=F32))


def _sigmoid(x):
    return 1.0 / (1.0 + jnp.exp(-x))


def _head_ones():
    r = lax.broadcasted_iota(jnp.int32, (2 * LANES, LANES), 0) % LANES // HEAD
    c = lax.broadcasted_iota(jnp.int32, (2 * LANES, LANES), 1) // HEAD
    return (r == c).astype(BF16)


def _head_sum(x, ones):
    parts = []
    for s in range(0, x.shape[1], LANES):
        xs = x[:, s:s + LANES]
        hi = xs.astype(BF16)
        lo = (xs - hi.astype(F32)).astype(BF16)
        parts.append(jnp.dot(jnp.concatenate([hi, lo], axis=1), ones, preferred_element_type=F32))
    return parts[0] if len(parts) == 1 else jnp.concatenate(parts, axis=1)


def _rms(x, g):
    return x * lax.rsqrt(jnp.mean(x * x, axis=-1, keepdims=True) + RMS_EPS) * g


def _lo_mix(pl_, ql, mu_l, n_wa, n_gl):
    n = n_wa + n_gl
    lo = pl_[:, :n]
    lo = lo + (ql[:, :n] - lo) * mu_l[:, :n]
    wa = lo[:, :n_wa]
    return jnp.tanh(wa), wa, _sigmoid(lo[:, n_wa:])


def _run(steps):
    try:
        while True:
            next(steps)
    except StopIteration as done:
        return done.value


def _col_mix_steps(pr, pk, pv, qr, qk, qv, mu_r, mu_k, mu_v, tanh_wa, wa, sig_gl, w0, w_up, a0, a_up, g_up,
                   k_k, k_a):
    r = pr + (qr - pr) * mu_r
    yield
    k = pk + (qk - pk) * mu_k
    yield
    v = pv + (qv - pv) * mu_v
    yield
    lw = -DECAY_SCALE * _sigmoid(w0 + _dot(tanh_wa, w_up))
    yield
    a = _sigmoid(a0 + _dot(wa, a_up))
    yield
    gate = _dot(sig_gl, g_up)
    yield
    kk = k * k_k
    ss = _head_sum(kk * kk, _head_ones())
    yield
    kk = kk / jnp.maximum(jnp.sqrt(ss), 1e-12)
    yield
    return r, lw, k * (1.0 + (a - 1.0) * k_a), v, -kk, kk * a, gate


def _rwkv_mix(pr, pk, pv, pl_, qr, qk, qv, ql, mu_r, mu_k, mu_v, mu_l, w0, w_up, a0, a_up, g_up, k_k, k_a):
    lo = _lo_mix(pl_, ql, mu_l, w_up.shape[0], g_up.shape[0])
    return _run(_col_mix_steps(pr, pk, pv, qr, qk, qv, mu_r, mu_k, mu_v, *lo, w0, w_up, a0, a_up, g_up,
                               k_k, k_a))


def _rwkv_out_steps(o, r, k, v, gate, lnx_g, lnx_b, r_k):
    ones = _head_ones()
    mu = _head_sum(o, ones) * (1.0 / HEAD)
    yield
    oc = o - mu
    var = _head_sum(oc * oc, ones) * (1.0 / HEAD)
    yield
    y = oc * lax.rsqrt(var + GN_EPS) * lnx_g + lnx_b
    yield
    bonus = _head_sum(r * k * r_k, ones)
    yield
    return (y + bonus * v) * gate


def _rwkv_out(*args):
    return _run(_rwkv_out_steps(*args))


def _each(fn, *lists):
    out = []
    for args in zip(*lists):
        out.append(fn(*args))
        yield
    return out


def _run_together(work):
    values = [None] * len(work)
    longest = max(n for _, n in work)
    credit = [0.0] * len(work)
    live = set(range(len(work)))
    while live:
        for i, (steps, n) in enumerate(work):
            credit[i] += n / longest
            while i in live and credit[i] >= 1.0:
                credit[i] -= 1.0
                try:
                    next(steps)
                except StopIteration as done:
                    values[i] = done.value
                    live.discard(i)
    return values


def _shifted(p, carry_ref):
    rows = lax.broadcasted_iota(jnp.int32, p.shape, 0)
    q = jnp.where(rows == 0, carry_ref[...], pltpu.roll(p, 1, axis=0))
    carry_ref[...] = p[p.shape[0] - 1:, :]
    return q


def _prep_sample_kernel(*refs):
    ins, rows, cols = refs[:-10], refs[-10:-6], refs[-6:]
    r, lw, k, v, aa, bb, gate = _rwkv_mix(*[ref[...] for ref in ins])
    for ref, val in zip(rows, (r, k, v, gate)):
        ref[...] = val
    for ref, val in zip(cols, (r, lw, k, v, aa, bb)):
        ref[...] = val.T


class _Layout:
    def __init__(self, d_rwkv, d_sgu, lora_w, lora_a, lora_g, tn):
        self.d_rwkv, self.d_sgu, self.tn = d_rwkv, d_sgu, tn
        self.wa_w = lora_w + lora_a
        self.gl_w = -(-lora_g // LANES) * LANES
        self.d_shift = 3 * d_rwkv + self.wa_w + lora_g
        assert self.wa_w == LANES and self.wa_w + self.gl_w <= tn and d_rwkv % tn == 0

    def rw_pieces(self, a):
        d = self.d_rwkv
        pad = [(0, 0)] * (a.ndim - 1) + [(0, self.tn - (self.d_shift - 3 * d))]
        return a[..., :d], a[..., d:2 * d], a[..., 2 * d:3 * d], jnp.pad(a[..., 3 * d:], pad)


def _prep_weights(lay, mu, w0, w_up, a0, a_up, g_up, k_k, k_a):
    d = lay.d_rwkv
    lora_w, lora_g = w_up.shape[0], g_up.shape[0]
    mus = [m[None, :] for m in lay.rw_pieces(mu)]
    w_up_p = jnp.pad(w_up, ((0, lay.wa_w - lora_w), (0, 0)))
    a_up_p = jnp.pad(a_up, ((lora_w, 0), (0, 0)))
    g_up_p = jnp.pad(g_up, ((0, lay.gl_w - lora_g), (0, 0)))
    return mus + [w0[None, :], w_up_p, a0[None, :], a_up_p, g_up_p, k_k.reshape(1, d), k_a.reshape(1, d)]


def _full(a):
    return pl.BlockSpec(a.shape, lambda *_: (0,) * a.ndim)


def _prep_sample(p, prev, lay, weights):
    d = lay.d_rwkv
    m = p.shape[0]
    p_specs = [pl.BlockSpec((m, d), lambda i: (0, 0)),
               pl.BlockSpec((m, d), lambda i: (0, 1)),
               pl.BlockSpec((m, d), lambda i: (0, 2)),
               pl.BlockSpec((m, lay.tn), lambda i: (0, 3 * d // lay.tn))]
    prevs = list(lay.rw_pieces(prev))
    row_spec = pl.BlockSpec((m, d), lambda i: (0, 0))
    col_spec = pl.BlockSpec((d, m), lambda i: (0, 0))
    outs = pl.pallas_call(
        _prep_sample_kernel,
        grid=(1,),
        in_specs=p_specs + [_full(q) for q in prevs] + [_full(w) for w in weights],
        out_specs=[row_spec] * 4 + [col_spec] * 6,
        out_shape=[jax.ShapeDtypeStruct((m, d), F32)] * 4 + [jax.ShapeDtypeStruct((d, m), F32)] * 6,
        compiler_params=_params(("arbitrary",)),
        name="rwkv_prep_sample",
    )(p, p, p, p, *prevs, *weights)
    return outs[:4], outs[4:]


def _head_stack(x):
    head = lax.broadcasted_iota(jnp.int32, x.shape, 1) % GROUP // HEAD
    return jnp.concatenate([jnp.where(head == h, x, 0.0) for h in range(GROUP_HEADS)], axis=0)


def _wkv_pre_steps(r, lw, k, v, a, b, tri, gram_mask):
    c = WKV_CHUNK
    n_groups = r.shape[1] // GROUP

    def cut(x):
        return [x[:, g * GROUP:(g + 1) * GROUP] for g in range(n_groups)]

    cum = _dot_exact_lhs(tri, lw)
    yield
    e_out = jnp.exp(-cum)
    a_s = cut(a * jnp.exp(cum - lw))
    r_s = cut(r * jnp.exp(cum))
    yield
    b_s = cut(b * e_out)
    k_s = cut(k * e_out)
    yield
    last = cum[c - 1:, :]
    e_end = jnp.exp(last - cum)
    bk_e = cut(jnp.concatenate([b * e_end, k * e_end], axis=0))
    decay = cut(jnp.exp(last))
    vs = cut(v)
    yield

    grams = yield from _each(lambda ai, ri, bi, ki: jnp.where(
        gram_mask, _dot_nt(jnp.concatenate([ai, ri], axis=0),
                           jnp.concatenate([_head_stack(bi), _head_stack(ki)], axis=0)), 0.0),
        a_s, r_s, b_s, k_s)
    v_st = [_head_stack(x) for x in vs]
    kvs = yield from _each(lambda g, vi: _dot(g[:, GROUP:], vi), grams, v_st)
    xs = [jnp.concatenate([ai, kv[:c]], axis=1) for ai, kv in zip(a_s, kvs)]
    pws = [g[:c, :GROUP] for g in grams]
    ns = pws
    pws = yield from _each(lambda pw: _dot(pw, _head_stack(pw)), pws)
    n = 2
    while n < c // 2:
        both = yield from _each(
            lambda pw, nn: _dot(jnp.concatenate([pw, nn], axis=0), _head_stack(pw)), pws, ns)
        ns = [nn + pw + bo[c:] for nn, pw, bo in zip(ns, pws, both)]
        pws = [bo[:c] for bo in both]
        n *= 2
    ns = yield from _each(lambda nn, pw: nn + pw + _dot(nn, _head_stack(pw)), ns, pws)
    xs = yield from _each(lambda x, nn: x + _dot(nn, _head_stack(x)), xs, ns)
    qos = yield from _each(lambda g, x: _dot(g[c:, :GROUP], _head_stack(x)), grams, xs)
    qp = [jnp.concatenate([ri + qo[:, :GROUP], x[:, :GROUP]], axis=0) for ri, qo, x in zip(r_s, qos, xs)]
    o2 = [qo[:, GROUP:] + kv[c:] for qo, kv in zip(qos, kvs)]
    return dict(qp=qp, o2=o2, u2=[x[:, GROUP:] for x in xs], v=vs, bk_e=bk_e, decay=decay)


def _wkv_state_steps(pre, states, state_mask):
    c = WKV_CHUNK
    ous = yield from _each(_dot_nt, pre["qp"], states)
    upds = yield from _each(lambda ou, u2, v, bk: _dot_tn(jnp.concatenate([ou[c:] + u2, v], axis=0), bk),
                            ous, pre["u2"], pre["v"], pre["bk_e"])
    for p, (upd, decay) in enumerate(zip(upds, pre["decay"])):
        states[p] = states[p] * decay + jnp.where(state_mask, upd, 0.0)
    return [ou[:c] + o2 for ou, o2 in zip(ous, pre["o2"])]


def _mixer_prompt_kernel(*refs):
    x_ref, xs_ref, g1_ref, w_ref = refs[:4]
    mix_refs, out_refs, sgu_refs = refs[4:15], refs[15:18], refs[18:22]
    y_ref, yb_ref, sf_ref, sh_ref, psr_ref, pss_ref = refs[22:28]
    carries, s_ref, o_scr, p_scr, sg_scr = refs[28:32], refs[32], refs[33], refs[34], refs[35]
    c = WKV_CHUNK
    tb = y_ref.shape[0]
    d = y_ref.shape[1]
    n_groups = d // GROUP
    n_rw, n_sg = p_scr.shape[1], sg_scr.shape[1]
    sg_row0 = w_ref.shape[0] - n_sg
    t = pl.program_id(1)
    first = t == 0
    last = t == pl.num_programs(1) - 1

    @pl.when(first)
    def _():
        s_ref[...] = jnp.zeros_like(s_ref)
        for carry in carries:
            carry[...] = jnp.zeros_like(carry)

    @pl.when(first & (pl.program_id(0) == 0))
    def _():
        hs = _rms(xs_ref[...], g1_ref[...]).astype(BF16)
        for j in range(0, n_rw, PROJ_COLS):
            psr_ref[:, j:j + PROJ_COLS] = _dot_nt(hs, w_ref[j:j + PROJ_COLS, :])
        for j in range(0, n_sg, PROJ_COLS):
            pss_ref[:, j:j + PROJ_COLS] = _dot_nt(hs, w_ref[sg_row0 + j:sg_row0 + j + PROJ_COLS, :])

    h = _rms(x_ref[...], g1_ref[...]).astype(BF16)
    for j in range(0, n_rw, PROJ_COLS):
        p_scr[:, j:j + PROJ_COLS] = _dot_nt(h, w_ref[j:j + PROJ_COLS, :])
    p_refs = [p_scr.at[:, j * d:(j + 1) * d] for j in range(3)] + [p_scr.at[:, 3 * d:]]

    def sg_proj_steps():
        for j in range(0, n_sg, PROJ_COLS):
            sg_scr[:, j:j + PROJ_COLS] = _dot_nt(h, w_ref[sg_row0 + j:sg_row0 + j + PROJ_COLS, :])
            yield

    sg_ng, sg_nb, sg_w, sg_bias = sgu_refs
    sri = lax.broadcasted_iota(jnp.int32, (SGU_CHUNK, SGU_CHUNK), 0)
    sci = lax.broadcasted_iota(jnp.int32, (SGU_CHUNK, SGU_CHUNK), 1)
    sg_ws = [jnp.where(sci <= sri, sg_w[g], 0.0).astype(BF16) for g in range(sg_w.shape[0])]

    def sgu_steps(rows):
        u = _gelu(sg_scr[rows, :d])
        yield
        vs = _layernorm(_gelu(sg_scr[rows, d:]), sg_ng[...], sg_nb[...]).astype(BF16)
        yield
        for g, w in enumerate(sg_ws):
            cols = slice(g * SGU_CHUNK, (g + 1) * SGU_CHUNK)
            mix = jnp.dot(w, vs[:, cols], preferred_element_type=F32) + sg_bias[:, cols]
            yb_ref[rows, cols] = (u[:, cols] * mix).astype(yb_ref.dtype)
            yield

    mu_r, mu_k, mu_v, mu_l, w0, w_up, a0, a_up, g_up, k_k, k_a = mix_refs
    ri = lax.broadcasted_iota(jnp.int32, (c, c), 0)
    ci = lax.broadcasted_iota(jnp.int32, (c, c), 1)
    tri = (ci <= ri).astype(F32)
    gr = lax.broadcasted_iota(jnp.int32, (2 * c, 2 * GROUP), 0)
    gc = lax.broadcasted_iota(jnp.int32, (2 * c, 2 * GROUP), 1) % c
    gram_mask = gc <= jnp.where(gr < c, gr - 1, gr - c)
    sr = lax.broadcasted_iota(jnp.int32, (GROUP, GROUP), 0) // HEAD
    sc = lax.broadcasted_iota(jnp.int32, (GROUP, GROUP), 1) // HEAD
    state_mask = sr == sc
    states = [s_ref[g] for g in range(n_groups)]

    def prep_steps(rows):
        pl_ = p_refs[3][rows, :]
        lo = _lo_mix(pl_, _shifted(pl_, carries[3]), mu_l[...], w_up.shape[0], g_up.shape[0])
        yield
        ps = [ref[rows, :] for ref in p_refs[:3]]
        yield
        qs = [_shifted(p, carry) for p, carry in zip(ps, carries[:3])]
        yield
        vals = yield from _col_mix_steps(*ps, *qs, mu_r[...], mu_k[...], mu_v[...], *lo, w0[...], w_up[...],
                                         a0[...], a_up[...], g_up[...], k_k[...], k_a[...])
        return vals

    def state_steps(rows, pre):
        outs = yield from _wkv_state_steps(pre, states, state_mask)
        o_scr[rows, :] = jnp.concatenate(outs, axis=1)

    def post_steps(rows, vals):
        r, _, k, v, _, _, gate = vals
        y = yield from _rwkv_out_steps(o_scr[rows, :], r, k, v, gate, *[ref[...] for ref in out_refs])
        y_ref[rows, :] = y.astype(y_ref.dtype)

    n_chunks = tb // c
    rows = [slice(j * c, (j + 1) * c) for j in range(n_chunks)]
    n_stages = n_chunks + 3
    sgu_at = {min(2 + 2 * j, n_stages - 1): j for j in range(tb // SGU_CHUNK)}
    vals, pre = {}, {}
    for s in range(n_stages):
        work = []
        if s == 0:
            work.append(("sg_proj", 0, sg_proj_steps(), n_sg // PROJ_COLS))
        if s in sgu_at:
            j = sgu_at[s]
            work.append(("sgu", j, sgu_steps(slice(j * SGU_CHUNK, (j + 1) * SGU_CHUNK)), 2 + len(sg_ws)))
        if s < n_chunks:
            work.append(("prep", s, prep_steps(rows[s]), 12))
        if 0 <= s - 1 < n_chunks:
            work.append(("pre", s - 1, _wkv_pre_steps(*vals[s - 1][:6], tri, gram_mask), 10 * n_groups + 5))
        if 0 <= s - 2 < n_chunks:
            work.append(("state", s - 2, state_steps(rows[s - 2], pre[s - 2]), 2 * n_groups + 1))
        if 0 <= s - 3 < n_chunks:
            work.append(("post", s - 3, post_steps(rows[s - 3], vals[s - 3]), 5))
        done = _run_together([(steps, n) for _, _, steps, n in work])
        for (kind, j, _, _), value in zip(work, done):
            if kind == "prep":
                vals[j] = value
            elif kind == "pre":
                pre[j] = value
    for g, s in enumerate(states):
        s_ref[g] = s

    @pl.when(last)
    def _():
        sh_ref[0] = p_scr[tb - 1:, :]
        for g in range(n_groups):
            for h in range(GROUP_HEADS):
                sf_ref[0, g * GROUP_HEADS + h] = s_ref[g, h * HEAD:(h + 1) * HEAD, h * HEAD:(h + 1) * HEAD]


def _resident(a):
    return pl.BlockSpec(a.shape, lambda *_: (0,) * a.ndim, pipeline_mode=pl.Buffered(1))


def _mixer_prompt(x, xs, norm_g, wt, lay, mix_weights, out_weights, sgu_weights, batch, seq, tb):
    d = lay.d_rwkv
    d_model = x.shape[1]
    ms = xs.shape[0]
    n_rw, n_sg = 3 * d + lay.tn, 2 * lay.d_sgu
    assert lay.d_sgu == d and lay.tn % PROJ_COLS == 0 and wt.shape[0] == lay.d_shift + n_sg
    n_heads = d // HEAD
    nt = seq // tb
    row = lambda b, i: (b * nt + i, 0)
    weights = list(mix_weights) + list(out_weights) + list(sgu_weights)
    y_spec = pl.BlockSpec((tb, d), row)
    return pl.pallas_call(
        _mixer_prompt_kernel,
        grid=(batch, nt),
        in_specs=([pl.BlockSpec((tb, d_model), row), _resident(xs), _full(norm_g), _resident(wt)]
                  + [_full(w) for w in weights]),
        out_specs=[y_spec, y_spec,
                   pl.BlockSpec((1, n_heads, HEAD, HEAD), lambda b, i: (b, 0, 0, 0)),
                   pl.BlockSpec((1, 1, n_rw), lambda b, i: (b, 0, 0)),
                   pl.BlockSpec((ms, n_rw), lambda b, i: (0, 0), pipeline_mode=pl.Buffered(1)),
                   pl.BlockSpec((ms, n_sg), lambda b, i: (0, 0), pipeline_mode=pl.Buffered(1))],
        out_shape=[jax.ShapeDtypeStruct((batch * seq, d), BF16),
                   jax.ShapeDtypeStruct((batch * seq, d), BF16),
                   jax.ShapeDtypeStruct((batch, n_heads, HEAD, HEAD), F32),
                   jax.ShapeDtypeStruct((batch, 1, n_rw), F32),
                   jax.ShapeDtypeStruct((ms, n_rw), F32),
                   jax.ShapeDtypeStruct((ms, n_sg), F32)],
        scratch_shapes=([pltpu.VMEM((1, d), F32)] * 3
                        + [pltpu.VMEM((1, lay.tn), F32), pltpu.VMEM((d // GROUP, GROUP, GROUP), F32),
                           pltpu.VMEM((tb, d), F32), pltpu.VMEM((tb, n_rw), F32),
                           pltpu.VMEM((tb, n_sg), F32)]),
        compiler_params=_params(("arbitrary", "arbitrary")),
        name="mixer_prompt",
    )(x, xs, norm_g, wt, *weights)


STEP_HEADS = 2
STEP_UNROLL = 8


def _wkv_step_kernel(r_ref, lw_ref, k_ref, v_ref, a_ref, b_ref, s_ref, o_ref, sn_ref):
    for h in range(s_ref.shape[0]):
        feat = slice(h * HEAD, (h + 1) * HEAD)
        a, b, k, r = a_ref[feat, :], b_ref[feat, :], k_ref[feat, :], r_ref[feat, :]
        w = jnp.exp(lw_ref[feat, :])

        def body(j, carry, h=h, a=a, b=b, k=k, r=r, w=w):
            for u in range(STEP_UNROLL):
                i = j * STEP_UNROLL + u
                s = s_ref[h, i]
                sa = jnp.sum(s * a, axis=0, keepdims=True)
                s = s * w + sa * b + v_ref[pl.ds(h * HEAD + i, 1), :] * k
                sn_ref[h, i] = s
                o_ref[pl.ds(h * HEAD + i, 1), :] = jnp.sum(s * r, axis=0, keepdims=True)
            return carry

        lax.fori_loop(0, HEAD // STEP_UNROLL, body, 0)


def _wkv_step(cols, state):
    d, m = cols[0].shape
    n_heads = d // HEAD
    vec = pl.BlockSpec((STEP_HEADS * HEAD, m), lambda h: (h, 0))
    st = pl.BlockSpec((STEP_HEADS, HEAD, HEAD, m), lambda h: (h, 0, 0, 0))
    o, s = pl.pallas_call(
        _wkv_step_kernel,
        grid=(n_heads // STEP_HEADS,),
        in_specs=[vec] * 6 + [st],
        out_specs=[vec, st],
        out_shape=[jax.ShapeDtypeStruct((d, m), F32), jax.ShapeDtypeStruct((n_heads, HEAD, HEAD, m), F32)],
        compiler_params=_params(("parallel",)),
        name="wkv_step",
    )(*cols, jnp.transpose(state, (1, 2, 3, 0)))
    return o, jnp.transpose(s, (3, 0, 1, 2))


def _post_kernel(ot_ref, *refs):
    y_ref = refs[-1]
    y_ref[...] = _rwkv_out(ot_ref[...].T, *[ref[...] for ref in refs[:-1]]).astype(y_ref.dtype)


def _rwkv_post(o_t, r, k, v, g, out_weights):
    m, d = r.shape
    spec = pl.BlockSpec((m, d), lambda i: (0, 0))
    return pl.pallas_call(
        _post_kernel,
        grid=(1,),
        in_specs=[pl.BlockSpec((d, m), lambda i: (0, 0))] + [spec] * 4 + [_full(w) for w in out_weights],
        out_specs=spec,
        out_shape=jax.ShapeDtypeStruct((m, d), BF16),
        compiler_params=_params(("arbitrary",)),
        name="rwkv_post",
    )(o_t, r, k, v, g, *out_weights)


def _gelu(x):
    return 0.5 * x * (1.0 + jnp.tanh(GELU_C * (x + 0.044715 * (x * x * x))))


def _layernorm(x, g, b):
    mu = jnp.mean(x, axis=-1, keepdims=True)
    xc = x - mu
    var = jnp.mean(xc * xc, axis=-1, keepdims=True)
    return xc * lax.rsqrt(var + LN_EPS) * g + b


def _sgu_sample_kernel(pu_ref, pv_ref, ng_ref, nb_ref, w_ref, bias_ref, y_ref, vs_ref):
    u = _gelu(pu_ref[...])
    vs = _layernorm(_gelu(pv_ref[...]), ng_ref[...], nb_ref[...])
    vs_ref[...] = vs
    y_ref[...] = (u * (w_ref[...] * vs + bias_ref[...])).astype(y_ref.dtype)


def _sgu_sample(p, lay, norm_g, norm_b, sgu_w, sgu_b):
    d = lay.d_sgu
    m = p.shape[0]
    w0 = jnp.repeat(sgu_w[:, 0, 0], SGU_CHUNK)[None, :]
    b0 = jnp.repeat(sgu_b[:, 0], SGU_CHUNK)[None, :]
    vec = pl.BlockSpec((1, d), lambda i: (0, 0))
    out = pl.BlockSpec((m, d), lambda i: (0, 0))
    return pl.pallas_call(
        _sgu_sample_kernel,
        grid=(1,),
        in_specs=[pl.BlockSpec((m, d), lambda i: (0, 0)),
                  pl.BlockSpec((m, d), lambda i: (0, 1)), vec, vec, vec, vec],
        out_specs=[out, out],
        out_shape=[jax.ShapeDtypeStruct((m, d), BF16), jax.ShapeDtypeStruct((m, d), F32)],
        compiler_params=_params(("arbitrary",)),
        name="sgu_sample",
    )(p, p, norm_g[None, :], norm_b[None, :], w0, b0)


def _out_proj_kernel(x_ref, ya_ref, yb_ref, w_ref, o_ref):
    da = ya_ref.shape[1]
    o_ref[...] = (x_ref[...] + jnp.dot(ya_ref[...], w_ref[:da, :].astype(BF16), preferred_element_type=F32)
                  + jnp.dot(yb_ref[...], w_ref[da:, :].astype(BF16), preferred_element_type=F32))


def _out_proj(x, ya, yb, w, tm):
    m, d = x.shape
    da = ya.shape[1]
    return pl.pallas_call(
        _out_proj_kernel,
        grid=(m // tm,),
        in_specs=[pl.BlockSpec((tm, d), lambda i: (i, 0)),
                  pl.BlockSpec((tm, da), lambda i: (i, 0)),
                  pl.BlockSpec((tm, da), lambda i: (i, 0)),
                  _resident(w)],
        out_specs=pl.BlockSpec((tm, d), lambda i: (i, 0)),
        out_shape=jax.ShapeDtypeStruct((m, d), F32),
        compiler_params=_params(("parallel",)),
        name="out_proj",
    )(x, ya, yb, w)


def _ffn_kernel(x_ref, xs_ref, g2_ref, wu_ref, wd_ref, gf_ref, o_ref, os_ref, h_ref, a_ref):
    f = pl.program_id(1)
    last = pl.num_programs(1) - 1
    tm = x_ref.shape[0]

    def up():
        a = jnp.dot(h_ref[...], wu_ref[...].astype(BF16), preferred_element_type=F32)
        return jnp.square(jnp.maximum(a, 0.0)).astype(BF16)

    def down():
        return jnp.dot(a_ref[...], wd_ref[...].astype(BF16), preferred_element_type=F32)

    @pl.when(f == 0)
    def _():
        x, xs = x_ref[...], xs_ref[...]
        h_ref[:tm, :] = _rms(x, g2_ref[...]).astype(BF16)
        h_ref[tm:, :] = _rms(xs, g2_ref[...]).astype(BF16)
        o_ref[...] = x
        os_ref[...] = xs
        a_ref[...] = up()

    @pl.when((f > 0) & (f < last))
    def _():
        acc = down()
        a_new = up()
        o_ref[...] += acc[:tm]
        os_ref[...] += acc[tm:]
        a_ref[...] = a_new

    @pl.when(f == last)
    def _():
        acc = down()
        o_ref[...] = _rms(o_ref[...] + acc[:tm], gf_ref[...])
        os_ref[...] = _rms(os_ref[...] + acc[tm:], gf_ref[...])


def _ffn(x, xs, g2, w_up, w_down, gf, tm, tf):
    m, d = x.shape
    n_blocks = m // tm
    ts = xs.shape[0] // n_blocks
    assert ts * n_blocks == xs.shape[0] and ts % 8 == 0
    nf = w_up.shape[1] // tf
    return pl.pallas_call(
        _ffn_kernel,
        grid=(n_blocks, nf + 1),
        in_specs=[pl.BlockSpec((tm, d), lambda i, f: (i, 0)),
                  pl.BlockSpec((ts, d), lambda i, f: (i, 0)),
                  pl.BlockSpec((1, d), lambda i, f: (0, 0)),
                  pl.BlockSpec((d, tf), lambda i, f: (0, jnp.minimum(f, nf - 1))),
                  pl.BlockSpec((tf, d), lambda i, f: (jnp.maximum(f - 1, 0), 0)),
                  pl.BlockSpec((1, d), lambda i, f: (0, 0))],
        out_specs=[pl.BlockSpec((tm, d), lambda i, f: (i, 0)),
                   pl.BlockSpec((ts, d), lambda i, f: (i, 0))],
        out_shape=[jax.ShapeDtypeStruct((m, d), F32), jax.ShapeDtypeStruct(xs.shape, F32)],
        scratch_shapes=[pltpu.VMEM((tm + ts, d), BF16), pltpu.VMEM((tm + ts, tf), BF16)],
        compiler_params=_params(("parallel", "arbitrary")),
        name="ffn",
    )(x, xs, g2[None, :], w_up, w_down, gf[None, :])


def _row_tile(m, cap):
    t = min(m, cap)
    assert m % t == 0
    return t


def kernel(x_prompt, x_sample, state_wkv, state_shift, norm1_g, w_in, mu_shift, w0, w_up, a0, a_up, g_up,
           k_k, k_a, r_k, lnx_g, lnx_b, sgu_norm_g, sgu_norm_b, sgu_w, sgu_b, w_out, norm2_g, w_ffn_up,
           w_ffn_down, norm_f_g):
    batch, seq, d_model = x_prompt.shape
    n_dec, dec_seq, _ = x_sample.shape
    depth = w_in.shape[0]
    assert depth == 1 and dec_seq == 1
    d_rwkv = w0.shape[1]
    d_sgu = sgu_norm_g.shape[1]
    lay = _Layout(d_rwkv, d_sgu, w_up.shape[1], a_up.shape[1], g_up.shape[1], PROJ_COLS)
    w_in_t = w_in[0].T.astype(BF16)
    prep_w = _prep_weights(lay, mu_shift[0], w0[0], w_up[0], a0[0], a_up[0], g_up[0], k_k[0], k_a[0])

    out_w = [w.reshape(1, d_rwkv) for w in (lnx_g[0], lnx_b[0], r_k[0])]

    xp = x_prompt.reshape(batch * seq, d_model)
    xs = x_sample.reshape(n_dec, d_model)
    sgu_bias = jnp.repeat(sgu_b[0].T, SGU_CHUNK, axis=1)
    ya, yb, wkv_p, last_p, ps_rw, ps_sg = _mixer_prompt(
        xp, xs, norm1_g, w_in_t, lay, prep_w, out_w, [sgu_norm_g, sgu_norm_b, sgu_w[0], sgu_bias],
        batch, seq, _row_tile(seq, MIXER_ROWS))
    x1p = _out_proj(xp, ya, yb, w_out[0], _row_tile(batch * seq, OUT_PROJ_ROWS))
    shift_p = last_p[:, 0, :lay.d_shift]

    (r, k, v, g), step_cols = _prep_sample(ps_rw, state_shift[0], lay, prep_w)
    o_t, wkv_s = _wkv_step(step_cols, state_wkv[0])
    yb, vs = _sgu_sample(ps_sg, lay, sgu_norm_g[0], sgu_norm_b[0], sgu_w[0], sgu_b[0])
    ya = _rwkv_post(o_t, r, k, v, g, out_w)
    x1s = _out_proj(xs, ya, yb, w_out[0], n_dec)
    shift_s = ps_rw[:, :lay.d_shift]

    y_prompt, y_sample = _ffn(x1p, x1s, norm2_g[0], w_ffn_up[0], w_ffn_down[0], norm_f_g,
                              _row_tile(batch * seq, FFN_ROWS), FFN_HIDDEN)
    y_prompt = y_prompt.reshape(batch, seq, d_model)
    y_sample = y_sample.reshape(n_dec, 1, d_model)
    return (y_prompt, y_sample, wkv_p[None], shift_p[None], wkv_s[None], shift_s[None],
            vs.reshape(1, n_dec, 1, d_sgu))
```

```python
import math

import jax
import jax.numpy as jnp
from jax import lax
from jax.experimental import pallas as pl
from jax.experimental.pallas import tpu as pltpu

F32 = jnp.float32
BF16 = jnp.bfloat16

HEAD = 64
LANES = 128
GROUP_HEADS = 2
GROUP = GROUP_HEADS * HEAD
SGU_CHUNK = 128
WKV_CHUNK = 64
PROJ_COLS = 512
MIXER_ROWS = 512
OUT_PROJ_ROWS = 512
FFN_ROWS = 1024
FFN_HIDDEN = 512
RMS_EPS = 1e-5
LN_EPS = 1e-5
GN_EPS = 64e-5
DECAY_SCALE = math.exp(-0.5)
GELU_C = math.sqrt(2.0 / math.pi)
V7X_VMEM_BYTES = 64 * 1024 * 1024
VMEM_LIMIT = V7X_VMEM_BYTES - 2 * 1024 * 1024


def _params(sem):
    return pltpu.CompilerParams(dimension_semantics=sem, vmem_limit_bytes=VMEM_LIMIT)


def _dot(a, b):
    return jnp.dot(a.astype(BF16), b.astype(BF16), preferred_element_type=F32)


def _dot_nt(a, b):
    return lax.dot_general(a.astype(BF16), b.astype(BF16), (((1,), (1,)), ((), ())),
                           preferred_element_type=F32)


def _dot_tn(a, b):
    return lax.dot_general(a.astype(BF16), b.astype(BF16), (((0,), (0,)), ((), ())),
                           preferred_element_type=F32)


def _split3(x):
    hi = x.astype(BF16)
    r1 = x - hi.astype(F32)
    mid = r1.astype(BF16)
    lo = (r1 - mid.astype(F32)).astype(BF16)
    return hi, mid, lo


def _dot_exact_lhs(m, x):
    hi, mid, lo = _split3(x)
    mb = m.astype(BF16)
    return (jnp.dot(mb, hi, preferred_element_type=F32) + jnp.dot(mb, mid, preferred_element_type=F32)
            + jnp.dot(mb, lo, preferred_element_type=F32))


def _sigmoid(x):
    return 1.0 / (1.0 + jnp.exp(-x))


def _head_ones():
    r = lax.broadcasted_iota(jnp.int32, (2 * LANES, LANES), 0) % LANES // HEAD
    c = lax.broadcasted_iota(jnp.int32, (2 * LANES, LANES), 1) // HEAD
    return (r == c).astype(BF16)


def _head_sum(x, ones):
    parts = []
    for s in range(0, x.shape[1], LANES):
        xs = x[:, s:s + LANES]
        hi = xs.astype(BF16)
        lo = (xs - hi.astype(F32)).astype(BF16)
        parts.append(jnp.dot(jnp.concatenate([hi, lo], axis=1), ones, preferred_element_type=F32))
    return parts[0] if len(parts) == 1 else jnp.concatenate(parts, axis=1)


def _rms(x, g):
    return x * lax.rsqrt(jnp.mean(x * x, axis=-1, keepdims=True) + RMS_EPS) * g


def _lo_mix(pl_, ql, mu_l, n_wa, n_gl):
    n = n_wa + n_gl
    lo = pl_[:, :n]
    lo = lo + (ql[:, :n] - lo) * mu_l[:, :n]
    wa = lo[:, :n_wa]
    return jnp.tanh(wa), wa, _sigmoid(lo[:, n_wa:])


def _run(steps):
    try:
        while True:
            next(steps)
    except StopIteration as done:
        return done.value


def _col_mix_steps(pr, pk, pv, qr, qk, qv, mu_r, mu_k, mu_v, tanh_wa, wa, sig_gl, w0, w_up, a0, a_up, g_up,
                   k_k, k_a):
    r = pr + (qr - pr) * mu_r
    yield
    k = pk + (qk - pk) * mu_k
    yield
    v = pv + (qv - pv) * mu_v
    yield
    lw = -DECAY_SCALE * _sigmoid(w0 + _dot(tanh_wa, w_up))
    yield
    a = _sigmoid(a0 + _dot(wa, a_up))
    yield
    gate = _dot(sig_gl, g_up)
    yield
    kk = k * k_k
    ss = _head_sum(kk * kk, _head_ones())
    yield
    kk = kk / jnp.maximum(jnp.sqrt(ss), 1e-12)
    yield
    return r, lw, k * (1.0 + (a - 1.0) * k_a), v, -kk, kk * a, gate


def _rwkv_mix(pr, pk, pv, pl_, qr, qk, qv, ql, mu_r, mu_k, mu_v, mu_l, w0, w_up, a0, a_up, g_up, k_k, k_a):
    lo = _lo_mix(pl_, ql, mu_l, w_up.shape[0], g_up.shape[0])
    return _run(_col_mix_steps(pr, pk, pv, qr, qk, qv, mu_r, mu_k, mu_v, *lo, w0, w_up, a0, a_up, g_up,
                               k_k, k_a))


def _rwkv_out_steps(o, r, k, v, gate, lnx_g, lnx_b, r_k):
    ones = _head_ones()
    mu = _head_sum(o, ones) * (1.0 / HEAD)
    yield
    oc = o - mu
    var = _head_sum(oc * oc, ones) * (1.0 / HEAD)
    yield
    y = oc * lax.rsqrt(var + GN_EPS) * lnx_g + lnx_b
    yield
    bonus = _head_sum(r * k * r_k, ones)
    yield
    return (y + bonus * v) * gate


def _rwkv_out(*args):
    return _run(_rwkv_out_steps(*args))


def _each(fn, *lists):
    out = []
    for args in zip(*lists):
        out.append(fn(*args))
        yield
    return out


def _run_together(work):
    values = [None] * len(work)
    longest = max(n for _, n in work)
    credit = [0.0] * len(work)
    live = set(range(len(work)))
    while live:
        for i, (steps, n) in enumerate(work):
            credit[i] += n / longest
            while i in live and credit[i] >= 1.0:
                credit[i] -= 1.0
                try:
                    next(steps)
                except StopIteration as done:
                    values[i] = done.value
                    live.discard(i)
    return values


def _shifted(p, carry_ref):
    rows = lax.broadcasted_iota(jnp.int32, p.shape, 0)
    q = jnp.where(rows == 0, carry_ref[...], pltpu.roll(p, 1, axis=0))
    carry_ref[...] = p[p.shape[0] - 1:, :]
    return q


def _prep_sample_kernel(*refs):
    ins, rows, cols = refs[:-10], refs[-10:-6], refs[-6:]
    r, lw, k, v, aa, bb, gate = _rwkv_mix(*[ref[...] for ref in ins])
    for ref, val in zip(rows, (r, k, v, gate)):
        ref[...] = val
    for ref, val in zip(cols, (r, lw, k, v, aa, bb)):
        ref[...] = val.T


class _Layout:
    def __init__(self, d_rwkv, d_sgu, lora_w, lora_a, lora_g, tn):
        self.d_rwkv, self.d_sgu, self.tn = d_rwkv, d_sgu, tn
        self.wa_w = lora_w + lora_a
        self.gl_w = -(-lora_g // LANES) * LANES
        self.d_shift = 3 * d_rwkv + self.wa_w + lora_g
        assert self.wa_w == LANES and self.wa_w + self.gl_w <= tn and d_rwkv % tn == 0

    def rw_pieces(self, a):
        d = self.d_rwkv
        pad = [(0, 0)] * (a.ndim - 1) + [(0, self.tn - (self.d_shift - 3 * d))]
        return a[..., :d], a[..., d:2 * d], a[..., 2 * d:3 * d], jnp.pad(a[..., 3 * d:], pad)


def _prep_weights(lay, mu, w0, w_up, a0, a_up, g_up, k_k, k_a):
    d = lay.d_rwkv
    lora_w, lora_g = w_up.shape[0], g_up.shape[0]
    mus = [m[None, :] for m in lay.rw_pieces(mu)]
    w_up_p = jnp.pad(w_up, ((0, lay.wa_w - lora_w), (0, 0)))
    a_up_p = jnp.pad(a_up, ((lora_w, 0), (0, 0)))
    g_up_p = jnp.pad(g_up, ((0, lay.gl_w - lora_g), (0, 0)))
    return mus + [w0[None, :], w_up_p, a0[None, :], a_up_p, g_up_p, k_k.reshape(1, d), k_a.reshape(1, d)]


def _full(a):
    return pl.BlockSpec(a.shape, lambda *_: (0,) * a.ndim)


def _prep_sample(p, prev, lay, weights):
    d = lay.d_rwkv
    m = p.shape[0]
    p_specs = [pl.BlockSpec((m, d), lambda i: (0, 0)),
               pl.BlockSpec((m, d), lambda i: (0, 1)),
               pl.BlockSpec((m, d), lambda i: (0, 2)),
               pl.BlockSpec((m, lay.tn), lambda i: (0, 3 * d // lay.tn))]
    prevs = list(lay.rw_pieces(prev))
    row_spec = pl.BlockSpec((m, d), lambda i: (0, 0))
    col_spec = pl.BlockSpec((d, m), lambda i: (0, 0))
    outs = pl.pallas_call(
        _prep_sample_kernel,
        grid=(1,),
        in_specs=p_specs + [_full(q) for q in prevs] + [_full(w) for w in weights],
        out_specs=[row_spec] * 4 + [col_spec] * 6,
        out_shape=[jax.ShapeDtypeStruct((m, d), F32)] * 4 + [jax.ShapeDtypeStruct((d, m), F32)] * 6,
        compiler_params=_params(("arbitrary",)),
        name="rwkv_prep_sample",
    )(p, p, p, p, *prevs, *weights)
    return outs[:4], outs[4:]


def _head_stack(x):
    head = lax.broadcasted_iota(jnp.int32, x.shape, 1) % GROUP // HEAD
    return jnp.concatenate([jnp.where(head == h, x, 0.0) for h in range(GROUP_HEADS)], axis=0)


def _wkv_pre_steps(r, lw, k, v, a, b, tri, gram_mask):
    c = WKV_CHUNK
    n_groups = r.shape[1] // GROUP

    def cut(x):
        return [x[:, g * GROUP:(g + 1) * GROUP] for g in range(n_groups)]

    cum = _dot_exact_lhs(tri, lw)
    yield
    e_out = jnp.exp(-cum)
    a_s = cut(a * jnp.exp(cum - lw))
    r_s = cut(r * jnp.exp(cum))
    yield
    b_s = cut(b * e_out)
    k_s = cut(k * e_out)
    yield
    last = cum[c - 1:, :]
    e_end = jnp.exp(last - cum)
    bk_e = cut(jnp.concatenate([b * e_end, k * e_end], axis=0))
    decay = cut(jnp.exp(last))
    vs = cut(v)
    yield

    grams = yield from _each(lambda ai, ri, bi, ki: jnp.where(
        gram_mask, _dot_nt(jnp.concatenate([ai, ri], axis=0),
                           jnp.concatenate([_head_stack(bi), _head_stack(ki)], axis=0)), 0.0),
        a_s, r_s, b_s, k_s)
    v_st = [_head_stack(x) for x in vs]
    kvs = yield from _each(lambda g, vi: _dot(g[:, GROUP:], vi), grams, v_st)
    xs = [jnp.concatenate([ai, kv[:c]], axis=1) for ai, kv in zip(a_s, kvs)]
    pws = [g[:c, :GROUP] for g in grams]
    ns = pws
    pws = yield from _each(lambda pw: _dot(pw, _head_stack(pw)), pws)
    n = 2
    while n < c // 2:
        both = yield from _each(
            lambda pw, nn: _dot(jnp.concatenate([pw, nn], axis=0), _head_stack(pw)), pws, ns)
        ns = [nn + pw + bo[c:] for nn, pw, bo in zip(ns, pws, both)]
        pws = [bo[:c] for bo in both]
        n *= 2
    ns = yield from _each(lambda nn, pw: nn + pw + _dot(nn, _head_stack(pw)), ns, pws)
    xs = yield from _each(lambda x, nn: x + _dot(nn, _head_stack(x)), xs, ns)
    qos = yield from _each(lambda g, x: _dot(g[c:, :GROUP], _head_stack(x)), grams, xs)
    qp = [jnp.concatenate([ri + qo[:, :GROUP], x[:, :GROUP]], axis=0) for ri, qo, x in zip(r_s, qos, xs)]
    o2 = [qo[:, GROUP:] + kv[c:] for qo, kv in zip(qos, kvs)]
    return dict(qp=qp, o2=o2, u2=[x[:, GROUP:] for x in xs], v=vs, bk_e=bk_e, decay=decay)


def _wkv_state_steps(pre, states, state_mask):
    c = WKV_CHUNK
    ous = yield from _each(_dot_nt, pre["qp"], states)
    upds = yield from _each(lambda ou, u2, v, bk: _dot_tn(jnp.concatenate([ou[c:] + u2, v], axis=0), bk),
                            ous, pre["u2"], pre["v"], pre["bk_e"])
    for p, (upd, decay) in enumerate(zip(upds, pre["decay"])):
        states[p] = states[p] * decay + jnp.where(state_mask, upd, 0.0)
    return [ou[:c] + o2 for ou, o2 in zip(ous, pre["o2"])]


def _mixer_prompt_kernel(*refs):
    x_ref, xs_ref, g1_ref, w_ref = refs[:4]
    mix_refs, out_refs, sgu_refs = refs[4:15], refs[15:18], refs[18:22]
    y_ref, yb_ref, sf_ref, sh_ref, psr_ref, pss_ref = refs[22:28]
    carries, s_ref, o_scr, p_scr, sg_scr = refs[28:32], refs[32], refs[33], refs[34], refs[35]
    c = WKV_CHUNK
    tb = y_ref.shape[0]
    d = y_ref.shape[1]
    n_groups = d // GROUP
    n_rw, n_sg = p_scr.shape[1], sg_scr.shape[1]
    sg_row0 = w_ref.shape[0] - n_sg
    t = pl.program_id(1)
    first = t == 0
    last = t == pl.num_programs(1) - 1

    @pl.when(first)
    def _():
        s_ref[...] = jnp.zeros_like(s_ref)
        for carry in carries:
            carry[...] = jnp.zeros_like(carry)

    @pl.when(first & (pl.program_id(0) == 0))
    def _():
        hs = _rms(xs_ref[...], g1_ref[...]).astype(BF16)
        for j in range(0, n_rw, PROJ_COLS):
            psr_ref[:, j:j + PROJ_COLS] = _dot_nt(hs, w_ref[j:j + PROJ_COLS, :])
        for j in range(0, n_sg, PROJ_COLS):
            pss_ref[:, j:j + PROJ_COLS] = _dot_nt(hs, w_ref[sg_row0 + j:sg_row0 + j + PROJ_COLS, :])

    h = _rms(x_ref[...], g1_ref[...]).astype(BF16)
    for j in range(0, n_sg, PROJ_COLS):
        sg_scr[:, j:j + PROJ_COLS] = _dot_nt(h, w_ref[sg_row0 + j:sg_row0 + j + PROJ_COLS, :])
    p_refs = [p_scr.at[:, j * d:(j + 1) * d] for j in range(3)] + [p_scr.at[:, 3 * d:]]

    def rw_proj_steps():
        for j in range(0, n_rw, PROJ_COLS):
            p_scr[:, j:j + PROJ_COLS] = _dot_nt(h, w_ref[j:j + PROJ_COLS, :])
            yield

    sg_ng, sg_nb, sg_w, sg_bias = sgu_refs
    sri = lax.broadcasted_iota(jnp.int32, (SGU_CHUNK, SGU_CHUNK), 0)
    sci = lax.broadcasted_iota(jnp.int32, (SGU_CHUNK, SGU_CHUNK), 1)
    sg_ws = [jnp.where(sci <= sri, sg_w[g], 0.0).astype(BF16) for g in range(sg_w.shape[0])]

    def sgu_steps(rows):
        u = _gelu(sg_scr[rows, :d])
        yield
        vs = _layernorm(_gelu(sg_scr[rows, d:]), sg_ng[...], sg_nb[...]).astype(BF16)
        yield
        for g, w in enumerate(sg_ws):
            cols = slice(g * SGU_CHUNK, (g + 1) * SGU_CHUNK)
            mix = jnp.dot(w, vs[:, cols], preferred_element_type=F32) + sg_bias[:, cols]
            yb_ref[rows, cols] = (u[:, cols] * mix).astype(yb_ref.dtype)
            yield

    mu_r, mu_k, mu_v, mu_l, w0, w_up, a0, a_up, g_up, k_k, k_a = mix_refs
    ri = lax.broadcasted_iota(jnp.int32, (c, c), 0)
    ci = lax.broadcasted_iota(jnp.int32, (c, c), 1)
    tri = (ci <= ri).astype(F32)
    gr = lax.broadcasted_iota(jnp.int32, (2 * c, 2 * GROUP), 0)
    gc = lax.broadcasted_iota(jnp.int32, (2 * c, 2 * GROUP), 1) % c
    gram_mask = gc <= jnp.where(gr < c, gr - 1, gr - c)
    sr = lax.broadcasted_iota(jnp.int32, (GROUP, GROUP), 0) // HEAD
    sc = lax.broadcasted_iota(jnp.int32, (GROUP, GROUP), 1) // HEAD
    state_mask = sr == sc
    states = [s_ref[g] for g in range(n_groups)]

    def prep_steps(rows):
        pl_ = p_refs[3][rows, :]
        lo = _lo_mix(pl_, _shifted(pl_, carries[3]), mu_l[...], w_up.shape[0], g_up.shape[0])
        yield
        ps = [ref[rows, :] for ref in p_refs[:3]]
        yield
        qs = [_shifted(p, carry) for p, carry in zip(ps, carries[:3])]
        yield
        vals = yield from _col_mix_steps(*ps, *qs, mu_r[...], mu_k[...], mu_v[...], *lo, w0[...], w_up[...],
                                         a0[...], a_up[...], g_up[...], k_k[...], k_a[...])
        return vals

    def state_steps(rows, pre):
        outs = yield from _wkv_state_steps(pre, states, state_mask)
        o_scr[rows, :] = jnp.concatenate(outs, axis=1)

    def post_steps(rows, vals):
        r, _, k, v, _, _, gate = vals
        y = yield from _rwkv_out_steps(o_scr[rows, :], r, k, v, gate, *[ref[...] for ref in out_refs])
        y_ref[rows, :] = y.astype(y_ref.dtype)

    def all_sgu_steps():
        for j in range(tb // SGU_CHUNK):
            yield from sgu_steps(slice(j * SGU_CHUNK, (j + 1) * SGU_CHUNK))

    _run_together([(rw_proj_steps(), n_rw // PROJ_COLS),
                   (all_sgu_steps(), (tb // SGU_CHUNK) * (2 + len(sg_ws)))])
    n_chunks = tb // c
    rows = [slice(j * c, (j + 1) * c) for j in range(n_chunks)]
    n_stages = n_chunks + 3
    vals, pre = {}, {}
    for s in range(n_stages):
        work = []
        if s < n_chunks:
            work.append(("prep", s, prep_steps(rows[s]), 12))
        if 0 <= s - 1 < n_chunks:
            work.append(("pre", s - 1, _wkv_pre_steps(*vals[s - 1][:6], tri, gram_mask), 10 * n_groups + 5))
        if 0 <= s - 2 < n_chunks:
            work.append(("state", s - 2, state_steps(rows[s - 2], pre[s - 2]), 2 * n_groups + 1))
        if 0 <= s - 3 < n_chunks:
            work.append(("post", s - 3, post_steps(rows[s - 3], vals[s - 3]), 5))
        done = _run_together([(steps, n) for _, _, steps, n in work])
        for (kind, j, _, _), value in zip(work, done):
            if kind == "prep":
                vals[j] = value
            elif kind == "pre":
                pre[j] = value
    for g, s in enumerate(states):
        s_ref[g] = s

    @pl.when(last)
    def _():
        sh_ref[0] = p_scr[tb - 1:, :]
        for g in range(n_groups):
            for h in range(GROUP_HEADS):
                sf_ref[0, g * GROUP_HEADS + h] = s_ref[g, h * HEAD:(h + 1) * HEAD, h * HEAD:(h + 1) * HEAD]


def _resident(a):
    return pl.BlockSpec(a.shape, lambda *_: (0,) * a.ndim, pipeline_mode=pl.Buffered(1))


def _mixer_prompt(x, xs, norm_g, wt, lay, mix_weights, out_weights, sgu_weights, batch, seq, tb):
    d = lay.d_rwkv
    d_model = x.shape[1]
    ms = xs.shape[0]
    n_rw, n_sg = 3 * d + lay.tn, 2 * lay.d_sgu
    assert lay.d_sgu == d and lay.tn % PROJ_COLS == 0 and wt.shape[0] == lay.d_shift + n_sg
    n_heads = d // HEAD
    nt = seq // tb
    row = lambda b, i: (b * nt + i, 0)
    weights = list(mix_weights) + list(out_weights) + list(sgu_weights)
    y_spec = pl.BlockSpec((tb, d), row)
    return pl.pallas_call(
        _mixer_prompt_kernel,
        grid=(batch, nt),
        in_specs=([pl.BlockSpec((tb, d_model), row), _resident(xs), _full(norm_g), _resident(wt)]
                  + [_full(w) for w in weights]),
        out_specs=[y_spec, y_spec,
                   pl.BlockSpec((1, n_heads, HEAD, HEAD), lambda b, i: (b, 0, 0, 0)),
                   pl.BlockSpec((1, 1, n_rw), lambda b, i: (b, 0, 0)),
                   pl.BlockSpec((ms, n_rw), lambda b, i: (0, 0), pipeline_mode=pl.Buffered(1)),
                   pl.BlockSpec((ms, n_sg), lambda b, i: (0, 0), pipeline_mode=pl.Buffered(1))],
        out_shape=[jax.ShapeDtypeStruct((batch * seq, d), BF16),
                   jax.ShapeDtypeStruct((batch * seq, d), BF16),
                   jax.ShapeDtypeStruct((batch, n_heads, HEAD, HEAD), F32),
                   jax.ShapeDtypeStruct((batch, 1, n_rw), F32),
                   jax.ShapeDtypeStruct((ms, n_rw), F32),
                   jax.ShapeDtypeStruct((ms, n_sg), F32)],
        scratch_shapes=([pltpu.VMEM((1, d), F32)] * 3
                        + [pltpu.VMEM((1, lay.tn), F32), pltpu.VMEM((d // GROUP, GROUP, GROUP), F32),
                           pltpu.VMEM((tb, d), F32), pltpu.VMEM((tb, n_rw), F32),
                           pltpu.VMEM((tb, n_sg), F32)]),
        compiler_params=_params(("arbitrary", "arbitrary")),
        name="mixer_prompt",
    )(x, xs, norm_g, wt, *weights)


STEP_HEADS = 4
STEP_UNROLL = 8


def _wkv_step_kernel(r_ref, lw_ref, k_ref, v_ref, a_ref, b_ref, s_ref, o_ref, sn_ref):
    for h in range(s_ref.shape[0]):
        feat = slice(h * HEAD, (h + 1) * HEAD)
        a, b, k, r = a_ref[feat, :], b_ref[feat, :], k_ref[feat, :], r_ref[feat, :]
        w = jnp.exp(lw_ref[feat, :])

        def body(j, carry, h=h, a=a, b=b, k=k, r=r, w=w):
            for u in range(STEP_UNROLL):
                i = j * STEP_UNROLL + u
                s = s_ref[h, i]
                sa = jnp.sum(s * a, axis=0, keepdims=True)
                s = s * w + sa * b + v_ref[pl.ds(h * HEAD + i, 1), :] * k
                sn_ref[h, i] = s
                o_ref[pl.ds(h * HEAD + i, 1), :] = jnp.sum(s * r, axis=0, keepdims=True)
            return carry

        lax.fori_loop(0, HEAD // STEP_UNROLL, body, 0)


def _wkv_step(cols, state):
    d, m = cols[0].shape
    n_heads = d // HEAD
    vec = pl.BlockSpec((STEP_HEADS * HEAD, m), lambda h: (h, 0))
    st = pl.BlockSpec((STEP_HEADS, HEAD, HEAD, m), lambda h: (h, 0, 0, 0))
    o, s = pl.pallas_call(
        _wkv_step_kernel,
        grid=(n_heads // STEP_HEADS,),
        in_specs=[vec] * 6 + [st],
        out_specs=[vec, st],
        out_shape=[jax.ShapeDtypeStruct((d, m), F32), jax.ShapeDtypeStruct((n_heads, HEAD, HEAD, m), F32)],
        compiler_params=_params(("parallel",)),
        name="wkv_step",
    )(*cols, jnp.transpose(state, (1, 2, 3, 0)))
    return o, jnp.transpose(s, (3, 0, 1, 2))


def _post_kernel(ot_ref, *refs):
    y_ref = refs[-1]
    y_ref[...] = _rwkv_out(ot_ref[...].T, *[ref[...] for ref in refs[:-1]]).astype(y_ref.dtype)


def _rwkv_post(o_t, r, k, v, g, out_weights):
    m, d = r.shape
    spec = pl.BlockSpec((m, d), lambda i: (0, 0))
    return pl.pallas_call(
        _post_kernel,
        grid=(1,),
        in_specs=[pl.BlockSpec((d, m), lambda i: (0, 0))] + [spec] * 4 + [_full(w) for w in out_weights],
        out_specs=spec,
        out_shape=jax.ShapeDtypeStruct((m, d), BF16),
        compiler_params=_params(("arbitrary",)),
        name="rwkv_post",
    )(o_t, r, k, v, g, *out_weights)


def _gelu(x):
    return 0.5 * x * (1.0 + jnp.tanh(GELU_C * (x + 0.044715 * (x * x * x))))


def _layernorm(x, g, b):
    mu = jnp.mean(x, axis=-1, keepdims=True)
    xc = x - mu
    var = jnp.mean(xc * xc, axis=-1, keepdims=True)
    return xc * lax.rsqrt(var + LN_EPS) * g + b


def _sgu_sample_kernel(pu_ref, pv_ref, ng_ref, nb_ref, w_ref, bias_ref, y_ref, vs_ref):
    u = _gelu(pu_ref[...])
    vs = _layernorm(_gelu(pv_ref[...]), ng_ref[...], nb_ref[...])
    vs_ref[...] = vs
    y_ref[...] = (u * (w_ref[...] * vs + bias_ref[...])).astype(y_ref.dtype)


def _sgu_sample(p, lay, norm_g, norm_b, sgu_w, sgu_b):
    d = lay.d_sgu
    m = p.shape[0]
    w0 = jnp.repeat(sgu_w[:, 0, 0], SGU_CHUNK)[None, :]
    b0 = jnp.repeat(sgu_b[:, 0], SGU_CHUNK)[None, :]
    vec = pl.BlockSpec((1, d), lambda i: (0, 0))
    out = pl.BlockSpec((m, d), lambda i: (0, 0))
    return pl.pallas_call(
        _sgu_sample_kernel,
        grid=(1,),
        in_specs=[pl.BlockSpec((m, d), lambda i: (0, 0)),
                  pl.BlockSpec((m, d), lambda i: (0, 1)), vec, vec, vec, vec],
        out_specs=[out, out],
        out_shape=[jax.ShapeDtypeStruct((m, d), BF16), jax.ShapeDtypeStruct((m, d), F32)],
        compiler_params=_params(("arbitrary",)),
        name="sgu_sample",
    )(p, p, norm_g[None, :], norm_b[None, :], w0, b0)


def _out_proj_kernel(x_ref, ya_ref, yb_ref, w_ref, o_ref):
    da = ya_ref.shape[1]
    o_ref[...] = (x_ref[...] + jnp.dot(ya_ref[...], w_ref[:da, :].astype(BF16), preferred_element_type=F32)
                  + jnp.dot(yb_ref[...], w_ref[da:, :].astype(BF16), preferred_element_type=F32))


def _out_proj(x, ya, yb, w, tm):
    m, d = x.shape
    da = ya.shape[1]
    return pl.pallas_call(
        _out_proj_kernel,
        grid=(m // tm,),
        in_specs=[pl.BlockSpec((tm, d), lambda i: (i, 0)),
                  pl.BlockSpec((tm, da), lambda i: (i, 0)),
                  pl.BlockSpec((tm, da), lambda i: (i, 0)),
                  _resident(w)],
        out_specs=pl.BlockSpec((tm, d), lambda i: (i, 0)),
        out_shape=jax.ShapeDtypeStruct((m, d), F32),
        compiler_params=_params(("parallel",)),
        name="out_proj",
    )(x, ya, yb, w)


def _ffn_kernel(x_ref, xs_ref, g2_ref, wu_ref, wd_ref, gf_ref, o_ref, os_ref, h_ref, a_ref):
    f = pl.program_id(1)
    last = pl.num_programs(1) - 1
    tm = x_ref.shape[0]

    def up():
        a = jnp.dot(h_ref[...], wu_ref[...].astype(BF16), preferred_element_type=F32)
        return jnp.square(jnp.maximum(a, 0.0)).astype(BF16)

    def down():
        return jnp.dot(a_ref[...], wd_ref[...].astype(BF16), preferred_element_type=F32)

    @pl.when(f == 0)
    def _():
        x, xs = x_ref[...], xs_ref[...]
        h_ref[:tm, :] = _rms(x, g2_ref[...]).astype(BF16)
        h_ref[tm:, :] = _rms(xs, g2_ref[...]).astype(BF16)
        o_ref[...] = x
        os_ref[...] = xs
        a_ref[...] = up()

    @pl.when((f > 0) & (f < last))
    def _():
        acc = down()
        a_new = up()
        o_ref[...] += acc[:tm]
        os_ref[...] += acc[tm:]
        a_ref[...] = a_new

    @pl.when(f == last)
    def _():
        acc = down()
        o_ref[...] = _rms(o_ref[...] + acc[:tm], gf_ref[...])
        os_ref[...] = _rms(os_ref[...] + acc[tm:], gf_ref[...])


def _ffn(x, xs, g2, w_up, w_down, gf, tm, tf):
    m, d = x.shape
    n_blocks = m // tm
    ts = xs.shape[0] // n_blocks
    assert ts * n_blocks == xs.shape[0] and ts % 8 == 0
    nf = w_up.shape[1] // tf
    return pl.pallas_call(
        _ffn_kernel,
        grid=(n_blocks, nf + 1),
        in_specs=[pl.BlockSpec((tm, d), lambda i, f: (i, 0)),
                  pl.BlockSpec((ts, d), lambda i, f: (i, 0)),
                  pl.BlockSpec((1, d), lambda i, f: (0, 0)),
                  pl.BlockSpec((d, tf), lambda i, f: (0, jnp.minimum(f, nf - 1))),
                  pl.BlockSpec((tf, d), lambda i, f: (jnp.maximum(f - 1, 0), 0)),
                  pl.BlockSpec((1, d), lambda i, f: (0, 0))],
        out_specs=[pl.BlockSpec((tm, d), lambda i, f: (i, 0)),
                   pl.BlockSpec((ts, d), lambda i, f: (i, 0))],
        out_shape=[jax.ShapeDtypeStruct((m, d), F32), jax.ShapeDtypeStruct(xs.shape, F32)],
        scratch_shapes=[pltpu.VMEM((tm + ts, d), BF16), pltpu.VMEM((tm + ts, tf), BF16)],
        compiler_params=_params(("parallel", "arbitrary")),
        name="ffn",
    )(x, xs, g2[None, :], w_up, w_down, gf[None, :])


def _row_tile(m, cap):
    t = min(m, cap)
    assert m % t == 0
    return t


def kernel(x_prompt, x_sample, state_wkv, state_shift, norm1_g, w_in, mu_shift, w0, w_up, a0, a_up, g_up,
           k_k, k_a, r_k, lnx_g, lnx_b, sgu_norm_g, sgu_norm_b, sgu_w, sgu_b, w_out, norm2_g, w_ffn_up,
           w_ffn_down, norm_f_g):
    batch, seq, d_model = x_prompt.shape
    n_dec, dec_seq, _ = x_sample.shape
    depth = w_in.shape[0]
    assert depth == 1 and dec_seq == 1
    d_rwkv = w0.shape[1]
    d_sgu = sgu_norm_g.shape[1]
    lay = _Layout(d_rwkv, d_sgu, w_up.shape[1], a_up.shape[1], g_up.shape[1], PROJ_COLS)
    w_in_t = w_in[0].T.astype(BF16)
    prep_w = _prep_weights(lay, mu_shift[0], w0[0], w_up[0], a0[0], a_up[0], g_up[0], k_k[0], k_a[0])

    out_w = [w.reshape(1, d_rwkv) for w in (lnx_g[0], lnx_b[0], r_k[0])]

    xp = x_prompt.reshape(batch * seq, d_model)
    xs = x_sample.reshape(n_dec, d_model)
    sgu_bias = jnp.repeat(sgu_b[0].T, SGU_CHUNK, axis=1)
    ya, yb, wkv_p, last_p, ps_rw, ps_sg = _mixer_prompt(
        xp, xs, norm1_g, w_in_t, lay, prep_w, out_w, [sgu_norm_g, sgu_norm_b, sgu_w[0], sgu_bias],
        batch, seq, _row_tile(seq, MIXER_ROWS))
    x1p = _out_proj(xp, ya, yb, w_out[0], _row_tile(batch * seq, OUT_PROJ_ROWS))
    shift_p = last_p[:, 0, :lay.d_shift]

    (r, k, v, g), step_cols = _prep_sample(ps_rw, state_shift[0], lay, prep_w)
    o_t, wkv_s = _wkv_step(step_cols, state_wkv[0])
    yb, vs = _sgu_sample(ps_sg, lay, sgu_norm_g[0], sgu_norm_b[0], sgu_w[0], sgu_b[0])
    ya = _rwkv_post(o_t, r, k, v, g, out_w)
    x1s = _out_proj(xs, ya, yb, w_out[0], n_dec)
    shift_s = ps_rw[:, :lay.d_shift]

    y_prompt, y_sample = _ffn(x1p, x1s, norm2_g[0], w_ffn_up[0], w_ffn_down[0], norm_f_g,
                              _row_tile(batch * seq, FFN_ROWS), FFN_HIDDEN)
    y_prompt = y_prompt.reshape(batch, seq, d_model)
    y_sample = y_sample.reshape(n_dec, 1, d_model)
    return (y_prompt, y_sample, wkv_p[None], shift_p[None], wkv_s[None], shift_s[None],
            vs.reshape(1, n_dec, 1, d_sgu))
```
